```python
import jax, jax.numpy as jnp
from jax import lax
import numpy as np

D_MODEL = 2048
BATCH = 8
SEQ = 8192
DEPTH = 4

N_MIXERS = 3
BLOCK = 128
EPS = 1e-6

FOX_HEADS = 16
FOX_HEAD_DIM = D_MODEL // FOX_HEADS
FOX_FORGET_BIAS = 2.0

SGU_WIDTH = D_MODEL
SGU_GROUPS = 16
SGU_GROUP_DIM = SGU_WIDTH // SGU_GROUPS
SGU_CHUNK = 128

SWA_HEAD_DIM = 64
SWA_Q_HEADS = D_MODEL // SWA_HEAD_DIM
SWA_KV_HEADS = 8
SWA_WINDOW = 128
ROPE_DIM = SWA_HEAD_DIM // 4
ROPE_THETA = 500000.0

D_FF = ((8 * D_MODEL + 3 * 256 - 1) // (3 * 256)) * 256

N_FOX = (DEPTH + 2) // 3
N_SGU = (DEPTH + 1) // 3
N_SWA = DEPTH // 3

kernel_name = "hybrid_fox_gmlp_swa_sink_adaln"

F32 = jnp.float32


def rmsnorm(x, g):
    xf = x.astype(F32)
    y = xf * lax.rsqrt(jnp.mean(xf * xf, axis=-1, keepdims=True) + EPS)
    return (y * g.astype(F32)).astype(x.dtype)


def layernorm(x, g, b):
    xf = x.astype(F32)
    mu = jnp.mean(xf, axis=-1, keepdims=True)
    var = jnp.mean(jnp.square(xf - mu), axis=-1, keepdims=True)
    y = (xf - mu) * lax.rsqrt(var + EPS)
    return (y * g.astype(F32) + b.astype(F32)).astype(x.dtype)


def modulate(h, shift, scale):
    return h * (1 + scale[:, None, :]) + shift[:, None, :]


def rope_tables(positions):
    inv = ROPE_THETA ** (-jnp.arange(0, ROPE_DIM, 2, dtype=F32) / ROPE_DIM)
    ang = positions.astype(F32)[..., None] * inv
    return jnp.cos(ang), jnp.sin(ang)


def apply_partial_rope(x, cos, sin):
    half = ROPE_DIM // 2
    x1 = x[..., :half]
    x2 = x[..., half:ROPE_DIM]
    rest = x[..., ROPE_DIM:]
    c = cos[:, :, None, :].astype(x.dtype)
    s = sin[:, :, None, :].astype(x.dtype)
    return jnp.concatenate([x1 * c - x2 * s, x2 * c + x1 * s, rest], axis=-1)


def fox_attention(h, w_in, b_f, w_out):
    B, S, _ = h.shape
    H, Dh = FOX_HEADS, FOX_HEAD_DIM
    proj = h @ w_in
    q, k, v, fg = jnp.split(proj, [H * Dh, 2 * H * Dh, 3 * H * Dh], axis=-1)
    q = q.reshape(B, S, H, Dh)
    k = k.reshape(B, S, H, Dh)
    v = v.reshape(B, S, H, Dh)
    log_f = jax.nn.log_sigmoid((fg + b_f).astype(F32))
    cum = lax.cumsum(log_f, axis=1)
    cum_t = cum.transpose(0, 2, 1)
    nb = S // BLOCK
    qb = q.reshape(B, nb, BLOCK, H, Dh).transpose(1, 0, 2, 3, 4)
    fq = cum.reshape(B, nb, BLOCK, H).transpose(1, 0, 3, 2)
    kpos = jnp.arange(S)
    scale = Dh ** -0.5

    def block_fn(args):
        i, q_i, f_i = args
        s = jnp.einsum('bqhd,bkhd->bhqk', q_i, k, preferred_element_type=F32) * scale
        s = s + f_i[..., None] - cum_t[:, :, None, :]
        qpos = i * BLOCK + jnp.arange(BLOCK)
        mask = kpos[None, :] <= qpos[:, None]
        s = jnp.where(mask, s, -jnp.inf)
        p = jax.nn.softmax(s, axis=-1)
        return jnp.einsum('bhqk,bkhd->bqhd', p.astype(v.dtype), v)

    out = lax.map(block_fn, (jnp.arange(nb), qb, fq))
    out = out.transpose(1, 0, 2, 3, 4).reshape(B, S, H * Dh)
    return out @ w_out


def gmlp_sgu(h, w_in, ln_g, ln_b, w_s, b_s, w_out):
    B, S, _ = h.shape
    z = jax.nn.gelu(h @ w_in)
    u, v = jnp.split(z, 2, axis=-1)
    v = layernorm(v, ln_g, ln_b)
    nc = S // SGU_CHUNK
    vg = v.reshape(B, nc, SGU_CHUNK, SGU_GROUPS, SGU_GROUP_DIM)
    causal = jnp.tril(jnp.ones((SGU_CHUNK, SGU_CHUNK), dtype=bool))
    ws = jnp.where(causal[None], w_s, jnp.zeros_like(w_s))
    f = jnp.einsum('gts,bcsgd->bctgd', ws, vg)
    f = f + b_s.T[None, None, :, :, None]
    gated = u * f.reshape(B, S, SGU_WIDTH)
    return gated @ w_out


def swa_sink_attention(h, w_in, sinks, w_out, cos, sin):
    B, S, _ = h.shape
    Hq, Hk, Dh = SWA_Q_HEADS, SWA_KV_HEADS, SWA_HEAD_DIM
    G = Hq // Hk
    proj = h @ w_in
    q, k, v = jnp.split(proj, [Hq * Dh, (Hq + Hk) * Dh], axis=-1)
    q = apply_partial_rope(q.reshape(B, S, Hq, Dh), cos, sin)
    k = apply_partial_rope(k.reshape(B, S, Hk, Dh), cos, sin)
    v = v.reshape(B, S, Hk, Dh)
    nb = S // BLOCK
    qb = q.reshape(B, nb, BLOCK, Hk, G, Dh)
    kb = k.reshape(B, nb, BLOCK, Hk, Dh)
    vb = v.reshape(B, nb, BLOCK, Hk, Dh)
    pad = ((0, 0), (1, 0), (0, 0), (0, 0), (0, 0))
    kband = jnp.concatenate([jnp.pad(kb[:, :-1], pad), kb], axis=2)
    vband = jnp.concatenate([jnp.pad(vb[:, :-1], pad), vb], axis=2)
    s = jnp.einsum('bnqhgd,bnkhd->bnhgqk', qb, kband, preferred_element_type=F32) * (Dh ** -0.5)
    qi = jnp.arange(BLOCK)[:, None]
    ki = jnp.arange(2 * BLOCK)[None, :] - BLOCK
    rel = qi - ki
    valid = (rel >= 0) & (rel < SWA_WINDOW)
    in_seq = (jnp.arange(nb)[:, None, None] * BLOCK + ki[None]) >= 0
    mask = valid[None] & in_seq
    s = jnp.where(mask[None, :, None, None], s, -jnp.inf)
    sink = sinks.astype(F32).reshape(Hk, G)[None, None, :, :, None, None]
    m = jnp.maximum(jnp.max(s, axis=-1, keepdims=True), sink)
    p = jnp.exp(s - m)
    p = p / (jnp.sum(p, axis=-1, keepdims=True) + jnp.exp(sink - m))
    o = jnp.einsum('bnhgqk,bnkhd->bnqhgd', p.astype(v.dtype), vband)
    return o.reshape(B, S, Hq * Dh) @ w_out


def swiglu(h, w_gu, w_down):
    g, u = jnp.split(h @ w_gu, 2, axis=-1)
    return (jax.nn.silu(g) * u) @ w_down


def _fwd_setup_inputs(seed: int = 0) -> dict:
    key = jax.random.key(seed)
    ks = jax.random.split(key, 32)
    D = D_MODEL
    nrm = lambda k, shape, fan_in, mult=1.0: jax.random.normal(k, shape, F32) * (mult * fan_in ** -0.5)
    fox_in = 3 * FOX_HEADS * FOX_HEAD_DIM + FOX_HEADS
    swa_in = (SWA_Q_HEADS + 2 * SWA_KV_HEADS) * SWA_HEAD_DIM
    x = jax.random.normal(ks[0], (BATCH, SEQ, D), F32)
    c = jax.random.normal(ks[1], (BATCH, D), F32)
    offset = jax.random.randint(ks[2], (BATCH, 1), 0, 4096, dtype=jnp.int32)
    positions = offset + jnp.arange(SEQ, dtype=jnp.int32)[None, :]
    gain = lambda k, shape: 1.0 + 0.02 * jax.random.normal(k, shape, F32)
    return {
        "x": x,
        "c": c,
        "positions": positions,
        "ada_w": nrm(ks[3], (DEPTH, D, 6 * D), D, 0.5),
        "ada_b": 0.01 * jax.random.normal(ks[4], (DEPTH, 6 * D), F32),
        "mix_pre_g": gain(ks[5], (DEPTH, D)),
        "mix_post_g": gain(ks[6], (DEPTH, D)),
        "ffn_pre_g": gain(ks[7], (DEPTH, D)),
        "ffn_post_g": gain(ks[8], (DEPTH, D)),
        "ffn_w_gu": nrm(ks[9], (DEPTH, D, 2 * D_FF), D),
        "ffn_w_down": nrm(ks[10], (DEPTH, D_FF, D), D_FF),
        "fox_w_in": nrm(ks[11], (N_FOX, D, fox_in), D),
        "fox_b_f": FOX_FORGET_BIAS + 0.5 * jax.random.normal(ks[12], (N_FOX, FOX_HEADS), F32),
        "fox_w_out": nrm(ks[13], (N_FOX, FOX_HEADS * FOX_HEAD_DIM, D), FOX_HEADS * FOX_HEAD_DIM),
        "sgu_w_in": nrm(ks[14], (N_SGU, D, 2 * SGU_WIDTH), D),
        "sgu_ln_g": gain(ks[15], (N_SGU, SGU_WIDTH)),
        "sgu_ln_b": 0.01 * jax.random.normal(ks[16], (N_SGU, SGU_WIDTH), F32),
        "sgu_w_s": nrm(ks[17], (N_SGU, SGU_GROUPS, SGU_CHUNK, SGU_CHUNK), SGU_CHUNK),
        "sgu_b_s": 1.0 + 0.02 * jax.random.normal(ks[18], (N_SGU, SGU_GROUPS, SGU_CHUNK), F32),
        "sgu_w_out": nrm(ks[19], (N_SGU, SGU_WIDTH, D), SGU_WIDTH),
        "swa_w_in": nrm(ks[20], (N_SWA, D, swa_in), D),
        "swa_sinks": 0.5 * jax.random.normal(ks[21], (N_SWA, SWA_Q_HEADS), F32),
        "swa_w_out": nrm(ks[22], (N_SWA, SWA_Q_HEADS * SWA_HEAD_DIM, D), SWA_Q_HEADS * SWA_HEAD_DIM),
    }


def _fwd_reference(x, c, positions, ada_w, ada_b, mix_pre_g, mix_post_g, ffn_pre_g, ffn_post_g,
              ffn_w_gu, ffn_w_down, fox_w_in, fox_b_f, fox_w_out, sgu_w_in, sgu_ln_g,
              sgu_ln_b, sgu_w_s, sgu_b_s, sgu_w_out, swa_w_in, swa_sinks, swa_w_out):
    cos, sin = rope_tables(positions)
    c_act = jax.nn.silu(c)
    for i in range(DEPTH):
        mod = c_act @ ada_w[i] + ada_b[i]
        sh_m, sc_m, g_m, sh_f, sc_f, g_f = jnp.split(mod, 6, axis=-1)
        h = modulate(rmsnorm(x, mix_pre_g[i]), sh_m, sc_m)
        kind, j = i % N_MIXERS, i // N_MIXERS
        if kind == 0:
            y = fox_attention(h, fox_w_in[j], fox_b_f[j], fox_w_out[j])
        elif kind == 1:
            y = gmlp_sgu(h, sgu_w_in[j], sgu_ln_g[j], sgu_ln_b[j], sgu_w_s[j], sgu_b_s[j], sgu_w_out[j])
        else:
            y = swa_sink_attention(h, swa_w_in[j], swa_sinks[j], swa_w_out[j], cos, sin)
        x = x + g_m[:, None, :] * rmsnorm(y, mix_post_g[i])
        h = modulate(rmsnorm(x, ffn_pre_g[i]), sh_f, sc_f)
        y = swiglu(h, ffn_w_gu[i], ffn_w_down[i])
        x = x + g_f[:, None, :] * rmsnorm(y, ffn_post_g[i])
    return x


import jax as _jax
import jax.numpy as _jnp

TWIN_FORMAT = 'train_step'
FWD_PARAMS = ['x', 'c', 'positions', 'ada_w', 'ada_b', 'mix_pre_g', 'mix_post_g', 'ffn_pre_g', 'ffn_post_g', 'ffn_w_gu', 'ffn_w_down', 'fox_w_in', 'fox_b_f', 'fox_w_out', 'sgu_w_in', 'sgu_ln_g', 'sgu_ln_b', 'sgu_w_s', 'sgu_b_s', 'sgu_w_out', 'swa_w_in', 'swa_sinks', 'swa_w_out']
TWIN_WEIGHTS = ['ada_w', 'ada_b', 'mix_pre_g', 'mix_post_g', 'ffn_pre_g', 'ffn_post_g', 'ffn_w_gu', 'ffn_w_down', 'fox_w_in', 'fox_b_f', 'fox_w_out', 'sgu_w_in', 'sgu_ln_g', 'sgu_ln_b', 'sgu_w_s', 'sgu_b_s', 'sgu_w_out', 'swa_w_in', 'swa_sinks', 'swa_w_out']
TWIN_DIFF_INPUT = 'x'
TWIN_INPUTS = ['x', 'c', 'positions', 'ada_w', 'ada_b', 'mix_pre_g', 'mix_post_g', 'ffn_pre_g', 'ffn_post_g', 'ffn_w_gu', 'ffn_w_down', 'fox_w_in', 'fox_b_f', 'fox_w_out', 'sgu_w_in', 'sgu_ln_g', 'sgu_ln_b', 'sgu_w_s', 'sgu_b_s', 'sgu_w_out', 'swa_w_in', 'swa_sinks', 'swa_w_out', 'loss_target', 'm_ada_w', 'm_ada_b', 'm_mix_pre_g', 'm_mix_post_g', 'm_ffn_pre_g', 'm_ffn_post_g', 'm_ffn_w_gu', 'm_ffn_w_down', 'm_fox_w_in', 'm_fox_b_f', 'm_fox_w_out', 'm_sgu_w_in', 'm_sgu_ln_g', 'm_sgu_ln_b', 'm_sgu_w_s', 'm_sgu_b_s', 'm_sgu_w_out', 'm_swa_w_in', 'm_swa_sinks', 'm_swa_w_out', 'v_ada_w', 'v_ada_b', 'v_mix_pre_g', 'v_mix_post_g', 'v_ffn_pre_g', 'v_ffn_post_g', 'v_ffn_w_gu', 'v_ffn_w_down', 'v_fox_w_in', 'v_fox_b_f', 'v_fox_w_out', 'v_sgu_w_in', 'v_sgu_ln_g', 'v_sgu_ln_b', 'v_sgu_w_s', 'v_sgu_b_s', 'v_sgu_w_out', 'v_swa_w_in', 'v_swa_sinks', 'v_swa_w_out']
TWIN_OUTPUTS = ['loss', 'grad_x', 'grad_ada_w', 'grad_ada_b', 'grad_mix_pre_g', 'grad_mix_post_g', 'grad_ffn_pre_g', 'grad_ffn_post_g', 'grad_ffn_w_gu', 'grad_ffn_w_down', 'grad_fox_w_in', 'grad_fox_b_f', 'grad_fox_w_out', 'grad_sgu_w_in', 'grad_sgu_ln_g', 'grad_sgu_ln_b', 'grad_sgu_w_s', 'grad_sgu_b_s', 'grad_sgu_w_out', 'grad_swa_w_in', 'grad_swa_sinks', 'grad_swa_w_out', 'delta_ada_w', 'delta_ada_b', 'delta_mix_pre_g', 'delta_mix_post_g', 'delta_ffn_pre_g', 'delta_ffn_post_g', 'delta_ffn_w_gu', 'delta_ffn_w_down', 'delta_fox_w_in', 'delta_fox_b_f', 'delta_fox_w_out', 'delta_sgu_w_in', 'delta_sgu_ln_g', 'delta_sgu_ln_b', 'delta_sgu_w_s', 'delta_sgu_b_s', 'delta_sgu_w_out', 'delta_swa_w_in', 'delta_swa_sinks', 'delta_swa_w_out', 'new_m_ada_w', 'new_m_ada_b', 'new_m_mix_pre_g', 'new_m_mix_post_g', 'new_m_ffn_pre_g', 'new_m_ffn_post_g', 'new_m_ffn_w_gu', 'new_m_ffn_w_down', 'new_m_fox_w_in', 'new_m_fox_b_f', 'new_m_fox_w_out', 'new_m_sgu_w_in', 'new_m_sgu_ln_g', 'new_m_sgu_ln_b', 'new_m_sgu_w_s', 'new_m_sgu_b_s', 'new_m_sgu_w_out', 'new_m_swa_w_in', 'new_m_swa_sinks', 'new_m_swa_w_out', 'new_v_ada_w', 'new_v_ada_b', 'new_v_mix_pre_g', 'new_v_mix_post_g', 'new_v_ffn_pre_g', 'new_v_ffn_post_g', 'new_v_ffn_w_gu', 'new_v_ffn_w_down', 'new_v_fox_w_in', 'new_v_fox_b_f', 'new_v_fox_w_out', 'new_v_sgu_w_in', 'new_v_sgu_ln_g', 'new_v_sgu_ln_b', 'new_v_sgu_w_s', 'new_v_sgu_b_s', 'new_v_sgu_w_out', 'new_v_swa_w_in', 'new_v_swa_sinks', 'new_v_swa_w_out']
TWIN_LEAF_KINDS = {'loss': 'loss', 'grad_x': 'grad_x', 'grad_ada_w': 'grad_w', 'grad_ada_b': 'grad_w', 'grad_mix_pre_g': 'grad_w', 'grad_mix_post_g': 'grad_w', 'grad_ffn_pre_g': 'grad_w', 'grad_ffn_post_g': 'grad_w', 'grad_ffn_w_gu': 'grad_w', 'grad_ffn_w_down': 'grad_w', 'grad_fox_w_in': 'grad_w', 'grad_fox_b_f': 'grad_w', 'grad_fox_w_out': 'grad_w', 'grad_sgu_w_in': 'grad_w', 'grad_sgu_ln_g': 'grad_w', 'grad_sgu_ln_b': 'grad_w', 'grad_sgu_w_s': 'grad_w', 'grad_sgu_b_s': 'grad_w', 'grad_sgu_w_out': 'grad_w', 'grad_swa_w_in': 'grad_w', 'grad_swa_sinks': 'grad_w', 'grad_swa_w_out': 'grad_w', 'delta_ada_w': 'delta_w', 'delta_ada_b': 'delta_w', 'delta_mix_pre_g': 'delta_w', 'delta_mix_post_g': 'delta_w', 'delta_ffn_pre_g': 'delta_w', 'delta_ffn_post_g': 'delta_w', 'delta_ffn_w_gu': 'delta_w', 'delta_ffn_w_down': 'delta_w', 'delta_fox_w_in': 'delta_w', 'delta_fox_b_f': 'delta_w', 'delta_fox_w_out': 'delta_w', 'delta_sgu_w_in': 'delta_w', 'delta_sgu_ln_g': 'delta_w', 'delta_sgu_ln_b': 'delta_w', 'delta_sgu_w_s': 'delta_w', 'delta_sgu_b_s': 'delta_w', 'delta_sgu_w_out': 'delta_w', 'delta_swa_w_in': 'delta_w', 'delta_swa_sinks': 'delta_w', 'delta_swa_w_out': 'delta_w', 'new_m_ada_w': 'new_m', 'new_m_ada_b': 'new_m', 'new_m_mix_pre_g': 'new_m', 'new_m_mix_post_g': 'new_m', 'new_m_ffn_pre_g': 'new_m', 'new_m_ffn_post_g': 'new_m', 'new_m_ffn_w_gu': 'new_m', 'new_m_ffn_w_down': 'new_m', 'new_m_fox_w_in': 'new_m', 'new_m_fox_b_f': 'new_m', 'new_m_fox_w_out': 'new_m', 'new_m_sgu_w_in': 'new_m', 'new_m_sgu_ln_g': 'new_m', 'new_m_sgu_ln_b': 'new_m', 'new_m_sgu_w_s': 'new_m', 'new_m_sgu_b_s': 'new_m', 'new_m_sgu_w_out': 'new_m', 'new_m_swa_w_in': 'new_m', 'new_m_swa_sinks': 'new_m', 'new_m_swa_w_out': 'new_m', 'new_v_ada_w': 'new_v', 'new_v_ada_b': 'new_v', 'new_v_mix_pre_g': 'new_v', 'new_v_mix_post_g': 'new_v', 'new_v_ffn_pre_g': 'new_v', 'new_v_ffn_post_g': 'new_v', 'new_v_ffn_w_gu': 'new_v', 'new_v_ffn_w_down': 'new_v', 'new_v_fox_w_in': 'new_v', 'new_v_fox_b_f': 'new_v', 'new_v_fox_w_out': 'new_v', 'new_v_sgu_w_in': 'new_v', 'new_v_sgu_ln_g': 'new_v', 'new_v_sgu_ln_b': 'new_v', 'new_v_sgu_w_s': 'new_v', 'new_v_sgu_b_s': 'new_v', 'new_v_sgu_w_out': 'new_v', 'new_v_swa_w_in': 'new_v', 'new_v_swa_sinks': 'new_v', 'new_v_swa_w_out': 'new_v'}


def _forward(args):
    return _fwd_reference(*[args[k] for k in FWD_PARAMS])


def _output_shape():
    def fwd():
        inp = _fwd_setup_inputs(0)
        return _fwd_reference(*[inp[k] for k in FWD_PARAMS])
    out = _jax.eval_shape(fwd)
    return out.shape, out.dtype

N_MICROBATCH = 1
ADAM_LR = 0.001
ADAM_B1 = 0.9
ADAM_B2 = 0.999
ADAM_EPS = 1e-08
ADAM_WD = 0.01
ADAM_STEP = 10
PER_EXAMPLE_BATCH_AXIS = {'x': 0, 'c': 0, 'positions': 0, 'loss_target': 0}
SHARED_INPUTS = []
_WEIGHT_DTYPES = {'ada_w': _jnp.float32, 'ada_b': _jnp.float32, 'mix_pre_g': _jnp.float32, 'mix_post_g': _jnp.float32, 'ffn_pre_g': _jnp.float32, 'ffn_post_g': _jnp.float32, 'ffn_w_gu': _jnp.float32, 'ffn_w_down': _jnp.float32, 'fox_w_in': _jnp.float32, 'fox_b_f': _jnp.float32, 'fox_w_out': _jnp.float32, 'sgu_w_in': _jnp.float32, 'sgu_ln_g': _jnp.float32, 'sgu_ln_b': _jnp.float32, 'sgu_w_s': _jnp.float32, 'sgu_b_s': _jnp.float32, 'sgu_w_out': _jnp.float32, 'swa_w_in': _jnp.float32, 'swa_sinks': _jnp.float32, 'swa_w_out': _jnp.float32}
MOMENT_SCALE = {'ada_w': 1.772314e+00, 'ada_b': 3.385867e+00, 'mix_pre_g': 4.814718e-01, 'mix_post_g': 3.670090e+00, 'ffn_pre_g': 1.787061e-01, 'ffn_post_g': 3.238670e+00, 'ffn_w_gu': 1.120779e-01, 'ffn_w_down': 2.256036e-01, 'fox_w_in': 6.013812e-01, 'fox_b_f': 8.203883e-01, 'fox_w_out': 1.104169e+00, 'sgu_w_in': 2.213304e-01, 'sgu_ln_g': 4.654713e-02, 'sgu_ln_b': 5.766471e-02, 'sgu_w_s': 4.515342e-02, 'sgu_b_s': 6.852305e-02, 'sgu_w_out': 7.848993e-01, 'swa_w_in': 1.318647e+00, 'swa_sinks': 3.293019e-02, 'swa_w_out': 1.600851e+00}


def _to_microbatches(a, axis):
    t = _jnp.moveaxis(a, axis, 0)
    t = t.reshape((N_MICROBATCH, t.shape[0] // N_MICROBATCH) + t.shape[1:])
    return _jnp.moveaxis(t, 1, axis + 1)


def setup_inputs(seed: int = 0) -> dict:
    inp = _fwd_setup_inputs(seed)
    key = _jax.random.fold_in(_jax.random.key(seed), 7919)
    shape, _ = _output_shape()
    out = dict(inp)
    out["loss_target"] = _jax.random.normal(_jax.random.fold_in(key, 0), shape, _jnp.float32)
    for i, name in enumerate(TWIN_WEIGHTS):
        w = inp[name].astype(_jnp.float32)
        if MOMENT_SCALE is None:
            s = _jnp.sqrt(_jnp.mean(_jnp.square(w)) + 1e-30)
        else:
            s = MOMENT_SCALE[name]
        km, kv = _jax.random.split(_jax.random.fold_in(key, i + 1))
        out[name] = w
        out["m_" + name] = s * _jax.random.normal(km, w.shape, _jnp.float32)
        out["v_" + name] = (s * s) * _jax.random.uniform(kv, w.shape, _jnp.float32, 0.5, 1.5)
    if N_MICROBATCH > 1:
        for name, axis in PER_EXAMPLE_BATCH_AXIS.items():
            out[name] = _to_microbatches(out[name], axis)
    return {'x': out['x'], 'c': out['c'], 'positions': out['positions'], 'ada_w': out['ada_w'], 'ada_b': out['ada_b'], 'mix_pre_g': out['mix_pre_g'], 'mix_post_g': out['mix_post_g'], 'ffn_pre_g': out['ffn_pre_g'], 'ffn_post_g': out['ffn_post_g'], 'ffn_w_gu': out['ffn_w_gu'], 'ffn_w_down': out['ffn_w_down'], 'fox_w_in': out['fox_w_in'], 'fox_b_f': out['fox_b_f'], 'fox_w_out': out['fox_w_out'], 'sgu_w_in': out['sgu_w_in'], 'sgu_ln_g': out['sgu_ln_g'], 'sgu_ln_b': out['sgu_ln_b'], 'sgu_w_s': out['sgu_w_s'], 'sgu_b_s': out['sgu_b_s'], 'sgu_w_out': out['sgu_w_out'], 'swa_w_in': out['swa_w_in'], 'swa_sinks': out['swa_sinks'], 'swa_w_out': out['swa_w_out'], 'loss_target': out['loss_target'], 'm_ada_w': out['m_ada_w'], 'm_ada_b': out['m_ada_b'], 'm_mix_pre_g': out['m_mix_pre_g'], 'm_mix_post_g': out['m_mix_post_g'], 'm_ffn_pre_g': out['m_ffn_pre_g'], 'm_ffn_post_g': out['m_ffn_post_g'], 'm_ffn_w_gu': out['m_ffn_w_gu'], 'm_ffn_w_down': out['m_ffn_w_down'], 'm_fox_w_in': out['m_fox_w_in'], 'm_fox_b_f': out['m_fox_b_f'], 'm_fox_w_out': out['m_fox_w_out'], 'm_sgu_w_in': out['m_sgu_w_in'], 'm_sgu_ln_g': out['m_sgu_ln_g'], 'm_sgu_ln_b': out['m_sgu_ln_b'], 'm_sgu_w_s': out['m_sgu_w_s'], 'm_sgu_b_s': out['m_sgu_b_s'], 'm_sgu_w_out': out['m_sgu_w_out'], 'm_swa_w_in': out['m_swa_w_in'], 'm_swa_sinks': out['m_swa_sinks'], 'm_swa_w_out': out['m_swa_w_out'], 'v_ada_w': out['v_ada_w'], 'v_ada_b': out['v_ada_b'], 'v_mix_pre_g': out['v_mix_pre_g'], 'v_mix_post_g': out['v_mix_post_g'], 'v_ffn_pre_g': out['v_ffn_pre_g'], 'v_ffn_post_g': out['v_ffn_post_g'], 'v_ffn_w_gu': out['v_ffn_w_gu'], 'v_ffn_w_down': out['v_ffn_w_down'], 'v_fox_w_in': out['v_fox_w_in'], 'v_fox_b_f': out['v_fox_b_f'], 'v_fox_w_out': out['v_fox_w_out'], 'v_sgu_w_in': out['v_sgu_w_in'], 'v_sgu_ln_g': out['v_sgu_ln_g'], 'v_sgu_ln_b': out['v_sgu_ln_b'], 'v_sgu_w_s': out['v_sgu_w_s'], 'v_sgu_b_s': out['v_sgu_b_s'], 'v_sgu_w_out': out['v_sgu_w_out'], 'v_swa_w_in': out['v_swa_w_in'], 'v_swa_sinks': out['v_swa_sinks'], 'v_swa_w_out': out['v_swa_w_out']}


def _loss(weights, diff, rest, loss_target):
    with _jax.named_scope("forward"):
        args = {**rest, TWIN_DIFF_INPUT: diff, **{k: w.astype(_WEIGHT_DTYPES[k]) for k, w in weights.items()}}
        y = _forward(args)
    with _jax.named_scope("loss_head"):
        err = _jnp.square(y.astype(_jnp.float32) - loss_target)
        return 0.5 * _jnp.sum(_jnp.mean(err, axis=-1)) if err.ndim else 0.5 * err


def _adamw(w, g, m, v):
    m = ADAM_B1 * m + (1.0 - ADAM_B1) * g
    v = ADAM_B2 * v + (1.0 - ADAM_B2) * _jnp.square(g)
    m_hat = m / (1.0 - ADAM_B1 ** ADAM_STEP)
    v_hat = v / (1.0 - ADAM_B2 ** ADAM_STEP)
    delta = -ADAM_LR * (m_hat / (_jnp.sqrt(v_hat) + ADAM_EPS) + ADAM_WD * w)
    return delta, m, v


def reference(x, c, positions, ada_w, ada_b, mix_pre_g, mix_post_g, ffn_pre_g, ffn_post_g, ffn_w_gu, ffn_w_down, fox_w_in, fox_b_f, fox_w_out, sgu_w_in, sgu_ln_g, sgu_ln_b, sgu_w_s, sgu_b_s, sgu_w_out, swa_w_in, swa_sinks, swa_w_out, loss_target, m_ada_w, m_ada_b, m_mix_pre_g, m_mix_post_g, m_ffn_pre_g, m_ffn_post_g, m_ffn_w_gu, m_ffn_w_down, m_fox_w_in, m_fox_b_f, m_fox_w_out, m_sgu_w_in, m_sgu_ln_g, m_sgu_ln_b, m_sgu_w_s, m_sgu_b_s, m_sgu_w_out, m_swa_w_in, m_swa_sinks, m_swa_w_out, v_ada_w, v_ada_b, v_mix_pre_g, v_mix_post_g, v_ffn_pre_g, v_ffn_post_g, v_ffn_w_gu, v_ffn_w_down, v_fox_w_in, v_fox_b_f, v_fox_w_out, v_sgu_w_in, v_sgu_ln_g, v_sgu_ln_b, v_sgu_w_s, v_sgu_b_s, v_sgu_w_out, v_swa_w_in, v_swa_sinks, v_swa_w_out):
    given = dict(x=x, c=c, positions=positions, ada_w=ada_w, ada_b=ada_b, mix_pre_g=mix_pre_g, mix_post_g=mix_post_g, ffn_pre_g=ffn_pre_g, ffn_post_g=ffn_post_g, ffn_w_gu=ffn_w_gu, ffn_w_down=ffn_w_down, fox_w_in=fox_w_in, fox_b_f=fox_b_f, fox_w_out=fox_w_out, sgu_w_in=sgu_w_in, sgu_ln_g=sgu_ln_g, sgu_ln_b=sgu_ln_b, sgu_w_s=sgu_w_s, sgu_b_s=sgu_b_s, sgu_w_out=sgu_w_out, swa_w_in=swa_w_in, swa_sinks=swa_sinks, swa_w_out=swa_w_out, loss_target=loss_target, m_ada_w=m_ada_w, m_ada_b=m_ada_b, m_mix_pre_g=m_mix_pre_g, m_mix_post_g=m_mix_post_g, m_ffn_pre_g=m_ffn_pre_g, m_ffn_post_g=m_ffn_post_g, m_ffn_w_gu=m_ffn_w_gu, m_ffn_w_down=m_ffn_w_down, m_fox_w_in=m_fox_w_in, m_fox_b_f=m_fox_b_f, m_fox_w_out=m_fox_w_out, m_sgu_w_in=m_sgu_w_in, m_sgu_ln_g=m_sgu_ln_g, m_sgu_ln_b=m_sgu_ln_b, m_sgu_w_s=m_sgu_w_s, m_sgu_b_s=m_sgu_b_s, m_sgu_w_out=m_sgu_w_out, m_swa_w_in=m_swa_w_in, m_swa_sinks=m_swa_sinks, m_swa_w_out=m_swa_w_out, v_ada_w=v_ada_w, v_ada_b=v_ada_b, v_mix_pre_g=v_mix_pre_g, v_mix_post_g=v_mix_post_g, v_ffn_pre_g=v_ffn_pre_g, v_ffn_post_g=v_ffn_post_g, v_ffn_w_gu=v_ffn_w_gu, v_ffn_w_down=v_ffn_w_down, v_fox_w_in=v_fox_w_in, v_fox_b_f=v_fox_b_f, v_fox_w_out=v_fox_w_out, v_sgu_w_in=v_sgu_w_in, v_sgu_ln_g=v_sgu_ln_g, v_sgu_ln_b=v_sgu_ln_b, v_sgu_w_s=v_sgu_w_s, v_sgu_b_s=v_sgu_b_s, v_sgu_w_out=v_sgu_w_out, v_swa_w_in=v_swa_w_in, v_swa_sinks=v_swa_sinks, v_swa_w_out=v_swa_w_out)
    weights = {n: given[n] for n in TWIN_WEIGHTS}
    shared = {n: given[n] for n in SHARED_INPUTS}
    per_example = {n: given[n] for n in ['x', 'c', 'positions']}
    grad_fn = _jax.value_and_grad(_loss, argnums=(0, 1))

    def one_microbatch(ex, loss_target):
        ex = dict(ex)
        diff = ex.pop(TWIN_DIFF_INPUT)
        return grad_fn(weights, diff, {**shared, **ex}, loss_target)

    if N_MICROBATCH == 1:
        loss, (grad_w, grad_x) = one_microbatch(per_example, given["loss_target"])
    else:
        def body(carry, xs):
            loss_sum, grad_sum = carry
            l_k, (gw_k, gx_k) = one_microbatch(xs[0], xs[1])
            with _jax.named_scope("update"):
                return (loss_sum + l_k, _jax.tree.map(_jnp.add, grad_sum, gw_k)), gx_k

        init = (_jnp.zeros((), _jnp.float32), _jax.tree.map(_jnp.zeros_like, weights))
        (loss, grad_w), grad_x = _jax.lax.scan(body, init, (per_example, given["loss_target"]))
    with _jax.named_scope("update"):
        delta_w, new_m, new_v = {}, {}, {}
        for n in TWIN_WEIGHTS:
            delta_w[n], new_m[n], new_v[n] = _adamw(weights[n], grad_w[n], given["m_" + n], given["v_" + n])
    return (loss, grad_x, *[grad_w[n] for n in TWIN_WEIGHTS], *[delta_w[n] for n in TWIN_WEIGHTS],
            *[new_m[n] for n in TWIN_WEIGHTS], *[new_v[n] for n in TWIN_WEIGHTS])
```

```python
import functools

import numpy as np
import jax
import jax.numpy as jnp
from jax import lax
from jax.experimental import pallas as pl
from jax.experimental.pallas import tpu as pltpu

F32 = jnp.float32
BF16 = jnp.bfloat16
MESH = pl.DeviceIdType.MESH

EPS = 1e-6
NEG = -1e30
V7X_VMEM_BYTES = 64 * 1024 * 1024
VMEM_LIMIT = V7X_VMEM_BYTES - 8 * 1024 * 1024
LANES = 128
PACK_W = 512
SEQ_BLOCK = 128
SWA_HEAD_DIM = 64
ROPE_DIM = SWA_HEAD_DIM // 4
ROPE_THETA = 500000.0

ADAM_LR = 0.001
ADAM_B1 = 0.9
ADAM_B2 = 0.999
ADAM_EPS = 1e-08
ADAM_WD = 0.01
ADAM_STEP = 10


def _cp(sem=None, **kw):
    return pltpu.CompilerParams(dimension_semantics=sem, vmem_limit_bytes=VMEM_LIMIT, **kw)


def _tile(dim, pref, mult=LANES):
    if dim <= pref:
        return dim
    t = (pref // mult) * mult
    while t >= mult:
        if dim % t == 0:
            return t
        t -= mult
    return dim


_DIMS = {"nn": (((1,), (0,)), ((), ())), "nt": (((1,), (1,)), ((), ())), "tn": (((0,), (0,)), ((), ()))}


def mm(a, b, mode="nn", out_dtype=F32, add=None, name="mm", b_cols=None, tm=1024, tn=1024, tk=1024):
    c0 = 0
    if b_cols is not None:
        c0, csize = b_cols
    if mode == "nn":
        (M, K), (K2, N) = a.shape, b.shape
        if b_cols is not None:
            N = csize
    elif mode == "nt":
        (M, K), (N, K2) = a.shape, b.shape
        if b_cols is not None:
            K2 = csize
    else:
        (K, M), (K2, N) = a.shape, b.shape
        assert b_cols is None
    assert K == K2, (a.shape, b.shape, mode)
    tm = _tile(M, tm, LANES if mode == "tn" else 16)
    tn = _tile(N, tn)
    tk = _tile(K, tk, LANES if mode != "tn" else 16)
    nk = K // tk
    if b_cols is not None:
        assert c0 % (tn if mode == "nn" else tk) == 0, (b_cols, tn, tk)
    bo = c0 // (tn if mode == "nn" else tk)
    dims = _DIMS[mode]
    has_add = add is not None

    def body(a_ref, b_ref, *rest):
        if has_add:
            add_ref, o_ref, acc_ref = rest
        else:
            o_ref, acc_ref = rest
        k = pl.program_id(2)
        p = lax.dot_general(a_ref[...].astype(BF16), b_ref[...].astype(BF16), dims, preferred_element_type=F32)

        @pl.when(k == 0)
        def _():
            acc_ref[...] = p + add_ref[...].astype(F32) if has_add else p

        @pl.when(k > 0)
        def _():
            acc_ref[...] += p

        @pl.when(k == nk - 1)
        def _():
            o_ref[...] = acc_ref[...].astype(o_ref.dtype)

    a_spec = pl.BlockSpec((tk, tm), lambda i, j, k: (k, i)) if mode == "tn" else pl.BlockSpec((tm, tk), lambda i, j, k: (i, k))
    b_spec = pl.BlockSpec((tn, tk), lambda i, j, k: (j, k + bo)) if mode == "nt" else pl.BlockSpec((tk, tn), lambda i, j, k: (k, j + bo))
    o_spec = pl.BlockSpec((tm, tn), lambda i, j, k: (i, j))
    in_specs = [a_spec, b_spec] + ([o_spec] if has_add else [])
    args = (a, b) + ((add,) if has_add else ())
    return pl.pallas_call(
        body, name=name, grid=(M // tm, N // tn, nk), in_specs=in_specs, out_specs=o_spec,
        out_shape=jax.ShapeDtypeStruct((M, N), out_dtype), scratch_shapes=[pltpu.VMEM((tm, tn), F32)],
        compiler_params=_cp(("parallel", "parallel", "arbitrary")),
    )(*args)


def _rms(x, g):
    return (x * lax.rsqrt(jnp.mean(x * x, axis=-1, keepdims=True) + EPS)) * g


def _pre(x, g, sh, sc):
    return _rms(x, g) * (1 + sc) + sh


def _post(x, y, g, gate):
    return x + gate * _rms(y, g)


ROW_TILE = 256


def _row_spec(tr, d):
    return pl.BlockSpec((tr, d), lambda i: (i, 0))


def _vec_spec(d):
    return pl.BlockSpec((1, d), lambda i: (0, 0))


def pre_fwd(x, g, sh, sc, name):
    S, D = x.shape
    tr = _tile(S, ROW_TILE, 16)

    def body(x_ref, g_ref, sh_ref, sc_ref, h_ref):
        h_ref[...] = _pre(x_ref[...], g_ref[...], sh_ref[...], sc_ref[...]).astype(h_ref.dtype)

    return pl.pallas_call(
        body, name=name, grid=(S // tr,), in_specs=[_row_spec(tr, D)] + [_vec_spec(D)] * 3, out_specs=_row_spec(tr, D),
        out_shape=jax.ShapeDtypeStruct((S, D), BF16), compiler_params=_cp(("parallel",)),
    )(x, g, sh, sc)


def pre_bwd(x, g, sh, sc, dh, dres, name):
    S, D = x.shape
    tr = _tile(S, ROW_TILE, 16)

    def body(x_ref, g_ref, sh_ref, sc_ref, dh_ref, dres_ref, dx_ref, dg_ref, dsh_ref, dsc_ref):
        _, vjp = jax.vjp(_pre, x_ref[...], g_ref[...], sh_ref[...], sc_ref[...])
        dx, dg, dsh, dsc = vjp(dh_ref[...].astype(F32))
        dx_ref[...] = dres_ref[...] + dx

        @pl.when(pl.program_id(0) == 0)
        def _():
            dg_ref[...] = jnp.zeros_like(dg_ref)
            dsh_ref[...] = jnp.zeros_like(dsh_ref)
            dsc_ref[...] = jnp.zeros_like(dsc_ref)

        dg_ref[...] += dg
        dsh_ref[...] += dsh
        dsc_ref[...] += dsc

    vec = jax.ShapeDtypeStruct((1, D), F32)
    return pl.pallas_call(
        body, name=name, grid=(S // tr,), in_specs=[_row_spec(tr, D)] + [_vec_spec(D)] * 3 + [_row_spec(tr, D)] * 2,
        out_specs=[_row_spec(tr, D)] + [_vec_spec(D)] * 3, out_shape=[jax.ShapeDtypeStruct((S, D), F32), vec, vec, vec],
        compiler_params=_cp(("arbitrary",)),
    )(x, g, sh, sc, dh, dres)


def post_fwd(x, y, g, gate, name):
    S, D = x.shape
    tr = _tile(S, ROW_TILE, 16)

    def body(x_ref, y_ref, g_ref, gate_ref, o_ref):
        o_ref[...] = _post(x_ref[...], y_ref[...], g_ref[...], gate_ref[...])

    return pl.pallas_call(
        body, name=name, grid=(S // tr,), in_specs=[_row_spec(tr, D)] * 2 + [_vec_spec(D)] * 2, out_specs=_row_spec(tr, D),
        out_shape=jax.ShapeDtypeStruct((S, D), F32), compiler_params=_cp(("parallel",)),
    )(x, y, g, gate)


def post_bwd(y, g, gate, dxn, name):
    S, D = y.shape
    tr = _tile(S, ROW_TILE, 16)

    def body(y_ref, g_ref, gate_ref, dxn_ref, dy_ref, dg_ref, dgate_ref):
        fn = lambda yy, gg, gt: gt * _rms(yy, gg)
        _, vjp = jax.vjp(fn, y_ref[...], g_ref[...], gate_ref[...])
        dy, dg, dgate = vjp(dxn_ref[...])
        dy_ref[...] = dy.astype(dy_ref.dtype)

        @pl.when(pl.program_id(0) == 0)
        def _():
            dg_ref[...] = jnp.zeros_like(dg_ref)
            dgate_ref[...] = jnp.zeros_like(dgate_ref)

        dg_ref[...] += dg
        dgate_ref[...] += dgate

    vec = jax.ShapeDtypeStruct((1, D), F32)
    return pl.pallas_call(
        body, name=name, grid=(S // tr,), in_specs=[_row_spec(tr, D)] + [_vec_spec(D)] * 2 + [_row_spec(tr, D)],
        out_specs=[_row_spec(tr, D)] + [_vec_spec(D)] * 2, out_shape=[jax.ShapeDtypeStruct((S, D), BF16), vec, vec],
        compiler_params=_cp(("arbitrary",)),
    )(y, g, gate, dxn)


def _swiglu(g, u):
    return jax.nn.silu(g) * u


def act_fwd(g, u, name):
    S, F = g.shape
    tr, tc = _tile(S, 512, 16), _tile(F, 1024)
    spec = pl.BlockSpec((tr, tc), lambda i, j: (i, j))

    def body(g_ref, u_ref, a_ref):
        a_ref[...] = _swiglu(g_ref[...].astype(F32), u_ref[...].astype(F32)).astype(a_ref.dtype)

    return pl.pallas_call(
        body, name=name, grid=(S // tr, F // tc), in_specs=[spec, spec], out_specs=spec,
        out_shape=jax.ShapeDtypeStruct((S, F), BF16), compiler_params=_cp(("parallel", "parallel")),
    )(g, u)


def act_bwd(g, u, da, name):
    S, F = g.shape
    tr, tc = _tile(S, 512, 16), _tile(F, 1024)
    spec = pl.BlockSpec((tr, tc), lambda i, j: (i, j))

    def body(g_ref, u_ref, da_ref, dg_ref, du_ref):
        _, vjp = jax.vjp(_swiglu, g_ref[...].astype(F32), u_ref[...].astype(F32))
        dg, du = vjp(da_ref[...].astype(F32))
        dg_ref[...] = dg.astype(dg_ref.dtype)
        du_ref[...] = du.astype(du_ref.dtype)

    out = jax.ShapeDtypeStruct((S, F), BF16)
    return pl.pallas_call(
        body, name=name, grid=(S // tr, F // tc), in_specs=[spec] * 3, out_specs=[spec, spec], out_shape=[out, out],
        compiler_params=_cp(("parallel", "parallel")),
    )(g, u, da)


def _tri(upper):
    r = lax.broadcasted_iota(jnp.int32, (LANES, LANES), 0)
    c = lax.broadcasted_iota(jnp.int32, (LANES, LANES), 1)
    return ((r <= c) if upper else (r >= c)).astype(F32)


def _hdot(a, b):
    return jnp.dot(a, b, precision=lax.Precision.HIGHEST, preferred_element_type=F32)


def fox_gate_fwd(fgT, b, name):
    H, S = fgT.shape
    spec = pl.BlockSpec((H, LANES), lambda ch: (0, ch))

    def body(fg_ref, b_ref, cum_ref, carry_ref):
        @pl.when(pl.program_id(0) == 0)
        def _():
            carry_ref[...] = jnp.zeros_like(carry_ref)

        lf = jax.nn.log_sigmoid(fg_ref[...] + b_ref[...])
        cum_ref[...] = _hdot(lf, _tri(True)) + carry_ref[...]
        carry_ref[...] += _hdot(lf, jnp.ones((LANES, LANES), F32))

    return pl.pallas_call(
        body, name=name, grid=(S // LANES,), in_specs=[spec, pl.BlockSpec((H, 1), lambda ch: (0, 0))], out_specs=spec,
        out_shape=jax.ShapeDtypeStruct((H, S), F32), scratch_shapes=[pltpu.VMEM((H, LANES), F32)], compiler_params=_cp(("arbitrary",)),
    )(fgT, b)


def fox_gate_bwd(dcum, fgT, b, name):
    H, S = fgT.shape
    nch = S // LANES
    spec = pl.BlockSpec((H, LANES), lambda t: (0, nch - 1 - t))

    def body(dcum_ref, fg_ref, b_ref, dfg_ref, db_ref, tail_ref):
        @pl.when(pl.program_id(0) == 0)
        def _():
            tail_ref[...] = jnp.zeros_like(tail_ref)
            db_ref[...] = jnp.zeros_like(db_ref)

        dlf = _hdot(dcum_ref[...], _tri(False)) + tail_ref[...]
        dfg = dlf * jax.nn.sigmoid(-(fg_ref[...] + b_ref[...]))
        dfg_ref[...] = dfg
        ones = jnp.ones((LANES, LANES), F32)
        tail_ref[...] += _hdot(dcum_ref[...], ones)
        db_ref[...] += _hdot(dfg, ones)

    return pl.pallas_call(
        body, name=name, grid=(nch,), in_specs=[spec, spec, pl.BlockSpec((H, 1), lambda t: (0, 0))],
        out_specs=[spec, pl.BlockSpec((H, LANES), lambda t: (0, 0))],
        out_shape=[jax.ShapeDtypeStruct((H, S), F32), jax.ShapeDtypeStruct((H, LANES), F32)],
        scratch_shapes=[pltpu.VMEM((H, LANES), F32)], compiler_params=_cp(("arbitrary",)),
    )(dcum, fgT, b)


FOX_TILE = 512


def _causal_pairs(n, by_key):
    pairs = [(i, j) for i in range(n) for j in range(i + 1)]
    if by_key:
        pairs.sort(key=lambda p: (p[1], p[0]))
    qi = np.asarray([p[0] for p in pairs], np.int32)
    kj = np.asarray([p[1] for p in pairs], np.int32)
    return qi, kj


def _fox_scores(q, k, fq, fk, i, j, T, scale, transposed):
    if transposed:
        s = lax.dot_general(k, q, _DIMS["nt"], preferred_element_type=F32) * scale + fq - fk
        kpos = j * T + lax.broadcasted_iota(jnp.int32, (T, T), 0)
        qpos = i * T + lax.broadcasted_iota(jnp.int32, (T, T), 1)
    else:
        s = lax.dot_general(q, k, _DIMS["nt"], preferred_element_type=F32) * scale + fq - fk
        qpos = i * T + lax.broadcasted_iota(jnp.int32, (T, T), 0)
        kpos = j * T + lax.broadcasted_iota(jnp.int32, (T, T), 1)
    return s, kpos <= qpos


def fox_attn_fwd(qkv, cum_col, cum_row, H, name):
    S = qkv.shape[0]
    Dh = qkv.shape[1] // (3 * H)
    T = _tile(S, FOX_TILE)
    n = S // T
    qi, kj = _causal_pairs(n, by_key=False)
    scale = Dh ** -0.5

    def body(qi_ref, kj_ref, q_ref, k_ref, v_ref, fq_ref, fk_ref, o_ref, lse_ref, m_ref, l_ref, acc_ref):
        p_id = pl.program_id(1)
        i, j = qi_ref[p_id], kj_ref[p_id]

        @pl.when(j == 0)
        def _():
            m_ref[...] = jnp.full_like(m_ref, NEG)
            l_ref[...] = jnp.zeros_like(l_ref)
            acc_ref[...] = jnp.zeros_like(acc_ref)

        s, mask = _fox_scores(q_ref[...], k_ref[...], fq_ref[0], fk_ref[0], i, j, T, scale, False)
        s = jnp.where(mask, s, NEG)
        m_new = jnp.maximum(m_ref[...], jnp.max(s, axis=1, keepdims=True))
        alpha = jnp.exp(m_ref[...] - m_new)
        p = jnp.exp(s - m_new)
        l_ref[...] = alpha * l_ref[...] + jnp.sum(p, axis=1, keepdims=True)
        acc_ref[...] = alpha * acc_ref[...] + jnp.dot(p.astype(BF16), v_ref[...], preferred_element_type=F32)
        m_ref[...] = m_new

        @pl.when(j == i)
        def _():
            o_ref[...] = (acc_ref[...] / l_ref[...]).astype(o_ref.dtype)
            lse_ref[0] = m_ref[...] + jnp.log(l_ref[...])

    grid_spec = pltpu.PrefetchScalarGridSpec(
        num_scalar_prefetch=2, grid=(H, len(qi)),
        in_specs=[
            pl.BlockSpec((T, Dh), lambda h, p, qi, kj: (qi[p], h)),
            pl.BlockSpec((T, Dh), lambda h, p, qi, kj: (kj[p], H + h)),
            pl.BlockSpec((T, Dh), lambda h, p, qi, kj: (kj[p], 2 * H + h)),
            pl.BlockSpec((1, T, 1), lambda h, p, qi, kj: (h, qi[p], 0)),
            pl.BlockSpec((1, 1, T), lambda h, p, qi, kj: (h, 0, kj[p])),
        ],
        out_specs=[
            pl.BlockSpec((T, Dh), lambda h, p, qi, kj: (qi[p], h)),
            pl.BlockSpec((1, T, 1), lambda h, p, qi, kj: (h, qi[p], 0)),
        ],
        scratch_shapes=[pltpu.VMEM((T, 1), F32), pltpu.VMEM((T, 1), F32), pltpu.VMEM((T, Dh), F32)],
    )
    return pl.pallas_call(
        body, name=name, grid_spec=grid_spec,
        out_shape=[jax.ShapeDtypeStruct((S, H * Dh), F32), jax.ShapeDtypeStruct((H, S, 1), F32)],
        compiler_params=_cp(("parallel", "arbitrary")),
    )(jnp.asarray(qi), jnp.asarray(kj), qkv, qkv, qkv, cum_col, cum_row)


def fox_attn_bwd_dq(qkv, do, o, lse, cum_col, cum_row, H, name):
    S = qkv.shape[0]
    Dh = qkv.shape[1] // (3 * H)
    T = _tile(S, FOX_TILE)
    n = S // T
    qi, kj = _causal_pairs(n, by_key=False)
    scale = Dh ** -0.5

    def body(qi_ref, kj_ref, q_ref, k_ref, v_ref, do_ref, o_ref, lse_ref, fq_ref, fk_ref, dq_ref, delta_ref, acc_ref, dl_ref,
             rs_ref):
        p_id = pl.program_id(1)
        i, j = qi_ref[p_id], kj_ref[p_id]

        @pl.when(j == 0)
        def _():
            acc_ref[...] = jnp.zeros_like(acc_ref)
            rs_ref[...] = jnp.zeros_like(rs_ref)
            dl_ref[...] = jnp.sum(do_ref[...].astype(F32) * o_ref[...].astype(F32), axis=1, keepdims=True)

        s, mask = _fox_scores(q_ref[...], k_ref[...], fq_ref[0], fk_ref[0], i, j, T, scale, False)
        p = jnp.where(mask, jnp.exp(s - lse_ref[0]), 0.0)
        dp = lax.dot_general(do_ref[...], v_ref[...], _DIMS["nt"], preferred_element_type=F32)
        ds = p * (dp - dl_ref[...])
        rs_ref[...] += jnp.sum(ds, axis=1, keepdims=True)
        acc_ref[...] += jnp.dot(ds.astype(BF16), k_ref[...], preferred_element_type=F32)

        @pl.when(j == i)
        def _():
            dq_ref[...] = (acc_ref[...] * scale).astype(dq_ref.dtype)
            delta_ref[0] = dl_ref[...] + rs_ref[...]

    qspec = pl.BlockSpec((T, Dh), lambda h, p, qi, kj: (qi[p], h))
    colspec = pl.BlockSpec((1, T, 1), lambda h, p, qi, kj: (h, qi[p], 0))
    grid_spec = pltpu.PrefetchScalarGridSpec(
        num_scalar_prefetch=2, grid=(H, len(qi)),
        in_specs=[
            qspec,
            pl.BlockSpec((T, Dh), lambda h, p, qi, kj: (kj[p], H + h)),
            pl.BlockSpec((T, Dh), lambda h, p, qi, kj: (kj[p], 2 * H + h)),
            qspec, qspec, colspec, colspec,
            pl.BlockSpec((1, 1, T), lambda h, p, qi, kj: (h, 0, kj[p])),
        ],
        out_specs=[qspec, colspec],
        scratch_shapes=[pltpu.VMEM((T, Dh), F32), pltpu.VMEM((T, 1), F32), pltpu.VMEM((T, 1), F32)],
    )
    return pl.pallas_call(
        body, name=name, grid_spec=grid_spec,
        out_shape=[jax.ShapeDtypeStruct((S, H * Dh), BF16), jax.ShapeDtypeStruct((H, S, 1), F32)],
        compiler_params=_cp(("parallel", "arbitrary")),
    )(jnp.asarray(qi), jnp.asarray(kj), qkv, qkv, qkv, do, o, lse, cum_col, cum_row)


def fox_attn_bwd_dkv(qkv, do, lse_row, delta_row, cum_col, cum_row, H, name):
    S = qkv.shape[0]
    Dh = qkv.shape[1] // (3 * H)
    T = _tile(S, FOX_TILE)
    n = S // T
    qi, kj = _causal_pairs(n, by_key=True)
    scale = Dh ** -0.5

    def body(qi_ref, kj_ref, q_ref, k_ref, v_ref, do_ref, lse_ref, dl_ref, fq_ref, fk_ref, dk_ref, dv_ref, dcum_ref,
             dk_acc, dv_acc, df_acc):
        p_id = pl.program_id(1)
        i, j = qi_ref[p_id], kj_ref[p_id]

        @pl.when(i == j)
        def _():
            dk_acc[...] = jnp.zeros_like(dk_acc)
            dv_acc[...] = jnp.zeros_like(dv_acc)
            df_acc[...] = jnp.zeros_like(df_acc)

        sT, mask = _fox_scores(q_ref[...], k_ref[...], fq_ref[0], fk_ref[0], i, j, T, scale, True)
        pT = jnp.where(mask, jnp.exp(sT - lse_ref[0]), 0.0)
        dv_acc[...] += jnp.dot(pT.astype(BF16), do_ref[...], preferred_element_type=F32)
        dpT = lax.dot_general(v_ref[...], do_ref[...], _DIMS["nt"], preferred_element_type=F32)
        dsT = pT * (dpT - dl_ref[0])
        dk_acc[...] += jnp.dot(dsT.astype(BF16), q_ref[...], preferred_element_type=F32)
        df_acc[...] -= jnp.sum(dsT, axis=1, keepdims=True)

        @pl.when(i == n - 1)
        def _():
            dk_ref[...] = (dk_acc[...] * scale).astype(dk_ref.dtype)
            dv_ref[...] = dv_acc[...].astype(dv_ref.dtype)
            dcum_ref[0] = df_acc[...]

    qspec = pl.BlockSpec((T, Dh), lambda h, p, qi, kj: (qi[p], h))
    kspec = pl.BlockSpec((T, Dh), lambda h, p, qi, kj: (kj[p], H + h))
    vspec = pl.BlockSpec((T, Dh), lambda h, p, qi, kj: (kj[p], 2 * H + h))
    qrow = pl.BlockSpec((1, 1, T), lambda h, p, qi, kj: (h, 0, qi[p]))
    kcol = pl.BlockSpec((1, T, 1), lambda h, p, qi, kj: (h, kj[p], 0))
    grid_spec = pltpu.PrefetchScalarGridSpec(
        num_scalar_prefetch=2, grid=(H, len(qi)),
        in_specs=[qspec, kspec, vspec, qspec, qrow, qrow, qrow, kcol],
        out_specs=[pl.BlockSpec((T, Dh), lambda h, p, qi, kj: (kj[p], h))] * 2 + [kcol],
        scratch_shapes=[pltpu.VMEM((T, Dh), F32), pltpu.VMEM((T, Dh), F32), pltpu.VMEM((T, 1), F32)],
    )
    out = jax.ShapeDtypeStruct((S, H * Dh), BF16)
    return pl.pallas_call(
        body, name=name, grid_spec=grid_spec, out_shape=[out, out, jax.ShapeDtypeStruct((H, S, 1), F32)],
        compiler_params=_cp(("parallel", "arbitrary")),
    )(jnp.asarray(qi), jnp.asarray(kj), qkv, qkv, qkv, do, lse_row, delta_row, cum_row, cum_col)


def _sgu_ln(zu, zv, ln_g, ln_b):
    u = jax.nn.gelu(zu)
    v = jax.nn.gelu(zv)
    mu = jnp.mean(v, axis=-1, keepdims=True)
    var = jnp.mean(jnp.square(v - mu), axis=-1, keepdims=True)
    return u, (v - mu) * lax.rsqrt(var + EPS) * ln_g + ln_b


def _tril_mask():
    r = lax.broadcasted_iota(jnp.int32, (SEQ_BLOCK, SEQ_BLOCK), 0)
    c = lax.broadcasted_iota(jnp.int32, (SEQ_BLOCK, SEQ_BLOCK), 1)
    return r >= c


def _sgu_spatial(ws_ref, bsT, selT, vn, G):
    tril = _tril_mask()
    fs = []
    for g in range(G):
        wg = jnp.where(tril, ws_ref[g], 0.0).astype(BF16)
        fs.append(jnp.dot(wg, vn[:, g * SEQ_BLOCK:(g + 1) * SEQ_BLOCK].astype(BF16), preferred_element_type=F32))
    bias = jnp.dot(bsT, selT, precision=lax.Precision.HIGHEST, preferred_element_type=F32)
    return jnp.concatenate(fs, axis=1) + bias


def _sgu_specs(W, G):
    return [
        pl.BlockSpec((SEQ_BLOCK, 2 * W), lambda n: (n, 0)),
        pl.BlockSpec((1, W), lambda n: (0, 0)),
        pl.BlockSpec((1, W), lambda n: (0, 0)),
        pl.BlockSpec((G, SEQ_BLOCK, SEQ_BLOCK), lambda n: (0, 0, 0)),
        pl.BlockSpec((SEQ_BLOCK, G), lambda n: (0, 0)),
        pl.BlockSpec((G, W), lambda n: (0, 0)),
    ]


def sgu_fwd(zp, ln_g, ln_b, ws, bsT, selT, name):
    S, W2 = zp.shape
    W = W2 // 2
    G = ws.shape[0]

    def body(z_ref, lg_ref, lb_ref, ws_ref, bs_ref, sel_ref, o_ref):
        u, vn = _sgu_ln(z_ref[:, :W], z_ref[:, W:], lg_ref[...], lb_ref[...])
        o_ref[...] = (u * _sgu_spatial(ws_ref, bs_ref[...], sel_ref[...], vn, G)).astype(o_ref.dtype)

    return pl.pallas_call(
        body, name=name, grid=(S // SEQ_BLOCK,), in_specs=_sgu_specs(W, G), out_specs=pl.BlockSpec((SEQ_BLOCK, W), lambda n: (n, 0)),
        out_shape=jax.ShapeDtypeStruct((S, W), BF16), compiler_params=_cp(("parallel",)),
    )(zp, ln_g, ln_b, ws, bsT, selT)


def sgu_bwd(zp, ln_g, ln_b, ws, bsT, selT, dgated, name):
    S, W2 = zp.shape
    W = W2 // 2
    G = ws.shape[0]

    def body(z_ref, lg_ref, lb_ref, ws_ref, bs_ref, sel_ref, dgt_ref, dz_ref, dlg_ref, dlb_ref, dws_ref, dbs_ref):
        (u, vn), vjp = jax.vjp(_sgu_ln, z_ref[:, :W], z_ref[:, W:], lg_ref[...], lb_ref[...])
        f = _sgu_spatial(ws_ref, bs_ref[...], sel_ref[...], vn, G)
        dgt = dgt_ref[...].astype(F32)
        du, df = dgt * f, dgt * u

        @pl.when(pl.program_id(0) == 0)
        def _():
            dlg_ref[...] = jnp.zeros_like(dlg_ref)
            dlb_ref[...] = jnp.zeros_like(dlb_ref)
            dws_ref[...] = jnp.zeros_like(dws_ref)
            dbs_ref[...] = jnp.zeros_like(dbs_ref)

        dbs_ref[...] += lax.dot_general(df, sel_ref[...], _DIMS["nt"], precision=lax.Precision.HIGHEST, preferred_element_type=F32)
        tril = _tril_mask()
        dvn = []
        for g in range(G):
            sl = slice(g * SEQ_BLOCK, (g + 1) * SEQ_BLOCK)
            wg = jnp.where(tril, ws_ref[g], 0.0).astype(BF16)
            df_g = df[:, sl].astype(BF16)
            dw = lax.dot_general(df_g, vn[:, sl].astype(BF16), _DIMS["nt"], preferred_element_type=F32)
            dws_ref[g] += jnp.where(tril, dw, 0.0)
            dvn.append(lax.dot_general(wg, df_g, _DIMS["tn"], preferred_element_type=F32))
        dzu, dzv, dlg, dlb = vjp((du, jnp.concatenate(dvn, axis=1)))
        dz_ref[:, :W] = dzu.astype(dz_ref.dtype)
        dz_ref[:, W:] = dzv.astype(dz_ref.dtype)
        dlg_ref[...] += dlg
        dlb_ref[...] += dlb

    vec = jax.ShapeDtypeStruct((1, W), F32)
    return pl.pallas_call(
        body, name=name, grid=(S // SEQ_BLOCK,),
        in_specs=_sgu_specs(W, G) + [pl.BlockSpec((SEQ_BLOCK, W), lambda n: (n, 0))],
        out_specs=[
            pl.BlockSpec((SEQ_BLOCK, 2 * W), lambda n: (n, 0)),
            pl.BlockSpec((1, W), lambda n: (0, 0)),
            pl.BlockSpec((1, W), lambda n: (0, 0)),
            pl.BlockSpec((G, SEQ_BLOCK, SEQ_BLOCK), lambda n: (0, 0, 0)),
            pl.BlockSpec((SEQ_BLOCK, G), lambda n: (0, 0)),
        ],
        out_shape=[jax.ShapeDtypeStruct((S, W2), BF16), vec, vec, jax.ShapeDtypeStruct(ws.shape, F32), jax.ShapeDtypeStruct((SEQ_BLOCK, G), F32)],
        compiler_params=_cp(("arbitrary",)),
    )(zp, ln_g, ln_b, ws, bsT, selT, dgated)


def _rope_matrix():
    half = ROPE_DIM // 2
    R = np.zeros((SWA_HEAD_DIM, SWA_HEAD_DIM), np.float32)
    for j in range(half):
        R[j + half, j] = -1.0
        R[j, j + half] = 1.0
    return R


def _swa_block(q4, kp, kc, vp, vc, sink, Cq, Sq, Cp, Sp, R, n, G):
    B, Dh = SEQ_BLOCK, SWA_HEAD_DIM
    rot = lambda t: jnp.dot(t, R, precision=lax.Precision.HIGHEST, preferred_element_type=F32)
    q = q4.reshape(G * B, Dh)
    Cq4 = jnp.concatenate([Cq] * G, axis=0)
    Sq4 = jnp.concatenate([Sq] * G, axis=0)
    qr = q * Cq4 + rot(q) * Sq4
    kb = jnp.concatenate([kp * Cp + rot(kp) * Sp, kc * Cq + rot(kc) * Sq], axis=0)
    vb = jnp.concatenate([vp, vc], axis=0)
    s = lax.dot_general(qr.astype(BF16), kb.astype(BF16), _DIMS["nt"], preferred_element_type=F32) * (Dh ** -0.5)
    qi = lax.broadcasted_iota(jnp.int32, (G * B, 2 * B), 0) & (B - 1)
    ki = lax.broadcasted_iota(jnp.int32, (G * B, 2 * B), 1) - B
    rel = qi - ki
    valid = (rel >= 0) & (rel < B) & (n * B + ki >= 0)
    s = jnp.where(valid, s, NEG)
    m = lax.stop_gradient(jnp.maximum(jnp.max(s, axis=1, keepdims=True), sink))
    p = jnp.exp(s - m)
    p = p / (jnp.sum(p, axis=1, keepdims=True) + jnp.exp(sink - m))
    o = jnp.dot(p.astype(BF16), vb.astype(BF16), preferred_element_type=F32)
    return o.reshape(G, B, Dh)


def _swa_specs(G):
    B, Dh = SEQ_BLOCK, SWA_HEAD_DIM
    prev = lambda n: jnp.maximum(n - 1, 0)
    return [
        pl.BlockSpec((G, B, Dh), lambda h, n: (h, n, 0)),
        pl.BlockSpec((1, B, Dh), lambda h, n: (h, prev(n), 0)),
        pl.BlockSpec((1, B, Dh), lambda h, n: (h, n, 0)),
        pl.BlockSpec((1, B, Dh), lambda h, n: (h, prev(n), 0)),
        pl.BlockSpec((1, B, Dh), lambda h, n: (h, n, 0)),
        pl.BlockSpec((1, G * B, 1), lambda h, n: (h, 0, 0)),
        pl.BlockSpec((B, Dh), lambda h, n: (n, 0)),
        pl.BlockSpec((B, Dh), lambda h, n: (n, 0)),
        pl.BlockSpec((B, Dh), lambda h, n: (prev(n), 0)),
        pl.BlockSpec((B, Dh), lambda h, n: (prev(n), 0)),
        pl.BlockSpec((Dh, Dh), lambda h, n: (0, 0)),
    ]


def swa_fwd(qh, kh, vh, sink_col, C, Sn, R, name):
    Hq, S, Dh = qh.shape
    Hk = kh.shape[0]
    G = Hq // Hk

    def body(q_ref, kp_ref, kc_ref, vp_ref, vc_ref, sk_ref, cq_ref, sq_ref, cp_ref, sp_ref, r_ref, o_ref):
        o = _swa_block(q_ref[...], kp_ref[0], kc_ref[0], vp_ref[0], vc_ref[0], sk_ref[0], cq_ref[...], sq_ref[...], cp_ref[...],
                       sp_ref[...], r_ref[...], pl.program_id(1), G)
        o_ref[...] = o.astype(o_ref.dtype)

    return pl.pallas_call(
        body, name=name, grid=(Hk, S // SEQ_BLOCK), in_specs=_swa_specs(G),
        out_specs=pl.BlockSpec((G, SEQ_BLOCK, Dh), lambda h, n: (h, n, 0)),
        out_shape=jax.ShapeDtypeStruct((Hq, S, Dh), BF16), compiler_params=_cp(("parallel", "parallel")),
    )(qh, kh, kh, vh, vh, sink_col, C, Sn, C, Sn, R)


def swa_bwd(qh, kh, vh, sink_col, C, Sn, R, doh, name):
    Hq, S, Dh = qh.shape
    Hk = kh.shape[0]
    G = Hq // Hk
    B = SEQ_BLOCK

    def body(q_ref, kp_ref, kc_ref, vp_ref, vc_ref, sk_ref, cq_ref, sq_ref, cp_ref, sp_ref, r_ref, do_ref,
             dq_ref, dkp_ref, dkc_ref, dvp_ref, dvc_ref, dsk_ref):
        n = pl.program_id(1)
        fn = lambda q4, kp, kc, vp, vc, sk: _swa_block(q4, kp, kc, vp, vc, sk, cq_ref[...], sq_ref[...], cp_ref[...], sp_ref[...],
                                                      r_ref[...], n, G)
        _, vjp = jax.vjp(fn, q_ref[...], kp_ref[0], kc_ref[0], vp_ref[0], vc_ref[0], sk_ref[0])
        dq, dkp, dkc, dvp, dvc, dsk = vjp(do_ref[...].astype(F32))
        dq_ref[...] = dq
        dkp_ref[0] = dkp
        dkc_ref[0] = dkc
        dvp_ref[0] = dvp
        dvc_ref[0] = dvc

        @pl.when(n == 0)
        def _():
            dsk_ref[...] = jnp.zeros_like(dsk_ref)

        for g in range(G):
            part = jnp.sum(dsk[g * B:(g + 1) * B], axis=0, keepdims=True)
            dsk_ref[0, g:g + 1, :] += jnp.broadcast_to(part, (1, LANES))

    qspec = pl.BlockSpec((G, B, Dh), lambda h, n: (h, n, 0))
    kspec = pl.BlockSpec((1, B, Dh), lambda h, n: (h, n, 0))
    kshape = jax.ShapeDtypeStruct((Hk, S, Dh), F32)
    return pl.pallas_call(
        body, name=name, grid=(Hk, S // B), in_specs=_swa_specs(G) + [qspec],
        out_specs=[qspec, kspec, kspec, kspec, kspec, pl.BlockSpec((1, G, LANES), lambda h, n: (h, 0, 0))],
        out_shape=[jax.ShapeDtypeStruct((Hq, S, Dh), F32), kshape, kshape, kshape, kshape, jax.ShapeDtypeStruct((Hk, G, LANES), F32)],
        compiler_params=_cp(("parallel", "arbitrary")),
    )(qh, kh, kh, vh, vh, sink_col, C, Sn, C, Sn, R, doh)


def shift_add(cur, prev, name):
    Hk, S, Dh = cur.shape
    B = SEQ_BLOCK
    nb = S // B

    def body(c_ref, p_ref, o_ref):
        last = pl.program_id(1) == nb - 1
        o_ref[...] = c_ref[...] + jnp.where(last, 0.0, p_ref[...])

    spec = pl.BlockSpec((1, B, Dh), lambda h, n: (h, n, 0))
    nxt = pl.BlockSpec((1, B, Dh), lambda h, n: (h, jnp.minimum(n + 1, nb - 1), 0))
    return pl.pallas_call(
        body, name=name, grid=(Hk, nb), in_specs=[spec, nxt], out_specs=spec, out_shape=jax.ShapeDtypeStruct(cur.shape, F32),
        compiler_params=_cp(("parallel", "parallel")),
    )(cur, prev)


def loss_head(y, target, name):
    S, D = y.shape
    tr = _tile(S, ROW_TILE, 16)

    def body(y_ref, t_ref, acc_ref, dy_ref):
        err = y_ref[...] - t_ref[...]
        dy_ref[...] = err * (1.0 / D)

        @pl.when(pl.program_id(0) == 0)
        def _():
            acc_ref[...] = jnp.zeros_like(acc_ref)

        acc_ref[...] += jnp.broadcast_to(jnp.sum(err * err).reshape(1, 1), (1, LANES))

    return pl.pallas_call(
        body, name=name, grid=(S // tr,), in_specs=[_row_spec(tr, D)] * 2,
        out_specs=[pl.BlockSpec((1, LANES), lambda i: (0, 0)), _row_spec(tr, D)],
        out_shape=[jax.ShapeDtypeStruct((1, LANES), F32), jax.ShapeDtypeStruct((S, D), F32)], compiler_params=_cp(("arbitrary",)),
    )(y, target)


def _adam_update(w, g, m, v):
    m = ADAM_B1 * m + (1.0 - ADAM_B1) * g
    v = ADAM_B2 * v + (1.0 - ADAM_B2) * jnp.square(g)
    m_hat = m / (1.0 - ADAM_B1 ** ADAM_STEP)
    v_hat = v / (1.0 - ADAM_B2 ** ADAM_STEP)
    delta = -ADAM_LR * (m_hat / (jnp.sqrt(v_hat) + ADAM_EPS) + ADAM_WD * w)
    return delta, m, v


def adamw(w, m, v, gparts, name, gstack=0):
    R, C = w.shape
    tr = _tile(R, max(8, (128 * 1024) // C), 8)
    spec = pl.BlockSpec((tr, C), lambda i: (i, 0))
    nplain = len(gparts) - (1 if gstack else 0)

    def body(w_ref, m_ref, v_ref, *rest):
        g_refs, (g_ref, d_ref, mo_ref, vo_ref) = rest[:len(gparts)], rest[len(gparts):]
        g = None
        for r in g_refs[:nplain]:
            g = r[...].astype(F32) if g is None else g + r[...].astype(F32)
        if gstack:
            for t in range(gstack):
                part = g_refs[-1][t].astype(F32)
                g = part if g is None else g + part
        d, mn, vn = _adam_update(w_ref[...], g, m_ref[...], v_ref[...])
        g_ref[...] = g
        d_ref[...] = d
        mo_ref[...] = mn
        vo_ref[...] = vn

    gspecs = [spec] * nplain + ([pl.BlockSpec((gstack, tr, C), lambda i: (0, i, 0))] if gstack else [])
    out = jax.ShapeDtypeStruct((R, C), F32)
    return pl.pallas_call(
        body, name=name, grid=(R // tr,), in_specs=[spec] * 3 + gspecs, out_specs=[spec] * 4, out_shape=[out] * 4,
        compiler_params=_cp(("parallel",)),
    )(w, m, v, *gparts)


def ada_fwd(c_all, ada_w, ada_b, name):
    L, D, N = ada_w.shape
    Bp = c_all.shape[0]
    tn = _tile(N, 512)

    def body(c_ref, w_ref, b_ref, o_ref):
        ca = jax.nn.silu(c_ref[...]).astype(BF16)
        o_ref[0] = jnp.dot(ca, w_ref[0].astype(BF16), preferred_element_type=F32) + b_ref[0]

    return pl.pallas_call(
        body, name=name, grid=(L, N // tn),
        in_specs=[pl.BlockSpec((Bp, D), lambda l, j: (0, 0)), pl.BlockSpec((1, D, tn), lambda l, j: (l, 0, j)),
                  pl.BlockSpec((1, 1, tn), lambda l, j: (l, 0, j))],
        out_specs=pl.BlockSpec((1, Bp, tn), lambda l, j: (l, 0, j)), out_shape=jax.ShapeDtypeStruct((L, Bp, N), F32),
        compiler_params=_cp(("parallel", "parallel")),
    )(c_all, ada_w, ada_b)


def ada_wgrad(c_all, dmod, name):
    L, Bp, N = dmod.shape
    D = c_all.shape[1]
    tn = _tile(N, 512)

    def body(c_ref, d_ref, o_ref):
        ca = jax.nn.silu(c_ref[...]).astype(BF16)
        o_ref[0] = lax.dot_general(ca, d_ref[0].astype(BF16), _DIMS["tn"], preferred_element_type=F32)

    return pl.pallas_call(
        body, name=name, grid=(L, N // tn),
        in_specs=[pl.BlockSpec((Bp, D), lambda l, j: (0, 0)), pl.BlockSpec((1, Bp, tn), lambda l, j: (l, 0, j))],
        out_specs=pl.BlockSpec((1, D, tn), lambda l, j: (l, 0, j)), out_shape=jax.ShapeDtypeStruct((L, D, N), F32),
        compiler_params=_cp(("parallel", "parallel")),
    )(c_all, dmod)


N_DEV = 8
N_CHIP = 4
ANY = pl.BlockSpec(memory_space=pl.ANY)


def _place():
    return lax.axis_index("x"), lax.axis_index("y"), lax.axis_index("c")


def _other_chips(x, y):
    chips = [(1 - x, y), (x, 1 - y), (1 - x, 1 - y)]
    return chips, [2 * cx + cy for cx, cy in chips]


def _rcopy(src, dst, ssem, rsem, to):
    return pltpu.make_async_remote_copy(src_ref=src, dst_ref=dst, send_sem=ssem, recv_sem=rsem, device_id=to, device_id_type=MESH)


def ag_small(xs, name):
    R, Wd = xs.shape

    def body(x_ref, out_ref, send_sems, recv_sems, local_sem):
        x, y, c = _place()
        me, sibling = (x, y, c), (x, y, 1 - c)
        chips, _ = _other_chips(x, y)

        def slot(px, py, pc):
            return out_ref.at[4 * px + 2 * py + pc]

        def copy(k, block, to, src=None):
            return _rcopy(slot(*block) if src is None else src, slot(*block), send_sems.at[k], recv_sems.at[k], to)

        mine = pltpu.make_async_copy(x_ref, slot(*me), local_sem)
        mine.start()
        first = [copy(0, me, sibling, src=x_ref)]
        first += [copy(1 + j, me, (*chip, c), src=x_ref) for j, chip in enumerate(chips)]
        for cp in first:
            cp.start()
        passed = [copy(4 + j, (*chip, c), sibling) for j, chip in enumerate(chips)]
        for j, chip in enumerate(chips):
            copy(1 + j, (*chip, c), me).wait_recv()
            passed[j].start()
        copy(0, sibling, me).wait_recv()
        for j, chip in enumerate(chips):
            copy(4 + j, (*chip, 1 - c), me).wait_recv()
        for cp in first + passed:
            cp.wait_send()
        mine.wait()

    vm = pl.BlockSpec(memory_space=pltpu.VMEM)
    return pl.pallas_call(
        body, name=name, out_shape=jax.ShapeDtypeStruct((N_DEV, R, Wd), xs.dtype), in_specs=[vm], out_specs=vm,
        scratch_shapes=[pltpu.SemaphoreType.DMA((7,)), pltpu.SemaphoreType.DMA((7,)), pltpu.SemaphoreType.DMA],
        compiler_params=_cp(),
    )(xs)


def ag_weights(flat, name, nchunk=4):
    _, Rh, Wd = flat.shape
    if Rh % (nchunk * 16):
        nchunk = 1
    rows = Rh // nchunk

    def body(x_ref, o_ref, lsem, s_ici, r_ici, s_d2d, r_d2d):
        x, y, c = _place()
        q = 2 * x + y
        sibling = (x, y, 1 - c)
        chips, qs = _other_chips(x, y)

        def rows_of(k):
            return pl.ds(k * rows, rows)

        def ici(j, k, landing_q):
            return _rcopy(x_ref.at[c, rows_of(k)], o_ref.at[landing_q, c, rows_of(k)], s_ici.at[j, k], r_ici.at[j, k], (*chips[j], c))

        def handoff(j, k, half):
            blk = o_ref.at[qs[j], half, rows_of(k)]
            return _rcopy(blk, blk, s_d2d.at[j, k], r_d2d.at[j, k], sibling)

        mine = pltpu.make_async_copy(x_ref, o_ref.at[q], lsem)
        mine.start()
        for k in range(nchunk):
            for j in range(3):
                ici(j, k, q).start()
        for k in range(nchunk):
            for j in range(3):
                ici(j, k, qs[j]).wait_recv()
                handoff(j, k, c).start()
        for k in range(nchunk):
            for j in range(3):
                handoff(j, k, 1 - c).wait_recv()
        for k in range(nchunk):
            for j in range(3):
                ici(j, k, q).wait_send()
                handoff(j, k, c).wait_send()
        mine.wait()

    dma = pltpu.SemaphoreType.DMA
    return pl.pallas_call(
        body, name=name, out_shape=jax.ShapeDtypeStruct((N_CHIP, 2, Rh, Wd), flat.dtype), in_specs=[ANY], out_specs=ANY,
        scratch_shapes=[dma, dma((3, nchunk)), dma((3, nchunk)), dma((3, nchunk)), dma((3, nchunk))], compiler_params=_cp(),
    )(flat)


def sibling_fold(g, name):
    _, _, Rh, Wd = g.shape

    def body(x_ref, o_ref, ssem, rsem):
        x, y, c = _place()
        cps = [_rcopy(x_ref.at[s, 1 - c], o_ref.at[s], ssem.at[s], rsem.at[s], (x, y, 1 - c)) for s in range(N_CHIP)]
        for cp in cps:
            cp.start()
        for cp in cps:
            cp.wait()

    dma = pltpu.SemaphoreType.DMA
    return pl.pallas_call(
        body, name=name, out_shape=jax.ShapeDtypeStruct((N_CHIP, Rh, Wd), g.dtype), in_specs=[ANY], out_specs=ANY,
        scratch_shapes=[dma((N_CHIP,)), dma((N_CHIP,))], compiler_params=_cp(),
    )(g)


def chip_exchange(r, name):
    _, Rh, Wd = r.shape

    def body(x_ref, o_ref, ssem, rsem):
        x, y, c = _place()
        chips, qs = _other_chips(x, y)
        cps = [_rcopy(x_ref.at[qs[j]], o_ref.at[j], ssem.at[j], rsem.at[j], (*chips[j], c)) for j in range(3)]
        for cp in cps:
            cp.start()
        for cp in cps:
            cp.wait()

    dma = pltpu.SemaphoreType.DMA
    return pl.pallas_call(
        body, name=name, out_shape=jax.ShapeDtypeStruct((3, Rh, Wd), r.dtype), in_specs=[ANY], out_specs=ANY,
        scratch_shapes=[dma((3,)), dma((3,))], compiler_params=_cp(),
    )(r)


def sibling_share(f, name):
    Rh, Wd = f.shape

    def body(x_ref, o_ref, lsem, ssem, rsem):
        x, y, c = _place()
        mine = pltpu.make_async_copy(x_ref, o_ref.at[c], lsem)
        mine.start()
        cp = _rcopy(x_ref, o_ref.at[c], ssem, rsem, (x, y, 1 - c))
        cp.start()
        _rcopy(x_ref, o_ref.at[1 - c], ssem, rsem, (x, y, 1 - c)).wait_recv()
        cp.wait_send()
        mine.wait()

    dma = pltpu.SemaphoreType.DMA
    return pl.pallas_call(
        body, name=name, out_shape=jax.ShapeDtypeStruct((2, Rh, Wd), f.dtype), in_specs=[ANY], out_specs=ANY,
        scratch_shapes=[dma, dma, dma], compiler_params=_cp(),
    )(f)


def fold_sum(g, recv, c_idx, name):
    _, _, Rh, Wd = g.shape
    tr = _tile(Rh, 512, 16)

    def body(c_ref, g_ref, r_ref, o_ref):
        del c_ref
        o_ref[0] = (g_ref[0, 0].astype(F32) + r_ref[0].astype(F32)).astype(o_ref.dtype)

    grid_spec = pltpu.PrefetchScalarGridSpec(
        num_scalar_prefetch=1, grid=(N_CHIP, Rh // tr),
        in_specs=[pl.BlockSpec((1, 1, tr, Wd), lambda s, i, c: (s, c[0], i, 0)), pl.BlockSpec((1, tr, Wd), lambda s, i, c: (s, i, 0))],
        out_specs=pl.BlockSpec((1, tr, Wd), lambda s, i, c: (s, i, 0)),
    )
    return pl.pallas_call(
        body, name=name, grid_spec=grid_spec, out_shape=jax.ShapeDtypeStruct((N_CHIP, Rh, Wd), BF16),
        compiler_params=_cp(("parallel", "parallel")),
    )(c_idx, g, recv)


def chip_sum(r, ex, q_idx, name):
    _, Rh, Wd = r.shape
    tr = _tile(Rh, 512, 16)

    def body(q_ref, r_ref, e_ref, o_ref):
        del q_ref
        o_ref[...] = ((r_ref[0].astype(F32) + e_ref[0].astype(F32)) + e_ref[1].astype(F32)) + e_ref[2].astype(F32)

    grid_spec = pltpu.PrefetchScalarGridSpec(
        num_scalar_prefetch=1, grid=(Rh // tr,),
        in_specs=[pl.BlockSpec((1, tr, Wd), lambda i, q: (q[0], i, 0)), pl.BlockSpec((3, tr, Wd), lambda i, q: (0, i, 0))],
        out_specs=pl.BlockSpec((tr, Wd), lambda i, q: (i, 0)),
    )
    return pl.pallas_call(
        body, name=name, grid_spec=grid_spec, out_shape=jax.ShapeDtypeStruct((Rh, Wd), F32), compiler_params=_cp(("parallel",)),
    )(q_idx, r, ex)


BIG = ("ffn_w_gu", "ffn_w_down", "fox_w_in", "fox_w_out", "sgu_w_in", "sgu_w_out", "swa_w_in", "swa_w_out")
COLUMN_SHARDED = ("ffn_w_gu", "fox_w_in", "sgu_w_in", "swa_w_in")
SMALL = ("ada_b", "mix_pre_g", "mix_post_g", "ffn_pre_g", "ffn_post_g", "fox_b_f", "sgu_ln_g", "sgu_ln_b", "sgu_w_s", "sgu_b_s",
         "swa_sinks")
WEIGHTS = ("ada_w", "ada_b", "mix_pre_g", "mix_post_g", "ffn_pre_g", "ffn_post_g", "ffn_w_gu", "ffn_w_down", "fox_w_in", "fox_b_f",
           "fox_w_out", "sgu_w_in", "sgu_ln_g", "sgu_ln_b", "sgu_w_s", "sgu_b_s", "sgu_w_out", "swa_w_in", "swa_sinks", "swa_w_out")
INPUTS = ("x", "c", "positions") + WEIGHTS + ("loss_target",) + tuple("m_" + n for n in WEIGHTS) + tuple("v_" + n for n in WEIGHTS)


def _half_rows(shape):
    L, A, B = shape
    n = L * A * B // 2
    assert n % PACK_W == 0 and (L % 2 == 0 or (L == 1 and A % 2 == 0)), shape
    return n // PACK_W


def _to_halves(shard):
    return shard.reshape(2, _half_rows(shard.shape), PACK_W)


def _shards_of(full, name):
    L, A, B = full.shape
    if name in COLUMN_SHARDED:
        return full.reshape(L, A, N_CHIP, B // N_CHIP).transpose(2, 0, 1, 3)
    return full.reshape(L, N_CHIP, A // N_CHIP, B).transpose(1, 0, 2, 3)


def _full_of(shards, name):
    _, L, A, B = shards.shape
    if name in COLUMN_SHARDED:
        return shards.transpose(1, 2, 0, 3).reshape(L, A, N_CHIP * B)
    return shards.transpose(1, 0, 2, 3).reshape(L, N_CHIP * A, B)


def _pad_rows(flat1d):
    n = flat1d.shape[0]
    pad = (-n) % (8 * LANES)
    return jnp.pad(flat1d, (0, pad)).reshape(-1, LANES)


def _pack_small(parts):
    return jnp.concatenate([_pad_rows(parts[n].astype(F32).reshape(-1)) for n in SMALL], axis=0)


def _unpack_small(packed, shapes):
    out, off = {}, 0
    for n in SMALL:
        size = int(np.prod(shapes[n]))
        rows = (size + 8 * LANES - 1) // (8 * LANES) * 8
        out[n] = packed[off:off + rows].reshape(-1)[:size].reshape(shapes[n])
        off += rows
    return out


def _fox_fwd(h, w_in, b_f, w_out, tag):
    S, D = h.shape
    H = b_f.shape[0]
    qkv = mm(h, w_in, "nn", BF16, name=tag + "_qkv", b_cols=(0, 3 * D))
    fgp = mm(h, w_in, "nn", F32, name=tag + "_fg", b_cols=(3 * D, LANES))
    fgT = fgp[:, :H].T
    cum = fox_gate_fwd(fgT, b_f.reshape(H, 1), tag + "_gate")
    cum_col, cum_row = cum.reshape(H, S, 1), cum.reshape(H, 1, S)
    o, lse = fox_attn_fwd(qkv, cum_col, cum_row, H, tag + "_attn")
    y = mm(o, w_out, "nn", F32, name=tag + "_out")
    return y, (qkv, fgT, cum_col, cum_row, o, lse)


def _fox_bwd(dy, h, w_in, b_f, w_out, ctx, tag):
    qkv, fgT, cum_col, cum_row, o, lse = ctx
    S, D = h.shape
    H = b_f.shape[0]
    do = mm(dy, w_out, "nt", BF16, name=tag + "_do")
    dw_out = mm(o, dy, "tn", F32, name=tag + "_dwout")
    dq, delta = fox_attn_bwd_dq(qkv, do, o, lse, cum_col, cum_row, H, tag + "_dq")
    dk, dv, dcum = fox_attn_bwd_dkv(qkv, do, lse.reshape(H, 1, S), delta.reshape(H, 1, S), cum_col, cum_row, H, tag + "_dkv")
    dfgT, db = fox_gate_bwd(dcum.reshape(H, S), fgT, b_f.reshape(H, 1), tag + "_dgate")
    dfgp = jnp.pad(dfgT.T, ((0, 0), (0, LANES - H))).astype(BF16)
    dh = mm(dq, w_in, "nt", F32, name=tag + "_dhq", b_cols=(0, D))
    dh = mm(dk, w_in, "nt", F32, add=dh, name=tag + "_dhk", b_cols=(D, D))
    dh = mm(dv, w_in, "nt", F32, add=dh, name=tag + "_dhv", b_cols=(2 * D, D))
    dh = mm(dfgp, w_in, "nt", F32, add=dh, name=tag + "_dhf", b_cols=(3 * D, LANES))
    dw_in = jnp.concatenate(
        [mm(h, dq, "tn", F32, name=tag + "_dwq"), mm(h, dk, "tn", F32, name=tag + "_dwk"), mm(h, dv, "tn", F32, name=tag + "_dwv"),
         mm(h, dfgp, "tn", F32, name=tag + "_dwf")[:, :H]], axis=1)
    return dh, dw_in, dw_out, db[:, 0]


def _sgu_consts(G, W):
    return jnp.asarray(np.repeat(np.eye(G, dtype=np.float32), W // G, axis=1))


def _sgu_fwd(h, w_in, ln_g, ln_b, w_s, b_s, w_out, tag):
    G, W = w_s.shape[0], ln_g.shape[0]
    zp = mm(h, w_in, "nn", F32, name=tag + "_in")
    args = (zp, ln_g.reshape(1, W), ln_b.reshape(1, W), w_s, b_s.T, _sgu_consts(G, W))
    gated = sgu_fwd(*args, tag + "_core")
    y = mm(gated, w_out, "nn", F32, name=tag + "_out")
    return y, (args, gated)


def _sgu_bwd(dy, h, w_in, w_out, ctx, tag):
    args, gated = ctx
    dgated = mm(dy, w_out, "nt", BF16, name=tag + "_dgated")
    dw_out = mm(gated, dy, "tn", F32, name=tag + "_dwout")
    dzp, dlg, dlb, dws, dbsT = sgu_bwd(*args, dgated, tag + "_dcore")
    dh = mm(dzp, w_in, "nt", F32, name=tag + "_dh")
    dw_in = mm(h, dzp, "tn", F32, name=tag + "_dwin")
    return dh, dw_in, dw_out, dlg[0], dlb[0], dws, dbsT.T


def _rope_tables(positions):
    inv = ROPE_THETA ** (-jnp.arange(0, ROPE_DIM, 2, dtype=F32) / ROPE_DIM)
    ang = positions.astype(F32)[:, None] * inv
    S = positions.shape[0]
    rest = SWA_HEAD_DIM - ROPE_DIM
    C = jnp.concatenate([jnp.cos(ang), jnp.cos(ang), jnp.ones((S, rest), F32)], axis=1)
    Sn = jnp.concatenate([jnp.sin(ang), jnp.sin(ang), jnp.zeros((S, rest), F32)], axis=1)
    return C, Sn


def _heads(t, n):
    return t.reshape(t.shape[0], n, SWA_HEAD_DIM).transpose(1, 0, 2)


def _unheads(t):
    return t.transpose(1, 0, 2).reshape(t.shape[1], -1)


def _swa_fwd(h, w_in, sinks, w_out, tables, tag):
    Hq = sinks.shape[0]
    Hk = (w_in.shape[1] // SWA_HEAD_DIM - Hq) // 2
    G = Hq // Hk
    proj = mm(h, w_in, "nn", F32, name=tag + "_in")
    qh = _heads(proj[:, :Hq * SWA_HEAD_DIM], Hq)
    kh = _heads(proj[:, Hq * SWA_HEAD_DIM:(Hq + Hk) * SWA_HEAD_DIM], Hk)
    vh = _heads(proj[:, (Hq + Hk) * SWA_HEAD_DIM:], Hk)
    sink_col = jnp.repeat(sinks.reshape(Hk, G), SEQ_BLOCK, axis=1).reshape(Hk, G * SEQ_BLOCK, 1)
    args = (qh, kh, vh, sink_col, tables[0], tables[1], jnp.asarray(_rope_matrix()))
    o = _unheads(swa_fwd(*args, tag + "_core"))
    y = mm(o, w_out, "nn", F32, name=tag + "_out")
    return y, (args, o)


def _swa_bwd(dy, h, w_in, w_out, ctx, tag):
    args, o = ctx
    Hq = args[0].shape[0]
    do = mm(dy, w_out, "nt", BF16, name=tag + "_do")
    dw_out = mm(o, dy, "tn", F32, name=tag + "_dwout")
    dqh, dkp, dkc, dvp, dvc, dsk = swa_bwd(*args, _heads(do, Hq), tag + "_dcore")
    dk = shift_add(dkc, dkp, tag + "_dk")
    dv = shift_add(dvc, dvp, tag + "_dv")
    dproj = jnp.concatenate([_unheads(dqh), _unheads(dk), _unheads(dv)], axis=1).astype(BF16)
    dh = mm(dproj, w_in, "nt", F32, name=tag + "_dh")
    dw_in = mm(h, dproj, "tn", F32, name=tag + "_dwin")
    return dh, dw_in, dw_out, dsk[:, :, 0].reshape(Hq)


def kernel(x, c, positions, ada_w, ada_b, mix_pre_g, mix_post_g, ffn_pre_g, ffn_post_g, ffn_w_gu, ffn_w_down, fox_w_in, fox_b_f, fox_w_out, sgu_w_in, sgu_ln_g, sgu_ln_b, sgu_w_s, sgu_b_s, sgu_w_out, swa_w_in, swa_sinks, swa_w_out, loss_target, m_ada_w, m_ada_b, m_mix_pre_g, m_mix_post_g, m_ffn_pre_g, m_ffn_post_g, m_ffn_w_gu, m_ffn_w_down, m_fox_w_in, m_fox_b_f, m_fox_w_out, m_sgu_w_in, m_sgu_ln_g, m_sgu_ln_b, m_sgu_w_s, m_sgu_b_s, m_sgu_w_out, m_swa_w_in, m_swa_sinks, m_swa_w_out, v_ada_w, v_ada_b, v_mix_pre_g, v_mix_post_g, v_ffn_pre_g, v_ffn_post_g, v_ffn_w_gu, v_ffn_w_down, v_fox_w_in, v_fox_b_f, v_fox_w_out, v_sgu_w_in, v_sgu_ln_g, v_sgu_ln_b, v_sgu_w_s, v_sgu_b_s, v_sgu_w_out, v_swa_w_in, v_swa_sinks, v_swa_w_out):
    P = dict(zip(INPUTS, (x, c, positions, ada_w, ada_b, mix_pre_g, mix_post_g, ffn_pre_g, ffn_post_g, ffn_w_gu, ffn_w_down, fox_w_in, fox_b_f, fox_w_out, sgu_w_in, sgu_ln_g, sgu_ln_b, sgu_w_s, sgu_b_s, sgu_w_out, swa_w_in, swa_sinks, swa_w_out, loss_target, m_ada_w, m_ada_b, m_mix_pre_g, m_mix_post_g, m_ffn_pre_g, m_ffn_post_g, m_ffn_w_gu, m_ffn_w_down, m_fox_w_in, m_fox_b_f, m_fox_w_out, m_sgu_w_in, m_sgu_ln_g, m_sgu_ln_b, m_sgu_w_s, m_sgu_b_s, m_sgu_w_out, m_swa_w_in, m_swa_sinks, m_swa_w_out, v_ada_w, v_ada_b, v_mix_pre_g, v_mix_post_g, v_ffn_pre_g, v_ffn_post_g, v_ffn_w_gu, v_ffn_w_down, v_fox_w_in, v_fox_b_f, v_fox_w_out, v_sgu_w_in, v_sgu_ln_g, v_sgu_ln_b, v_sgu_w_s, v_sgu_b_s, v_sgu_w_out, v_swa_w_in, v_swa_sinks, v_swa_w_out)))
    xs, target, pos = x[0], loss_target[0], positions[0]
    S, D = xs.shape
    L = ada_w.shape[0]
    n_mix = 3
    F = ffn_w_down.shape[1] * N_CHIP
    xi, yi, ci = _place()
    q_me = 2 * xi + yi
    dev = 4 * xi + 2 * yi + ci

    flat = jnp.concatenate([_to_halves(P[n].astype(BF16)) for n in BIG], axis=1)
    gathered = ag_weights(flat, "ag_weights")
    Wt, off = {}, 0
    for n in BIG:
        rows = _half_rows(P[n].shape)
        Wt[n] = _full_of(gathered[:, :, off:off + rows].reshape((N_CHIP,) + P[n].shape), n)
        off += rows
    fox_pad = 3 * D + LANES - Wt["fox_w_in"].shape[2]
    Wt["fox_w_in"] = jnp.pad(Wt["fox_w_in"], ((0, 0), (0, 0), (0, fox_pad)))

    c_all = ag_small(c.reshape(D // LANES, LANES), "ag_c").reshape(N_DEV, D)
    c_all = jnp.pad(c_all, ((0, 16 - N_DEV), (0, 0)))
    Nm = ada_w.shape[2]
    ada_b_mine = lax.dynamic_slice_in_dim(ada_b, q_me * Nm, Nm, axis=1).reshape(L, 1, Nm)
    modp = ada_fwd(c_all, ada_w, ada_b_mine, "ada_fwd")[:, :N_DEV]
    mod_all = ag_small(modp.reshape(-1, LANES), "ag_mod").reshape(N_DEV, L, N_DEV, Nm)
    mod_mine = lax.dynamic_index_in_dim(mod_all[0::2], dev, axis=2, keepdims=False)
    mods = mod_mine.transpose(1, 0, 2).reshape(L, 6, 1, D)

    tables = _rope_tables(pos)

    saved = []
    xc = xs
    for i in range(L):
        kind, j = i % n_mix, i // n_mix
        sh_m, sc_m, g_m, sh_f, sc_f, g_f = (mods[i, t] for t in range(6))
        t = f"l{i}"
        h1 = pre_fwd(xc, mix_pre_g[i:i + 1], sh_m, sc_m, t + "_pre_m")
        if kind == 0:
            y1, ctx = _fox_fwd(h1, Wt["fox_w_in"][j], fox_b_f[j], Wt["fox_w_out"][j], t + "_fox")
        elif kind == 1:
            y1, ctx = _sgu_fwd(h1, Wt["sgu_w_in"][j], sgu_ln_g[j], sgu_ln_b[j], sgu_w_s[j], sgu_b_s[j], Wt["sgu_w_out"][j], t + "_sgu")
        else:
            y1, ctx = _swa_fwd(h1, Wt["swa_w_in"][j], swa_sinks[j], Wt["swa_w_out"][j], tables, t + "_swa")
        xm = post_fwd(xc, y1, mix_post_g[i:i + 1], g_m, t + "_post_m")
        h2 = pre_fwd(xm, ffn_pre_g[i:i + 1], sh_f, sc_f, t + "_pre_f")
        gg = mm(h2, Wt["ffn_w_gu"][i], "nn", BF16, name=t + "_ffn_g", b_cols=(0, F))
        uu = mm(h2, Wt["ffn_w_gu"][i], "nn", BF16, name=t + "_ffn_u", b_cols=(F, F))
        a = act_fwd(gg, uu, t + "_act")
        y2 = mm(a, Wt["ffn_w_down"][i], "nn", F32, name=t + "_ffn_down")
        xn = post_fwd(xm, y2, ffn_post_g[i:i + 1], g_f, t + "_post_f")
        saved.append((xc, h1, y1, ctx, xm, h2, gg, uu, a, y2))
        xc = xn

    sq, dx = loss_head(xc, target, "loss_head")
    loss = lax.psum(sq[0, 0] * (0.5 / D), ("x", "y", "c"))

    big_g = {n: [None] * P[n].shape[0] for n in BIG}
    small_g = {n: [None] * P[n].shape[0] for n in SMALL}
    for i in reversed(range(L)):
        kind, j = i % n_mix, i // n_mix
        sh_m, sc_m, g_m, sh_f, sc_f, g_f = (mods[i, t] for t in range(6))
        xc, h1, y1, ctx, xm, h2, gg, uu, a, y2 = saved[i]
        t = f"l{i}"
        dy2, dgpost_f, dgate_f = post_bwd(y2, ffn_post_g[i:i + 1], g_f, dx, t + "_dpost_f")
        da = mm(dy2, Wt["ffn_w_down"][i], "nt", F32, name=t + "_da")
        big_g["ffn_w_down"][i] = mm(a, dy2, "tn", F32, name=t + "_dwdown")
        dgg, duu = act_bwd(gg, uu, da, t + "_dact")
        dh2 = mm(dgg, Wt["ffn_w_gu"][i], "nt", F32, name=t + "_dh2g", b_cols=(0, F))
        dh2 = mm(duu, Wt["ffn_w_gu"][i], "nt", F32, add=dh2, name=t + "_dh2u", b_cols=(F, F))
        big_g["ffn_w_gu"][i] = jnp.concatenate(
            [mm(h2, dgg, "tn", F32, name=t + "_dwg"), mm(h2, duu, "tn", F32, name=t + "_dwu")], axis=1)
        dxm, dgpre_f, dsh_f, dsc_f = pre_bwd(xm, ffn_pre_g[i:i + 1], sh_f, sc_f, dh2, dx, t + "_dpre_f")
        dy1, dgpost_m, dgate_m = post_bwd(y1, mix_post_g[i:i + 1], g_m, dxm, t + "_dpost_m")
        if kind == 0:
            dh1, dw_in, dw_out, db = _fox_bwd(dy1, h1, Wt["fox_w_in"][j], fox_b_f[j], Wt["fox_w_out"][j], ctx, t + "_fox")
            big_g["fox_w_in"][j], big_g["fox_w_out"][j], small_g["fox_b_f"][j] = dw_in, dw_out, db
        elif kind == 1:
            dh1, dw_in, dw_out, dlg, dlb, dws, dbs = _sgu_bwd(dy1, h1, Wt["sgu_w_in"][j], Wt["sgu_w_out"][j], ctx, t + "_sgu")
            big_g["sgu_w_in"][j], big_g["sgu_w_out"][j] = dw_in, dw_out
            small_g["sgu_ln_g"][j], small_g["sgu_ln_b"][j], small_g["sgu_w_s"][j], small_g["sgu_b_s"][j] = dlg, dlb, dws, dbs
        else:
            dh1, dw_in, dw_out, dsk = _swa_bwd(dy1, h1, Wt["swa_w_in"][j], Wt["swa_w_out"][j], ctx, t + "_swa")
            big_g["swa_w_in"][j], big_g["swa_w_out"][j], small_g["swa_sinks"][j] = dw_in, dw_out, dsk
        dx, dgpre_m, dsh_m, dsc_m = pre_bwd(xc, mix_pre_g[i:i + 1], sh_m, sc_m, dh1, dxm, t + "_dpre_m")
        small_g["ada_b"][i] = jnp.concatenate([dsh_m, dsc_m, dgate_m, dsh_f, dsc_f, dgate_f], axis=1)[0]
        small_g["mix_pre_g"][i], small_g["mix_post_g"][i] = dgpre_m[0], dgpost_m[0]
        small_g["ffn_pre_g"][i], small_g["ffn_post_g"][i] = dgpre_f[0], dgpost_f[0]
    grad_x = dx[None]

    shapes = {n: P[n].shape for n in SMALL}
    small_parts = ag_small(_pack_small({n: jnp.stack(small_g[n]) for n in SMALL}), "ag_small_grads")
    sg, sd, sm, sv = adamw(_pack_small({n: P[n] for n in SMALL}), _pack_small({n: P["m_" + n] for n in SMALL}),
                           _pack_small({n: P["v_" + n] for n in SMALL}), [small_parts], "adamw_small", gstack=N_DEV)
    out_g, out_d, out_m, out_v = (_unpack_small(t, shapes) for t in (sg, sd, sm, sv))

    dmod_all = small_parts[:, :L * 6 * D // LANES].reshape(N_DEV, L, 6 * D)
    dmod_mine = lax.dynamic_slice_in_dim(dmod_all, q_me * Nm, Nm, axis=2).transpose(1, 0, 2)
    dmod_mine = jnp.pad(dmod_mine, ((0, 0), (0, 16 - N_DEV), (0, 0)))
    g_ada = ada_wgrad(c_all, dmod_mine, "ada_wgrad")
    r2 = lambda t: t.reshape(-1, t.shape[-1])
    res = adamw(r2(ada_w), r2(m_ada_w), r2(v_ada_w), [r2(g_ada)], "adamw_ada_w")
    out_g["ada_w"], out_d["ada_w"], out_m["ada_w"], out_v["ada_w"] = (t.reshape(ada_w.shape) for t in res)

    gflat = jnp.concatenate(
        [_shards_of(jnp.stack(big_g[n]), n).reshape(N_CHIP, 2, _half_rows(P[n].shape), PACK_W).astype(BF16) for n in BIG], axis=2)
    from_sibling = sibling_fold(gflat, "rs_fold")
    chip_part = fold_sum(gflat, from_sibling, ci.reshape(1).astype(jnp.int32), "rs_fold_sum")
    from_chips = chip_exchange(chip_part, "rs_exchange")
    mine = chip_sum(chip_part, from_chips, q_me.reshape(1).astype(jnp.int32), "rs_chip_sum")
    gboth = sibling_share(mine, "rs_share")
    off = 0
    for n in BIG:
        rows = _half_rows(P[n].shape)
        gsh = gboth[:, off:off + rows].reshape(P[n].shape)
        off += rows
        res = adamw(r2(P[n]), r2(P["m_" + n]), r2(P["v_" + n]), [r2(gsh)], "adamw_" + n)
        out_g[n], out_d[n], out_m[n], out_v[n] = (t.reshape(P[n].shape) for t in res)

    return (loss, grad_x, *[out_g[n] for n in WEIGHTS], *[out_d[n] for n in WEIGHTS], *[out_m[n] for n in WEIGHTS],
            *[out_v[n] for n in WEIGHTS])
```

```python
import functools

import numpy as np
import jax
import jax.numpy as jnp
from jax import lax
from jax.experimental import pallas as pl
from jax.experimental.pallas import tpu as pltpu

F32 = jnp.float32
BF16 = jnp.bfloat16
MESH = pl.DeviceIdType.MESH

EPS = 1e-6
NEG = -1e30
V7X_VMEM_BYTES = 64 * 1024 * 1024
VMEM_LIMIT = V7X_VMEM_BYTES - 8 * 1024 * 1024
LANES = 128
PACK_W = 512
EXCHANGE_ROW_MULTIPLE = 512
SEQ_BLOCK = 128
SWA_HEAD_DIM = 64
ROPE_DIM = SWA_HEAD_DIM // 4
ROPE_THETA = 500000.0

ADAM_LR = 0.001
ADAM_B1 = 0.9
ADAM_B2 = 0.999
ADAM_EPS = 1e-08
ADAM_WD = 0.01
ADAM_STEP = 10


def _cp(sem=None, **kw):
    return pltpu.CompilerParams(dimension_semantics=sem, vmem_limit_bytes=VMEM_LIMIT, **kw)


def _tile(dim, pref, mult=LANES):
    if dim <= pref:
        return dim
    t = (pref // mult) * mult
    while t >= mult:
        if dim % t == 0:
            return t
        t -= mult
    return dim


_DIMS = {"nn": (((1,), (0,)), ((), ())), "nt": (((1,), (1,)), ((), ())), "tn": (((0,), (0,)), ((), ()))}


MM_TILES = {"nn": (1024, 512, 2048), "nt": (1024, 1024, 2816), "tn": (512, 512, 4096)}


def mm(a, b, mode="nn", out_dtype=F32, add=None, name="mm", b_cols=None, tm=None, tn=None, tk=None):
    tm, tn, tk = (d if t is None else t for t, d in zip((tm, tn, tk), MM_TILES[mode]))
    c0 = 0
    if b_cols is not None:
        c0, csize = b_cols
    if mode == "nn":
        (M, K), (K2, N) = a.shape, b.shape
        if b_cols is not None:
            N = csize
    elif mode == "nt":
        (M, K), (N, K2) = a.shape, b.shape
        if b_cols is not None:
            K2 = csize
    else:
        (K, M), (K2, N) = a.shape, b.shape
        assert b_cols is None
    assert K == K2, (a.shape, b.shape, mode)
    tm = _tile(M, tm, LANES if mode == "tn" else 16)
    tn = _tile(N, tn)
    tk = _tile(K, tk, LANES if mode != "tn" else 16)
    nk = K // tk
    if b_cols is not None:
        assert c0 % (tn if mode == "nn" else tk) == 0, (b_cols, tn, tk)
    bo = c0 // (tn if mode == "nn" else tk)
    dims = _DIMS[mode]
    has_add = add is not None

    def body(a_ref, b_ref, *rest):
        if has_add:
            add_ref, o_ref, acc_ref = rest
        else:
            o_ref, acc_ref = rest
        k = pl.program_id(2)
        p = lax.dot_general(a_ref[...].astype(BF16), b_ref[...].astype(BF16), dims, preferred_element_type=F32)

        @pl.when(k == 0)
        def _():
            acc_ref[...] = p + add_ref[...].astype(F32) if has_add else p

        @pl.when(k > 0)
        def _():
            acc_ref[...] += p

        @pl.when(k == nk - 1)
        def _():
            o_ref[...] = acc_ref[...].astype(o_ref.dtype)

    a_spec = pl.BlockSpec((tk, tm), lambda i, j, k: (k, i)) if mode == "tn" else pl.BlockSpec((tm, tk), lambda i, j, k: (i, k))
    b_spec = pl.BlockSpec((tn, tk), lambda i, j, k: (j, k + bo)) if mode == "nt" else pl.BlockSpec((tk, tn), lambda i, j, k: (k, j + bo))
    o_spec = pl.BlockSpec((tm, tn), lambda i, j, k: (i, j))
    in_specs = [a_spec, b_spec] + ([o_spec] if has_add else [])
    args = (a, b) + ((add,) if has_add else ())
    return pl.pallas_call(
        body, name=name, grid=(M // tm, N // tn, nk), in_specs=in_specs, out_specs=o_spec,
        out_shape=jax.ShapeDtypeStruct((M, N), out_dtype), scratch_shapes=[pltpu.VMEM((tm, tn), F32)],
        compiler_params=_cp(("parallel", "parallel", "arbitrary")),
    )(*args)


def _rms(x, g):
    return (x * lax.rsqrt(jnp.mean(x * x, axis=-1, keepdims=True) + EPS)) * g


def _pre(x, g, sh, sc):
    return _rms(x, g) * (1 + sc) + sh


def _post(x, y, g, gate):
    return x + gate * _rms(y, g)


ROW_TILE = 256


def _row_spec(tr, d):
    return pl.BlockSpec((tr, d), lambda i: (i, 0))


def _vec_spec(d):
    return pl.BlockSpec((1, d), lambda i: (0, 0))


def pre_fwd(x, g, sh, sc, name):
    S, D = x.shape
    tr = _tile(S, ROW_TILE, 16)

    def body(x_ref, g_ref, sh_ref, sc_ref, h_ref):
        h_ref[...] = _pre(x_ref[...], g_ref[...], sh_ref[...], sc_ref[...]).astype(h_ref.dtype)

    return pl.pallas_call(
        body, name=name, grid=(S // tr,), in_specs=[_row_spec(tr, D)] + [_vec_spec(D)] * 3, out_specs=_row_spec(tr, D),
        out_shape=jax.ShapeDtypeStruct((S, D), BF16), compiler_params=_cp(("parallel",)),
    )(x, g, sh, sc)


def pre_bwd(x, g, sh, sc, dh, dres, name):
    S, D = x.shape
    tr = _tile(S, ROW_TILE, 16)

    def body(x_ref, g_ref, sh_ref, sc_ref, dh_ref, dres_ref, dx_ref, dg_ref, dsh_ref, dsc_ref):
        _, vjp = jax.vjp(_pre, x_ref[...], g_ref[...], sh_ref[...], sc_ref[...])
        dx, dg, dsh, dsc = vjp(dh_ref[...].astype(F32))
        dx_ref[...] = dres_ref[...] + dx

        @pl.when(pl.program_id(0) == 0)
        def _():
            dg_ref[...] = jnp.zeros_like(dg_ref)
            dsh_ref[...] = jnp.zeros_like(dsh_ref)
            dsc_ref[...] = jnp.zeros_like(dsc_ref)

        dg_ref[...] += dg
        dsh_ref[...] += dsh
        dsc_ref[...] += dsc

    vec = jax.ShapeDtypeStruct((1, D), F32)
    return pl.pallas_call(
        body, name=name, grid=(S // tr,), in_specs=[_row_spec(tr, D)] + [_vec_spec(D)] * 3 + [_row_spec(tr, D)] * 2,
        out_specs=[_row_spec(tr, D)] + [_vec_spec(D)] * 3, out_shape=[jax.ShapeDtypeStruct((S, D), F32), vec, vec, vec],
        compiler_params=_cp(("arbitrary",)),
    )(x, g, sh, sc, dh, dres)


def post_fwd(x, y, g, gate, name):
    S, D = x.shape
    tr = _tile(S, ROW_TILE, 16)

    def body(x_ref, y_ref, g_ref, gate_ref, o_ref):
        o_ref[...] = _post(x_ref[...], y_ref[...], g_ref[...], gate_ref[...])

    return pl.pallas_call(
        body, name=name, grid=(S // tr,), in_specs=[_row_spec(tr, D)] * 2 + [_vec_spec(D)] * 2, out_specs=_row_spec(tr, D),
        out_shape=jax.ShapeDtypeStruct((S, D), F32), compiler_params=_cp(("parallel",)),
    )(x, y, g, gate)


def post_bwd(y, g, gate, dxn, name):
    S, D = y.shape
    tr = _tile(S, ROW_TILE, 16)

    def body(y_ref, g_ref, gate_ref, dxn_ref, dy_ref, dg_ref, dgate_ref):
        fn = lambda yy, gg, gt: gt * _rms(yy, gg)
        _, vjp = jax.vjp(fn, y_ref[...], g_ref[...], gate_ref[...])
        dy, dg, dgate = vjp(dxn_ref[...])
        dy_ref[...] = dy.astype(dy_ref.dtype)

        @pl.when(pl.program_id(0) == 0)
        def _():
            dg_ref[...] = jnp.zeros_like(dg_ref)
            dgate_ref[...] = jnp.zeros_like(dgate_ref)

        dg_ref[...] += dg
        dgate_ref[...] += dgate

    vec = jax.ShapeDtypeStruct((1, D), F32)
    return pl.pallas_call(
        body, name=name, grid=(S // tr,), in_specs=[_row_spec(tr, D)] + [_vec_spec(D)] * 2 + [_row_spec(tr, D)],
        out_specs=[_row_spec(tr, D)] + [_vec_spec(D)] * 2, out_shape=[jax.ShapeDtypeStruct((S, D), BF16), vec, vec],
        compiler_params=_cp(("arbitrary",)),
    )(y, g, gate, dxn)


def _swiglu(g, u):
    return jax.nn.silu(g) * u


def act_fwd(g, u, name):
    S, F = g.shape
    tr, tc = _tile(S, 512, 16), _tile(F, 1024)
    spec = pl.BlockSpec((tr, tc), lambda i, j: (i, j))

    def body(g_ref, u_ref, a_ref):
        a_ref[...] = _swiglu(g_ref[...].astype(F32), u_ref[...].astype(F32)).astype(a_ref.dtype)

    return pl.pallas_call(
        body, name=name, grid=(S // tr, F // tc), in_specs=[spec, spec], out_specs=spec,
        out_shape=jax.ShapeDtypeStruct((S, F), BF16), compiler_params=_cp(("parallel", "parallel")),
    )(g, u)


def act_bwd(g, u, da, name):
    S, F = g.shape
    tr, tc = _tile(S, 512, 16), _tile(F, 1024)
    spec = pl.BlockSpec((tr, tc), lambda i, j: (i, j))

    def body(g_ref, u_ref, da_ref, dg_ref, du_ref):
        _, vjp = jax.vjp(_swiglu, g_ref[...].astype(F32), u_ref[...].astype(F32))
        dg, du = vjp(da_ref[...].astype(F32))
        dg_ref[...] = dg.astype(dg_ref.dtype)
        du_ref[...] = du.astype(du_ref.dtype)

    out = jax.ShapeDtypeStruct((S, F), BF16)
    return pl.pallas_call(
        body, name=name, grid=(S // tr, F // tc), in_specs=[spec] * 3, out_specs=[spec, spec], out_shape=[out, out],
        compiler_params=_cp(("parallel", "parallel")),
    )(g, u, da)


def _tri(upper):
    r = lax.broadcasted_iota(jnp.int32, (LANES, LANES), 0)
    c = lax.broadcasted_iota(jnp.int32, (LANES, LANES), 1)
    return ((r <= c) if upper else (r >= c)).astype(F32)


def _hdot(a, b):
    return jnp.dot(a, b, precision=lax.Precision.HIGHEST, preferred_element_type=F32)


def fox_gate_fwd(fgT, b, name):
    H, S = fgT.shape
    spec = pl.BlockSpec((H, LANES), lambda ch: (0, ch))

    def body(fg_ref, b_ref, cum_ref, carry_ref):
        @pl.when(pl.program_id(0) == 0)
        def _():
            carry_ref[...] = jnp.zeros_like(carry_ref)

        lf = jax.nn.log_sigmoid(fg_ref[...] + b_ref[...])
        cum_ref[...] = _hdot(lf, _tri(True)) + carry_ref[...]
        carry_ref[...] += _hdot(lf, jnp.ones((LANES, LANES), F32))

    return pl.pallas_call(
        body, name=name, grid=(S // LANES,), in_specs=[spec, pl.BlockSpec((H, 1), lambda ch: (0, 0))], out_specs=spec,
        out_shape=jax.ShapeDtypeStruct((H, S), F32), scratch_shapes=[pltpu.VMEM((H, LANES), F32)], compiler_params=_cp(("arbitrary",)),
    )(fgT, b)


def fox_gate_bwd(dcum, fgT, b, name):
    H, S = fgT.shape
    nch = S // LANES
    spec = pl.BlockSpec((H, LANES), lambda t: (0, nch - 1 - t))

    def body(dcum_ref, fg_ref, b_ref, dfg_ref, db_ref, tail_ref):
        @pl.when(pl.program_id(0) == 0)
        def _():
            tail_ref[...] = jnp.zeros_like(tail_ref)
            db_ref[...] = jnp.zeros_like(db_ref)

        dlf = _hdot(dcum_ref[...], _tri(False)) + tail_ref[...]
        dfg = dlf * jax.nn.sigmoid(-(fg_ref[...] + b_ref[...]))
        dfg_ref[...] = dfg
        ones = jnp.ones((LANES, LANES), F32)
        tail_ref[...] += _hdot(dcum_ref[...], ones)
        db_ref[...] += _hdot(dfg, ones)

    return pl.pallas_call(
        body, name=name, grid=(nch,), in_specs=[spec, spec, pl.BlockSpec((H, 1), lambda t: (0, 0))],
        out_specs=[spec, pl.BlockSpec((H, LANES), lambda t: (0, 0))],
        out_shape=[jax.ShapeDtypeStruct((H, S), F32), jax.ShapeDtypeStruct((H, LANES), F32)],
        scratch_shapes=[pltpu.VMEM((H, LANES), F32)], compiler_params=_cp(("arbitrary",)),
    )(dcum, fgT, b)


FOX_TILE = 1024


def _on_and_below_diagonal(i, j, tile):
    @pl.when(j < i)
    def _():
        tile(False)

    @pl.when(j == i)
    def _():
        tile(True)


def _causal_pairs(n, by_key):
    pairs = [(i, j) for i in range(n) for j in range(i + 1)]
    if by_key:
        pairs.sort(key=lambda p: (p[1], p[0]))
    qi = np.asarray([p[0] for p in pairs], np.int32)
    kj = np.asarray([p[1] for p in pairs], np.int32)
    return qi, kj


def _fox_scores(q, k, fq, fk, T, scale, transposed):
    r = lax.broadcasted_iota(jnp.int32, (T, T), 0)
    c = lax.broadcasted_iota(jnp.int32, (T, T), 1)
    if transposed:
        return lax.dot_general(k, q, _DIMS["nt"], preferred_element_type=F32) * scale + (fq - fk), r <= c
    return lax.dot_general(q, k, _DIMS["nt"], preferred_element_type=F32) * scale + (fq - fk), c <= r


def fox_attn_fwd(qkv, cum_col, cum_row, H, name):
    S = qkv.shape[0]
    Dh = qkv.shape[1] // (3 * H)
    T = _tile(S, FOX_TILE)
    n = S // T
    qi, kj = _causal_pairs(n, by_key=False)
    scale = Dh ** -0.5

    def body(qi_ref, kj_ref, q_ref, k_ref, v_ref, fq_ref, fk_ref, o_ref, lse_ref, m_ref, l_ref, acc_ref):
        p_id = pl.program_id(1)
        i, j = qi_ref[p_id], kj_ref[p_id]

        @pl.when(j == 0)
        def _():
            m_ref[...] = jnp.full_like(m_ref, NEG)
            l_ref[...] = jnp.zeros_like(l_ref)
            acc_ref[...] = jnp.zeros_like(acc_ref)

        def tile(masked):
            s, mask = _fox_scores(q_ref[...], k_ref[...], fq_ref[0], fk_ref[0], T, scale, False)
            if masked:
                s = jnp.where(mask, s, NEG)
            m_new = jnp.maximum(m_ref[...], jnp.max(s, axis=1, keepdims=True))
            alpha = jnp.exp(m_ref[...] - m_new)
            p = jnp.exp(s - m_new)
            l_ref[...] = alpha * l_ref[...] + jnp.sum(p, axis=1, keepdims=True)
            acc_ref[...] = alpha * acc_ref[...] + jnp.dot(p.astype(BF16), v_ref[...], preferred_element_type=F32)
            m_ref[...] = m_new

        _on_and_below_diagonal(i, j, tile)

        @pl.when(j == i)
        def _():
            o_ref[...] = (acc_ref[...] / l_ref[...]).astype(o_ref.dtype)
            lse_ref[0] = m_ref[...] + jnp.log(l_ref[...])

    grid_spec = pltpu.PrefetchScalarGridSpec(
        num_scalar_prefetch=2, grid=(H, len(qi)),
        in_specs=[
            pl.BlockSpec((T, Dh), lambda h, p, qi, kj: (qi[p], h)),
            pl.BlockSpec((T, Dh), lambda h, p, qi, kj: (kj[p], H + h)),
            pl.BlockSpec((T, Dh), lambda h, p, qi, kj: (kj[p], 2 * H + h)),
            pl.BlockSpec((1, T, 1), lambda h, p, qi, kj: (h, qi[p], 0)),
            pl.BlockSpec((1, 1, T), lambda h, p, qi, kj: (h, 0, kj[p])),
        ],
        out_specs=[
            pl.BlockSpec((T, Dh), lambda h, p, qi, kj: (qi[p], h)),
            pl.BlockSpec((1, T, 1), lambda h, p, qi, kj: (h, qi[p], 0)),
        ],
        scratch_shapes=[pltpu.VMEM((T, 1), F32), pltpu.VMEM((T, 1), F32), pltpu.VMEM((T, Dh), F32)],
    )
    return pl.pallas_call(
        body, name=name, grid_spec=grid_spec,
        out_shape=[jax.ShapeDtypeStruct((S, H * Dh), F32), jax.ShapeDtypeStruct((H, S, 1), F32)],
        compiler_params=_cp(("parallel", "arbitrary")),
    )(jnp.asarray(qi), jnp.asarray(kj), qkv, qkv, qkv, cum_col, cum_row)


def fox_attn_bwd_dq(qkv, do, o, lse, cum_col, cum_row, H, name):
    S = qkv.shape[0]
    Dh = qkv.shape[1] // (3 * H)
    T = _tile(S, FOX_TILE)
    n = S // T
    qi, kj = _causal_pairs(n, by_key=False)
    scale = Dh ** -0.5

    def body(qi_ref, kj_ref, q_ref, k_ref, v_ref, do_ref, o_ref, lse_ref, fq_ref, fk_ref, dq_ref, delta_ref, acc_ref, dl_ref,
             rs_ref):
        p_id = pl.program_id(1)
        i, j = qi_ref[p_id], kj_ref[p_id]

        @pl.when(j == 0)
        def _():
            acc_ref[...] = jnp.zeros_like(acc_ref)
            rs_ref[...] = jnp.zeros_like(rs_ref)
            dl_ref[...] = jnp.sum(do_ref[...].astype(F32) * o_ref[...].astype(F32), axis=1, keepdims=True)

        def tile(masked):
            s, mask = _fox_scores(q_ref[...], k_ref[...], fq_ref[0], fk_ref[0], T, scale, False)
            p = jnp.exp(s - lse_ref[0])
            if masked:
                p = jnp.where(mask, p, 0.0)
            dp = lax.dot_general(do_ref[...], v_ref[...], _DIMS["nt"], preferred_element_type=F32)
            ds = p * (dp - dl_ref[...])
            rs_ref[...] += jnp.sum(ds, axis=1, keepdims=True)
            acc_ref[...] += jnp.dot(ds.astype(BF16), k_ref[...], preferred_element_type=F32)

        _on_and_below_diagonal(i, j, tile)

        @pl.when(j == i)
        def _():
            dq_ref[...] = (acc_ref[...] * scale).astype(dq_ref.dtype)
            delta_ref[0] = dl_ref[...] + rs_ref[...]

    qspec = pl.BlockSpec((T, Dh), lambda h, p, qi, kj: (qi[p], h))
    colspec = pl.BlockSpec((1, T, 1), lambda h, p, qi, kj: (h, qi[p], 0))
    grid_spec = pltpu.PrefetchScalarGridSpec(
        num_scalar_prefetch=2, grid=(H, len(qi)),
        in_specs=[
            qspec,
            pl.BlockSpec((T, Dh), lambda h, p, qi, kj: (kj[p], H + h)),
            pl.BlockSpec((T, Dh), lambda h, p, qi, kj: (kj[p], 2 * H + h)),
            qspec, qspec, colspec, colspec,
            pl.BlockSpec((1, 1, T), lambda h, p, qi, kj: (h, 0, kj[p])),
        ],
        out_specs=[qspec, colspec],
        scratch_shapes=[pltpu.VMEM((T, Dh), F32), pltpu.VMEM((T, 1), F32), pltpu.VMEM((T, 1), F32)],
    )
    return pl.pallas_call(
        body, name=name, grid_spec=grid_spec,
        out_shape=[jax.ShapeDtypeStruct((S, H * Dh), BF16), jax.ShapeDtypeStruct((H, S, 1), F32)],
        compiler_params=_cp(("parallel", "arbitrary")),
    )(jnp.asarray(qi), jnp.asarray(kj), qkv, qkv, qkv, do, o, lse, cum_col, cum_row)


def fox_attn_bwd_dkv(qkv, do, lse_row, delta_row, cum_col, cum_row, H, name):
    S = qkv.shape[0]
    Dh = qkv.shape[1] // (3 * H)
    T = _tile(S, FOX_TILE)
    n = S // T
    qi, kj = _causal_pairs(n, by_key=True)
    scale = Dh ** -0.5

    def body(qi_ref, kj_ref, q_ref, k_ref, v_ref, do_ref, lse_ref, dl_ref, fq_ref, fk_ref, dk_ref, dv_ref, dcum_ref,
             dk_acc, dv_acc, df_acc):
        p_id = pl.program_id(1)
        i, j = qi_ref[p_id], kj_ref[p_id]

        @pl.when(i == j)
        def _():
            dk_acc[...] = jnp.zeros_like(dk_acc)
            dv_acc[...] = jnp.zeros_like(dv_acc)
            df_acc[...] = jnp.zeros_like(df_acc)

        def tile(masked):
            sT, mask = _fox_scores(q_ref[...], k_ref[...], fq_ref[0], fk_ref[0], T, scale, True)
            pT = jnp.exp(sT - lse_ref[0])
            if masked:
                pT = jnp.where(mask, pT, 0.0)
            dv_acc[...] += jnp.dot(pT.astype(BF16), do_ref[...], preferred_element_type=F32)
            dpT = lax.dot_general(v_ref[...], do_ref[...], _DIMS["nt"], preferred_element_type=F32)
            dsT = pT * (dpT - dl_ref[0])
            dk_acc[...] += jnp.dot(dsT.astype(BF16), q_ref[...], preferred_element_type=F32)
            df_acc[...] -= jnp.sum(dsT, axis=1, keepdims=True)

        _on_and_below_diagonal(i, j, tile)

        @pl.when(i == n - 1)
        def _():
            dk_ref[...] = (dk_acc[...] * scale).astype(dk_ref.dtype)
            dv_ref[...] = dv_acc[...].astype(dv_ref.dtype)
            dcum_ref[0] = df_acc[...]

    qspec = pl.BlockSpec((T, Dh), lambda h, p, qi, kj: (qi[p], h))
    kspec = pl.BlockSpec((T, Dh), lambda h, p, qi, kj: (kj[p], H + h))
    vspec = pl.BlockSpec((T, Dh), lambda h, p, qi, kj: (kj[p], 2 * H + h))
    qrow = pl.BlockSpec((1, 1, T), lambda h, p, qi, kj: (h, 0, qi[p]))
    kcol = pl.BlockSpec((1, T, 1), lambda h, p, qi, kj: (h, kj[p], 0))
    grid_spec = pltpu.PrefetchScalarGridSpec(
        num_scalar_prefetch=2, grid=(H, len(qi)),
        in_specs=[qspec, kspec, vspec, qspec, qrow, qrow, qrow, kcol],
        out_specs=[pl.BlockSpec((T, Dh), lambda h, p, qi, kj: (kj[p], h))] * 2 + [kcol],
        scratch_shapes=[pltpu.VMEM((T, Dh), F32), pltpu.VMEM((T, Dh), F32), pltpu.VMEM((T, 1), F32)],
    )
    out = jax.ShapeDtypeStruct((S, H * Dh), BF16)
    return pl.pallas_call(
        body, name=name, grid_spec=grid_spec, out_shape=[out, out, jax.ShapeDtypeStruct((H, S, 1), F32)],
        compiler_params=_cp(("parallel", "arbitrary")),
    )(jnp.asarray(qi), jnp.asarray(kj), qkv, qkv, qkv, do, lse_row, delta_row, cum_row, cum_col)


def _sgu_ln(zu, zv, ln_g, ln_b):
    u = jax.nn.gelu(zu)
    v = jax.nn.gelu(zv)
    mu = jnp.mean(v, axis=-1, keepdims=True)
    var = jnp.mean(jnp.square(v - mu), axis=-1, keepdims=True)
    return u, (v - mu) * lax.rsqrt(var + EPS) * ln_g + ln_b


def _tril_mask():
    r = lax.broadcasted_iota(jnp.int32, (SEQ_BLOCK, SEQ_BLOCK), 0)
    c = lax.broadcasted_iota(jnp.int32, (SEQ_BLOCK, SEQ_BLOCK), 1)
    return r >= c


def _sgu_spatial(ws_ref, bsT, selT, vn, G):
    tril = _tril_mask()
    fs = []
    for g in range(G):
        wg = jnp.where(tril, ws_ref[g], 0.0).astype(BF16)
        fs.append(jnp.dot(wg, vn[:, g * SEQ_BLOCK:(g + 1) * SEQ_BLOCK].astype(BF16), preferred_element_type=F32))
    bias = jnp.dot(bsT, selT, precision=lax.Precision.HIGHEST, preferred_element_type=F32)
    return jnp.concatenate(fs, axis=1) + bias


def _sgu_specs(W, G):
    return [
        pl.BlockSpec((SEQ_BLOCK, 2 * W), lambda n: (n, 0)),
        pl.BlockSpec((1, W), lambda n: (0, 0)),
        pl.BlockSpec((1, W), lambda n: (0, 0)),
        pl.BlockSpec((G, SEQ_BLOCK, SEQ_BLOCK), lambda n: (0, 0, 0)),
        pl.BlockSpec((SEQ_BLOCK, G), lambda n: (0, 0)),
        pl.BlockSpec((G, W), lambda n: (0, 0)),
    ]


def sgu_fwd(zp, ln_g, ln_b, ws, bsT, selT, name):
    S, W2 = zp.shape
    W = W2 // 2
    G = ws.shape[0]

    def body(z_ref, lg_ref, lb_ref, ws_ref, bs_ref, sel_ref, o_ref):
        u, vn = _sgu_ln(z_ref[:, :W], z_ref[:, W:], lg_ref[...], lb_ref[...])
        o_ref[...] = (u * _sgu_spatial(ws_ref, bs_ref[...], sel_ref[...], vn, G)).astype(o_ref.dtype)

    return pl.pallas_call(
        body, name=name, grid=(S // SEQ_BLOCK,), in_specs=_sgu_specs(W, G), out_specs=pl.BlockSpec((SEQ_BLOCK, W), lambda n: (n, 0)),
        out_shape=jax.ShapeDtypeStruct((S, W), BF16), compiler_params=_cp(("parallel",)),
    )(zp, ln_g, ln_b, ws, bsT, selT)


def sgu_bwd(zp, ln_g, ln_b, ws, bsT, selT, dgated, name):
    S, W2 = zp.shape
    W = W2 // 2
    G = ws.shape[0]

    def body(z_ref, lg_ref, lb_ref, ws_ref, bs_ref, sel_ref, dgt_ref, dz_ref, dlg_ref, dlb_ref, dws_ref, dbs_ref):
        (u, vn), vjp = jax.vjp(_sgu_ln, z_ref[:, :W], z_ref[:, W:], lg_ref[...], lb_ref[...])
        f = _sgu_spatial(ws_ref, bs_ref[...], sel_ref[...], vn, G)
        dgt = dgt_ref[...].astype(F32)
        du, df = dgt * f, dgt * u

        @pl.when(pl.program_id(0) == 0)
        def _():
            dlg_ref[...] = jnp.zeros_like(dlg_ref)
            dlb_ref[...] = jnp.zeros_like(dlb_ref)
            dws_ref[...] = jnp.zeros_like(dws_ref)
            dbs_ref[...] = jnp.zeros_like(dbs_ref)

        dbs_ref[...] += lax.dot_general(df, sel_ref[...], _DIMS["nt"], precision=lax.Precision.HIGHEST, preferred_element_type=F32)
        tril = _tril_mask()
        dvn = []
        for g in range(G):
            sl = slice(g * SEQ_BLOCK, (g + 1) * SEQ_BLOCK)
            wg = jnp.where(tril, ws_ref[g], 0.0).astype(BF16)
            df_g = df[:, sl].astype(BF16)
            dw = lax.dot_general(df_g, vn[:, sl].astype(BF16), _DIMS["nt"], preferred_element_type=F32)
            dws_ref[g] += jnp.where(tril, dw, 0.0)
            dvn.append(lax.dot_general(wg, df_g, _DIMS["tn"], preferred_element_type=F32))
        dzu, dzv, dlg, dlb = vjp((du, jnp.concatenate(dvn, axis=1)))
        dz_ref[:, :W] = dzu.astype(dz_ref.dtype)
        dz_ref[:, W:] = dzv.astype(dz_ref.dtype)
        dlg_ref[...] += dlg
        dlb_ref[...] += dlb

    vec = jax.ShapeDtypeStruct((1, W), F32)
    return pl.pallas_call(
        body, name=name, grid=(S // SEQ_BLOCK,),
        in_specs=_sgu_specs(W, G) + [pl.BlockSpec((SEQ_BLOCK, W), lambda n: (n, 0))],
        out_specs=[
            pl.BlockSpec((SEQ_BLOCK, 2 * W), lambda n: (n, 0)),
            pl.BlockSpec((1, W), lambda n: (0, 0)),
            pl.BlockSpec((1, W), lambda n: (0, 0)),
            pl.BlockSpec((G, SEQ_BLOCK, SEQ_BLOCK), lambda n: (0, 0, 0)),
            pl.BlockSpec((SEQ_BLOCK, G), lambda n: (0, 0)),
        ],
        out_shape=[jax.ShapeDtypeStruct((S, W2), BF16), vec, vec, jax.ShapeDtypeStruct(ws.shape, F32), jax.ShapeDtypeStruct((SEQ_BLOCK, G), F32)],
        compiler_params=_cp(("arbitrary",)),
    )(zp, ln_g, ln_b, ws, bsT, selT, dgated)


def _rope_matrix():
    half = ROPE_DIM // 2
    R = np.zeros((SWA_HEAD_DIM, SWA_HEAD_DIM), np.float32)
    for j in range(half):
        R[j + half, j] = -1.0
        R[j, j + half] = 1.0
    return R


def _swa_block(q4, kp, kc, vp, vc, sink, Cq, Sq, Cp, Sp, R, n, G):
    B, Dh = SEQ_BLOCK, SWA_HEAD_DIM
    rot = lambda t: jnp.dot(t, R, precision=lax.Precision.HIGHEST, preferred_element_type=F32)
    q = q4.reshape(G * B, Dh)
    Cq4 = jnp.concatenate([Cq] * G, axis=0)
    Sq4 = jnp.concatenate([Sq] * G, axis=0)
    qr = q * Cq4 + rot(q) * Sq4
    kb = jnp.concatenate([kp * Cp + rot(kp) * Sp, kc * Cq + rot(kc) * Sq], axis=0)
    vb = jnp.concatenate([vp, vc], axis=0)
    s = lax.dot_general(qr.astype(BF16), kb.astype(BF16), _DIMS["nt"], preferred_element_type=F32) * (Dh ** -0.5)
    qi = lax.broadcasted_iota(jnp.int32, (G * B, 2 * B), 0) & (B - 1)
    ki = lax.broadcasted_iota(jnp.int32, (G * B, 2 * B), 1) - B
    rel = qi - ki
    valid = (rel >= 0) & (rel < B) & (n * B + ki >= 0)
    s = jnp.where(valid, s, NEG)
    m = lax.stop_gradient(jnp.maximum(jnp.max(s, axis=1, keepdims=True), sink))
    p = jnp.exp(s - m)
    p = p / (jnp.sum(p, axis=1, keepdims=True) + jnp.exp(sink - m))
    o = jnp.dot(p.astype(BF16), vb.astype(BF16), preferred_element_type=F32)
    return o.reshape(G, B, Dh)


def _swa_specs(G):
    B, Dh = SEQ_BLOCK, SWA_HEAD_DIM
    prev = lambda n: jnp.maximum(n - 1, 0)
    return [
        pl.BlockSpec((G, B, Dh), lambda h, n: (h, n, 0)),
        pl.BlockSpec((1, B, Dh), lambda h, n: (h, prev(n), 0)),
        pl.BlockSpec((1, B, Dh), lambda h, n: (h, n, 0)),
        pl.BlockSpec((1, B, Dh), lambda h, n: (h, prev(n), 0)),
        pl.BlockSpec((1, B, Dh), lambda h, n: (h, n, 0)),
        pl.BlockSpec((1, G * B, 1), lambda h, n: (h, 0, 0)),
        pl.BlockSpec((B, Dh), lambda h, n: (n, 0)),
        pl.BlockSpec((B, Dh), lambda h, n: (n, 0)),
        pl.BlockSpec((B, Dh), lambda h, n: (prev(n), 0)),
        pl.BlockSpec((B, Dh), lambda h, n: (prev(n), 0)),
        pl.BlockSpec((Dh, Dh), lambda h, n: (0, 0)),
    ]


def swa_fwd(qh, kh, vh, sink_col, C, Sn, R, name):
    Hq, S, Dh = qh.shape
    Hk = kh.shape[0]
    G = Hq // Hk

    def body(q_ref, kp_ref, kc_ref, vp_ref, vc_ref, sk_ref, cq_ref, sq_ref, cp_ref, sp_ref, r_ref, o_ref):
        o = _swa_block(q_ref[...], kp_ref[0], kc_ref[0], vp_ref[0], vc_ref[0], sk_ref[0], cq_ref[...], sq_ref[...], cp_ref[...],
                       sp_ref[...], r_ref[...], pl.program_id(1), G)
        o_ref[...] = o.astype(o_ref.dtype)

    return pl.pallas_call(
        body, name=name, grid=(Hk, S // SEQ_BLOCK), in_specs=_swa_specs(G),
        out_specs=pl.BlockSpec((G, SEQ_BLOCK, Dh), lambda h, n: (h, n, 0)),
        out_shape=jax.ShapeDtypeStruct((Hq, S, Dh), BF16), compiler_params=_cp(("parallel", "parallel")),
    )(qh, kh, kh, vh, vh, sink_col, C, Sn, C, Sn, R)


def swa_bwd(qh, kh, vh, sink_col, C, Sn, R, doh, name):
    Hq, S, Dh = qh.shape
    Hk = kh.shape[0]
    G = Hq // Hk
    B = SEQ_BLOCK

    def body(q_ref, kp_ref, kc_ref, vp_ref, vc_ref, sk_ref, cq_ref, sq_ref, cp_ref, sp_ref, r_ref, do_ref,
             dq_ref, dkp_ref, dkc_ref, dvp_ref, dvc_ref, dsk_ref):
        n = pl.program_id(1)
        fn = lambda q4, kp, kc, vp, vc, sk: _swa_block(q4, kp, kc, vp, vc, sk, cq_ref[...], sq_ref[...], cp_ref[...], sp_ref[...],
                                                      r_ref[...], n, G)
        _, vjp = jax.vjp(fn, q_ref[...], kp_ref[0], kc_ref[0], vp_ref[0], vc_ref[0], sk_ref[0])
        dq, dkp, dkc, dvp, dvc, dsk = vjp(do_ref[...].astype(F32))
        dq_ref[...] = dq
        dkp_ref[0] = dkp
        dkc_ref[0] = dkc
        dvp_ref[0] = dvp
        dvc_ref[0] = dvc

        @pl.when(n == 0)
        def _():
            dsk_ref[...] = jnp.zeros_like(dsk_ref)

        for g in range(G):
            part = jnp.sum(dsk[g * B:(g + 1) * B], axis=0, keepdims=True)
            dsk_ref[0, g:g + 1, :] += jnp.broadcast_to(part, (1, LANES))

    qspec = pl.BlockSpec((G, B, Dh), lambda h, n: (h, n, 0))
    kspec = pl.BlockSpec((1, B, Dh), lambda h, n: (h, n, 0))
    kshape = jax.ShapeDtypeStruct((Hk, S, Dh), F32)
    return pl.pallas_call(
        body, name=name, grid=(Hk, S // B), in_specs=_swa_specs(G) + [qspec],
        out_specs=[qspec, kspec, kspec, kspec, kspec, pl.BlockSpec((1, G, LANES), lambda h, n: (h, 0, 0))],
        out_shape=[jax.ShapeDtypeStruct((Hq, S, Dh), F32), kshape, kshape, kshape, kshape, jax.ShapeDtypeStruct((Hk, G, LANES), F32)],
        compiler_params=_cp(("parallel", "arbitrary")),
    )(qh, kh, kh, vh, vh, sink_col, C, Sn, C, Sn, R, doh)


def shift_add(cur, prev, name):
    Hk, S, Dh = cur.shape
    B = SEQ_BLOCK
    nb = S // B

    def body(c_ref, p_ref, o_ref):
        last = pl.program_id(1) == nb - 1
        o_ref[...] = c_ref[...] + jnp.where(last, 0.0, p_ref[...])

    spec = pl.BlockSpec((1, B, Dh), lambda h, n: (h, n, 0))
    nxt = pl.BlockSpec((1, B, Dh), lambda h, n: (h, jnp.minimum(n + 1, nb - 1), 0))
    return pl.pallas_call(
        body, name=name, grid=(Hk, nb), in_specs=[spec, nxt], out_specs=spec, out_shape=jax.ShapeDtypeStruct(cur.shape, F32),
        compiler_params=_cp(("parallel", "parallel")),
    )(cur, prev)


def loss_head(y, target, name):
    S, D = y.shape
    tr = _tile(S, ROW_TILE, 16)

    def body(y_ref, t_ref, acc_ref, dy_ref):
        err = y_ref[...] - t_ref[...]
        dy_ref[...] = err * (1.0 / D)

        @pl.when(pl.program_id(0) == 0)
        def _():
            acc_ref[...] = jnp.zeros_like(acc_ref)

        acc_ref[...] += jnp.broadcast_to(jnp.sum(err * err).reshape(1, 1), (1, LANES))

    return pl.pallas_call(
        body, name=name, grid=(S // tr,), in_specs=[_row_spec(tr, D)] * 2,
        out_specs=[pl.BlockSpec((1, LANES), lambda i: (0, 0)), _row_spec(tr, D)],
        out_shape=[jax.ShapeDtypeStruct((1, LANES), F32), jax.ShapeDtypeStruct((S, D), F32)], compiler_params=_cp(("arbitrary",)),
    )(y, target)


def _adam_update(w, g, m, v):
    m = ADAM_B1 * m + (1.0 - ADAM_B1) * g
    v = ADAM_B2 * v + (1.0 - ADAM_B2) * jnp.square(g)
    m_hat = m / (1.0 - ADAM_B1 ** ADAM_STEP)
    v_hat = v / (1.0 - ADAM_B2 ** ADAM_STEP)
    delta = -ADAM_LR * (m_hat / (jnp.sqrt(v_hat) + ADAM_EPS) + ADAM_WD * w)
    return delta, m, v


def adamw(w, m, v, gparts, name, gstack=0):
    R, C = w.shape
    tr = _tile(R, max(8, (128 * 1024) // C), 8)
    spec = pl.BlockSpec((tr, C), lambda i: (i, 0))
    nplain = len(gparts) - (1 if gstack else 0)

    def body(w_ref, m_ref, v_ref, *rest):
        g_refs, (g_ref, d_ref, mo_ref, vo_ref) = rest[:len(gparts)], rest[len(gparts):]
        g = None
        for r in g_refs[:nplain]:
            g = r[...].astype(F32) if g is None else g + r[...].astype(F32)
        if gstack:
            for t in range(gstack):
                part = g_refs[-1][t].astype(F32)
                g = part if g is None else g + part
        d, mn, vn = _adam_update(w_ref[...], g, m_ref[...], v_ref[...])
        g_ref[...] = g
        d_ref[...] = d
        mo_ref[...] = mn
        vo_ref[...] = vn

    gspecs = [spec] * nplain + ([pl.BlockSpec((gstack, tr, C), lambda i: (0, i, 0))] if gstack else [])
    out = jax.ShapeDtypeStruct((R, C), F32)
    return pl.pallas_call(
        body, name=name, grid=(R // tr,), in_specs=[spec] * 3 + gspecs, out_specs=[spec] * 4, out_shape=[out] * 4,
        compiler_params=_cp(("parallel",)),
    )(w, m, v, *gparts)


def ada_fwd(c_all, ada_w, ada_b, name):
    L, D, N = ada_w.shape
    Bp = c_all.shape[0]
    tn = _tile(N, 512)

    def body(c_ref, w_ref, b_ref, o_ref):
        ca = jax.nn.silu(c_ref[...]).astype(BF16)
        o_ref[0] = jnp.dot(ca, w_ref[0].astype(BF16), preferred_element_type=F32) + b_ref[0]

    return pl.pallas_call(
        body, name=name, grid=(L, N // tn),
        in_specs=[pl.BlockSpec((Bp, D), lambda l, j: (0, 0)), pl.BlockSpec((1, D, tn), lambda l, j: (l, 0, j)),
                  pl.BlockSpec((1, 1, tn), lambda l, j: (l, 0, j))],
        out_specs=pl.BlockSpec((1, Bp, tn), lambda l, j: (l, 0, j)), out_shape=jax.ShapeDtypeStruct((L, Bp, N), F32),
        compiler_params=_cp(("parallel", "parallel")),
    )(c_all, ada_w, ada_b)


def ada_wgrad(c_all, dmod, name):
    L, Bp, N = dmod.shape
    D = c_all.shape[1]
    tn = _tile(N, 512)

    def body(c_ref, d_ref, o_ref):
        ca = jax.nn.silu(c_ref[...]).astype(BF16)
        o_ref[0] = lax.dot_general(ca, d_ref[0].astype(BF16), _DIMS["tn"], preferred_element_type=F32)

    return pl.pallas_call(
        body, name=name, grid=(L, N // tn),
        in_specs=[pl.BlockSpec((Bp, D), lambda l, j: (0, 0)), pl.BlockSpec((1, Bp, tn), lambda l, j: (l, 0, j))],
        out_specs=pl.BlockSpec((1, D, tn), lambda l, j: (l, 0, j)), out_shape=jax.ShapeDtypeStruct((L, D, N), F32),
        compiler_params=_cp(("parallel", "parallel")),
    )(c_all, dmod)


N_DEV = 8
N_CHIP = 4
ANY = pl.BlockSpec(memory_space=pl.ANY)


def _place():
    return lax.axis_index("x"), lax.axis_index("y"), lax.axis_index("c")


def _other_chips(x, y):
    chips = [(1 - x, y), (x, 1 - y), (1 - x, 1 - y)]
    return chips, [2 * cx + cy for cx, cy in chips]


def _rcopy(src, dst, ssem, rsem, to):
    return pltpu.make_async_remote_copy(src_ref=src, dst_ref=dst, send_sem=ssem, recv_sem=rsem, device_id=to, device_id_type=MESH)


def ag_small(xs, name):
    R, Wd = xs.shape

    def body(x_ref, out_ref, send_sems, recv_sems, local_sem):
        x, y, c = _place()
        me, sibling = (x, y, c), (x, y, 1 - c)
        chips, _ = _other_chips(x, y)

        def slot(px, py, pc):
            return out_ref.at[4 * px + 2 * py + pc]

        def copy(k, block, to, src=None):
            return _rcopy(slot(*block) if src is None else src, slot(*block), send_sems.at[k], recv_sems.at[k], to)

        mine = pltpu.make_async_copy(x_ref, slot(*me), local_sem)
        mine.start()
        first = [copy(0, me, sibling, src=x_ref)]
        first += [copy(1 + j, me, (*chip, c), src=x_ref) for j, chip in enumerate(chips)]
        for cp in first:
            cp.start()
        passed = [copy(4 + j, (*chip, c), sibling) for j, chip in enumerate(chips)]
        for j, chip in enumerate(chips):
            copy(1 + j, (*chip, c), me).wait_recv()
            passed[j].start()
        copy(0, sibling, me).wait_recv()
        for j, chip in enumerate(chips):
            copy(4 + j, (*chip, 1 - c), me).wait_recv()
        for cp in first + passed:
            cp.wait_send()
        mine.wait()

    vm = pl.BlockSpec(memory_space=pltpu.VMEM)
    return pl.pallas_call(
        body, name=name, out_shape=jax.ShapeDtypeStruct((N_DEV, R, Wd), xs.dtype), in_specs=[vm], out_specs=vm,
        scratch_shapes=[pltpu.SemaphoreType.DMA((7,)), pltpu.SemaphoreType.DMA((7,)), pltpu.SemaphoreType.DMA],
        compiler_params=_cp(),
    )(xs)


def ag_weights(flat, name, nchunk=4):
    _, Rh, Wd = flat.shape
    assert Rh % (nchunk * 16) == 0, (Rh, nchunk)
    rows = Rh // nchunk

    def body(x_ref, o_ref, s_ici, r_ici, s_d2d, r_d2d):
        x, y, c = _place()
        q = 2 * x + y
        sibling = (x, y, 1 - c)
        chips, qs = _other_chips(x, y)

        def rows_of(k):
            return pl.ds(k * rows, rows)

        def ici(j, k, landing_q):
            return _rcopy(x_ref.at[c, rows_of(k)], o_ref.at[landing_q, c, rows_of(k)], s_ici.at[j, k], r_ici.at[j, k], (*chips[j], c))

        def handoff(j, k, half):
            blk = o_ref.at[qs[j], half, rows_of(k)]
            return _rcopy(blk, blk, s_d2d.at[j, k], r_d2d.at[j, k], sibling)

        for k in range(nchunk):
            for j in range(3):
                ici(j, k, q).start()
        for k in range(nchunk):
            for j in range(3):
                ici(j, k, qs[j]).wait_recv()
                handoff(j, k, c).start()
        for k in range(nchunk):
            for j in range(3):
                handoff(j, k, 1 - c).wait_recv()
        for k in range(nchunk):
            for j in range(3):
                ici(j, k, q).wait_send()
                handoff(j, k, c).wait_send()

    dma = pltpu.SemaphoreType.DMA
    return pl.pallas_call(
        body, name=name, out_shape=jax.ShapeDtypeStruct((N_CHIP, 2, Rh, Wd), flat.dtype), in_specs=[ANY], out_specs=ANY,
        scratch_shapes=[dma((3, nchunk)), dma((3, nchunk)), dma((3, nchunk)), dma((3, nchunk))], compiler_params=_cp(),
    )(flat)


def sibling_fold(g, name):
    _, _, Rh, Wd = g.shape

    def body(x_ref, o_ref, ssem, rsem):
        x, y, c = _place()
        cps = [_rcopy(x_ref.at[s, 1 - c], o_ref.at[s], ssem.at[s], rsem.at[s], (x, y, 1 - c)) for s in range(N_CHIP)]
        for cp in cps:
            cp.start()
        for cp in cps:
            cp.wait()

    dma = pltpu.SemaphoreType.DMA
    return pl.pallas_call(
        body, name=name, out_shape=jax.ShapeDtypeStruct((N_CHIP, Rh, Wd), g.dtype), in_specs=[ANY], out_specs=ANY,
        scratch_shapes=[dma((N_CHIP,)), dma((N_CHIP,))], compiler_params=_cp(),
    )(g)


def chip_exchange(r, name):
    _, Rh, Wd = r.shape

    def body(x_ref, o_ref, ssem, rsem):
        x, y, c = _place()
        chips, qs = _other_chips(x, y)
        cps = [_rcopy(x_ref.at[qs[j]], o_ref.at[j], ssem.at[j], rsem.at[j], (*chips[j], c)) for j in range(3)]
        for cp in cps:
            cp.start()
        for cp in cps:
            cp.wait()

    dma = pltpu.SemaphoreType.DMA
    return pl.pallas_call(
        body, name=name, out_shape=jax.ShapeDtypeStruct((3, Rh, Wd), r.dtype), in_specs=[ANY], out_specs=ANY,
        scratch_shapes=[dma((3,)), dma((3,))], compiler_params=_cp(),
    )(r)


def sibling_share(f, name):
    Rh, Wd = f.shape

    def body(x_ref, o_ref, ssem, rsem):
        x, y, c = _place()
        cp = _rcopy(x_ref, o_ref.at[c], ssem, rsem, (x, y, 1 - c))
        cp.start()
        _rcopy(x_ref, o_ref.at[1 - c], ssem, rsem, (x, y, 1 - c)).wait_recv()
        cp.wait_send()

    dma = pltpu.SemaphoreType.DMA
    return pl.pallas_call(
        body, name=name, out_shape=jax.ShapeDtypeStruct((2, Rh, Wd), f.dtype), in_specs=[ANY], out_specs=ANY,
        scratch_shapes=[dma, dma], compiler_params=_cp(),
    )(f)


def fold_sum(g, recv, c_idx, name):
    _, _, Rh, Wd = g.shape
    tr = _tile(Rh, 512, 16)

    def body(c_ref, g_ref, r_ref, o_ref):
        del c_ref
        o_ref[0] = (g_ref[0, 0].astype(F32) + r_ref[0].astype(F32)).astype(o_ref.dtype)

    grid_spec = pltpu.PrefetchScalarGridSpec(
        num_scalar_prefetch=1, grid=(N_CHIP, Rh // tr),
        in_specs=[pl.BlockSpec((1, 1, tr, Wd), lambda s, i, c: (s, c[0], i, 0)), pl.BlockSpec((1, tr, Wd), lambda s, i, c: (s, i, 0))],
        out_specs=pl.BlockSpec((1, tr, Wd), lambda s, i, c: (s, i, 0)),
    )
    return pl.pallas_call(
        body, name=name, grid_spec=grid_spec, out_shape=jax.ShapeDtypeStruct((N_CHIP, Rh, Wd), BF16),
        compiler_params=_cp(("parallel", "parallel")),
    )(c_idx, g, recv)


def chip_sum(r, ex, q_idx, name):
    _, Rh, Wd = r.shape
    tr = _tile(Rh, 512, 16)

    def body(q_ref, r_ref, e_ref, o_ref):
        del q_ref
        o_ref[...] = ((r_ref[0].astype(F32) + e_ref[0].astype(F32)) + e_ref[1].astype(F32)) + e_ref[2].astype(F32)

    grid_spec = pltpu.PrefetchScalarGridSpec(
        num_scalar_prefetch=1, grid=(Rh // tr,),
        in_specs=[pl.BlockSpec((1, tr, Wd), lambda i, q: (q[0], i, 0)), pl.BlockSpec((3, tr, Wd), lambda i, q: (0, i, 0))],
        out_specs=pl.BlockSpec((tr, Wd), lambda i, q: (i, 0)),
    )
    return pl.pallas_call(
        body, name=name, grid_spec=grid_spec, out_shape=jax.ShapeDtypeStruct((Rh, Wd), F32), compiler_params=_cp(("parallel",)),
    )(q_idx, r, ex)


BIG = ("ffn_w_gu", "ffn_w_down", "fox_w_in", "fox_w_out", "sgu_w_in", "sgu_w_out", "swa_w_in", "swa_w_out")
COLUMN_SHARDED = ("ffn_w_gu", "fox_w_in", "sgu_w_in", "swa_w_in")
SMALL = ("ada_b", "mix_pre_g", "mix_post_g", "ffn_pre_g", "ffn_post_g", "fox_b_f", "sgu_ln_g", "sgu_ln_b", "sgu_w_s", "sgu_b_s",
         "swa_sinks")
WEIGHTS = ("ada_w", "ada_b", "mix_pre_g", "mix_post_g", "ffn_pre_g", "ffn_post_g", "ffn_w_gu", "ffn_w_down", "fox_w_in", "fox_b_f",
           "fox_w_out", "sgu_w_in", "sgu_ln_g", "sgu_ln_b", "sgu_w_s", "sgu_b_s", "sgu_w_out", "swa_w_in", "swa_sinks", "swa_w_out")
INPUTS = ("x", "c", "positions") + WEIGHTS + ("loss_target",) + tuple("m_" + n for n in WEIGHTS) + tuple("v_" + n for n in WEIGHTS)


def _half_rows(shape):
    L, A, B = shape
    n = L * A * B // 2
    assert n % PACK_W == 0 and (L % 2 == 0 or (L == 1 and A % 2 == 0)), shape
    return n // PACK_W


def _to_halves(shard):
    return shard.reshape(2, _half_rows(shard.shape), PACK_W)


def _shards_of(full, name):
    L, A, B = full.shape
    if name in COLUMN_SHARDED:
        return full.reshape(L, A, N_CHIP, B // N_CHIP).transpose(2, 0, 1, 3)
    return full.reshape(L, N_CHIP, A // N_CHIP, B).transpose(1, 0, 2, 3)


def _full_of(shards, name):
    _, L, A, B = shards.shape
    if name in COLUMN_SHARDED:
        return shards.transpose(1, 2, 0, 3).reshape(L, A, N_CHIP * B)
    return shards.transpose(1, 0, 2, 3).reshape(L, N_CHIP * A, B)


def _pad_rows(flat1d):
    n = flat1d.shape[0]
    pad = (-n) % (8 * LANES)
    return jnp.pad(flat1d, (0, pad)).reshape(-1, LANES)


def _pack_small(parts):
    return jnp.concatenate([_pad_rows(parts[n].astype(F32).reshape(-1)) for n in SMALL], axis=0)


def _unpack_small(packed, shapes):
    out, off = {}, 0
    for n in SMALL:
        size = int(np.prod(shapes[n]))
        rows = (size + 8 * LANES - 1) // (8 * LANES) * 8
        out[n] = packed[off:off + rows].reshape(-1)[:size].reshape(shapes[n])
        off += rows
    return out


def _fox_fwd(h, w_in, b_f, w_out, tag):
    S, D = h.shape
    H = b_f.shape[0]
    qkv = mm(h, w_in, "nn", BF16, name=tag + "_qkv", b_cols=(0, 3 * D))
    fgp = mm(h, w_in, "nn", F32, name=tag + "_fg", b_cols=(3 * D, LANES))
    fgT = fgp[:, :H].T
    cum = fox_gate_fwd(fgT, b_f.reshape(H, 1), tag + "_gate")
    cum_col, cum_row = cum.reshape(H, S, 1), cum.reshape(H, 1, S)
    o, lse = fox_attn_fwd(qkv, cum_col, cum_row, H, tag + "_attn")
    y = mm(o, w_out, "nn", F32, name=tag + "_out")
    return y, (qkv, fgT, cum_col, cum_row, o, lse)


def _fox_bwd(dy, h, w_in, b_f, w_out, ctx, tag):
    qkv, fgT, cum_col, cum_row, o, lse = ctx
    S, D = h.shape
    H = b_f.shape[0]
    do = mm(dy, w_out, "nt", BF16, name=tag + "_do")
    dw_out = mm(o, dy, "tn", F32, name=tag + "_dwout")
    dq, delta = fox_attn_bwd_dq(qkv, do, o, lse, cum_col, cum_row, H, tag + "_dq")
    dk, dv, dcum = fox_attn_bwd_dkv(qkv, do, lse.reshape(H, 1, S), delta.reshape(H, 1, S), cum_col, cum_row, H, tag + "_dkv")
    dfgT, db = fox_gate_bwd(dcum.reshape(H, S), fgT, b_f.reshape(H, 1), tag + "_dgate")
    dfgp = jnp.pad(dfgT.T, ((0, 0), (0, LANES - H))).astype(BF16)
    dh = mm(dq, w_in, "nt", F32, name=tag + "_dhq", b_cols=(0, D))
    dh = mm(dk, w_in, "nt", F32, add=dh, name=tag + "_dhk", b_cols=(D, D))
    dh = mm(dv, w_in, "nt", F32, add=dh, name=tag + "_dhv", b_cols=(2 * D, D))
    dh = mm(dfgp, w_in, "nt", F32, add=dh, name=tag + "_dhf", b_cols=(3 * D, LANES))
    dw_in = jnp.concatenate(
        [mm(h, dq, "tn", F32, name=tag + "_dwq"), mm(h, dk, "tn", F32, name=tag + "_dwk"), mm(h, dv, "tn", F32, name=tag + "_dwv"),
         mm(h, dfgp, "tn", F32, name=tag + "_dwf")[:, :H]], axis=1)
    return dh, dw_in, dw_out, db[:, 0]


def _sgu_consts(G, W):
    return jnp.asarray(np.repeat(np.eye(G, dtype=np.float32), W // G, axis=1))


def _sgu_fwd(h, w_in, ln_g, ln_b, w_s, b_s, w_out, tag):
    G, W = w_s.shape[0], ln_g.shape[0]
    zp = mm(h, w_in, "nn", F32, name=tag + "_in")
    args = (zp, ln_g.reshape(1, W), ln_b.reshape(1, W), w_s, b_s.T, _sgu_consts(G, W))
    gated = sgu_fwd(*args, tag + "_core")
    y = mm(gated, w_out, "nn", F32, name=tag + "_out")
    return y, (args, gated)


def _sgu_bwd(dy, h, w_in, w_out, ctx, tag):
    args, gated = ctx
    dgated = mm(dy, w_out, "nt", BF16, name=tag + "_dgated")
    dw_out = mm(gated, dy, "tn", F32, name=tag + "_dwout")
    dzp, dlg, dlb, dws, dbsT = sgu_bwd(*args, dgated, tag + "_dcore")
    dh = mm(dzp, w_in, "nt", F32, name=tag + "_dh")
    dw_in = mm(h, dzp, "tn", F32, name=tag + "_dwin")
    return dh, dw_in, dw_out, dlg[0], dlb[0], dws, dbsT.T


def _rope_tables(positions):
    inv = ROPE_THETA ** (-jnp.arange(0, ROPE_DIM, 2, dtype=F32) / ROPE_DIM)
    ang = positions.astype(F32)[:, None] * inv
    S = positions.shape[0]
    rest = SWA_HEAD_DIM - ROPE_DIM
    C = jnp.concatenate([jnp.cos(ang), jnp.cos(ang), jnp.ones((S, rest), F32)], axis=1)
    Sn = jnp.concatenate([jnp.sin(ang), jnp.sin(ang), jnp.zeros((S, rest), F32)], axis=1)
    return C, Sn


def _heads(t, n):
    return t.reshape(t.shape[0], n, SWA_HEAD_DIM).transpose(1, 0, 2)


def _unheads(t):
    return t.transpose(1, 0, 2).reshape(t.shape[1], -1)


def _swa_fwd(h, w_in, sinks, w_out, tables, tag):
    Hq = sinks.shape[0]
    Hk = (w_in.shape[1] // SWA_HEAD_DIM - Hq) // 2
    G = Hq // Hk
    proj = mm(h, w_in, "nn", F32, name=tag + "_in")
    qh = _heads(proj[:, :Hq * SWA_HEAD_DIM], Hq)
    kh = _heads(proj[:, Hq * SWA_HEAD_DIM:(Hq + Hk) * SWA_HEAD_DIM], Hk)
    vh = _heads(proj[:, (Hq + Hk) * SWA_HEAD_DIM:], Hk)
    sink_col = jnp.repeat(sinks.reshape(Hk, G), SEQ_BLOCK, axis=1).reshape(Hk, G * SEQ_BLOCK, 1)
    args = (qh, kh, vh, sink_col, tables[0], tables[1], jnp.asarray(_rope_matrix()))
    o = _unheads(swa_fwd(*args, tag + "_core"))
    y = mm(o, w_out, "nn", F32, name=tag + "_out")
    return y, (args, o)


def _swa_bwd(dy, h, w_in, w_out, ctx, tag):
    args, o = ctx
    Hq = args[0].shape[0]
    do = mm(dy, w_out, "nt", BF16, name=tag + "_do")
    dw_out = mm(o, dy, "tn", F32, name=tag + "_dwout")
    dqh, dkp, dkc, dvp, dvc, dsk = swa_bwd(*args, _heads(do, Hq), tag + "_dcore")
    dk = shift_add(dkc, dkp, tag + "_dk")
    dv = shift_add(dvc, dvp, tag + "_dv")
    dproj = jnp.concatenate([_unheads(dqh), _unheads(dk), _unheads(dv)], axis=1).astype(BF16)
    dh = mm(dproj, w_in, "nt", F32, name=tag + "_dh")
    dw_in = mm(h, dproj, "tn", F32, name=tag + "_dwin")
    return dh, dw_in, dw_out, dsk[:, :, 0].reshape(Hq)


def kernel(x, c, positions, ada_w, ada_b, mix_pre_g, mix_post_g, ffn_pre_g, ffn_post_g, ffn_w_gu, ffn_w_down, fox_w_in, fox_b_f, fox_w_out, sgu_w_in, sgu_ln_g, sgu_ln_b, sgu_w_s, sgu_b_s, sgu_w_out, swa_w_in, swa_sinks, swa_w_out, loss_target, m_ada_w, m_ada_b, m_mix_pre_g, m_mix_post_g, m_ffn_pre_g, m_ffn_post_g, m_ffn_w_gu, m_ffn_w_down, m_fox_w_in, m_fox_b_f, m_fox_w_out, m_sgu_w_in, m_sgu_ln_g, m_sgu_ln_b, m_sgu_w_s, m_sgu_b_s, m_sgu_w_out, m_swa_w_in, m_swa_sinks, m_swa_w_out, v_ada_w, v_ada_b, v_mix_pre_g, v_mix_post_g, v_ffn_pre_g, v_ffn_post_g, v_ffn_w_gu, v_ffn_w_down, v_fox_w_in, v_fox_b_f, v_fox_w_out, v_sgu_w_in, v_sgu_ln_g, v_sgu_ln_b, v_sgu_w_s, v_sgu_b_s, v_sgu_w_out, v_swa_w_in, v_swa_sinks, v_swa_w_out):
    P = dict(zip(INPUTS, (x, c, positions, ada_w, ada_b, mix_pre_g, mix_post_g, ffn_pre_g, ffn_post_g, ffn_w_gu, ffn_w_down, fox_w_in, fox_b_f, fox_w_out, sgu_w_in, sgu_ln_g, sgu_ln_b, sgu_w_s, sgu_b_s, sgu_w_out, swa_w_in, swa_sinks, swa_w_out, loss_target, m_ada_w, m_ada_b, m_mix_pre_g, m_mix_post_g, m_ffn_pre_g, m_ffn_post_g, m_ffn_w_gu, m_ffn_w_down, m_fox_w_in, m_fox_b_f, m_fox_w_out, m_sgu_w_in, m_sgu_ln_g, m_sgu_ln_b, m_sgu_w_s, m_sgu_b_s, m_sgu_w_out, m_swa_w_in, m_swa_sinks, m_swa_w_out, v_ada_w, v_ada_b, v_mix_pre_g, v_mix_post_g, v_ffn_pre_g, v_ffn_post_g, v_ffn_w_gu, v_ffn_w_down, v_fox_w_in, v_fox_b_f, v_fox_w_out, v_sgu_w_in, v_sgu_ln_g, v_sgu_ln_b, v_sgu_w_s, v_sgu_b_s, v_sgu_w_out, v_swa_w_in, v_swa_sinks, v_swa_w_out)))
    xs, target, pos = x[0], loss_target[0], positions[0]
    S, D = xs.shape
    L = ada_w.shape[0]
    n_mix = 3
    F = ffn_w_down.shape[1] * N_CHIP
    xi, yi, ci = _place()
    q_me = 2 * xi + yi
    dev = 4 * xi + 2 * yi + ci

    halves = [_to_halves(P[n].astype(BF16)) for n in BIG]
    pad_rows = (-sum(h.shape[1] for h in halves)) % EXCHANGE_ROW_MULTIPLE
    flat = jnp.concatenate(halves + [jnp.zeros((2, pad_rows, PACK_W), BF16)], axis=1)
    gathered = lax.dynamic_update_index_in_dim(ag_weights(flat, "ag_weights"), flat, q_me, axis=0)
    Wt, off = {}, 0
    for n in BIG:
        rows = _half_rows(P[n].shape)
        Wt[n] = _full_of(gathered[:, :, off:off + rows].reshape((N_CHIP,) + P[n].shape), n)
        off += rows
    fox_pad = 3 * D + LANES - Wt["fox_w_in"].shape[2]
    Wt["fox_w_in"] = jnp.pad(Wt["fox_w_in"], ((0, 0), (0, 0), (0, fox_pad)))

    c_all = ag_small(c.reshape(D // LANES, LANES), "ag_c").reshape(N_DEV, D)
    c_all = jnp.pad(c_all, ((0, 16 - N_DEV), (0, 0)))
    Nm = ada_w.shape[2]
    ada_b_mine = lax.dynamic_slice_in_dim(ada_b, q_me * Nm, Nm, axis=1).reshape(L, 1, Nm)
    modp = ada_fwd(c_all, ada_w, ada_b_mine, "ada_fwd")[:, :N_DEV]
    mod_all = ag_small(modp.reshape(-1, LANES), "ag_mod").reshape(N_DEV, L, N_DEV, Nm)
    mod_mine = lax.dynamic_index_in_dim(mod_all[0::2], dev, axis=2, keepdims=False)
    mods = mod_mine.transpose(1, 0, 2).reshape(L, 6, 1, D)

    tables = _rope_tables(pos)

    saved = []
    xc = xs
    for i in range(L):
        kind, j = i % n_mix, i // n_mix
        sh_m, sc_m, g_m, sh_f, sc_f, g_f = (mods[i, t] for t in range(6))
        t = f"l{i}"
        h1 = pre_fwd(xc, mix_pre_g[i:i + 1], sh_m, sc_m, t + "_pre_m")
        if kind == 0:
            y1, ctx = _fox_fwd(h1, Wt["fox_w_in"][j], fox_b_f[j], Wt["fox_w_out"][j], t + "_fox")
        elif kind == 1:
            y1, ctx = _sgu_fwd(h1, Wt["sgu_w_in"][j], sgu_ln_g[j], sgu_ln_b[j], sgu_w_s[j], sgu_b_s[j], Wt["sgu_w_out"][j], t + "_sgu")
        else:
            y1, ctx = _swa_fwd(h1, Wt["swa_w_in"][j], swa_sinks[j], Wt["swa_w_out"][j], tables, t + "_swa")
        xm = post_fwd(xc, y1, mix_post_g[i:i + 1], g_m, t + "_post_m")
        h2 = pre_fwd(xm, ffn_pre_g[i:i + 1], sh_f, sc_f, t + "_pre_f")
        gg = mm(h2, Wt["ffn_w_gu"][i], "nn", BF16, name=t + "_ffn_g", b_cols=(0, F))
        uu = mm(h2, Wt["ffn_w_gu"][i], "nn", BF16, name=t + "_ffn_u", b_cols=(F, F))
        a = act_fwd(gg, uu, t + "_act")
        y2 = mm(a, Wt["ffn_w_down"][i], "nn", F32, name=t + "_ffn_down")
        xn = post_fwd(xm, y2, ffn_post_g[i:i + 1], g_f, t + "_post_f")
        saved.append((xc, h1, y1, ctx, xm, h2, gg, uu, a, y2))
        xc = xn

    sq, dx = loss_head(xc, target, "loss_head")
    loss = lax.psum(sq[0, 0] * (0.5 / D), ("x", "y", "c"))

    big_g = {n: [None] * P[n].shape[0] for n in BIG}
    small_g = {n: [None] * P[n].shape[0] for n in SMALL}
    for i in reversed(range(L)):
        kind, j = i % n_mix, i // n_mix
        sh_m, sc_m, g_m, sh_f, sc_f, g_f = (mods[i, t] for t in range(6))
        xc, h1, y1, ctx, xm, h2, gg, uu, a, y2 = saved[i]
        t = f"l{i}"
        dy2, dgpost_f, dgate_f = post_bwd(y2, ffn_post_g[i:i + 1], g_f, dx, t + "_dpost_f")
        da = mm(dy2, Wt["ffn_w_down"][i], "nt", F32, name=t + "_da")
        big_g["ffn_w_down"][i] = mm(a, dy2, "tn", F32, name=t + "_dwdown")
        dgg, duu = act_bwd(gg, uu, da, t + "_dact")
        dh2 = mm(dgg, Wt["ffn_w_gu"][i], "nt", F32, name=t + "_dh2g", b_cols=(0, F))
        dh2 = mm(duu, Wt["ffn_w_gu"][i], "nt", F32, add=dh2, name=t + "_dh2u", b_cols=(F, F))
        big_g["ffn_w_gu"][i] = jnp.concatenate(
            [mm(h2, dgg, "tn", F32, name=t + "_dwg"), mm(h2, duu, "tn", F32, name=t + "_dwu")], axis=1)
        dxm, dgpre_f, dsh_f, dsc_f = pre_bwd(xm, ffn_pre_g[i:i + 1], sh_f, sc_f, dh2, dx, t + "_dpre_f")
        dy1, dgpost_m, dgate_m = post_bwd(y1, mix_post_g[i:i + 1], g_m, dxm, t + "_dpost_m")
        if kind == 0:
            dh1, dw_in, dw_out, db = _fox_bwd(dy1, h1, Wt["fox_w_in"][j], fox_b_f[j], Wt["fox_w_out"][j], ctx, t + "_fox")
            big_g["fox_w_in"][j], big_g["fox_w_out"][j], small_g["fox_b_f"][j] = dw_in, dw_out, db
        elif kind == 1:
            dh1, dw_in, dw_out, dlg, dlb, dws, dbs = _sgu_bwd(dy1, h1, Wt["sgu_w_in"][j], Wt["sgu_w_out"][j], ctx, t + "_sgu")
            big_g["sgu_w_in"][j], big_g["sgu_w_out"][j] = dw_in, dw_out
            small_g["sgu_ln_g"][j], small_g["sgu_ln_b"][j], small_g["sgu_w_s"][j], small_g["sgu_b_s"][j] = dlg, dlb, dws, dbs
        else:
            dh1, dw_in, dw_out, dsk = _swa_bwd(dy1, h1, Wt["swa_w_in"][j], Wt["swa_w_out"][j], ctx, t + "_swa")
            big_g["swa_w_in"][j], big_g["swa_w_out"][j], small_g["swa_sinks"][j] = dw_in, dw_out, dsk
        dx, dgpre_m, dsh_m, dsc_m = pre_bwd(xc, mix_pre_g[i:i + 1], sh_m, sc_m, dh1, dxm, t + "_dpre_m")
        small_g["ada_b"][i] = jnp.concatenate([dsh_m, dsc_m, dgate_m, dsh_f, dsc_f, dgate_f], axis=1)[0]
        small_g["mix_pre_g"][i], small_g["mix_post_g"][i] = dgpre_m[0], dgpost_m[0]
        small_g["ffn_pre_g"][i], small_g["ffn_post_g"][i] = dgpre_f[0], dgpost_f[0]
    grad_x = dx[None]

    shapes = {n: P[n].shape for n in SMALL}
    small_parts = ag_small(_pack_small({n: jnp.stack(small_g[n]) for n in SMALL}), "ag_small_grads")
    sg, sd, sm, sv = adamw(_pack_small({n: P[n] for n in SMALL}), _pack_small({n: P["m_" + n] for n in SMALL}),
                           _pack_small({n: P["v_" + n] for n in SMALL}), [small_parts], "adamw_small", gstack=N_DEV)
    out_g, out_d, out_m, out_v = (_unpack_small(t, shapes) for t in (sg, sd, sm, sv))

    dmod_all = small_parts[:, :L * 6 * D // LANES].reshape(N_DEV, L, 6 * D)
    dmod_mine = lax.dynamic_slice_in_dim(dmod_all, q_me * Nm, Nm, axis=2).transpose(1, 0, 2)
    dmod_mine = jnp.pad(dmod_mine, ((0, 0), (0, 16 - N_DEV), (0, 0)))
    g_ada = ada_wgrad(c_all, dmod_mine, "ada_wgrad")
    r2 = lambda t: t.reshape(-1, t.shape[-1])
    res = adamw(r2(ada_w), r2(m_ada_w), r2(v_ada_w), [r2(g_ada)], "adamw_ada_w")
    out_g["ada_w"], out_d["ada_w"], out_m["ada_w"], out_v["ada_w"] = (t.reshape(ada_w.shape) for t in res)

    gflat = jnp.concatenate(
        [_shards_of(jnp.stack(big_g[n]), n).reshape(N_CHIP, 2, _half_rows(P[n].shape), PACK_W).astype(BF16) for n in BIG]
        + [jnp.zeros((N_CHIP, 2, pad_rows, PACK_W), BF16)], axis=2)
    from_sibling = sibling_fold(gflat, "rs_fold")
    chip_part = fold_sum(gflat, from_sibling, ci.reshape(1).astype(jnp.int32), "rs_fold_sum")
    from_chips = chip_exchange(chip_part, "rs_exchange")
    mine = chip_sum(chip_part, from_chips, q_me.reshape(1).astype(jnp.int32), "rs_chip_sum")
    gboth = lax.dynamic_update_index_in_dim(sibling_share(mine, "rs_share"), mine, ci, axis=0)
    off = 0
    for n in BIG:
        rows = _half_rows(P[n].shape)
        gsh = gboth[:, off:off + rows].reshape(P[n].shape)
        off += rows
        res = adamw(r2(P[n]), r2(P["m_" + n]), r2(P["v_" + n]), [r2(gsh)], "adamw_" + n)
        out_g[n], out_d[n], out_m[n], out_v[n] = (t.reshape(P[n].shape) for t in res)

    return (loss, grad_x, *[out_g[n] for n in WEIGHTS], *[out_d[n] for n in WEIGHTS], *[out_m[n] for n in WEIGHTS],
            *[out_v[n] for n in WEIGHTS])
```

```python
import functools

import numpy as np
import jax
import jax.numpy as jnp
from jax import lax
from jax.experimental import pallas as pl
from jax.experimental.pallas import tpu as pltpu

F32 = jnp.float32
BF16 = jnp.bfloat16
MESH = pl.DeviceIdType.MESH

EPS = 1e-6
NEG = -1e30
V7X_VMEM_BYTES = 64 * 1024 * 1024
VMEM_LIMIT = V7X_VMEM_BYTES - 8 * 1024 * 1024
LANES = 128
SEQ_BLOCK = 128
SWA_HEAD_DIM = 64
ROPE_DIM = SWA_HEAD_DIM // 4
ROPE_THETA = 500000.0

ADAM_LR = 0.001
ADAM_B1 = 0.9
ADAM_B2 = 0.999
ADAM_EPS = 1e-08
ADAM_WD = 0.01
ADAM_STEP = 10


def _cp(sem=None, **kw):
    return pltpu.CompilerParams(dimension_semantics=sem, vmem_limit_bytes=VMEM_LIMIT, **kw)


def _tile(dim, pref, mult=LANES):
    if dim <= pref:
        return dim
    t = (pref // mult) * mult
    while t >= mult:
        if dim % t == 0:
            return t
        t -= mult
    return dim


_DIMS = {"nn": (((1,), (0,)), ((), ())), "nt": (((1,), (1,)), ((), ())), "tn": (((0,), (0,)), ((), ()))}


MM_TILES = {"nn": (1024, 512, 2816), "nt": (1024, 1024, 2816), "tn": (512, 512, 4096)}


def mm(a, b, mode="nn", out_dtype=F32, add=None, name="mm", b_cols=None, tm=None, tn=None, tk=None):
    tm, tn, tk = (d if t is None else t for t, d in zip((tm, tn, tk), MM_TILES[mode]))
    c0 = 0
    if b_cols is not None:
        c0, csize = b_cols
    if mode == "nn":
        (M, K), (K2, N) = a.shape, b.shape
        if b_cols is not None:
            N = csize
    elif mode == "nt":
        (M, K), (N, K2) = a.shape, b.shape
        if b_cols is not None:
            K2 = csize
    else:
        (K, M), (K2, N) = a.shape, b.shape
        assert b_cols is None
    assert K == K2, (a.shape, b.shape, mode)
    tm = _tile(M, tm, LANES if mode == "tn" else 16)
    tn = _tile(N, tn)
    tk = _tile(K, tk, LANES if mode != "tn" else 16)
    nk = K // tk
    if b_cols is not None:
        assert c0 % (tn if mode == "nn" else tk) == 0, (b_cols, tn, tk)
    bo = c0 // (tn if mode == "nn" else tk)
    dims = _DIMS[mode]
    has_add = add is not None

    def body(a_ref, b_ref, *rest):
        if has_add:
            add_ref, o_ref, acc_ref = rest
        else:
            o_ref, acc_ref = rest
        k = pl.program_id(2)
        p = lax.dot_general(a_ref[...].astype(BF16), b_ref[...].astype(BF16), dims, preferred_element_type=F32)

        @pl.when(k == 0)
        def _():
            acc_ref[...] = p + add_ref[...].astype(F32) if has_add else p

        @pl.when(k > 0)
        def _():
            acc_ref[...] += p

        @pl.when(k == nk - 1)
        def _():
            o_ref[...] = acc_ref[...].astype(o_ref.dtype)

    a_spec = pl.BlockSpec((tk, tm), lambda i, j, k: (k, i)) if mode == "tn" else pl.BlockSpec((tm, tk), lambda i, j, k: (i, k))
    b_spec = pl.BlockSpec((tn, tk), lambda i, j, k: (j, k + bo)) if mode == "nt" else pl.BlockSpec((tk, tn), lambda i, j, k: (k, j + bo))
    o_spec = pl.BlockSpec((tm, tn), lambda i, j, k: (i, j))
    in_specs = [a_spec, b_spec] + ([o_spec] if has_add else [])
    args = (a, b) + ((add,) if has_add else ())
    return pl.pallas_call(
        body, name=name, grid=(M // tm, N // tn, nk), in_specs=in_specs, out_specs=o_spec,
        out_shape=jax.ShapeDtypeStruct((M, N), out_dtype), scratch_shapes=[pltpu.VMEM((tm, tn), F32)],
        compiler_params=_cp(("parallel", "parallel", "arbitrary")),
    )(*args)


def _rms(x, g):
    return (x * lax.rsqrt(jnp.mean(x * x, axis=-1, keepdims=True) + EPS)) * g


def _pre(x, g, sh, sc):
    return _rms(x, g) * (1 + sc) + sh


def _post(x, y, g, gate):
    return x + gate * _rms(y, g)


ROW_TILE = 256


def _row_spec(tr, d):
    return pl.BlockSpec((tr, d), lambda i: (i, 0))


def _vec_spec(d):
    return pl.BlockSpec((1, d), lambda i: (0, 0))


def pre_fwd(x, g, sh, sc, name):
    S, D = x.shape
    tr = _tile(S, ROW_TILE, 16)

    def body(x_ref, g_ref, sh_ref, sc_ref, h_ref):
        h_ref[...] = _pre(x_ref[...], g_ref[...], sh_ref[...], sc_ref[...]).astype(h_ref.dtype)

    return pl.pallas_call(
        body, name=name, grid=(S // tr,), in_specs=[_row_spec(tr, D)] + [_vec_spec(D)] * 3, out_specs=_row_spec(tr, D),
        out_shape=jax.ShapeDtypeStruct((S, D), BF16), compiler_params=_cp(("parallel",)),
    )(x, g, sh, sc)


def pre_bwd(x, g, sh, sc, dh, dres, name):
    S, D = x.shape
    tr = _tile(S, ROW_TILE, 16)

    def body(x_ref, g_ref, sh_ref, sc_ref, dh_ref, dres_ref, dx_ref, dg_ref, dsh_ref, dsc_ref):
        _, vjp = jax.vjp(_pre, x_ref[...], g_ref[...], sh_ref[...], sc_ref[...])
        dx, dg, dsh, dsc = vjp(dh_ref[...].astype(F32))
        dx_ref[...] = dres_ref[...] + dx

        @pl.when(pl.program_id(0) == 0)
        def _():
            dg_ref[...] = jnp.zeros_like(dg_ref)
            dsh_ref[...] = jnp.zeros_like(dsh_ref)
            dsc_ref[...] = jnp.zeros_like(dsc_ref)

        dg_ref[...] += dg
        dsh_ref[...] += dsh
        dsc_ref[...] += dsc

    vec = jax.ShapeDtypeStruct((1, D), F32)
    return pl.pallas_call(
        body, name=name, grid=(S // tr,), in_specs=[_row_spec(tr, D)] + [_vec_spec(D)] * 3 + [_row_spec(tr, D)] * 2,
        out_specs=[_row_spec(tr, D)] + [_vec_spec(D)] * 3, out_shape=[jax.ShapeDtypeStruct((S, D), F32), vec, vec, vec],
        compiler_params=_cp(("arbitrary",)),
    )(x, g, sh, sc, dh, dres)


def post_fwd(x, y, g, gate, name):
    S, D = x.shape
    tr = _tile(S, ROW_TILE, 16)

    def body(x_ref, y_ref, g_ref, gate_ref, o_ref):
        o_ref[...] = _post(x_ref[...], y_ref[...], g_ref[...], gate_ref[...])

    return pl.pallas_call(
        body, name=name, grid=(S // tr,), in_specs=[_row_spec(tr, D)] * 2 + [_vec_spec(D)] * 2, out_specs=_row_spec(tr, D),
        out_shape=jax.ShapeDtypeStruct((S, D), F32), compiler_params=_cp(("parallel",)),
    )(x, y, g, gate)


def post_bwd(y, g, gate, dxn, name):
    S, D = y.shape
    tr = _tile(S, ROW_TILE, 16)

    def body(y_ref, g_ref, gate_ref, dxn_ref, dy_ref, dg_ref, dgate_ref):
        fn = lambda yy, gg, gt: gt * _rms(yy, gg)
        _, vjp = jax.vjp(fn, y_ref[...], g_ref[...], gate_ref[...])
        dy, dg, dgate = vjp(dxn_ref[...])
        dy_ref[...] = dy.astype(dy_ref.dtype)

        @pl.when(pl.program_id(0) == 0)
        def _():
            dg_ref[...] = jnp.zeros_like(dg_ref)
            dgate_ref[...] = jnp.zeros_like(dgate_ref)

        dg_ref[...] += dg
        dgate_ref[...] += dgate

    vec = jax.ShapeDtypeStruct((1, D), F32)
    return pl.pallas_call(
        body, name=name, grid=(S // tr,), in_specs=[_row_spec(tr, D)] + [_vec_spec(D)] * 2 + [_row_spec(tr, D)],
        out_specs=[_row_spec(tr, D)] + [_vec_spec(D)] * 2, out_shape=[jax.ShapeDtypeStruct((S, D), BF16), vec, vec],
        compiler_params=_cp(("arbitrary",)),
    )(y, g, gate, dxn)


def _swiglu(g, u):
    return jax.nn.silu(g) * u


ACT_ROWS = 256


def act_fwd(gu, name):
    S, F2 = gu.shape
    F = F2 // 2
    tr = _tile(S, ACT_ROWS, 16)

    def body(gu_ref, a_ref):
        a_ref[...] = _swiglu(gu_ref[:, :F].astype(F32), gu_ref[:, F:].astype(F32)).astype(a_ref.dtype)

    return pl.pallas_call(
        body, name=name, grid=(S // tr,), in_specs=[_row_spec(tr, F2)], out_specs=_row_spec(tr, F),
        out_shape=jax.ShapeDtypeStruct((S, F), BF16), compiler_params=_cp(("parallel",)),
    )(gu)


def act_bwd(gu, da, name):
    S, F2 = gu.shape
    F = F2 // 2
    tr = _tile(S, ACT_ROWS, 16)

    def body(gu_ref, da_ref, dgu_ref):
        _, vjp = jax.vjp(_swiglu, gu_ref[:, :F].astype(F32), gu_ref[:, F:].astype(F32))
        dg, du = vjp(da_ref[...].astype(F32))
        dgu_ref[:, :F] = dg.astype(dgu_ref.dtype)
        dgu_ref[:, F:] = du.astype(dgu_ref.dtype)

    return pl.pallas_call(
        body, name=name, grid=(S // tr,), in_specs=[_row_spec(tr, F2), _row_spec(tr, F)], out_specs=_row_spec(tr, F2),
        out_shape=jax.ShapeDtypeStruct((S, F2), BF16), compiler_params=_cp(("parallel",)),
    )(gu, da)


def _tri(upper):
    r = lax.broadcasted_iota(jnp.int32, (LANES, LANES), 0)
    c = lax.broadcasted_iota(jnp.int32, (LANES, LANES), 1)
    return ((r <= c) if upper else (r >= c)).astype(F32)


def _hdot(a, b):
    return jnp.dot(a, b, precision=lax.Precision.HIGHEST, preferred_element_type=F32)


def fox_gate_fwd(fgT, b, name):
    H, S = fgT.shape
    spec = pl.BlockSpec((H, LANES), lambda ch: (0, ch))

    def body(fg_ref, b_ref, cum_ref, carry_ref):
        @pl.when(pl.program_id(0) == 0)
        def _():
            carry_ref[...] = jnp.zeros_like(carry_ref)

        lf = jax.nn.log_sigmoid(fg_ref[...] + b_ref[...])
        cum_ref[...] = _hdot(lf, _tri(True)) + carry_ref[...]
        carry_ref[...] += _hdot(lf, jnp.ones((LANES, LANES), F32))

    return pl.pallas_call(
        body, name=name, grid=(S // LANES,), in_specs=[spec, pl.BlockSpec((H, 1), lambda ch: (0, 0))], out_specs=spec,
        out_shape=jax.ShapeDtypeStruct((H, S), F32), scratch_shapes=[pltpu.VMEM((H, LANES), F32)], compiler_params=_cp(("arbitrary",)),
    )(fgT, b)


def fox_gate_bwd(dcum, fgT, b, name):
    H, S = fgT.shape
    nch = S // LANES
    spec = pl.BlockSpec((H, LANES), lambda t: (0, nch - 1 - t))

    def body(dcum_ref, fg_ref, b_ref, dfg_ref, db_ref, tail_ref):
        @pl.when(pl.program_id(0) == 0)
        def _():
            tail_ref[...] = jnp.zeros_like(tail_ref)
            db_ref[...] = jnp.zeros_like(db_ref)

        dlf = _hdot(dcum_ref[...], _tri(False)) + tail_ref[...]
        dfg = dlf * jax.nn.sigmoid(-(fg_ref[...] + b_ref[...]))
        dfg_ref[...] = dfg
        ones = jnp.ones((LANES, LANES), F32)
        tail_ref[...] += _hdot(dcum_ref[...], ones)
        db_ref[...] += _hdot(dfg, ones)

    return pl.pallas_call(
        body, name=name, grid=(nch,), in_specs=[spec, spec, pl.BlockSpec((H, 1), lambda t: (0, 0))],
        out_specs=[spec, pl.BlockSpec((H, LANES), lambda t: (0, 0))],
        out_shape=[jax.ShapeDtypeStruct((H, S), F32), jax.ShapeDtypeStruct((H, LANES), F32)],
        scratch_shapes=[pltpu.VMEM((H, LANES), F32)], compiler_params=_cp(("arbitrary",)),
    )(dcum, fgT, b)


FOX_TILE = 1024


def _on_and_below_diagonal(i, j, tile):
    @pl.when(j < i)
    def _():
        tile(False)

    @pl.when(j == i)
    def _():
        tile(True)


def _causal_pairs(n, by_key):
    pairs = [(i, j) for i in range(n) for j in range(i + 1)]
    if by_key:
        pairs.sort(key=lambda p: (p[1], p[0]))
    qi = np.asarray([p[0] for p in pairs], np.int32)
    kj = np.asarray([p[1] for p in pairs], np.int32)
    return qi, kj


def _fox_scores(q, k, fq, fk, T, scale, transposed):
    r = lax.broadcasted_iota(jnp.int32, (T, T), 0)
    c = lax.broadcasted_iota(jnp.int32, (T, T), 1)
    if transposed:
        return lax.dot_general(k, q, _DIMS["nt"], preferred_element_type=F32) * scale + (fq - fk), r <= c
    return lax.dot_general(q, k, _DIMS["nt"], preferred_element_type=F32) * scale + (fq - fk), c <= r


def fox_attn_fwd(qkv, cum_col, cum_row, H, name):
    S = qkv.shape[0]
    Dh = qkv.shape[1] // (3 * H)
    T = _tile(S, FOX_TILE)
    n = S // T
    qi, kj = _causal_pairs(n, by_key=False)
    scale = Dh ** -0.5

    def body(qi_ref, kj_ref, q_ref, k_ref, v_ref, fq_ref, fk_ref, o_ref, lse_ref, m_ref, l_ref, acc_ref):
        p_id = pl.program_id(1)
        i, j = qi_ref[p_id], kj_ref[p_id]

        @pl.when(j == 0)
        def _():
            m_ref[...] = jnp.full_like(m_ref, NEG)
            l_ref[...] = jnp.zeros_like(l_ref)
            acc_ref[...] = jnp.zeros_like(acc_ref)

        def tile(masked):
            s, mask = _fox_scores(q_ref[...], k_ref[...], fq_ref[0], fk_ref[0], T, scale, False)
            if masked:
                s = jnp.where(mask, s, NEG)
            m_new = jnp.maximum(m_ref[...], jnp.max(s, axis=1, keepdims=True))
            alpha = jnp.exp(m_ref[...] - m_new)
            p = jnp.exp(s - m_new)
            l_ref[...] = alpha * l_ref[...] + jnp.sum(p, axis=1, keepdims=True)
            acc_ref[...] = alpha * acc_ref[...] + jnp.dot(p.astype(BF16), v_ref[...], preferred_element_type=F32)
            m_ref[...] = m_new

        _on_and_below_diagonal(i, j, tile)

        @pl.when(j == i)
        def _():
            o_ref[...] = (acc_ref[...] / l_ref[...]).astype(o_ref.dtype)
            lse_ref[0] = m_ref[...] + jnp.log(l_ref[...])

    grid_spec = pltpu.PrefetchScalarGridSpec(
        num_scalar_prefetch=2, grid=(H, len(qi)),
        in_specs=[
            pl.BlockSpec((T, Dh), lambda h, p, qi, kj: (qi[p], h)),
            pl.BlockSpec((T, Dh), lambda h, p, qi, kj: (kj[p], H + h)),
            pl.BlockSpec((T, Dh), lambda h, p, qi, kj: (kj[p], 2 * H + h)),
            pl.BlockSpec((1, T, 1), lambda h, p, qi, kj: (h, qi[p], 0)),
            pl.BlockSpec((1, 1, T), lambda h, p, qi, kj: (h, 0, kj[p])),
        ],
        out_specs=[
            pl.BlockSpec((T, Dh), lambda h, p, qi, kj: (qi[p], h)),
            pl.BlockSpec((1, T, 1), lambda h, p, qi, kj: (h, qi[p], 0)),
        ],
        scratch_shapes=[pltpu.VMEM((T, 1), F32), pltpu.VMEM((T, 1), F32), pltpu.VMEM((T, Dh), F32)],
    )
    return pl.pallas_call(
        body, name=name, grid_spec=grid_spec,
        out_shape=[jax.ShapeDtypeStruct((S, H * Dh), F32), jax.ShapeDtypeStruct((H, S, 1), F32)],
        compiler_params=_cp(("parallel", "arbitrary")),
    )(jnp.asarray(qi), jnp.asarray(kj), qkv, qkv, qkv, cum_col, cum_row)


def fox_attn_bwd_dq(qkv, do, o, lse, cum_col, cum_row, H, name):
    S = qkv.shape[0]
    Dh = qkv.shape[1] // (3 * H)
    T = _tile(S, FOX_TILE)
    n = S // T
    qi, kj = _causal_pairs(n, by_key=False)
    scale = Dh ** -0.5

    def body(qi_ref, kj_ref, q_ref, k_ref, v_ref, do_ref, o_ref, lse_ref, fq_ref, fk_ref, dq_ref, delta_ref, acc_ref, dl_ref,
             rs_ref):
        p_id = pl.program_id(1)
        i, j = qi_ref[p_id], kj_ref[p_id]

        @pl.when(j == 0)
        def _():
            acc_ref[...] = jnp.zeros_like(acc_ref)
            rs_ref[...] = jnp.zeros_like(rs_ref)
            dl_ref[...] = jnp.sum(do_ref[...].astype(F32) * o_ref[...].astype(F32), axis=1, keepdims=True)

        def tile(masked):
            s, mask = _fox_scores(q_ref[...], k_ref[...], fq_ref[0], fk_ref[0], T, scale, False)
            p = jnp.exp(s - lse_ref[0])
            if masked:
                p = jnp.where(mask, p, 0.0)
            dp = lax.dot_general(do_ref[...], v_ref[...], _DIMS["nt"], preferred_element_type=F32)
            ds = p * (dp - dl_ref[...])
            rs_ref[...] += jnp.sum(ds, axis=1, keepdims=True)
            acc_ref[...] += jnp.dot(ds.astype(BF16), k_ref[...], preferred_element_type=F32)

        _on_and_below_diagonal(i, j, tile)

        @pl.when(j == i)
        def _():
            dq_ref[...] = (acc_ref[...] * scale).astype(dq_ref.dtype)
            delta_ref[0] = dl_ref[...] + rs_ref[...]

    qspec = pl.BlockSpec((T, Dh), lambda h, p, qi, kj: (qi[p], h))
    colspec = pl.BlockSpec((1, T, 1), lambda h, p, qi, kj: (h, qi[p], 0))
    grid_spec = pltpu.PrefetchScalarGridSpec(
        num_scalar_prefetch=2, grid=(H, len(qi)),
        in_specs=[
            qspec,
            pl.BlockSpec((T, Dh), lambda h, p, qi, kj: (kj[p], H + h)),
            pl.BlockSpec((T, Dh), lambda h, p, qi, kj: (kj[p], 2 * H + h)),
            qspec, qspec, colspec, colspec,
            pl.BlockSpec((1, 1, T), lambda h, p, qi, kj: (h, 0, kj[p])),
        ],
        out_specs=[qspec, colspec],
        scratch_shapes=[pltpu.VMEM((T, Dh), F32), pltpu.VMEM((T, 1), F32), pltpu.VMEM((T, 1), F32)],
    )
    return pl.pallas_call(
        body, name=name, grid_spec=grid_spec,
        out_shape=[jax.ShapeDtypeStruct((S, H * Dh), BF16), jax.ShapeDtypeStruct((H, S, 1), F32)],
        compiler_params=_cp(("parallel", "arbitrary")),
    )(jnp.asarray(qi), jnp.asarray(kj), qkv, qkv, qkv, do, o, lse, cum_col, cum_row)


def fox_attn_bwd_dkv(qkv, do, lse_row, delta_row, cum_col, cum_row, H, name):
    S = qkv.shape[0]
    Dh = qkv.shape[1] // (3 * H)
    T = _tile(S, FOX_TILE)
    n = S // T
    qi, kj = _causal_pairs(n, by_key=True)
    scale = Dh ** -0.5

    def body(qi_ref, kj_ref, q_ref, k_ref, v_ref, do_ref, lse_ref, dl_ref, fq_ref, fk_ref, dk_ref, dv_ref, dcum_ref,
             dk_acc, dv_acc, df_acc):
        p_id = pl.program_id(1)
        i, j = qi_ref[p_id], kj_ref[p_id]

        @pl.when(i == j)
        def _():
            dk_acc[...] = jnp.zeros_like(dk_acc)
            dv_acc[...] = jnp.zeros_like(dv_acc)
            df_acc[...] = jnp.zeros_like(df_acc)

        def tile(masked):
            sT, mask = _fox_scores(q_ref[...], k_ref[...], fq_ref[0], fk_ref[0], T, scale, True)
            pT = jnp.exp(sT - lse_ref[0])
            if masked:
                pT = jnp.where(mask, pT, 0.0)
            dv_acc[...] += jnp.dot(pT.astype(BF16), do_ref[...], preferred_element_type=F32)
            dpT = lax.dot_general(v_ref[...], do_ref[...], _DIMS["nt"], preferred_element_type=F32)
            dsT = pT * (dpT - dl_ref[0])
            dk_acc[...] += jnp.dot(dsT.astype(BF16), q_ref[...], preferred_element_type=F32)
            df_acc[...] -= jnp.sum(dsT, axis=1, keepdims=True)

        _on_and_below_diagonal(i, j, tile)

        @pl.when(i == n - 1)
        def _():
            dk_ref[...] = (dk_acc[...] * scale).astype(dk_ref.dtype)
            dv_ref[...] = dv_acc[...].astype(dv_ref.dtype)
            dcum_ref[0] = df_acc[...]

    qspec = pl.BlockSpec((T, Dh), lambda h, p, qi, kj: (qi[p], h))
    kspec = pl.BlockSpec((T, Dh), lambda h, p, qi, kj: (kj[p], H + h))
    vspec = pl.BlockSpec((T, Dh), lambda h, p, qi, kj: (kj[p], 2 * H + h))
    qrow = pl.BlockSpec((1, 1, T), lambda h, p, qi, kj: (h, 0, qi[p]))
    kcol = pl.BlockSpec((1, T, 1), lambda h, p, qi, kj: (h, kj[p], 0))
    grid_spec = pltpu.PrefetchScalarGridSpec(
        num_scalar_prefetch=2, grid=(H, len(qi)),
        in_specs=[qspec, kspec, vspec, qspec, qrow, qrow, qrow, kcol],
        out_specs=[pl.BlockSpec((T, Dh), lambda h, p, qi, kj: (kj[p], h))] * 2 + [kcol],
        scratch_shapes=[pltpu.VMEM((T, Dh), F32), pltpu.VMEM((T, Dh), F32), pltpu.VMEM((T, 1), F32)],
    )
    out = jax.ShapeDtypeStruct((S, H * Dh), BF16)
    return pl.pallas_call(
        body, name=name, grid_spec=grid_spec, out_shape=[out, out, jax.ShapeDtypeStruct((H, S, 1), F32)],
        compiler_params=_cp(("parallel", "arbitrary")),
    )(jnp.asarray(qi), jnp.asarray(kj), qkv, qkv, qkv, do, lse_row, delta_row, cum_row, cum_col)


def _sgu_ln(zu, zv, ln_g, ln_b):
    u = jax.nn.gelu(zu)
    v = jax.nn.gelu(zv)
    mu = jnp.mean(v, axis=-1, keepdims=True)
    var = jnp.mean(jnp.square(v - mu), axis=-1, keepdims=True)
    return u, (v - mu) * lax.rsqrt(var + EPS) * ln_g + ln_b


def _tril_mask():
    r = lax.broadcasted_iota(jnp.int32, (SEQ_BLOCK, SEQ_BLOCK), 0)
    c = lax.broadcasted_iota(jnp.int32, (SEQ_BLOCK, SEQ_BLOCK), 1)
    return r >= c


def _sgu_spatial(ws_ref, bsT, selT, vn, G):
    tril = _tril_mask()
    fs = []
    for g in range(G):
        wg = jnp.where(tril, ws_ref[g], 0.0).astype(BF16)
        fs.append(jnp.dot(wg, vn[:, g * SEQ_BLOCK:(g + 1) * SEQ_BLOCK].astype(BF16), preferred_element_type=F32))
    bias = jnp.dot(bsT, selT, precision=lax.Precision.HIGHEST, preferred_element_type=F32)
    return jnp.concatenate(fs, axis=1) + bias


def _sgu_specs(W, G):
    return [
        pl.BlockSpec((SEQ_BLOCK, 2 * W), lambda n: (n, 0)),
        pl.BlockSpec((1, W), lambda n: (0, 0)),
        pl.BlockSpec((1, W), lambda n: (0, 0)),
        pl.BlockSpec((G, SEQ_BLOCK, SEQ_BLOCK), lambda n: (0, 0, 0)),
        pl.BlockSpec((SEQ_BLOCK, G), lambda n: (0, 0)),
        pl.BlockSpec((G, W), lambda n: (0, 0)),
    ]


def sgu_fwd(zp, ln_g, ln_b, ws, bsT, selT, name):
    S, W2 = zp.shape
    W = W2 // 2
    G = ws.shape[0]

    def body(z_ref, lg_ref, lb_ref, ws_ref, bs_ref, sel_ref, o_ref):
        u, vn = _sgu_ln(z_ref[:, :W], z_ref[:, W:], lg_ref[...], lb_ref[...])
        o_ref[...] = (u * _sgu_spatial(ws_ref, bs_ref[...], sel_ref[...], vn, G)).astype(o_ref.dtype)

    return pl.pallas_call(
        body, name=name, grid=(S // SEQ_BLOCK,), in_specs=_sgu_specs(W, G), out_specs=pl.BlockSpec((SEQ_BLOCK, W), lambda n: (n, 0)),
        out_shape=jax.ShapeDtypeStruct((S, W), BF16), compiler_params=_cp(("parallel",)),
    )(zp, ln_g, ln_b, ws, bsT, selT)


def sgu_bwd(zp, ln_g, ln_b, ws, bsT, selT, dgated, name):
    S, W2 = zp.shape
    W = W2 // 2
    G = ws.shape[0]

    def body(z_ref, lg_ref, lb_ref, ws_ref, bs_ref, sel_ref, dgt_ref, dz_ref, dlg_ref, dlb_ref, dws_ref, dbs_ref):
        (u, vn), vjp = jax.vjp(_sgu_ln, z_ref[:, :W], z_ref[:, W:], lg_ref[...], lb_ref[...])
        f = _sgu_spatial(ws_ref, bs_ref[...], sel_ref[...], vn, G)
        dgt = dgt_ref[...].astype(F32)
        du, df = dgt * f, dgt * u

        @pl.when(pl.program_id(0) == 0)
        def _():
            dlg_ref[...] = jnp.zeros_like(dlg_ref)
            dlb_ref[...] = jnp.zeros_like(dlb_ref)
            dws_ref[...] = jnp.zeros_like(dws_ref)
            dbs_ref[...] = jnp.zeros_like(dbs_ref)

        dbs_ref[...] += lax.dot_general(df, sel_ref[...], _DIMS["nt"], precision=lax.Precision.HIGHEST, preferred_element_type=F32)
        tril = _tril_mask()
        dvn = []
        for g in range(G):
            sl = slice(g * SEQ_BLOCK, (g + 1) * SEQ_BLOCK)
            wg = jnp.where(tril, ws_ref[g], 0.0).astype(BF16)
            df_g = df[:, sl].astype(BF16)
            dw = lax.dot_general(df_g, vn[:, sl].astype(BF16), _DIMS["nt"], preferred_element_type=F32)
            dws_ref[g] += jnp.where(tril, dw, 0.0)
            dvn.append(lax.dot_general(wg, df_g, _DIMS["tn"], preferred_element_type=F32))
        dzu, dzv, dlg, dlb = vjp((du, jnp.concatenate(dvn, axis=1)))
        dz_ref[:, :W] = dzu.astype(dz_ref.dtype)
        dz_ref[:, W:] = dzv.astype(dz_ref.dtype)
        dlg_ref[...] += dlg
        dlb_ref[...] += dlb

    vec = jax.ShapeDtypeStruct((1, W), F32)
    return pl.pallas_call(
        body, name=name, grid=(S // SEQ_BLOCK,),
        in_specs=_sgu_specs(W, G) + [pl.BlockSpec((SEQ_BLOCK, W), lambda n: (n, 0))],
        out_specs=[
            pl.BlockSpec((SEQ_BLOCK, 2 * W), lambda n: (n, 0)),
            pl.BlockSpec((1, W), lambda n: (0, 0)),
            pl.BlockSpec((1, W), lambda n: (0, 0)),
            pl.BlockSpec((G, SEQ_BLOCK, SEQ_BLOCK), lambda n: (0, 0, 0)),
            pl.BlockSpec((SEQ_BLOCK, G), lambda n: (0, 0)),
        ],
        out_shape=[jax.ShapeDtypeStruct((S, W2), BF16), vec, vec, jax.ShapeDtypeStruct(ws.shape, F32), jax.ShapeDtypeStruct((SEQ_BLOCK, G), F32)],
        compiler_params=_cp(("arbitrary",)),
    )(zp, ln_g, ln_b, ws, bsT, selT, dgated)


def _rope_matrix():
    half = ROPE_DIM // 2
    R = np.zeros((SWA_HEAD_DIM, SWA_HEAD_DIM), np.float32)
    for j in range(half):
        R[j + half, j] = -1.0
        R[j, j + half] = 1.0
    return R


def _swa_block(q4, kp, kc, vp, vc, sink, Cq, Sq, Cp, Sp, R, n, G):
    B, Dh = SEQ_BLOCK, SWA_HEAD_DIM
    rot = lambda t: jnp.dot(t, R, precision=lax.Precision.HIGHEST, preferred_element_type=F32)
    q = q4.reshape(G * B, Dh)
    Cq4 = jnp.concatenate([Cq] * G, axis=0)
    Sq4 = jnp.concatenate([Sq] * G, axis=0)
    qr = q * Cq4 + rot(q) * Sq4
    kb = jnp.concatenate([kp * Cp + rot(kp) * Sp, kc * Cq + rot(kc) * Sq], axis=0)
    vb = jnp.concatenate([vp, vc], axis=0)
    s = lax.dot_general(qr.astype(BF16), kb.astype(BF16), _DIMS["nt"], preferred_element_type=F32) * (Dh ** -0.5)
    qi = lax.broadcasted_iota(jnp.int32, (G * B, 2 * B), 0) & (B - 1)
    ki = lax.broadcasted_iota(jnp.int32, (G * B, 2 * B), 1) - B
    rel = qi - ki
    valid = (rel >= 0) & (rel < B) & (n * B + ki >= 0)
    s = jnp.where(valid, s, NEG)
    m = lax.stop_gradient(jnp.maximum(jnp.max(s, axis=1, keepdims=True), sink))
    p = jnp.exp(s - m)
    p = p / (jnp.sum(p, axis=1, keepdims=True) + jnp.exp(sink - m))
    o = jnp.dot(p.astype(BF16), vb.astype(BF16), preferred_element_type=F32)
    return o.reshape(G, B, Dh)


def _swa_specs(G):
    B, Dh = SEQ_BLOCK, SWA_HEAD_DIM
    prev = lambda n: jnp.maximum(n - 1, 0)
    return [
        pl.BlockSpec((G, B, Dh), lambda h, n: (h, n, 0)),
        pl.BlockSpec((1, B, Dh), lambda h, n: (h, prev(n), 0)),
        pl.BlockSpec((1, B, Dh), lambda h, n: (h, n, 0)),
        pl.BlockSpec((1, B, Dh), lambda h, n: (h, prev(n), 0)),
        pl.BlockSpec((1, B, Dh), lambda h, n: (h, n, 0)),
        pl.BlockSpec((1, G * B, 1), lambda h, n: (h, 0, 0)),
        pl.BlockSpec((B, Dh), lambda h, n: (n, 0)),
        pl.BlockSpec((B, Dh), lambda h, n: (n, 0)),
        pl.BlockSpec((B, Dh), lambda h, n: (prev(n), 0)),
        pl.BlockSpec((B, Dh), lambda h, n: (prev(n), 0)),
        pl.BlockSpec((Dh, Dh), lambda h, n: (0, 0)),
    ]


def swa_fwd(qh, kh, vh, sink_col, C, Sn, R, name):
    Hq, S, Dh = qh.shape
    Hk = kh.shape[0]
    G = Hq // Hk

    def body(q_ref, kp_ref, kc_ref, vp_ref, vc_ref, sk_ref, cq_ref, sq_ref, cp_ref, sp_ref, r_ref, o_ref):
        o = _swa_block(q_ref[...], kp_ref[0], kc_ref[0], vp_ref[0], vc_ref[0], sk_ref[0], cq_ref[...], sq_ref[...], cp_ref[...],
                       sp_ref[...], r_ref[...], pl.program_id(1), G)
        o_ref[...] = o.astype(o_ref.dtype)

    return pl.pallas_call(
        body, name=name, grid=(Hk, S // SEQ_BLOCK), in_specs=_swa_specs(G),
        out_specs=pl.BlockSpec((G, SEQ_BLOCK, Dh), lambda h, n: (h, n, 0)),
        out_shape=jax.ShapeDtypeStruct((Hq, S, Dh), BF16), compiler_params=_cp(("parallel", "parallel")),
    )(qh, kh, kh, vh, vh, sink_col, C, Sn, C, Sn, R)


def swa_bwd(qh, kh, vh, sink_col, C, Sn, R, doh, name):
    Hq, S, Dh = qh.shape
    Hk = kh.shape[0]
    G = Hq // Hk
    B = SEQ_BLOCK

    def body(q_ref, kp_ref, kc_ref, vp_ref, vc_ref, sk_ref, cq_ref, sq_ref, cp_ref, sp_ref, r_ref, do_ref,
             dq_ref, dkp_ref, dkc_ref, dvp_ref, dvc_ref, dsk_ref):
        n = pl.program_id(1)
        fn = lambda q4, kp, kc, vp, vc, sk: _swa_block(q4, kp, kc, vp, vc, sk, cq_ref[...], sq_ref[...], cp_ref[...], sp_ref[...],
                                                      r_ref[...], n, G)
        _, vjp = jax.vjp(fn, q_ref[...], kp_ref[0], kc_ref[0], vp_ref[0], vc_ref[0], sk_ref[0])
        dq, dkp, dkc, dvp, dvc, dsk = vjp(do_ref[...].astype(F32))
        dq_ref[...] = dq
        dkp_ref[0] = dkp
        dkc_ref[0] = dkc
        dvp_ref[0] = dvp
        dvc_ref[0] = dvc

        @pl.when(n == 0)
        def _():
            dsk_ref[...] = jnp.zeros_like(dsk_ref)

        for g in range(G):
            part = jnp.sum(dsk[g * B:(g + 1) * B], axis=0, keepdims=True)
            dsk_ref[0, g:g + 1, :] += jnp.broadcast_to(part, (1, LANES))

    qspec = pl.BlockSpec((G, B, Dh), lambda h, n: (h, n, 0))
    kspec = pl.BlockSpec((1, B, Dh), lambda h, n: (h, n, 0))
    kshape = jax.ShapeDtypeStruct((Hk, S, Dh), F32)
    return pl.pallas_call(
        body, name=name, grid=(Hk, S // B), in_specs=_swa_specs(G) + [qspec],
        out_specs=[qspec, kspec, kspec, kspec, kspec, pl.BlockSpec((1, G, LANES), lambda h, n: (h, 0, 0))],
        out_shape=[jax.ShapeDtypeStruct((Hq, S, Dh), F32), kshape, kshape, kshape, kshape, jax.ShapeDtypeStruct((Hk, G, LANES), F32)],
        compiler_params=_cp(("parallel", "arbitrary")),
    )(qh, kh, kh, vh, vh, sink_col, C, Sn, C, Sn, R, doh)


def shift_add(cur, prev, name):
    Hk, S, Dh = cur.shape
    B = SEQ_BLOCK
    nb = S // B

    def body(c_ref, p_ref, o_ref):
        last = pl.program_id(1) == nb - 1
        o_ref[...] = c_ref[...] + jnp.where(last, 0.0, p_ref[...])

    spec = pl.BlockSpec((1, B, Dh), lambda h, n: (h, n, 0))
    nxt = pl.BlockSpec((1, B, Dh), lambda h, n: (h, jnp.minimum(n + 1, nb - 1), 0))
    return pl.pallas_call(
        body, name=name, grid=(Hk, nb), in_specs=[spec, nxt], out_specs=spec, out_shape=jax.ShapeDtypeStruct(cur.shape, F32),
        compiler_params=_cp(("parallel", "parallel")),
    )(cur, prev)


def loss_head(y, target, name):
    S, D = y.shape
    tr = _tile(S, ROW_TILE, 16)

    def body(y_ref, t_ref, acc_ref, dy_ref):
        err = y_ref[...] - t_ref[...]
        dy_ref[...] = err * (1.0 / D)

        @pl.when(pl.program_id(0) == 0)
        def _():
            acc_ref[...] = jnp.zeros_like(acc_ref)

        acc_ref[...] += jnp.broadcast_to(jnp.sum(err * err).reshape(1, 1), (1, LANES))

    return pl.pallas_call(
        body, name=name, grid=(S // tr,), in_specs=[_row_spec(tr, D)] * 2,
        out_specs=[pl.BlockSpec((1, LANES), lambda i: (0, 0)), _row_spec(tr, D)],
        out_shape=[jax.ShapeDtypeStruct((1, LANES), F32), jax.ShapeDtypeStruct((S, D), F32)], compiler_params=_cp(("arbitrary",)),
    )(y, target)


def _adam_update(w, g, m, v):
    m = ADAM_B1 * m + (1.0 - ADAM_B1) * g
    v = ADAM_B2 * v + (1.0 - ADAM_B2) * jnp.square(g)
    m_hat = m / (1.0 - ADAM_B1 ** ADAM_STEP)
    v_hat = v / (1.0 - ADAM_B2 ** ADAM_STEP)
    delta = -ADAM_LR * (m_hat / (jnp.sqrt(v_hat) + ADAM_EPS) + ADAM_WD * w)
    return delta, m, v


def adamw(w, m, v, gparts, name, gstack=0, emit_g=True):
    R, C = w.shape
    tr = _tile(R, max(8, (128 * 1024) // C), 8)
    spec = pl.BlockSpec((tr, C), lambda i: (i, 0))
    nplain = len(gparts) - (1 if gstack else 0)
    nout = 4 if emit_g else 3

    def body(w_ref, m_ref, v_ref, *rest):
        g_refs, outs = rest[:len(gparts)], rest[len(gparts):]
        g = None
        for r in g_refs[:nplain]:
            g = r[...].astype(F32) if g is None else g + r[...].astype(F32)
        if gstack:
            for t in range(gstack):
                part = g_refs[-1][t].astype(F32)
                g = part if g is None else g + part
        res = _adam_update(w_ref[...], g, m_ref[...], v_ref[...])
        for o_ref, val in zip(outs, ((g,) if emit_g else ()) + res):
            o_ref[...] = val

    gspecs = [spec] * nplain + ([pl.BlockSpec((gstack, tr, C), lambda i: (0, i, 0))] if gstack else [])
    out = jax.ShapeDtypeStruct((R, C), F32)
    return pl.pallas_call(
        body, name=name, grid=(R // tr,), in_specs=[spec] * 3 + gspecs, out_specs=[spec] * nout, out_shape=[out] * nout,
        compiler_params=_cp(("parallel",)),
    )(w, m, v, *gparts)


def ada_fwd(c_all, ada_w, ada_b, name):
    L, D, N = ada_w.shape
    Bp = c_all.shape[0]
    tn = _tile(N, 512)

    def body(c_ref, w_ref, b_ref, o_ref):
        ca = jax.nn.silu(c_ref[...]).astype(BF16)
        o_ref[0] = jnp.dot(ca, w_ref[0].astype(BF16), preferred_element_type=F32) + b_ref[0]

    return pl.pallas_call(
        body, name=name, grid=(L, N // tn),
        in_specs=[pl.BlockSpec((Bp, D), lambda l, j: (0, 0)), pl.BlockSpec((1, D, tn), lambda l, j: (l, 0, j)),
                  pl.BlockSpec((1, 1, tn), lambda l, j: (l, 0, j))],
        out_specs=pl.BlockSpec((1, Bp, tn), lambda l, j: (l, 0, j)), out_shape=jax.ShapeDtypeStruct((L, Bp, N), F32),
        compiler_params=_cp(("parallel", "parallel")),
    )(c_all, ada_w, ada_b)


def ada_wgrad(c_all, dmod, name):
    L, Bp, N = dmod.shape
    D = c_all.shape[1]
    tn = _tile(N, 512)

    def body(c_ref, d_ref, o_ref):
        ca = jax.nn.silu(c_ref[...]).astype(BF16)
        o_ref[0] = lax.dot_general(ca, d_ref[0].astype(BF16), _DIMS["tn"], preferred_element_type=F32)

    return pl.pallas_call(
        body, name=name, grid=(L, N // tn),
        in_specs=[pl.BlockSpec((Bp, D), lambda l, j: (0, 0)), pl.BlockSpec((1, Bp, tn), lambda l, j: (l, 0, j))],
        out_specs=pl.BlockSpec((1, D, tn), lambda l, j: (l, 0, j)), out_shape=jax.ShapeDtypeStruct((L, D, N), F32),
        compiler_params=_cp(("parallel", "parallel")),
    )(c_all, dmod)


N_DEV = 8
N_CHIP = 4
ANY = pl.BlockSpec(memory_space=pl.ANY)


def _place():
    return lax.axis_index("x"), lax.axis_index("y"), lax.axis_index("c")


def _other_chips(x, y):
    chips = [(1 - x, y), (x, 1 - y), (1 - x, 1 - y)]
    return chips, [2 * cx + cy for cx, cy in chips]


def _rcopy(src, dst, ssem, rsem, to):
    return pltpu.make_async_remote_copy(src_ref=src, dst_ref=dst, send_sem=ssem, recv_sem=rsem, device_id=to, device_id_type=MESH)


def ag_small(xs, name):
    R, Wd = xs.shape

    def body(x_ref, out_ref, send_sems, recv_sems, local_sem):
        x, y, c = _place()
        me, sibling = (x, y, c), (x, y, 1 - c)
        chips, _ = _other_chips(x, y)

        def slot(px, py, pc):
            return out_ref.at[4 * px + 2 * py + pc]

        def copy(k, block, to, src=None):
            return _rcopy(slot(*block) if src is None else src, slot(*block), send_sems.at[k], recv_sems.at[k], to)

        mine = pltpu.make_async_copy(x_ref, slot(*me), local_sem)
        mine.start()
        first = [copy(0, me, sibling, src=x_ref)]
        first += [copy(1 + j, me, (*chip, c), src=x_ref) for j, chip in enumerate(chips)]
        for cp in first:
            cp.start()
        passed = [copy(4 + j, (*chip, c), sibling) for j, chip in enumerate(chips)]
        for j, chip in enumerate(chips):
            copy(1 + j, (*chip, c), me).wait_recv()
            passed[j].start()
        copy(0, sibling, me).wait_recv()
        for j, chip in enumerate(chips):
            copy(4 + j, (*chip, 1 - c), me).wait_recv()
        for cp in first + passed:
            cp.wait_send()
        mine.wait()

    vm = pl.BlockSpec(memory_space=pltpu.VMEM)
    return pl.pallas_call(
        body, name=name, out_shape=jax.ShapeDtypeStruct((N_DEV, R, Wd), xs.dtype), in_specs=[vm], out_specs=vm,
        scratch_shapes=[pltpu.SemaphoreType.DMA((7,)), pltpu.SemaphoreType.DMA((7,)), pltpu.SemaphoreType.DMA],
        compiler_params=_cp(),
    )(xs)


def _half_of_shard(by_cols, A, B, h):
    return (h * (A // 2), A // 2, 0, B) if by_cols else (0, A, h * (B // 2), B // 2)


def _shard_in_full(by_cols, A, B, q):
    return (0, q * B) if by_cols else (q * A, 0)


def _window(ref, r0, nr, c0, nc):
    return ref.at[:, pl.ds(r0, nr), pl.ds(c0, nc)]


def ag_weights(shards, by_cols, name):
    n = len(shards)
    geo = [(bc,) + s.shape[1:] for bc, s in zip(by_cols, shards)]
    full = [jax.ShapeDtypeStruct((s.shape[0], A, N_CHIP * B) if bc else (s.shape[0], N_CHIP * A, B), s.dtype)
            for (bc, A, B), s in zip(geo, shards)]

    def body(*refs):
        x_refs, o_refs, (s_ici, r_ici, s_d2d, r_d2d) = refs[:n], refs[n:2 * n], refs[2 * n:]
        x, y, c = _place()
        q = 2 * x + y
        sibling = (x, y, 1 - c)
        chips, qs = _other_chips(x, y)

        def landing(t, chip_q, half):
            r0, nr, c0, nc = _half_of_shard(*geo[t], half)
            ro, co = _shard_in_full(*geo[t], chip_q)
            return _window(o_refs[t], ro + r0, nr, co + c0, nc)

        def ici(t, j, landing_q):
            src = _window(x_refs[t], *_half_of_shard(*geo[t], c))
            return _rcopy(src, landing(t, landing_q, c), s_ici.at[t, j], r_ici.at[t, j], (*chips[j], c))

        def handoff(t, j, half):
            blk = landing(t, qs[j], half)
            return _rcopy(blk, blk, s_d2d.at[t, j], r_d2d.at[t, j], sibling)

        for t in range(n):
            for j in range(3):
                ici(t, j, q).start()
        for t in range(n):
            for j in range(3):
                ici(t, j, qs[j]).wait_recv()
                handoff(t, j, c).start()
        for t in range(n):
            for j in range(3):
                handoff(t, j, 1 - c).wait_recv()
        for t in range(n):
            for j in range(3):
                ici(t, j, q).wait_send()
                handoff(t, j, c).wait_send()

    dma = pltpu.SemaphoreType.DMA
    return pl.pallas_call(
        body, name=name, out_shape=full, in_specs=[ANY] * n, out_specs=[ANY] * n,
        scratch_shapes=[dma((n, 3))] * 4, compiler_params=_cp(),
    )(*shards)


def _half_of_full(by_cols, A, B, h):
    return (h * (A // 2), A // 2, 0, N_CHIP * B) if by_cols else (0, N_CHIP * A, h * (B // 2), B // 2)


def _half_shape(by_cols, L, A, B):
    return (L, A // 2, N_CHIP * B) if by_cols else (L, N_CHIP * A, B // 2)


def _piece_shape(by_cols, L, A, B):
    return (L, A // 2, B) if by_cols else (L, A, B // 2)


def sibling_fold(gs, geo, name):
    n = len(gs)

    def body(*refs):
        x_refs, o_refs, (ssem, rsem) = refs[:n], refs[n:2 * n], refs[2 * n:]
        x, y, c = _place()
        cps = [_rcopy(_window(x_refs[t], *_half_of_full(*geo[t], 1 - c)), o_refs[t], ssem.at[t], rsem.at[t], (x, y, 1 - c))
               for t in range(n)]
        for cp in cps:
            cp.start()
        for cp in cps:
            cp.wait()

    dma = pltpu.SemaphoreType.DMA
    out = [jax.ShapeDtypeStruct(_half_shape(bc, g.shape[0], A, B), g.dtype) for (bc, A, B), g in zip(geo, gs)]
    return pl.pallas_call(
        body, name=name, out_shape=out, in_specs=[ANY] * n, out_specs=[ANY] * n, scratch_shapes=[dma((n,)), dma((n,))],
        compiler_params=_cp(),
    )(*gs)


def chip_exchange(rs, geo, name):
    n = len(rs)

    def body(*refs):
        x_refs, o_refs, (ssem, rsem) = refs[:n], refs[n:2 * n], refs[2 * n:]
        x, y, c = _place()
        chips, qs = _other_chips(x, y)

        def part(t, chip_q):
            bc, A, B = geo[t]
            return _window(x_refs[t], 0, A // 2, chip_q * B, B) if bc else _window(x_refs[t], chip_q * A, A, 0, B // 2)

        cps = [_rcopy(part(t, qs[j]), o_refs[t].at[j], ssem.at[t, j], rsem.at[t, j], (*chips[j], c)) for t in range(n) for j in range(3)]
        for cp in cps:
            cp.start()
        for cp in cps:
            cp.wait()

    dma = pltpu.SemaphoreType.DMA
    out = [jax.ShapeDtypeStruct((3,) + _piece_shape(bc, r.shape[0], A, B), r.dtype) for (bc, A, B), r in zip(geo, rs)]
    return pl.pallas_call(
        body, name=name, out_shape=out, in_specs=[ANY] * n, out_specs=[ANY] * n, scratch_shapes=[dma((n, 3)), dma((n, 3))],
        compiler_params=_cp(),
    )(*rs)


def sibling_share(fs, geo, name):
    n = len(fs)

    def body(*refs):
        x_refs, o_refs, (ssem, rsem) = refs[:n], refs[n:2 * n], refs[2 * n:]
        x, y, c = _place()
        for t in range(n):
            mine = _window(o_refs[t], *_half_of_shard(*geo[t], c))
            _rcopy(mine, mine, ssem.at[t], rsem.at[t], (x, y, 1 - c)).start()
        for t in range(n):
            mine = _window(o_refs[t], *_half_of_shard(*geo[t], c))
            theirs = _window(o_refs[t], *_half_of_shard(*geo[t], 1 - c))
            _rcopy(mine, theirs, ssem.at[t], rsem.at[t], (x, y, 1 - c)).wait_recv()
            _rcopy(mine, mine, ssem.at[t], rsem.at[t], (x, y, 1 - c)).wait_send()
        del x_refs

    dma = pltpu.SemaphoreType.DMA
    return pl.pallas_call(
        body, name=name, out_shape=[jax.ShapeDtypeStruct(f.shape, f.dtype) for f in fs], in_specs=[ANY] * n, out_specs=[ANY] * n,
        input_output_aliases={t: t for t in range(n)}, scratch_shapes=[dma((n,)), dma((n,))], compiler_params=_cp(),
    )(*fs)


SUM_ROWS = 256


def fold_sum(g, recv, by_cols, A, B, qc_idx, name):
    L = g.shape[0]
    _, hr, hc = _half_shape(by_cols, L, A, B)
    tr, tc = _tile(A // 2 if by_cols else A, SUM_ROWS, 16), (B if by_cols else B // 2)
    ro, co = ((A // 2) // tr, 0) if by_cols else (0, 1)

    def body(qc_ref, g_ref, r_ref, o_ref):
        del qc_ref
        o_ref[...] = (g_ref[...].astype(F32) + r_ref[...].astype(F32)).astype(o_ref.dtype)

    spec = pl.BlockSpec((1, tr, tc), lambda l, i, j, qc: (l, i, j))
    grid_spec = pltpu.PrefetchScalarGridSpec(
        num_scalar_prefetch=1, grid=(L, hr // tr, hc // tc),
        in_specs=[pl.BlockSpec((1, tr, tc), lambda l, i, j, qc: (l, i + qc[1] * ro, j + qc[1] * co)), spec], out_specs=spec,
    )
    return pl.pallas_call(
        body, name=name, grid_spec=grid_spec, out_shape=jax.ShapeDtypeStruct((L, hr, hc), BF16),
        compiler_params=_cp(("parallel", "parallel", "parallel")),
    )(qc_idx, g, recv)


def chip_sum(r, ex, by_cols, A, B, qc_idx, name):
    L = r.shape[0]
    _, wr, wc = _piece_shape(by_cols, L, A, B)
    tr = _tile(wr, SUM_ROWS, 16)
    r_ro, r_co = (0, 1) if by_cols else (A // tr, 0)
    o_ro, o_co = ((A // 2) // tr, 0) if by_cols else (0, 1)

    def body(qc_ref, r_ref, e_ref, o_ref):
        del qc_ref
        o_ref[0] = ((r_ref[0].astype(F32) + e_ref[0, 0].astype(F32)) + e_ref[1, 0].astype(F32)) + e_ref[2, 0].astype(F32)

    grid_spec = pltpu.PrefetchScalarGridSpec(
        num_scalar_prefetch=1, grid=(L, wr // tr),
        in_specs=[pl.BlockSpec((1, tr, wc), lambda l, i, qc: (l, i + qc[0] * r_ro, qc[0] * r_co)),
                  pl.BlockSpec((3, 1, tr, wc), lambda l, i, qc: (0, l, i, 0))],
        out_specs=pl.BlockSpec((1, tr, wc), lambda l, i, qc: (l, i + qc[1] * o_ro, qc[1] * o_co)),
    )
    return pl.pallas_call(
        body, name=name, grid_spec=grid_spec, out_shape=jax.ShapeDtypeStruct((L, A, B), F32),
        compiler_params=_cp(("parallel", "parallel")),
    )(qc_idx, r, ex)


BIG = ("ffn_w_gu", "ffn_w_down", "fox_w_in", "fox_w_out", "sgu_w_in", "sgu_w_out", "swa_w_in", "swa_w_out")
COLUMN_SHARDED = ("ffn_w_gu", "fox_w_in", "sgu_w_in", "swa_w_in")
SMALL = ("ada_b", "mix_pre_g", "mix_post_g", "ffn_pre_g", "ffn_post_g", "fox_b_f", "sgu_ln_g", "sgu_ln_b", "sgu_w_s", "sgu_b_s",
         "swa_sinks")
WEIGHTS = ("ada_w", "ada_b", "mix_pre_g", "mix_post_g", "ffn_pre_g", "ffn_post_g", "ffn_w_gu", "ffn_w_down", "fox_w_in", "fox_b_f",
           "fox_w_out", "sgu_w_in", "sgu_ln_g", "sgu_ln_b", "sgu_w_s", "sgu_b_s", "sgu_w_out", "swa_w_in", "swa_sinks", "swa_w_out")
INPUTS = ("x", "c", "positions") + WEIGHTS + ("loss_target",) + tuple("m_" + n for n in WEIGHTS) + tuple("v_" + n for n in WEIGHTS)


def _lane_pad(n):
    return (-n) % LANES


def _pad_shard_columns(t, B):
    if _lane_pad(B) == 0:
        return t
    L, A, _ = t.shape
    return jnp.pad(t.reshape(L, A, N_CHIP, B), ((0, 0), (0, 0), (0, 0), (0, _lane_pad(B)))).reshape(L, A, -1)


def _unpad_shard_columns(t, B):
    if _lane_pad(B) == 0:
        return t
    L, A, _ = t.shape
    return t.reshape(L, A, N_CHIP, B + _lane_pad(B))[..., :B].reshape(L, A, N_CHIP * B)


def _put_own_shard(full, shard, by_cols, q):
    A, B = shard.shape[1:]
    ro, co = _shard_in_full(by_cols, A, B, q)
    return lax.dynamic_update_slice(full, shard, (0, ro, co))


def _pad_rows(flat1d):
    n = flat1d.shape[0]
    pad = (-n) % (8 * LANES)
    return jnp.pad(flat1d, (0, pad)).reshape(-1, LANES)


def _pack_small(parts):
    return jnp.concatenate([_pad_rows(parts[n].astype(F32).reshape(-1)) for n in SMALL], axis=0)


def _unpack_small(packed, shapes):
    out, off = {}, 0
    for n in SMALL:
        size = int(np.prod(shapes[n]))
        rows = (size + 8 * LANES - 1) // (8 * LANES) * 8
        out[n] = packed[off:off + rows].reshape(-1)[:size].reshape(shapes[n])
        off += rows
    return out


def _fox_fwd(h, w_in, b_f, w_out, tag):
    S, D = h.shape
    H = b_f.shape[0]
    qkv = mm(h, w_in, "nn", BF16, name=tag + "_qkv", b_cols=(0, 3 * D))
    fgp = mm(h, w_in, "nn", F32, name=tag + "_fg", b_cols=(3 * D, LANES))
    fgT = fgp[:, :H].T
    cum = fox_gate_fwd(fgT, b_f.reshape(H, 1), tag + "_gate")
    cum_col, cum_row = cum.reshape(H, S, 1), cum.reshape(H, 1, S)
    o, lse = fox_attn_fwd(qkv, cum_col, cum_row, H, tag + "_attn")
    y = mm(o, w_out, "nn", F32, name=tag + "_out")
    return y, (qkv, fgT, cum_col, cum_row, o, lse)


def _fox_bwd(dy, h, w_in, b_f, w_out, ctx, tag):
    qkv, fgT, cum_col, cum_row, o, lse = ctx
    S, D = h.shape
    H = b_f.shape[0]
    do = mm(dy, w_out, "nt", BF16, name=tag + "_do")
    dw_out = mm(o, dy, "tn", BF16, name=tag + "_dwout")
    dq, delta = fox_attn_bwd_dq(qkv, do, o, lse, cum_col, cum_row, H, tag + "_dq")
    dk, dv, dcum = fox_attn_bwd_dkv(qkv, do, lse.reshape(H, 1, S), delta.reshape(H, 1, S), cum_col, cum_row, H, tag + "_dkv")
    dfgT, db = fox_gate_bwd(dcum.reshape(H, S), fgT, b_f.reshape(H, 1), tag + "_dgate")
    dfgp = jnp.pad(dfgT.T, ((0, 0), (0, LANES - H))).astype(BF16)
    dh = mm(dq, w_in, "nt", F32, name=tag + "_dhq", b_cols=(0, D))
    dh = mm(dk, w_in, "nt", F32, add=dh, name=tag + "_dhk", b_cols=(D, D))
    dh = mm(dv, w_in, "nt", F32, add=dh, name=tag + "_dhv", b_cols=(2 * D, D))
    dh = mm(dfgp, w_in, "nt", F32, add=dh, name=tag + "_dhf", b_cols=(3 * D, LANES))
    dw_in = jnp.concatenate(
        [mm(h, dq, "tn", BF16, name=tag + "_dwq"), mm(h, dk, "tn", BF16, name=tag + "_dwk"), mm(h, dv, "tn", BF16, name=tag + "_dwv"),
         mm(h, dfgp, "tn", BF16, name=tag + "_dwf")[:, :H]], axis=1)
    return dh, dw_in, dw_out, db[:, 0]


def _sgu_consts(G, W):
    return jnp.asarray(np.repeat(np.eye(G, dtype=np.float32), W // G, axis=1))


def _sgu_fwd(h, w_in, ln_g, ln_b, w_s, b_s, w_out, tag):
    G, W = w_s.shape[0], ln_g.shape[0]
    zp = mm(h, w_in, "nn", F32, name=tag + "_in")
    args = (zp, ln_g.reshape(1, W), ln_b.reshape(1, W), w_s, b_s.T, _sgu_consts(G, W))
    gated = sgu_fwd(*args, tag + "_core")
    y = mm(gated, w_out, "nn", F32, name=tag + "_out")
    return y, (args, gated)


def _sgu_bwd(dy, h, w_in, w_out, ctx, tag):
    args, gated = ctx
    dgated = mm(dy, w_out, "nt", BF16, name=tag + "_dgated")
    dw_out = mm(gated, dy, "tn", BF16, name=tag + "_dwout")
    dzp, dlg, dlb, dws, dbsT = sgu_bwd(*args, dgated, tag + "_dcore")
    dh = mm(dzp, w_in, "nt", F32, name=tag + "_dh")
    dw_in = mm(h, dzp, "tn", BF16, name=tag + "_dwin")
    return dh, dw_in, dw_out, dlg[0], dlb[0], dws, dbsT.T


def _rope_tables(positions):
    inv = ROPE_THETA ** (-jnp.arange(0, ROPE_DIM, 2, dtype=F32) / ROPE_DIM)
    ang = positions.astype(F32)[:, None] * inv
    S = positions.shape[0]
    rest = SWA_HEAD_DIM - ROPE_DIM
    C = jnp.concatenate([jnp.cos(ang), jnp.cos(ang), jnp.ones((S, rest), F32)], axis=1)
    Sn = jnp.concatenate([jnp.sin(ang), jnp.sin(ang), jnp.zeros((S, rest), F32)], axis=1)
    return C, Sn


def _heads(t, n):
    return t.reshape(t.shape[0], n, SWA_HEAD_DIM).transpose(1, 0, 2)


def _unheads(t):
    return t.transpose(1, 0, 2).reshape(t.shape[1], -1)


def _swa_fwd(h, w_in, sinks, w_out, tables, tag):
    Hq = sinks.shape[0]
    Hk = (w_in.shape[1] // SWA_HEAD_DIM - Hq) // 2
    G = Hq // Hk
    proj = mm(h, w_in, "nn", F32, name=tag + "_in")
    qh = _heads(proj[:, :Hq * SWA_HEAD_DIM], Hq)
    kh = _heads(proj[:, Hq * SWA_HEAD_DIM:(Hq + Hk) * SWA_HEAD_DIM], Hk)
    vh = _heads(proj[:, (Hq + Hk) * SWA_HEAD_DIM:], Hk)
    sink_col = jnp.repeat(sinks.reshape(Hk, G), SEQ_BLOCK, axis=1).reshape(Hk, G * SEQ_BLOCK, 1)
    args = (qh, kh, vh, sink_col, tables[0], tables[1], jnp.asarray(_rope_matrix()))
    o = _unheads(swa_fwd(*args, tag + "_core"))
    y = mm(o, w_out, "nn", F32, name=tag + "_out")
    return y, (args, o)


def _swa_bwd(dy, h, w_in, w_out, ctx, tag):
    args, o = ctx
    Hq = args[0].shape[0]
    do = mm(dy, w_out, "nt", BF16, name=tag + "_do")
    dw_out = mm(o, dy, "tn", BF16, name=tag + "_dwout")
    dqh, dkp, dkc, dvp, dvc, dsk = swa_bwd(*args, _heads(do, Hq), tag + "_dcore")
    dk = shift_add(dkc, dkp, tag + "_dk")
    dv = shift_add(dvc, dvp, tag + "_dv")
    dproj = jnp.concatenate([_unheads(dqh), _unheads(dk), _unheads(dv)], axis=1).astype(BF16)
    dh = mm(dproj, w_in, "nt", F32, name=tag + "_dh")
    dw_in = mm(h, dproj, "tn", BF16, name=tag + "_dwin")
    return dh, dw_in, dw_out, dsk[:, :, 0].reshape(Hq)


def kernel(x, c, positions, ada_w, ada_b, mix_pre_g, mix_post_g, ffn_pre_g, ffn_post_g, ffn_w_gu, ffn_w_down, fox_w_in, fox_b_f, fox_w_out, sgu_w_in, sgu_ln_g, sgu_ln_b, sgu_w_s, sgu_b_s, sgu_w_out, swa_w_in, swa_sinks, swa_w_out, loss_target, m_ada_w, m_ada_b, m_mix_pre_g, m_mix_post_g, m_ffn_pre_g, m_ffn_post_g, m_ffn_w_gu, m_ffn_w_down, m_fox_w_in, m_fox_b_f, m_fox_w_out, m_sgu_w_in, m_sgu_ln_g, m_sgu_ln_b, m_sgu_w_s, m_sgu_b_s, m_sgu_w_out, m_swa_w_in, m_swa_sinks, m_swa_w_out, v_ada_w, v_ada_b, v_mix_pre_g, v_mix_post_g, v_ffn_pre_g, v_ffn_post_g, v_ffn_w_gu, v_ffn_w_down, v_fox_w_in, v_fox_b_f, v_fox_w_out, v_sgu_w_in, v_sgu_ln_g, v_sgu_ln_b, v_sgu_w_s, v_sgu_b_s, v_sgu_w_out, v_swa_w_in, v_swa_sinks, v_swa_w_out):
    P = dict(zip(INPUTS, (x, c, positions, ada_w, ada_b, mix_pre_g, mix_post_g, ffn_pre_g, ffn_post_g, ffn_w_gu, ffn_w_down, fox_w_in, fox_b_f, fox_w_out, sgu_w_in, sgu_ln_g, sgu_ln_b, sgu_w_s, sgu_b_s, sgu_w_out, swa_w_in, swa_sinks, swa_w_out, loss_target, m_ada_w, m_ada_b, m_mix_pre_g, m_mix_post_g, m_ffn_pre_g, m_ffn_post_g, m_ffn_w_gu, m_ffn_w_down, m_fox_w_in, m_fox_b_f, m_fox_w_out, m_sgu_w_in, m_sgu_ln_g, m_sgu_ln_b, m_sgu_w_s, m_sgu_b_s, m_sgu_w_out, m_swa_w_in, m_swa_sinks, m_swa_w_out, v_ada_w, v_ada_b, v_mix_pre_g, v_mix_post_g, v_ffn_pre_g, v_ffn_post_g, v_ffn_w_gu, v_ffn_w_down, v_fox_w_in, v_fox_b_f, v_fox_w_out, v_sgu_w_in, v_sgu_ln_g, v_sgu_ln_b, v_sgu_w_s, v_sgu_b_s, v_sgu_w_out, v_swa_w_in, v_swa_sinks, v_swa_w_out)))
    xs, target, pos = x[0], loss_target[0], positions[0]
    S, D = xs.shape
    L = ada_w.shape[0]
    n_mix = 3
    F = ffn_w_down.shape[1] * N_CHIP
    xi, yi, ci = _place()
    q_me = 2 * xi + yi
    dev = 4 * xi + 2 * yi + ci

    by_cols = [n in COLUMN_SHARDED for n in BIG]
    shards = [jnp.pad(P[n].astype(BF16), ((0, 0), (0, 0), (0, _lane_pad(P[n].shape[2])))) if bc else P[n].astype(BF16)
              for n, bc in zip(BIG, by_cols)]
    geo = [(bc,) + s.shape[1:] for bc, s in zip(by_cols, shards)]
    fulls = ag_weights(shards, by_cols, "ag_weights")
    Wt = {}
    for n, bc, s, f in zip(BIG, by_cols, shards, fulls):
        f = _put_own_shard(f, s, bc, q_me)
        Wt[n] = _unpad_shard_columns(f, P[n].shape[2]) if bc else f
    fox_pad = 3 * D + LANES - Wt["fox_w_in"].shape[2]
    Wt["fox_w_in"] = jnp.pad(Wt["fox_w_in"], ((0, 0), (0, 0), (0, fox_pad)))

    c_all = ag_small(c.reshape(D // LANES, LANES), "ag_c").reshape(N_DEV, D)
    c_all = jnp.pad(c_all, ((0, 16 - N_DEV), (0, 0)))
    Nm = ada_w.shape[2]
    ada_b_mine = lax.dynamic_slice_in_dim(ada_b, q_me * Nm, Nm, axis=1).reshape(L, 1, Nm)
    modp = ada_fwd(c_all, ada_w, ada_b_mine, "ada_fwd")[:, :N_DEV]
    mod_all = ag_small(modp.reshape(-1, LANES), "ag_mod").reshape(N_DEV, L, N_DEV, Nm)
    mod_mine = lax.dynamic_index_in_dim(mod_all[0::2], dev, axis=2, keepdims=False)
    mods = mod_mine.transpose(1, 0, 2).reshape(L, 6, 1, D)

    tables = _rope_tables(pos)

    saved = []
    xc = xs
    for i in range(L):
        kind, j = i % n_mix, i // n_mix
        sh_m, sc_m, g_m, sh_f, sc_f, g_f = (mods[i, t] for t in range(6))
        t = f"l{i}"
        h1 = pre_fwd(xc, mix_pre_g[i:i + 1], sh_m, sc_m, t + "_pre_m")
        if kind == 0:
            y1, ctx = _fox_fwd(h1, Wt["fox_w_in"][j], fox_b_f[j], Wt["fox_w_out"][j], t + "_fox")
        elif kind == 1:
            y1, ctx = _sgu_fwd(h1, Wt["sgu_w_in"][j], sgu_ln_g[j], sgu_ln_b[j], sgu_w_s[j], sgu_b_s[j], Wt["sgu_w_out"][j], t + "_sgu")
        else:
            y1, ctx = _swa_fwd(h1, Wt["swa_w_in"][j], swa_sinks[j], Wt["swa_w_out"][j], tables, t + "_swa")
        xm = post_fwd(xc, y1, mix_post_g[i:i + 1], g_m, t + "_post_m")
        h2 = pre_fwd(xm, ffn_pre_g[i:i + 1], sh_f, sc_f, t + "_pre_f")
        gu = mm(h2, Wt["ffn_w_gu"][i], "nn", BF16, name=t + "_ffn_gu")
        a = act_fwd(gu, t + "_act")
        y2 = mm(a, Wt["ffn_w_down"][i], "nn", F32, name=t + "_ffn_down")
        xn = post_fwd(xm, y2, ffn_post_g[i:i + 1], g_f, t + "_post_f")
        saved.append((xc, h1, y1, ctx, xm, h2, gu, a, y2))
        xc = xn

    sq, dx = loss_head(xc, target, "loss_head")
    loss = lax.psum(sq[0, 0] * (0.5 / D), ("x", "y", "c"))

    big_g = {n: [None] * P[n].shape[0] for n in BIG}
    small_g = {n: [None] * P[n].shape[0] for n in SMALL}
    for i in reversed(range(L)):
        kind, j = i % n_mix, i // n_mix
        sh_m, sc_m, g_m, sh_f, sc_f, g_f = (mods[i, t] for t in range(6))
        xc, h1, y1, ctx, xm, h2, gu, a, y2 = saved[i]
        t = f"l{i}"
        dy2, dgpost_f, dgate_f = post_bwd(y2, ffn_post_g[i:i + 1], g_f, dx, t + "_dpost_f")
        da = mm(dy2, Wt["ffn_w_down"][i], "nt", F32, name=t + "_da")
        big_g["ffn_w_down"][i] = mm(a, dy2, "tn", BF16, name=t + "_dwdown")
        dgu = act_bwd(gu, da, t + "_dact")
        dh2 = mm(dgu, Wt["ffn_w_gu"][i], "nt", F32, name=t + "_dh2")
        big_g["ffn_w_gu"][i] = mm(h2, dgu, "tn", BF16, name=t + "_dwgu")
        dxm, dgpre_f, dsh_f, dsc_f = pre_bwd(xm, ffn_pre_g[i:i + 1], sh_f, sc_f, dh2, dx, t + "_dpre_f")
        dy1, dgpost_m, dgate_m = post_bwd(y1, mix_post_g[i:i + 1], g_m, dxm, t + "_dpost_m")
        if kind == 0:
            dh1, dw_in, dw_out, db = _fox_bwd(dy1, h1, Wt["fox_w_in"][j], fox_b_f[j], Wt["fox_w_out"][j], ctx, t + "_fox")
            big_g["fox_w_in"][j], big_g["fox_w_out"][j], small_g["fox_b_f"][j] = dw_in, dw_out, db
        elif kind == 1:
            dh1, dw_in, dw_out, dlg, dlb, dws, dbs = _sgu_bwd(dy1, h1, Wt["sgu_w_in"][j], Wt["sgu_w_out"][j], ctx, t + "_sgu")
            big_g["sgu_w_in"][j], big_g["sgu_w_out"][j] = dw_in, dw_out
            small_g["sgu_ln_g"][j], small_g["sgu_ln_b"][j], small_g["sgu_w_s"][j], small_g["sgu_b_s"][j] = dlg, dlb, dws, dbs
        else:
            dh1, dw_in, dw_out, dsk = _swa_bwd(dy1, h1, Wt["swa_w_in"][j], Wt["swa_w_out"][j], ctx, t + "_swa")
            big_g["swa_w_in"][j], big_g["swa_w_out"][j], small_g["swa_sinks"][j] = dw_in, dw_out, dsk
        dx, dgpre_m, dsh_m, dsc_m = pre_bwd(xc, mix_pre_g[i:i + 1], sh_m, sc_m, dh1, dxm, t + "_dpre_m")
        small_g["ada_b"][i] = jnp.concatenate([dsh_m, dsc_m, dgate_m, dsh_f, dsc_f, dgate_f], axis=1)[0]
        small_g["mix_pre_g"][i], small_g["mix_post_g"][i] = dgpre_m[0], dgpost_m[0]
        small_g["ffn_pre_g"][i], small_g["ffn_post_g"][i] = dgpre_f[0], dgpost_f[0]
    grad_x = dx[None]

    shapes = {n: P[n].shape for n in SMALL}
    small_parts = ag_small(_pack_small({n: jnp.stack(small_g[n]) for n in SMALL}), "ag_small_grads")
    sg, sd, sm, sv = adamw(_pack_small({n: P[n] for n in SMALL}), _pack_small({n: P["m_" + n] for n in SMALL}),
                           _pack_small({n: P["v_" + n] for n in SMALL}), [small_parts], "adamw_small", gstack=N_DEV)
    out_g, out_d, out_m, out_v = (_unpack_small(t, shapes) for t in (sg, sd, sm, sv))

    dmod_all = small_parts[:, :L * 6 * D // LANES].reshape(N_DEV, L, 6 * D)
    dmod_mine = lax.dynamic_slice_in_dim(dmod_all, q_me * Nm, Nm, axis=2).transpose(1, 0, 2)
    dmod_mine = jnp.pad(dmod_mine, ((0, 0), (0, 16 - N_DEV), (0, 0)))
    g_ada = ada_wgrad(c_all, dmod_mine, "ada_wgrad")
    r2 = lambda t: t.reshape(-1, t.shape[-1])
    res = adamw(r2(ada_w), r2(m_ada_w), r2(v_ada_w), [r2(g_ada)], "adamw_ada_w", emit_g=False)
    out_g["ada_w"] = g_ada
    out_d["ada_w"], out_m["ada_w"], out_v["ada_w"] = (t.reshape(ada_w.shape) for t in res)

    gfull = [_pad_shard_columns(jnp.stack(big_g[n]), P[n].shape[2]) if bc else jnp.stack(big_g[n]) for n, bc in zip(BIG, by_cols)]
    qc = jnp.stack([q_me, ci]).astype(jnp.int32)
    from_sibling = sibling_fold(gfull, geo, "rs_fold")
    chip_part = [fold_sum(g, r, *gm, qc, "rs_fold_sum_" + n) for n, g, r, gm in zip(BIG, gfull, from_sibling, geo)]
    from_chips = chip_exchange(chip_part, geo, "rs_exchange")
    mine = [chip_sum(r, e, *gm, qc, "rs_chip_sum_" + n) for n, r, e, gm in zip(BIG, chip_part, from_chips, geo)]
    for n, gsh in zip(BIG, sibling_share(mine, geo, "rs_share")):
        gsh = gsh[:, :, :P[n].shape[2]]
        res = adamw(r2(P[n]), r2(P["m_" + n]), r2(P["v_" + n]), [r2(gsh)], "adamw_" + n, emit_g=False)
        out_g[n] = gsh
        out_d[n], out_m[n], out_v[n] = (t.reshape(P[n].shape) for t in res)

    return (loss, grad_x, *[out_g[n] for n in WEIGHTS], *[out_d[n] for n in WEIGHTS], *[out_m[n] for n in WEIGHTS],
            *[out_v[n] for n in WEIGHTS])
```

```python
from typing import Callable, NamedTuple

import numpy as np
import jax
import jax.numpy as jnp
from jax import lax
from jax.experimental import pallas as pl
from jax.experimental.pallas import tpu as pltpu

F32 = jnp.float32
BF16 = jnp.bfloat16
MESH = pl.DeviceIdType.MESH

EPS = 1e-6
NEG = -1e30
V7X_VMEM_BYTES = 64 * 1024 * 1024
VMEM_LIMIT = V7X_VMEM_BYTES - 8 * 1024 * 1024
LANES = 128
SEQ_BLOCK = 128
SWA_HEAD_DIM = 64
ROPE_DIM = SWA_HEAD_DIM // 4
ROPE_THETA = 500000.0

ADAM_LR = 0.001
ADAM_B1 = 0.9
ADAM_B2 = 0.999
ADAM_EPS = 1e-08
ADAM_WD = 0.01
ADAM_STEP = 10


def _cp(sem=None, **kw):
    return pltpu.CompilerParams(dimension_semantics=sem, vmem_limit_bytes=VMEM_LIMIT, **kw)


def _tile(dim, pref, mult=LANES):
    if dim <= pref:
        return dim
    t = (pref // mult) * mult
    while t >= mult:
        if dim % t == 0:
            return t
        t -= mult
    return dim


_DIMS = {"nn": (((1,), (0,)), ((), ())), "nt": (((1,), (1,)), ((), ())), "tn": (((0,), (0,)), ((), ()))}


MM_TILES = {"nn": (1024, 512, 2816), "nt": (1024, 1024, 2816), "tn": (512, 512, 4096)}


def mm(a, b, mode="nn", out_dtype=F32, add=None, name="mm", b_cols=None, tm=None, tn=None, tk=None):
    tm, tn, tk = (d if t is None else t for t, d in zip((tm, tn, tk), MM_TILES[mode]))
    b, b_layer = b if isinstance(b, tuple) else (b, None)
    b_shape = b.shape[-2:]
    c0 = 0
    if b_cols is not None:
        c0, csize = b_cols
    if mode == "nn":
        (M, K), (K2, N) = a.shape, b_shape
        if b_cols is not None:
            N = csize
    elif mode == "nt":
        (M, K), (N, K2) = a.shape, b_shape
        if b_cols is not None:
            K2 = csize
    else:
        (K, M), (K2, N) = a.shape, b_shape
        assert b_cols is None
    assert K == K2, (a.shape, b.shape, mode)
    tm = _tile(M, tm, LANES if mode == "tn" else 16)
    tn = _tile(N, tn)
    tk = _tile(K, tk, LANES if mode != "tn" else 16)
    nk = K // tk
    if b_cols is not None:
        assert c0 % (tn if mode == "nn" else tk) == 0, (b_cols, tn, tk)
    bo = c0 // (tn if mode == "nn" else tk)
    dims = _DIMS[mode]
    has_add = add is not None

    def body(a_ref, b_ref, *rest):
        if has_add:
            add_ref, o_ref, acc_ref = rest
        else:
            o_ref, acc_ref = rest
        k = pl.program_id(2)
        p = lax.dot_general(a_ref[...].astype(BF16), b_ref[...].astype(BF16), dims, preferred_element_type=F32)

        @pl.when(k == 0)
        def _():
            acc_ref[...] = p + add_ref[...].astype(F32) if has_add else p

        @pl.when(k > 0)
        def _():
            acc_ref[...] += p

        @pl.when(k == nk - 1)
        def _():
            o_ref[...] = acc_ref[...].astype(o_ref.dtype)

    a_spec = pl.BlockSpec((tk, tm), lambda i, j, k: (k, i)) if mode == "tn" else pl.BlockSpec((tm, tk), lambda i, j, k: (i, k))
    b_blk, b_idx = ((tn, tk), lambda i, j, k: (j, k + bo)) if mode == "nt" else ((tk, tn), lambda i, j, k: (k, j + bo))
    if b_layer is None:
        b_spec = pl.BlockSpec(b_blk, b_idx)
    else:
        b_spec = pl.BlockSpec((None,) + b_blk, lambda i, j, k: (b_layer,) + b_idx(i, j, k))
    o_spec = pl.BlockSpec((tm, tn), lambda i, j, k: (i, j))
    in_specs = [a_spec, b_spec] + ([o_spec] if has_add else [])
    args = (a, b) + ((add,) if has_add else ())
    return pl.pallas_call(
        body, name=name, grid=(M // tm, N // tn, nk), in_specs=in_specs, out_specs=o_spec,
        out_shape=jax.ShapeDtypeStruct((M, N), out_dtype), scratch_shapes=[pltpu.VMEM((tm, tn), F32)],
        compiler_params=_cp(("parallel", "parallel", "arbitrary")),
    )(*args)


def _rms(x, g):
    return (x * lax.rsqrt(jnp.mean(x * x, axis=-1, keepdims=True) + EPS)) * g


def _pre(x, g, sh, sc):
    return _rms(x, g) * (1 + sc) + sh


def _post(x, y, g, gate):
    return x + gate * _rms(y, g)


ROW_TILE = 256


def _row_spec(tr, d):
    return pl.BlockSpec((tr, d), lambda i: (i, 0))


def _vec_spec(d):
    return pl.BlockSpec((1, d), lambda i: (0, 0))


def pre_fwd(x, g, sh, sc, name):
    S, D = x.shape
    tr = _tile(S, ROW_TILE, 16)

    def body(x_ref, g_ref, sh_ref, sc_ref, h_ref):
        h_ref[...] = _pre(x_ref[...], g_ref[...], sh_ref[...], sc_ref[...]).astype(h_ref.dtype)

    return pl.pallas_call(
        body, name=name, grid=(S // tr,), in_specs=[_row_spec(tr, D)] + [_vec_spec(D)] * 3, out_specs=_row_spec(tr, D),
        out_shape=jax.ShapeDtypeStruct((S, D), BF16), compiler_params=_cp(("parallel",)),
    )(x, g, sh, sc)


def pre_bwd(x, g, sh, sc, dh, dres, name):
    S, D = x.shape
    tr = _tile(S, ROW_TILE, 16)

    def body(x_ref, g_ref, sh_ref, sc_ref, dh_ref, dres_ref, dx_ref, dg_ref, dsh_ref, dsc_ref):
        _, vjp = jax.vjp(_pre, x_ref[...], g_ref[...], sh_ref[...], sc_ref[...])
        dx, dg, dsh, dsc = vjp(dh_ref[...].astype(F32))
        dx_ref[...] = dres_ref[...] + dx

        @pl.when(pl.program_id(0) == 0)
        def _():
            dg_ref[...] = jnp.zeros_like(dg_ref)
            dsh_ref[...] = jnp.zeros_like(dsh_ref)
            dsc_ref[...] = jnp.zeros_like(dsc_ref)

        dg_ref[...] += dg
        dsh_ref[...] += dsh
        dsc_ref[...] += dsc

    vec = jax.ShapeDtypeStruct((1, D), F32)
    return pl.pallas_call(
        body, name=name, grid=(S // tr,), in_specs=[_row_spec(tr, D)] + [_vec_spec(D)] * 3 + [_row_spec(tr, D)] * 2,
        out_specs=[_row_spec(tr, D)] + [_vec_spec(D)] * 3, out_shape=[jax.ShapeDtypeStruct((S, D), F32), vec, vec, vec],
        compiler_params=_cp(("arbitrary",)),
    )(x, g, sh, sc, dh, dres)


def post_fwd(x, y, g, gate, name):
    S, D = x.shape
    tr = _tile(S, ROW_TILE, 16)

    def body(x_ref, y_ref, g_ref, gate_ref, o_ref):
        o_ref[...] = _post(x_ref[...], y_ref[...], g_ref[...], gate_ref[...])

    return pl.pallas_call(
        body, name=name, grid=(S // tr,), in_specs=[_row_spec(tr, D)] * 2 + [_vec_spec(D)] * 2, out_specs=_row_spec(tr, D),
        out_shape=jax.ShapeDtypeStruct((S, D), F32), compiler_params=_cp(("parallel",)),
    )(x, y, g, gate)


def post_bwd(y, g, gate, dxn, name):
    S, D = y.shape
    tr = _tile(S, ROW_TILE, 16)

    def body(y_ref, g_ref, gate_ref, dxn_ref, dy_ref, dg_ref, dgate_ref):
        fn = lambda yy, gg, gt: gt * _rms(yy, gg)
        _, vjp = jax.vjp(fn, y_ref[...], g_ref[...], gate_ref[...])
        dy, dg, dgate = vjp(dxn_ref[...])
        dy_ref[...] = dy.astype(dy_ref.dtype)

        @pl.when(pl.program_id(0) == 0)
        def _():
            dg_ref[...] = jnp.zeros_like(dg_ref)
            dgate_ref[...] = jnp.zeros_like(dgate_ref)

        dg_ref[...] += dg
        dgate_ref[...] += dgate

    vec = jax.ShapeDtypeStruct((1, D), F32)
    return pl.pallas_call(
        body, name=name, grid=(S // tr,), in_specs=[_row_spec(tr, D)] + [_vec_spec(D)] * 2 + [_row_spec(tr, D)],
        out_specs=[_row_spec(tr, D)] + [_vec_spec(D)] * 2, out_shape=[jax.ShapeDtypeStruct((S, D), BF16), vec, vec],
        compiler_params=_cp(("arbitrary",)),
    )(y, g, gate, dxn)


def _swiglu(g, u):
    return jax.nn.silu(g) * u


ACT_ROWS = 256


def act_fwd(gu, name):
    S, F2 = gu.shape
    F = F2 // 2
    tr = _tile(S, ACT_ROWS, 16)

    def body(gu_ref, a_ref):
        a_ref[...] = _swiglu(gu_ref[:, :F].astype(F32), gu_ref[:, F:].astype(F32)).astype(a_ref.dtype)

    return pl.pallas_call(
        body, name=name, grid=(S // tr,), in_specs=[_row_spec(tr, F2)], out_specs=_row_spec(tr, F),
        out_shape=jax.ShapeDtypeStruct((S, F), BF16), compiler_params=_cp(("parallel",)),
    )(gu)


def act_bwd(gu, da, name):
    S, F2 = gu.shape
    F = F2 // 2
    tr = _tile(S, ACT_ROWS, 16)

    def body(gu_ref, da_ref, dgu_ref):
        _, vjp = jax.vjp(_swiglu, gu_ref[:, :F].astype(F32), gu_ref[:, F:].astype(F32))
        dg, du = vjp(da_ref[...].astype(F32))
        dgu_ref[:, :F] = dg.astype(dgu_ref.dtype)
        dgu_ref[:, F:] = du.astype(dgu_ref.dtype)

    return pl.pallas_call(
        body, name=name, grid=(S // tr,), in_specs=[_row_spec(tr, F2), _row_spec(tr, F)], out_specs=_row_spec(tr, F2),
        out_shape=jax.ShapeDtypeStruct((S, F2), BF16), compiler_params=_cp(("parallel",)),
    )(gu, da)


def _tri(upper):
    r = lax.broadcasted_iota(jnp.int32, (LANES, LANES), 0)
    c = lax.broadcasted_iota(jnp.int32, (LANES, LANES), 1)
    return ((r <= c) if upper else (r >= c)).astype(F32)


def _hdot(a, b):
    return jnp.dot(a, b, precision=lax.Precision.HIGHEST, preferred_element_type=F32)


def fox_gate_fwd(fgT, b, name):
    H, S = fgT.shape
    spec = pl.BlockSpec((H, LANES), lambda ch: (0, ch))

    def body(fg_ref, b_ref, cum_ref, carry_ref):
        @pl.when(pl.program_id(0) == 0)
        def _():
            carry_ref[...] = jnp.zeros_like(carry_ref)

        lf = jax.nn.log_sigmoid(fg_ref[...] + b_ref[...])
        cum_ref[...] = _hdot(lf, _tri(True)) + carry_ref[...]
        carry_ref[...] += _hdot(lf, jnp.ones((LANES, LANES), F32))

    return pl.pallas_call(
        body, name=name, grid=(S // LANES,), in_specs=[spec, pl.BlockSpec((H, 1), lambda ch: (0, 0))], out_specs=spec,
        out_shape=jax.ShapeDtypeStruct((H, S), F32), scratch_shapes=[pltpu.VMEM((H, LANES), F32)], compiler_params=_cp(("arbitrary",)),
    )(fgT, b)


def fox_gate_bwd(dcum, fgT, b, name):
    H, S = fgT.shape
    nch = S // LANES
    spec = pl.BlockSpec((H, LANES), lambda t: (0, nch - 1 - t))

    def body(dcum_ref, fg_ref, b_ref, dfg_ref, db_ref, tail_ref):
        @pl.when(pl.program_id(0) == 0)
        def _():
            tail_ref[...] = jnp.zeros_like(tail_ref)
            db_ref[...] = jnp.zeros_like(db_ref)

        dlf = _hdot(dcum_ref[...], _tri(False)) + tail_ref[...]
        dfg = dlf * jax.nn.sigmoid(-(fg_ref[...] + b_ref[...]))
        dfg_ref[...] = dfg
        ones = jnp.ones((LANES, LANES), F32)
        tail_ref[...] += _hdot(dcum_ref[...], ones)
        db_ref[...] += _hdot(dfg, ones)

    return pl.pallas_call(
        body, name=name, grid=(nch,), in_specs=[spec, spec, pl.BlockSpec((H, 1), lambda t: (0, 0))],
        out_specs=[spec, pl.BlockSpec((H, LANES), lambda t: (0, 0))],
        out_shape=[jax.ShapeDtypeStruct((H, S), F32), jax.ShapeDtypeStruct((H, LANES), F32)],
        scratch_shapes=[pltpu.VMEM((H, LANES), F32)], compiler_params=_cp(("arbitrary",)),
    )(dcum, fgT, b)


FOX_TILE = 1024


def _on_and_below_diagonal(i, j, tile):
    @pl.when(j < i)
    def _():
        tile(False)

    @pl.when(j == i)
    def _():
        tile(True)


def _causal_pairs(n, by_key):
    pairs = [(i, j) for i in range(n) for j in range(i + 1)]
    if by_key:
        pairs.sort(key=lambda p: (p[1], p[0]))
    qi = np.asarray([p[0] for p in pairs], np.int32)
    kj = np.asarray([p[1] for p in pairs], np.int32)
    return qi, kj


def _fox_scores(q, k, fq, fk, T, scale, transposed):
    r = lax.broadcasted_iota(jnp.int32, (T, T), 0)
    c = lax.broadcasted_iota(jnp.int32, (T, T), 1)
    if transposed:
        return lax.dot_general(k, q, _DIMS["nt"], preferred_element_type=F32) * scale + (fq - fk), r <= c
    return lax.dot_general(q, k, _DIMS["nt"], preferred_element_type=F32) * scale + (fq - fk), c <= r


class SideCar(NamedTuple):
    arrays: list
    out_shape: list
    semaphores: list
    steps: Callable


def _sc_specs(sc, out):
    return [] if sc is None else [pl.BlockSpec(memory_space=pl.ANY)] * len(sc.out_shape if out else sc.arrays)


def _sc_sems(sc):
    return [] if sc is None else list(sc.semaphores)


def _sc_out(sc):
    return [] if sc is None else list(sc.out_shape)


def _sc_arrays(sc):
    return [] if sc is None else list(sc.arrays)


def _with_sidecar(sc, n_in, n_out, n_scratch, body, first, last):
    if sc is None:
        return body
    a, o = len(sc.arrays), len(sc.out_shape)

    def wrapped(*refs):
        ins, rest = refs[:n_in], refs[n_in:]
        sc_in, rest = rest[:a], rest[a:]
        outs, rest = rest[:n_out], rest[n_out:]
        sc_out, rest = rest[:o], rest[o:]
        scratch, sems = rest[:n_scratch], rest[n_scratch:]
        start, finish = sc.steps(sc_in, sc_out, sems)
        pl.when(first())(start)
        body(*ins, *outs, *scratch)
        pl.when(last())(finish)

    return wrapped


def fox_attn_fwd(qkv, cum_col, cum_row, H, name, sidecar=None):
    S = qkv.shape[0]
    Dh = qkv.shape[1] // (3 * H)
    T = _tile(S, FOX_TILE)
    n = S // T
    qi, kj = _causal_pairs(n, by_key=False)
    scale = Dh ** -0.5

    def body(qi_ref, kj_ref, q_ref, k_ref, v_ref, fq_ref, fk_ref, o_ref, lse_ref, m_ref, l_ref, acc_ref):
        p_id = pl.program_id(1)
        i, j = qi_ref[p_id], kj_ref[p_id]

        @pl.when(j == 0)
        def _():
            m_ref[...] = jnp.full_like(m_ref, NEG)
            l_ref[...] = jnp.zeros_like(l_ref)
            acc_ref[...] = jnp.zeros_like(acc_ref)

        def tile(masked):
            s, mask = _fox_scores(q_ref[...], k_ref[...], fq_ref[0], fk_ref[0], T, scale, False)
            if masked:
                s = jnp.where(mask, s, NEG)
            m_new = jnp.maximum(m_ref[...], jnp.max(s, axis=1, keepdims=True))
            alpha = jnp.exp(m_ref[...] - m_new)
            p = jnp.exp(s - m_new)
            l_ref[...] = alpha * l_ref[...] + jnp.sum(p, axis=1, keepdims=True)
            acc_ref[...] = alpha * acc_ref[...] + jnp.dot(p.astype(BF16), v_ref[...], preferred_element_type=F32)
            m_ref[...] = m_new

        _on_and_below_diagonal(i, j, tile)

        @pl.when(j == i)
        def _():
            o_ref[...] = (acc_ref[...] / l_ref[...]).astype(o_ref.dtype)
            lse_ref[0] = m_ref[...] + jnp.log(l_ref[...])

    grid_spec = pltpu.PrefetchScalarGridSpec(
        num_scalar_prefetch=2, grid=(H, len(qi)),
        in_specs=[
            pl.BlockSpec((T, Dh), lambda h, p, qi, kj: (qi[p], h)),
            pl.BlockSpec((T, Dh), lambda h, p, qi, kj: (kj[p], H + h)),
            pl.BlockSpec((T, Dh), lambda h, p, qi, kj: (kj[p], 2 * H + h)),
            pl.BlockSpec((1, T, 1), lambda h, p, qi, kj: (h, qi[p], 0)),
            pl.BlockSpec((1, 1, T), lambda h, p, qi, kj: (h, 0, kj[p])),
        ] + _sc_specs(sidecar, False),
        out_specs=[
            pl.BlockSpec((T, Dh), lambda h, p, qi, kj: (qi[p], h)),
            pl.BlockSpec((1, T, 1), lambda h, p, qi, kj: (h, qi[p], 0)),
        ] + _sc_specs(sidecar, True),
        scratch_shapes=[pltpu.VMEM((T, 1), F32), pltpu.VMEM((T, 1), F32), pltpu.VMEM((T, Dh), F32)] + _sc_sems(sidecar),
    )
    first = lambda: (pl.program_id(0) == 0) & (pl.program_id(1) == 0)
    last = lambda: (pl.program_id(0) == H - 1) & (pl.program_id(1) == len(qi) - 1)
    res = pl.pallas_call(
        _with_sidecar(sidecar, 7, 2, 3, body, first, last), name=name, grid_spec=grid_spec,
        out_shape=[jax.ShapeDtypeStruct((S, H * Dh), F32), jax.ShapeDtypeStruct((H, S, 1), F32)] + _sc_out(sidecar),
        compiler_params=_cp(("arbitrary", "arbitrary")),
    )(jnp.asarray(qi), jnp.asarray(kj), qkv, qkv, qkv, cum_col, cum_row, *_sc_arrays(sidecar))
    return res[0], res[1], res[2:]


def fox_attn_bwd_dq(qkv, do, o, lse, cum_col, cum_row, H, name):
    S = qkv.shape[0]
    Dh = qkv.shape[1] // (3 * H)
    T = _tile(S, FOX_TILE)
    n = S // T
    qi, kj = _causal_pairs(n, by_key=False)
    scale = Dh ** -0.5

    def body(qi_ref, kj_ref, q_ref, k_ref, v_ref, do_ref, o_ref, lse_ref, fq_ref, fk_ref, dq_ref, delta_ref, acc_ref, dl_ref,
             rs_ref):
        p_id = pl.program_id(1)
        i, j = qi_ref[p_id], kj_ref[p_id]

        @pl.when(j == 0)
        def _():
            acc_ref[...] = jnp.zeros_like(acc_ref)
            rs_ref[...] = jnp.zeros_like(rs_ref)
            dl_ref[...] = jnp.sum(do_ref[...].astype(F32) * o_ref[...].astype(F32), axis=1, keepdims=True)

        def tile(masked):
            s, mask = _fox_scores(q_ref[...], k_ref[...], fq_ref[0], fk_ref[0], T, scale, False)
            p = jnp.exp(s - lse_ref[0])
            if masked:
                p = jnp.where(mask, p, 0.0)
            dp = lax.dot_general(do_ref[...], v_ref[...], _DIMS["nt"], preferred_element_type=F32)
            ds = p * (dp - dl_ref[...])
            rs_ref[...] += jnp.sum(ds, axis=1, keepdims=True)
            acc_ref[...] += jnp.dot(ds.astype(BF16), k_ref[...], preferred_element_type=F32)

        _on_and_below_diagonal(i, j, tile)

        @pl.when(j == i)
        def _():
            dq_ref[...] = (acc_ref[...] * scale).astype(dq_ref.dtype)
            delta_ref[0] = dl_ref[...] + rs_ref[...]

    qspec = pl.BlockSpec((T, Dh), lambda h, p, qi, kj: (qi[p], h))
    colspec = pl.BlockSpec((1, T, 1), lambda h, p, qi, kj: (h, qi[p], 0))
    grid_spec = pltpu.PrefetchScalarGridSpec(
        num_scalar_prefetch=2, grid=(H, len(qi)),
        in_specs=[
            qspec,
            pl.BlockSpec((T, Dh), lambda h, p, qi, kj: (kj[p], H + h)),
            pl.BlockSpec((T, Dh), lambda h, p, qi, kj: (kj[p], 2 * H + h)),
            qspec, qspec, colspec, colspec,
            pl.BlockSpec((1, 1, T), lambda h, p, qi, kj: (h, 0, kj[p])),
        ],
        out_specs=[qspec, colspec],
        scratch_shapes=[pltpu.VMEM((T, Dh), F32), pltpu.VMEM((T, 1), F32), pltpu.VMEM((T, 1), F32)],
    )
    return pl.pallas_call(
        body, name=name, grid_spec=grid_spec,
        out_shape=[jax.ShapeDtypeStruct((S, H * Dh), BF16), jax.ShapeDtypeStruct((H, S, 1), F32)],
        compiler_params=_cp(("parallel", "arbitrary")),
    )(jnp.asarray(qi), jnp.asarray(kj), qkv, qkv, qkv, do, o, lse, cum_col, cum_row)


def fox_attn_bwd_dkv(qkv, do, lse_row, delta_row, cum_col, cum_row, H, name, sidecar=None):
    S = qkv.shape[0]
    Dh = qkv.shape[1] // (3 * H)
    T = _tile(S, FOX_TILE)
    n = S // T
    qi, kj = _causal_pairs(n, by_key=True)
    scale = Dh ** -0.5

    def body(qi_ref, kj_ref, q_ref, k_ref, v_ref, do_ref, lse_ref, dl_ref, fq_ref, fk_ref, dk_ref, dv_ref, dcum_ref,
             dk_acc, dv_acc, df_acc):
        p_id = pl.program_id(1)
        i, j = qi_ref[p_id], kj_ref[p_id]

        @pl.when(i == j)
        def _():
            dk_acc[...] = jnp.zeros_like(dk_acc)
            dv_acc[...] = jnp.zeros_like(dv_acc)
            df_acc[...] = jnp.zeros_like(df_acc)

        def tile(masked):
            sT, mask = _fox_scores(q_ref[...], k_ref[...], fq_ref[0], fk_ref[0], T, scale, True)
            pT = jnp.exp(sT - lse_ref[0])
            if masked:
                pT = jnp.where(mask, pT, 0.0)
            dv_acc[...] += jnp.dot(pT.astype(BF16), do_ref[...], preferred_element_type=F32)
            dpT = lax.dot_general(v_ref[...], do_ref[...], _DIMS["nt"], preferred_element_type=F32)
            dsT = pT * (dpT - dl_ref[0])
            dk_acc[...] += jnp.dot(dsT.astype(BF16), q_ref[...], preferred_element_type=F32)
            df_acc[...] -= jnp.sum(dsT, axis=1, keepdims=True)

        _on_and_below_diagonal(i, j, tile)

        @pl.when(i == n - 1)
        def _():
            dk_ref[...] = (dk_acc[...] * scale).astype(dk_ref.dtype)
            dv_ref[...] = dv_acc[...].astype(dv_ref.dtype)
            dcum_ref[0] = df_acc[...]

    qspec = pl.BlockSpec((T, Dh), lambda h, p, qi, kj: (qi[p], h))
    kspec = pl.BlockSpec((T, Dh), lambda h, p, qi, kj: (kj[p], H + h))
    vspec = pl.BlockSpec((T, Dh), lambda h, p, qi, kj: (kj[p], 2 * H + h))
    qrow = pl.BlockSpec((1, 1, T), lambda h, p, qi, kj: (h, 0, qi[p]))
    kcol = pl.BlockSpec((1, T, 1), lambda h, p, qi, kj: (h, kj[p], 0))
    grid_spec = pltpu.PrefetchScalarGridSpec(
        num_scalar_prefetch=2, grid=(H, len(qi)),
        in_specs=[qspec, kspec, vspec, qspec, qrow, qrow, qrow, kcol] + _sc_specs(sidecar, False),
        out_specs=[pl.BlockSpec((T, Dh), lambda h, p, qi, kj: (kj[p], h))] * 2 + [kcol] + _sc_specs(sidecar, True),
        scratch_shapes=[pltpu.VMEM((T, Dh), F32), pltpu.VMEM((T, Dh), F32), pltpu.VMEM((T, 1), F32)] + _sc_sems(sidecar),
    )
    out = jax.ShapeDtypeStruct((S, H * Dh), BF16)
    first = lambda: (pl.program_id(0) == 0) & (pl.program_id(1) == 0)
    last = lambda: (pl.program_id(0) == H - 1) & (pl.program_id(1) == len(qi) - 1)
    res = pl.pallas_call(
        _with_sidecar(sidecar, 10, 3, 3, body, first, last), name=name, grid_spec=grid_spec,
        out_shape=[out, out, jax.ShapeDtypeStruct((H, S, 1), F32)] + _sc_out(sidecar),
        compiler_params=_cp(("arbitrary", "arbitrary")),
    )(jnp.asarray(qi), jnp.asarray(kj), qkv, qkv, qkv, do, lse_row, delta_row, cum_row, cum_col, *_sc_arrays(sidecar))
    return res[0], res[1], res[2], res[3:]


def _sgu_ln(zu, zv, ln_g, ln_b):
    u = jax.nn.gelu(zu)
    v = jax.nn.gelu(zv)
    mu = jnp.mean(v, axis=-1, keepdims=True)
    var = jnp.mean(jnp.square(v - mu), axis=-1, keepdims=True)
    return u, (v - mu) * lax.rsqrt(var + EPS) * ln_g + ln_b


def _tril_mask():
    r = lax.broadcasted_iota(jnp.int32, (SEQ_BLOCK, SEQ_BLOCK), 0)
    c = lax.broadcasted_iota(jnp.int32, (SEQ_BLOCK, SEQ_BLOCK), 1)
    return r >= c


def _sgu_spatial(ws_ref, bsT, selT, vn, G):
    tril = _tril_mask()
    fs = []
    for g in range(G):
        wg = jnp.where(tril, ws_ref[g], 0.0).astype(BF16)
        fs.append(jnp.dot(wg, vn[:, g * SEQ_BLOCK:(g + 1) * SEQ_BLOCK].astype(BF16), preferred_element_type=F32))
    bias = jnp.dot(bsT, selT, precision=lax.Precision.HIGHEST, preferred_element_type=F32)
    return jnp.concatenate(fs, axis=1) + bias


def _sgu_specs(W, G):
    return [
        pl.BlockSpec((SEQ_BLOCK, 2 * W), lambda n: (n, 0)),
        pl.BlockSpec((1, W), lambda n: (0, 0)),
        pl.BlockSpec((1, W), lambda n: (0, 0)),
        pl.BlockSpec((G, SEQ_BLOCK, SEQ_BLOCK), lambda n: (0, 0, 0)),
        pl.BlockSpec((SEQ_BLOCK, G), lambda n: (0, 0)),
        pl.BlockSpec((G, W), lambda n: (0, 0)),
    ]


def sgu_fwd(zp, ln_g, ln_b, ws, bsT, selT, name):
    S, W2 = zp.shape
    W = W2 // 2
    G = ws.shape[0]

    def body(z_ref, lg_ref, lb_ref, ws_ref, bs_ref, sel_ref, o_ref):
        u, vn = _sgu_ln(z_ref[:, :W], z_ref[:, W:], lg_ref[...], lb_ref[...])
        o_ref[...] = (u * _sgu_spatial(ws_ref, bs_ref[...], sel_ref[...], vn, G)).astype(o_ref.dtype)

    return pl.pallas_call(
        body, name=name, grid=(S // SEQ_BLOCK,), in_specs=_sgu_specs(W, G), out_specs=pl.BlockSpec((SEQ_BLOCK, W), lambda n: (n, 0)),
        out_shape=jax.ShapeDtypeStruct((S, W), BF16), compiler_params=_cp(("parallel",)),
    )(zp, ln_g, ln_b, ws, bsT, selT)


def sgu_bwd(zp, ln_g, ln_b, ws, bsT, selT, dgated, name):
    S, W2 = zp.shape
    W = W2 // 2
    G = ws.shape[0]

    def body(z_ref, lg_ref, lb_ref, ws_ref, bs_ref, sel_ref, dgt_ref, dz_ref, dlg_ref, dlb_ref, dws_ref, dbs_ref):
        (u, vn), vjp = jax.vjp(_sgu_ln, z_ref[:, :W], z_ref[:, W:], lg_ref[...], lb_ref[...])
        f = _sgu_spatial(ws_ref, bs_ref[...], sel_ref[...], vn, G)
        dgt = dgt_ref[...].astype(F32)
        du, df = dgt * f, dgt * u

        @pl.when(pl.program_id(0) == 0)
        def _():
            dlg_ref[...] = jnp.zeros_like(dlg_ref)
            dlb_ref[...] = jnp.zeros_like(dlb_ref)
            dws_ref[...] = jnp.zeros_like(dws_ref)
            dbs_ref[...] = jnp.zeros_like(dbs_ref)

        dbs_ref[...] += lax.dot_general(df, sel_ref[...], _DIMS["nt"], precision=lax.Precision.HIGHEST, preferred_element_type=F32)
        tril = _tril_mask()
        dvn = []
        for g in range(G):
            sl = slice(g * SEQ_BLOCK, (g + 1) * SEQ_BLOCK)
            wg = jnp.where(tril, ws_ref[g], 0.0).astype(BF16)
            df_g = df[:, sl].astype(BF16)
            dw = lax.dot_general(df_g, vn[:, sl].astype(BF16), _DIMS["nt"], preferred_element_type=F32)
            dws_ref[g] += jnp.where(tril, dw, 0.0)
            dvn.append(lax.dot_general(wg, df_g, _DIMS["tn"], preferred_element_type=F32))
        dzu, dzv, dlg, dlb = vjp((du, jnp.concatenate(dvn, axis=1)))
        dz_ref[:, :W] = dzu.astype(dz_ref.dtype)
        dz_ref[:, W:] = dzv.astype(dz_ref.dtype)
        dlg_ref[...] += dlg
        dlb_ref[...] += dlb

    vec = jax.ShapeDtypeStruct((1, W), F32)
    return pl.pallas_call(
        body, name=name, grid=(S // SEQ_BLOCK,),
        in_specs=_sgu_specs(W, G) + [pl.BlockSpec((SEQ_BLOCK, W), lambda n: (n, 0))],
        out_specs=[
            pl.BlockSpec((SEQ_BLOCK, 2 * W), lambda n: (n, 0)),
            pl.BlockSpec((1, W), lambda n: (0, 0)),
            pl.BlockSpec((1, W), lambda n: (0, 0)),
            pl.BlockSpec((G, SEQ_BLOCK, SEQ_BLOCK), lambda n: (0, 0, 0)),
            pl.BlockSpec((SEQ_BLOCK, G), lambda n: (0, 0)),
        ],
        out_shape=[jax.ShapeDtypeStruct((S, W2), BF16), vec, vec, jax.ShapeDtypeStruct(ws.shape, F32), jax.ShapeDtypeStruct((SEQ_BLOCK, G), F32)],
        compiler_params=_cp(("arbitrary",)),
    )(zp, ln_g, ln_b, ws, bsT, selT, dgated)


def _rope_matrix():
    half = ROPE_DIM // 2
    R = np.zeros((SWA_HEAD_DIM, SWA_HEAD_DIM), np.float32)
    for j in range(half):
        R[j + half, j] = -1.0
        R[j, j + half] = 1.0
    return R


def _swa_block(q4, kp, kc, vp, vc, sink, Cq, Sq, Cp, Sp, R, n, G):
    B, Dh = SEQ_BLOCK, SWA_HEAD_DIM
    rot = lambda t: jnp.dot(t, R, precision=lax.Precision.HIGHEST, preferred_element_type=F32)
    q = q4.reshape(G * B, Dh)
    Cq4 = jnp.concatenate([Cq] * G, axis=0)
    Sq4 = jnp.concatenate([Sq] * G, axis=0)
    qr = q * Cq4 + rot(q) * Sq4
    kb = jnp.concatenate([kp * Cp + rot(kp) * Sp, kc * Cq + rot(kc) * Sq], axis=0)
    vb = jnp.concatenate([vp, vc], axis=0)
    s = lax.dot_general(qr.astype(BF16), kb.astype(BF16), _DIMS["nt"], preferred_element_type=F32) * (Dh ** -0.5)
    qi = lax.broadcasted_iota(jnp.int32, (G * B, 2 * B), 0) & (B - 1)
    ki = lax.broadcasted_iota(jnp.int32, (G * B, 2 * B), 1) - B
    rel = qi - ki
    valid = (rel >= 0) & (rel < B) & (n * B + ki >= 0)
    s = jnp.where(valid, s, NEG)
    m = lax.stop_gradient(jnp.maximum(jnp.max(s, axis=1, keepdims=True), sink))
    p = jnp.exp(s - m)
    p = p / (jnp.sum(p, axis=1, keepdims=True) + jnp.exp(sink - m))
    o = jnp.dot(p.astype(BF16), vb.astype(BF16), preferred_element_type=F32)
    return o.reshape(G, B, Dh)


def _swa_specs(G):
    B, Dh = SEQ_BLOCK, SWA_HEAD_DIM
    prev = lambda n: jnp.maximum(n - 1, 0)
    return [
        pl.BlockSpec((G, B, Dh), lambda h, n: (h, n, 0)),
        pl.BlockSpec((1, B, Dh), lambda h, n: (h, prev(n), 0)),
        pl.BlockSpec((1, B, Dh), lambda h, n: (h, n, 0)),
        pl.BlockSpec((1, B, Dh), lambda h, n: (h, prev(n), 0)),
        pl.BlockSpec((1, B, Dh), lambda h, n: (h, n, 0)),
        pl.BlockSpec((1, G * B, 1), lambda h, n: (h, 0, 0)),
        pl.BlockSpec((B, Dh), lambda h, n: (n, 0)),
        pl.BlockSpec((B, Dh), lambda h, n: (n, 0)),
        pl.BlockSpec((B, Dh), lambda h, n: (prev(n), 0)),
        pl.BlockSpec((B, Dh), lambda h, n: (prev(n), 0)),
        pl.BlockSpec((Dh, Dh), lambda h, n: (0, 0)),
    ]


def swa_fwd(qh, kh, vh, sink_col, C, Sn, R, name):
    Hq, S, Dh = qh.shape
    Hk = kh.shape[0]
    G = Hq // Hk

    def body(q_ref, kp_ref, kc_ref, vp_ref, vc_ref, sk_ref, cq_ref, sq_ref, cp_ref, sp_ref, r_ref, o_ref):
        o = _swa_block(q_ref[...], kp_ref[0], kc_ref[0], vp_ref[0], vc_ref[0], sk_ref[0], cq_ref[...], sq_ref[...], cp_ref[...],
                       sp_ref[...], r_ref[...], pl.program_id(1), G)
        o_ref[...] = o.astype(o_ref.dtype)

    return pl.pallas_call(
        body, name=name, grid=(Hk, S // SEQ_BLOCK), in_specs=_swa_specs(G),
        out_specs=pl.BlockSpec((G, SEQ_BLOCK, Dh), lambda h, n: (h, n, 0)),
        out_shape=jax.ShapeDtypeStruct((Hq, S, Dh), BF16), compiler_params=_cp(("parallel", "parallel")),
    )(qh, kh, kh, vh, vh, sink_col, C, Sn, C, Sn, R)


def swa_bwd(qh, kh, vh, sink_col, C, Sn, R, doh, name):
    Hq, S, Dh = qh.shape
    Hk = kh.shape[0]
    G = Hq // Hk
    B = SEQ_BLOCK

    def body(q_ref, kp_ref, kc_ref, vp_ref, vc_ref, sk_ref, cq_ref, sq_ref, cp_ref, sp_ref, r_ref, do_ref,
             dq_ref, dkp_ref, dkc_ref, dvp_ref, dvc_ref, dsk_ref):
        n = pl.program_id(1)
        fn = lambda q4, kp, kc, vp, vc, sk: _swa_block(q4, kp, kc, vp, vc, sk, cq_ref[...], sq_ref[...], cp_ref[...], sp_ref[...],
                                                      r_ref[...], n, G)
        _, vjp = jax.vjp(fn, q_ref[...], kp_ref[0], kc_ref[0], vp_ref[0], vc_ref[0], sk_ref[0])
        dq, dkp, dkc, dvp, dvc, dsk = vjp(do_ref[...].astype(F32))
        dq_ref[...] = dq
        dkp_ref[0] = dkp
        dkc_ref[0] = dkc
        dvp_ref[0] = dvp
        dvc_ref[0] = dvc

        @pl.when(n == 0)
        def _():
            dsk_ref[...] = jnp.zeros_like(dsk_ref)

        for g in range(G):
            part = jnp.sum(dsk[g * B:(g + 1) * B], axis=0, keepdims=True)
            dsk_ref[0, g:g + 1, :] += jnp.broadcast_to(part, (1, LANES))

    qspec = pl.BlockSpec((G, B, Dh), lambda h, n: (h, n, 0))
    kspec = pl.BlockSpec((1, B, Dh), lambda h, n: (h, n, 0))
    kshape = jax.ShapeDtypeStruct((Hk, S, Dh), F32)
    return pl.pallas_call(
        body, name=name, grid=(Hk, S // B), in_specs=_swa_specs(G) + [qspec],
        out_specs=[qspec, kspec, kspec, kspec, kspec, pl.BlockSpec((1, G, LANES), lambda h, n: (h, 0, 0))],
        out_shape=[jax.ShapeDtypeStruct((Hq, S, Dh), F32), kshape, kshape, kshape, kshape, jax.ShapeDtypeStruct((Hk, G, LANES), F32)],
        compiler_params=_cp(("parallel", "arbitrary")),
    )(qh, kh, kh, vh, vh, sink_col, C, Sn, C, Sn, R, doh)


def shift_add(cur, prev, name):
    Hk, S, Dh = cur.shape
    B = SEQ_BLOCK
    nb = S // B

    def body(c_ref, p_ref, o_ref):
        last = pl.program_id(1) == nb - 1
        o_ref[...] = c_ref[...] + jnp.where(last, 0.0, p_ref[...])

    spec = pl.BlockSpec((1, B, Dh), lambda h, n: (h, n, 0))
    nxt = pl.BlockSpec((1, B, Dh), lambda h, n: (h, jnp.minimum(n + 1, nb - 1), 0))
    return pl.pallas_call(
        body, name=name, grid=(Hk, nb), in_specs=[spec, nxt], out_specs=spec, out_shape=jax.ShapeDtypeStruct(cur.shape, F32),
        compiler_params=_cp(("parallel", "parallel")),
    )(cur, prev)


def loss_head(y, target, name):
    S, D = y.shape
    tr = _tile(S, ROW_TILE, 16)

    def body(y_ref, t_ref, acc_ref, dy_ref):
        err = y_ref[...] - t_ref[...]
        dy_ref[...] = err * (1.0 / D)

        @pl.when(pl.program_id(0) == 0)
        def _():
            acc_ref[...] = jnp.zeros_like(acc_ref)

        acc_ref[...] += jnp.broadcast_to(jnp.sum(err * err).reshape(1, 1), (1, LANES))

    return pl.pallas_call(
        body, name=name, grid=(S // tr,), in_specs=[_row_spec(tr, D)] * 2,
        out_specs=[pl.BlockSpec((1, LANES), lambda i: (0, 0)), _row_spec(tr, D)],
        out_shape=[jax.ShapeDtypeStruct((1, LANES), F32), jax.ShapeDtypeStruct((S, D), F32)], compiler_params=_cp(("arbitrary",)),
    )(y, target)


def _adam_update(w, g, m, v):
    m = ADAM_B1 * m + (1.0 - ADAM_B1) * g
    v = ADAM_B2 * v + (1.0 - ADAM_B2) * jnp.square(g)
    m_hat = m / (1.0 - ADAM_B1 ** ADAM_STEP)
    v_hat = v / (1.0 - ADAM_B2 ** ADAM_STEP)
    delta = -ADAM_LR * (m_hat / (jnp.sqrt(v_hat) + ADAM_EPS) + ADAM_WD * w)
    return delta, m, v


def adamw(w, m, v, gparts, name, gstack=0, emit_g=True):
    R, C = w.shape
    tr = _tile(R, max(8, (128 * 1024) // C), 8)
    spec = pl.BlockSpec((tr, C), lambda i: (i, 0))
    nplain = len(gparts) - (1 if gstack else 0)
    nout = 4 if emit_g else 3

    def body(w_ref, m_ref, v_ref, *rest):
        g_refs, outs = rest[:len(gparts)], rest[len(gparts):]
        g = None
        for r in g_refs[:nplain]:
            g = r[...].astype(F32) if g is None else g + r[...].astype(F32)
        if gstack:
            for t in range(gstack):
                part = g_refs[-1][t].astype(F32)
                g = part if g is None else g + part
        res = _adam_update(w_ref[...], g, m_ref[...], v_ref[...])
        for o_ref, val in zip(outs, ((g,) if emit_g else ()) + res):
            o_ref[...] = val

    gspecs = [spec] * nplain + ([pl.BlockSpec((gstack, tr, C), lambda i: (0, i, 0))] if gstack else [])
    out = jax.ShapeDtypeStruct((R, C), F32)
    return pl.pallas_call(
        body, name=name, grid=(R // tr,), in_specs=[spec] * 3 + gspecs, out_specs=[spec] * nout, out_shape=[out] * nout,
        compiler_params=_cp(("parallel",)),
    )(w, m, v, *gparts)


def ada_fwd(c_all, ada_w, ada_b, name):
    L, D, N = ada_w.shape
    Bp = c_all.shape[0]
    tn = _tile(N, 512)

    def body(c_ref, w_ref, b_ref, o_ref):
        ca = jax.nn.silu(c_ref[...]).astype(BF16)
        o_ref[0] = jnp.dot(ca, w_ref[0].astype(BF16), preferred_element_type=F32) + b_ref[0]

    return pl.pallas_call(
        body, name=name, grid=(L, N // tn),
        in_specs=[pl.BlockSpec((Bp, D), lambda l, j: (0, 0)), pl.BlockSpec((1, D, tn), lambda l, j: (l, 0, j)),
                  pl.BlockSpec((1, 1, tn), lambda l, j: (l, 0, j))],
        out_specs=pl.BlockSpec((1, Bp, tn), lambda l, j: (l, 0, j)), out_shape=jax.ShapeDtypeStruct((L, Bp, N), F32),
        compiler_params=_cp(("parallel", "parallel")),
    )(c_all, ada_w, ada_b)


def ada_wgrad(c_all, dmod, name):
    L, Bp, N = dmod.shape
    D = c_all.shape[1]
    tn = _tile(N, 512)

    def body(c_ref, d_ref, o_ref):
        ca = jax.nn.silu(c_ref[...]).astype(BF16)
        o_ref[0] = lax.dot_general(ca, d_ref[0].astype(BF16), _DIMS["tn"], preferred_element_type=F32)

    return pl.pallas_call(
        body, name=name, grid=(L, N // tn),
        in_specs=[pl.BlockSpec((Bp, D), lambda l, j: (0, 0)), pl.BlockSpec((1, Bp, tn), lambda l, j: (l, 0, j))],
        out_specs=pl.BlockSpec((1, D, tn), lambda l, j: (l, 0, j)), out_shape=jax.ShapeDtypeStruct((L, D, N), F32),
        compiler_params=_cp(("parallel", "parallel")),
    )(c_all, dmod)


N_DEV = 8
N_CHIP = 4
ANY = pl.BlockSpec(memory_space=pl.ANY)


def _place():
    return lax.axis_index("x"), lax.axis_index("y"), lax.axis_index("c")


def _other_chips(x, y):
    chips = [(1 - x, y), (x, 1 - y), (1 - x, 1 - y)]
    return chips, [2 * cx + cy for cx, cy in chips]


def _rcopy(src, dst, ssem, rsem, to):
    return pltpu.make_async_remote_copy(src_ref=src, dst_ref=dst, send_sem=ssem, recv_sem=rsem, device_id=to, device_id_type=MESH)


def ag_small(xs, name):
    R, Wd = xs.shape

    def body(x_ref, out_ref, send_sems, recv_sems, local_sem):
        x, y, c = _place()
        me, sibling = (x, y, c), (x, y, 1 - c)
        chips, _ = _other_chips(x, y)

        def slot(px, py, pc):
            return out_ref.at[4 * px + 2 * py + pc]

        def copy(k, block, to, src=None):
            return _rcopy(slot(*block) if src is None else src, slot(*block), send_sems.at[k], recv_sems.at[k], to)

        mine = pltpu.make_async_copy(x_ref, slot(*me), local_sem)
        mine.start()
        first = [copy(0, me, sibling, src=x_ref)]
        first += [copy(1 + j, me, (*chip, c), src=x_ref) for j, chip in enumerate(chips)]
        for cp in first:
            cp.start()
        passed = [copy(4 + j, (*chip, c), sibling) for j, chip in enumerate(chips)]
        for j, chip in enumerate(chips):
            copy(1 + j, (*chip, c), me).wait_recv()
            passed[j].start()
        copy(0, sibling, me).wait_recv()
        for j, chip in enumerate(chips):
            copy(4 + j, (*chip, 1 - c), me).wait_recv()
        for cp in first + passed:
            cp.wait_send()
        mine.wait()

    vm = pl.BlockSpec(memory_space=pltpu.VMEM)
    return pl.pallas_call(
        body, name=name, out_shape=jax.ShapeDtypeStruct((N_DEV, R, Wd), xs.dtype), in_specs=[vm], out_specs=vm,
        scratch_shapes=[pltpu.SemaphoreType.DMA((7,)), pltpu.SemaphoreType.DMA((7,)), pltpu.SemaphoreType.DMA],
        compiler_params=_cp(),
    )(xs)


def _half_of_shard(by_cols, A, B, h):
    return (h * (A // 2), A // 2, 0, B) if by_cols else (0, A, h * (B // 2), B // 2)


def _shard_in_full(by_cols, A, B, q):
    return (0, q * B) if by_cols else (q * A, 0)


def _window(ref, r0, nr, c0, nc):
    return ref.at[:, pl.ds(r0, nr), pl.ds(c0, nc)]


def ag_weights(shards, by_cols, name):
    n = len(shards)
    geo, full = _ag_shapes(shards, by_cols)

    def body(*refs):
        start, finish = _ag_steps(geo, refs[:n], refs[n:2 * n], refs[2 * n:])
        start()
        finish()

    return pl.pallas_call(
        body, name=name, out_shape=full, in_specs=[ANY] * n, out_specs=[ANY] * n, scratch_shapes=_ag_semaphores(n),
        compiler_params=_cp(),
    )(*shards)


def _ag_shapes(shards, by_cols):
    geo = [(bc,) + s.shape[1:] for bc, s in zip(by_cols, shards)]
    full = [jax.ShapeDtypeStruct((s.shape[0], A, N_CHIP * B) if bc else (s.shape[0], N_CHIP * A, B), s.dtype)
            for (bc, A, B), s in zip(geo, shards)]
    return geo, full


def _ag_semaphores(n):
    return [pltpu.SemaphoreType.DMA((n, 3)) for _ in range(4)]


def _ag_steps(geo, x_refs, o_refs, sems):
    s_ici, r_ici, s_d2d, r_d2d = sems
    pairs = [(t, j) for t in range(len(geo)) for j in range(3)]

    def copies():
        x, y, c = _place()
        q = 2 * x + y
        chips, qs = _other_chips(x, y)

        def landing(t, chip_q, half):
            r0, nr, c0, nc = _half_of_shard(*geo[t], half)
            ro, co = _shard_in_full(*geo[t], chip_q)
            return _window(o_refs[t], ro + r0, nr, co + c0, nc)

        def ici(t, j, landing_q):
            src = _window(x_refs[t], *_half_of_shard(*geo[t], c))
            return _rcopy(src, landing(t, landing_q, c), s_ici.at[t, j], r_ici.at[t, j], (*chips[j], c))

        def handoff(t, j, half):
            blk = landing(t, qs[j], half)
            return _rcopy(blk, blk, s_d2d.at[t, j], r_d2d.at[t, j], (x, y, 1 - c))

        return c, q, qs, ici, handoff

    def start():
        c, q, qs, ici, handoff = copies()
        for t, j in pairs:
            ici(t, j, q).start()

    def finish():
        c, q, qs, ici, handoff = copies()
        for t, j in pairs:
            ici(t, j, qs[j]).wait_recv()
            handoff(t, j, c).start()
        for t, j in pairs:
            handoff(t, j, 1 - c).wait_recv()
        for t, j in pairs:
            ici(t, j, q).wait_send()
            handoff(t, j, c).wait_send()

    return start, finish


def _half_of_full(by_cols, A, B, h):
    return (h * (A // 2), A // 2, 0, N_CHIP * B) if by_cols else (0, N_CHIP * A, h * (B // 2), B // 2)


def _half_shape(by_cols, L, A, B):
    return (L, A // 2, N_CHIP * B) if by_cols else (L, N_CHIP * A, B // 2)


def _piece_shape(by_cols, L, A, B):
    return (L, A // 2, B) if by_cols else (L, A, B // 2)


def sibling_fold(gs, geo, name):
    n = len(gs)

    def body(*refs):
        x_refs, o_refs, (ssem, rsem) = refs[:n], refs[n:2 * n], refs[2 * n:]
        x, y, c = _place()
        cps = [_rcopy(_window(x_refs[t], *_half_of_full(*geo[t], 1 - c)), o_refs[t], ssem.at[t], rsem.at[t], (x, y, 1 - c))
               for t in range(n)]
        for cp in cps:
            cp.start()
        for cp in cps:
            cp.wait()

    dma = pltpu.SemaphoreType.DMA
    out = [jax.ShapeDtypeStruct(_half_shape(bc, g.shape[0], A, B), g.dtype) for (bc, A, B), g in zip(geo, gs)]
    return pl.pallas_call(
        body, name=name, out_shape=out, in_specs=[ANY] * n, out_specs=[ANY] * n, scratch_shapes=[dma((n,)), dma((n,))],
        compiler_params=_cp(),
    )(*gs)


def chip_exchange(rs, geo, name):
    sc = exchange_sidecar(rs, geo)
    n = len(rs)

    def body(*refs):
        start, finish = sc.steps(refs[:n], refs[n:2 * n], refs[2 * n:])
        start()
        finish()

    return pl.pallas_call(
        body, name=name, out_shape=sc.out_shape, in_specs=[ANY] * n, out_specs=[ANY] * n, scratch_shapes=sc.semaphores,
        compiler_params=_cp(),
    )(*rs)


def exchange_sidecar(rs, geo):
    n = len(rs)

    def steps(x_refs, o_refs, sems):
        ssem, rsem = sems

        def copies():
            x, y, c = _place()
            chips, qs = _other_chips(x, y)

            def part(t, chip_q):
                bc, A, B = geo[t]
                return _window(x_refs[t], 0, A // 2, chip_q * B, B) if bc else _window(x_refs[t], chip_q * A, A, 0, B // 2)

            return [_rcopy(part(t, qs[j]), o_refs[t].at[j], ssem.at[t, j], rsem.at[t, j], (*chips[j], c))
                    for t in range(n) for j in range(3)]

        def start():
            for cp in copies():
                cp.start()

        def finish():
            for cp in copies():
                cp.wait()

        return start, finish

    dma = pltpu.SemaphoreType.DMA
    out = [jax.ShapeDtypeStruct((3,) + _piece_shape(bc, r.shape[0], A, B), r.dtype) for (bc, A, B), r in zip(geo, rs)]
    return SideCar(list(rs), out, [dma((n, 3)), dma((n, 3))], steps)


def ag_sidecar(shards, by_cols):
    geo, full = _ag_shapes(shards, by_cols)
    return SideCar(list(shards), full, _ag_semaphores(len(shards)), lambda ins, outs, sems: _ag_steps(geo, ins, outs, sems))


def sibling_share(fs, geo, name):
    n = len(fs)

    def body(*refs):
        x_refs, o_refs, (ssem, rsem) = refs[:n], refs[n:2 * n], refs[2 * n:]
        x, y, c = _place()
        for t in range(n):
            mine = _window(o_refs[t], *_half_of_shard(*geo[t], c))
            _rcopy(mine, mine, ssem.at[t], rsem.at[t], (x, y, 1 - c)).start()
        for t in range(n):
            mine = _window(o_refs[t], *_half_of_shard(*geo[t], c))
            theirs = _window(o_refs[t], *_half_of_shard(*geo[t], 1 - c))
            _rcopy(mine, theirs, ssem.at[t], rsem.at[t], (x, y, 1 - c)).wait_recv()
            _rcopy(mine, mine, ssem.at[t], rsem.at[t], (x, y, 1 - c)).wait_send()
        del x_refs

    dma = pltpu.SemaphoreType.DMA
    return pl.pallas_call(
        body, name=name, out_shape=[jax.ShapeDtypeStruct(f.shape, f.dtype) for f in fs], in_specs=[ANY] * n, out_specs=[ANY] * n,
        input_output_aliases={t: t for t in range(n)}, scratch_shapes=[dma((n,)), dma((n,))], compiler_params=_cp(),
    )(*fs)


SUM_ROWS = 256


def fold_sum(g, recv, by_cols, A, B, qc_idx, name):
    L = g.shape[0]
    _, hr, hc = _half_shape(by_cols, L, A, B)
    tr, tc = _tile(A // 2 if by_cols else A, SUM_ROWS, 16), (B if by_cols else B // 2)
    ro, co = ((A // 2) // tr, 0) if by_cols else (0, 1)

    def body(qc_ref, g_ref, r_ref, o_ref):
        del qc_ref
        o_ref[...] = (g_ref[...].astype(F32) + r_ref[...].astype(F32)).astype(o_ref.dtype)

    spec = pl.BlockSpec((1, tr, tc), lambda l, i, j, qc: (l, i, j))
    grid_spec = pltpu.PrefetchScalarGridSpec(
        num_scalar_prefetch=1, grid=(L, hr // tr, hc // tc),
        in_specs=[pl.BlockSpec((1, tr, tc), lambda l, i, j, qc: (l, i + qc[1] * ro, j + qc[1] * co)), spec], out_specs=spec,
    )
    return pl.pallas_call(
        body, name=name, grid_spec=grid_spec, out_shape=jax.ShapeDtypeStruct((L, hr, hc), BF16),
        compiler_params=_cp(("parallel", "parallel", "parallel")),
    )(qc_idx, g, recv)


def chip_sum(r, ex, by_cols, A, B, qc_idx, name):
    L = r.shape[0]
    _, wr, wc = _piece_shape(by_cols, L, A, B)
    tr = _tile(wr, SUM_ROWS, 16)
    r_ro, r_co = (0, 1) if by_cols else (A // tr, 0)
    o_ro, o_co = ((A // 2) // tr, 0) if by_cols else (0, 1)

    def body(qc_ref, r_ref, e_ref, o_ref):
        del qc_ref
        o_ref[0] = ((r_ref[0].astype(F32) + e_ref[0, 0].astype(F32)) + e_ref[1, 0].astype(F32)) + e_ref[2, 0].astype(F32)

    grid_spec = pltpu.PrefetchScalarGridSpec(
        num_scalar_prefetch=1, grid=(L, wr // tr),
        in_specs=[pl.BlockSpec((1, tr, wc), lambda l, i, qc: (l, i + qc[0] * r_ro, qc[0] * r_co)),
                  pl.BlockSpec((3, 1, tr, wc), lambda l, i, qc: (0, l, i, 0))],
        out_specs=pl.BlockSpec((1, tr, wc), lambda l, i, qc: (l, i + qc[1] * o_ro, qc[1] * o_co)),
    )
    return pl.pallas_call(
        body, name=name, grid_spec=grid_spec, out_shape=jax.ShapeDtypeStruct((L, A, B), F32),
        compiler_params=_cp(("parallel", "parallel")),
    )(qc_idx, r, ex)


BIG = ("ffn_w_gu", "ffn_w_down", "fox_w_in", "fox_w_out", "sgu_w_in", "sgu_w_out", "swa_w_in", "swa_w_out")
COLUMN_SHARDED = ("ffn_w_gu", "fox_w_in", "sgu_w_in", "swa_w_in")
SMALL = ("ada_b", "mix_pre_g", "mix_post_g", "ffn_pre_g", "ffn_post_g", "fox_b_f", "sgu_ln_g", "sgu_ln_b", "sgu_w_s", "sgu_b_s",
         "swa_sinks")
WEIGHTS = ("ada_w", "ada_b", "mix_pre_g", "mix_post_g", "ffn_pre_g", "ffn_post_g", "ffn_w_gu", "ffn_w_down", "fox_w_in", "fox_b_f",
           "fox_w_out", "sgu_w_in", "sgu_ln_g", "sgu_ln_b", "sgu_w_s", "sgu_b_s", "sgu_w_out", "swa_w_in", "swa_sinks", "swa_w_out")
INPUTS = ("x", "c", "positions") + WEIGHTS + ("loss_target",) + tuple("m_" + n for n in WEIGHTS) + tuple("v_" + n for n in WEIGHTS)


def _lane_pad(n):
    return (-n) % LANES


def _pad_shard_columns(t, B):
    if _lane_pad(B) == 0:
        return t
    L, A, _ = t.shape
    return jnp.pad(t.reshape(L, A, N_CHIP, B), ((0, 0), (0, 0), (0, 0), (0, _lane_pad(B)))).reshape(L, A, -1)


def _unpad_shard_columns(t, B):
    if _lane_pad(B) == 0:
        return t
    L, A, _ = t.shape
    return t.reshape(L, A, N_CHIP, B + _lane_pad(B))[..., :B].reshape(L, A, N_CHIP * B)


def place_shard(full, shard, by_cols, qc_idx, name):
    L, A, B = shard.shape
    tr = _tile(A, SUM_ROWS, 16)
    ro, co = (0, 1) if by_cols else (A // tr, 0)

    def body(qc_ref, s_ref, f_ref, o_ref):
        del qc_ref, f_ref
        o_ref[...] = s_ref[...]

    grid_spec = pltpu.PrefetchScalarGridSpec(
        num_scalar_prefetch=1, grid=(L, A // tr), in_specs=[pl.BlockSpec((1, tr, B), lambda l, i, qc: (l, i, 0)), ANY],
        out_specs=pl.BlockSpec((1, tr, B), lambda l, i, qc: (l, i + qc[0] * ro, qc[0] * co)),
    )
    return pl.pallas_call(
        body, name=name, grid_spec=grid_spec, out_shape=jax.ShapeDtypeStruct(full.shape, full.dtype),
        input_output_aliases={2: 0}, compiler_params=_cp(("parallel", "parallel")),
    )(qc_idx, shard, full)


def _pad_rows(flat1d):
    n = flat1d.shape[0]
    pad = (-n) % (8 * LANES)
    return jnp.pad(flat1d, (0, pad)).reshape(-1, LANES)


def _pack_small(parts):
    return jnp.concatenate([_pad_rows(parts[n].astype(F32).reshape(-1)) for n in SMALL], axis=0)


def _unpack_small(packed, shapes):
    out, off = {}, 0
    for n in SMALL:
        size = int(np.prod(shapes[n]))
        rows = (size + 8 * LANES - 1) // (8 * LANES) * 8
        out[n] = packed[off:off + rows].reshape(-1)[:size].reshape(shapes[n])
        off += rows
    return out


def _fox_fwd(h, w_in, b_f, w_out, tag, sidecar=None):
    S, D = h.shape
    H = b_f.shape[0]
    qkv = mm(h, w_in, "nn", BF16, name=tag + "_qkv", b_cols=(0, 3 * D))
    fgp = mm(h, w_in, "nn", F32, name=tag + "_fg", b_cols=(3 * D, LANES))
    fgT = fgp[:, :H].T
    cum = fox_gate_fwd(fgT, b_f.reshape(H, 1), tag + "_gate")
    cum_col, cum_row = cum.reshape(H, S, 1), cum.reshape(H, 1, S)
    o, lse, rode = fox_attn_fwd(qkv, cum_col, cum_row, H, tag + "_attn", sidecar)
    y = mm(o, w_out, "nn", F32, name=tag + "_out")
    return y, (qkv, fgT, cum_col, cum_row, o, lse), rode


def _fox_bwd(dy, h, w_in, b_f, w_out, ctx, tag, sidecar=None):
    qkv, fgT, cum_col, cum_row, o, lse = ctx
    S, D = h.shape
    H = b_f.shape[0]
    do = mm(dy, w_out, "nt", BF16, name=tag + "_do")
    dw_out = mm(o, dy, "tn", BF16, name=tag + "_dwout")
    dq, delta = fox_attn_bwd_dq(qkv, do, o, lse, cum_col, cum_row, H, tag + "_dq")
    dk, dv, dcum, rode = fox_attn_bwd_dkv(qkv, do, lse.reshape(H, 1, S), delta.reshape(H, 1, S), cum_col, cum_row, H, tag + "_dkv",
                                          sidecar)
    dfgT, db = fox_gate_bwd(dcum.reshape(H, S), fgT, b_f.reshape(H, 1), tag + "_dgate")
    dfgp = jnp.pad(dfgT.T, ((0, 0), (0, LANES - H))).astype(BF16)
    dh = mm(dq, w_in, "nt", F32, name=tag + "_dhq", b_cols=(0, D))
    dh = mm(dk, w_in, "nt", F32, add=dh, name=tag + "_dhk", b_cols=(D, D))
    dh = mm(dv, w_in, "nt", F32, add=dh, name=tag + "_dhv", b_cols=(2 * D, D))
    dh = mm(dfgp, w_in, "nt", F32, add=dh, name=tag + "_dhf", b_cols=(3 * D, LANES))
    dw_in = jnp.concatenate(
        [mm(h, dq, "tn", BF16, name=tag + "_dwq"), mm(h, dk, "tn", BF16, name=tag + "_dwk"), mm(h, dv, "tn", BF16, name=tag + "_dwv"),
         mm(h, dfgp, "tn", BF16, name=tag + "_dwf")[:, :H]], axis=1)
    return dh, dw_in, dw_out, db[:, 0], rode


def _sgu_consts(G, W):
    return jnp.asarray(np.repeat(np.eye(G, dtype=np.float32), W // G, axis=1))


def _sgu_fwd(h, w_in, ln_g, ln_b, w_s, b_s, w_out, tag):
    G, W = w_s.shape[0], ln_g.shape[0]
    zp = mm(h, w_in, "nn", F32, name=tag + "_in")
    args = (zp, ln_g.reshape(1, W), ln_b.reshape(1, W), w_s, b_s.T, _sgu_consts(G, W))
    gated = sgu_fwd(*args, tag + "_core")
    y = mm(gated, w_out, "nn", F32, name=tag + "_out")
    return y, (args, gated)


def _sgu_bwd(dy, h, w_in, w_out, ctx, tag):
    args, gated = ctx
    dgated = mm(dy, w_out, "nt", BF16, name=tag + "_dgated")
    dw_out = mm(gated, dy, "tn", BF16, name=tag + "_dwout")
    dzp, dlg, dlb, dws, dbsT = sgu_bwd(*args, dgated, tag + "_dcore")
    dh = mm(dzp, w_in, "nt", F32, name=tag + "_dh")
    dw_in = mm(h, dzp, "tn", BF16, name=tag + "_dwin")
    return dh, dw_in, dw_out, dlg[0], dlb[0], dws, dbsT.T


def _rope_tables(positions):
    inv = ROPE_THETA ** (-jnp.arange(0, ROPE_DIM, 2, dtype=F32) / ROPE_DIM)
    ang = positions.astype(F32)[:, None] * inv
    S = positions.shape[0]
    rest = SWA_HEAD_DIM - ROPE_DIM
    C = jnp.concatenate([jnp.cos(ang), jnp.cos(ang), jnp.ones((S, rest), F32)], axis=1)
    Sn = jnp.concatenate([jnp.sin(ang), jnp.sin(ang), jnp.zeros((S, rest), F32)], axis=1)
    return C, Sn


def _heads(t, n):
    return t.reshape(t.shape[0], n, SWA_HEAD_DIM).transpose(1, 0, 2)


def _unheads(t):
    return t.transpose(1, 0, 2).reshape(t.shape[1], -1)


def _swa_fwd(h, w_in, sinks, w_out, tables, tag):
    Hq = sinks.shape[0]
    Hk = (w_in[0].shape[-1] // SWA_HEAD_DIM - Hq) // 2
    G = Hq // Hk
    proj = mm(h, w_in, "nn", F32, name=tag + "_in")
    qh = _heads(proj[:, :Hq * SWA_HEAD_DIM], Hq)
    kh = _heads(proj[:, Hq * SWA_HEAD_DIM:(Hq + Hk) * SWA_HEAD_DIM], Hk)
    vh = _heads(proj[:, (Hq + Hk) * SWA_HEAD_DIM:], Hk)
    sink_col = jnp.repeat(sinks.reshape(Hk, G), SEQ_BLOCK, axis=1).reshape(Hk, G * SEQ_BLOCK, 1)
    args = (qh, kh, vh, sink_col, tables[0], tables[1], jnp.asarray(_rope_matrix()))
    o = _unheads(swa_fwd(*args, tag + "_core"))
    y = mm(o, w_out, "nn", F32, name=tag + "_out")
    return y, (args, o)


def _swa_bwd(dy, h, w_in, w_out, ctx, tag):
    args, o = ctx
    Hq = args[0].shape[0]
    do = mm(dy, w_out, "nt", BF16, name=tag + "_do")
    dw_out = mm(o, dy, "tn", BF16, name=tag + "_dwout")
    dqh, dkp, dkc, dvp, dvc, dsk = swa_bwd(*args, _heads(do, Hq), tag + "_dcore")
    dk = shift_add(dkc, dkp, tag + "_dk")
    dv = shift_add(dvc, dvp, tag + "_dv")
    dproj = jnp.concatenate([_unheads(dqh), _unheads(dk), _unheads(dv)], axis=1).astype(BF16)
    dh = mm(dproj, w_in, "nt", F32, name=tag + "_dh")
    dw_in = mm(h, dproj, "tn", BF16, name=tag + "_dwin")
    return dh, dw_in, dw_out, dsk[:, :, 0].reshape(Hq)


def kernel(x, c, positions, ada_w, ada_b, mix_pre_g, mix_post_g, ffn_pre_g, ffn_post_g, ffn_w_gu, ffn_w_down, fox_w_in, fox_b_f, fox_w_out, sgu_w_in, sgu_ln_g, sgu_ln_b, sgu_w_s, sgu_b_s, sgu_w_out, swa_w_in, swa_sinks, swa_w_out, loss_target, m_ada_w, m_ada_b, m_mix_pre_g, m_mix_post_g, m_ffn_pre_g, m_ffn_post_g, m_ffn_w_gu, m_ffn_w_down, m_fox_w_in, m_fox_b_f, m_fox_w_out, m_sgu_w_in, m_sgu_ln_g, m_sgu_ln_b, m_sgu_w_s, m_sgu_b_s, m_sgu_w_out, m_swa_w_in, m_swa_sinks, m_swa_w_out, v_ada_w, v_ada_b, v_mix_pre_g, v_mix_post_g, v_ffn_pre_g, v_ffn_post_g, v_ffn_w_gu, v_ffn_w_down, v_fox_w_in, v_fox_b_f, v_fox_w_out, v_sgu_w_in, v_sgu_ln_g, v_sgu_ln_b, v_sgu_w_s, v_sgu_b_s, v_sgu_w_out, v_swa_w_in, v_swa_sinks, v_swa_w_out):
    P = dict(zip(INPUTS, (x, c, positions, ada_w, ada_b, mix_pre_g, mix_post_g, ffn_pre_g, ffn_post_g, ffn_w_gu, ffn_w_down, fox_w_in, fox_b_f, fox_w_out, sgu_w_in, sgu_ln_g, sgu_ln_b, sgu_w_s, sgu_b_s, sgu_w_out, swa_w_in, swa_sinks, swa_w_out, loss_target, m_ada_w, m_ada_b, m_mix_pre_g, m_mix_post_g, m_ffn_pre_g, m_ffn_post_g, m_ffn_w_gu, m_ffn_w_down, m_fox_w_in, m_fox_b_f, m_fox_w_out, m_sgu_w_in, m_sgu_ln_g, m_sgu_ln_b, m_sgu_w_s, m_sgu_b_s, m_sgu_w_out, m_swa_w_in, m_swa_sinks, m_swa_w_out, v_ada_w, v_ada_b, v_mix_pre_g, v_mix_post_g, v_ffn_pre_g, v_ffn_post_g, v_ffn_w_gu, v_ffn_w_down, v_fox_w_in, v_fox_b_f, v_fox_w_out, v_sgu_w_in, v_sgu_ln_g, v_sgu_ln_b, v_sgu_w_s, v_sgu_b_s, v_sgu_w_out, v_swa_w_in, v_swa_sinks, v_swa_w_out)))
    xs, target, pos = x[0], loss_target[0], positions[0]
    S, D = xs.shape
    L = ada_w.shape[0]
    n_mix = 3
    F = ffn_w_down.shape[1] * N_CHIP
    xi, yi, ci = _place()
    q_me = 2 * xi + yi
    dev = 4 * xi + 2 * yi + ci

    qc = jnp.stack([q_me, ci]).astype(jnp.int32)
    by_cols = {n: n in COLUMN_SHARDED for n in BIG}
    geo = {n: (by_cols[n], P[n].shape[1], P[n].shape[2] + (_lane_pad(P[n].shape[2]) if by_cols[n] else 0)) for n in BIG}
    in_layer0 = [n for n in BIG if n.startswith(("ffn_", "fox_"))]
    first = {n: (0, 1) for n in in_layer0}
    rest = {n: (1 if n in first else 0, P[n].shape[0]) for n in BIG}
    rest = {n: r for n, r in rest.items() if r[1] > r[0]}

    def shard_of(n, lo, hi):
        s = P[n][lo:hi].astype(BF16)
        return jnp.pad(s, ((0, 0), (0, 0), (0, _lane_pad(s.shape[2])))) if by_cols[n] else s

    def finish_gather(group, shards, fulls, tag):
        out = {}
        for n, s, f in zip(group, shards, fulls):
            f = place_shard(f, s, by_cols[n], qc, tag + n)
            f = _unpad_shard_columns(f, P[n].shape[2]) if by_cols[n] else f
            out[n] = jnp.pad(f, ((0, 0), (0, 0), (0, 3 * D + LANES - f.shape[2]))) if n == "fox_w_in" else f
        return out

    first_shards = [shard_of(n, *first[n]) for n in first]
    rest_shards = [shard_of(n, *rest[n]) for n in rest]
    w_first = finish_gather(first, first_shards, ag_weights(first_shards, [by_cols[n] for n in first], "ag_weights_first"), "place_first_")
    Wt = {n: [(w_first[n], 0)] if n in first else [] for n in BIG}

    c_all = ag_small(c.reshape(D // LANES, LANES), "ag_c").reshape(N_DEV, D)
    c_all = jnp.pad(c_all, ((0, 16 - N_DEV), (0, 0)))
    Nm = ada_w.shape[2]
    ada_b_mine = lax.dynamic_slice_in_dim(ada_b, q_me * Nm, Nm, axis=1).reshape(L, 1, Nm)
    modp = ada_fwd(c_all, ada_w, ada_b_mine, "ada_fwd")[:, :N_DEV]
    mod_all = ag_small(modp.reshape(-1, LANES), "ag_mod").reshape(N_DEV, L, N_DEV, Nm)
    mod_mine = lax.dynamic_index_in_dim(mod_all[0::2], dev, axis=2, keepdims=False)
    mods = mod_mine.transpose(1, 0, 2).reshape(L, 6, 1, D)

    tables = _rope_tables(pos)

    saved = []
    xc = xs
    for i in range(L):
        kind, j = i % n_mix, i // n_mix
        sh_m, sc_m, g_m, sh_f, sc_f, g_f = (mods[i, t] for t in range(6))
        t = f"l{i}"
        h1 = pre_fwd(xc, mix_pre_g[i:i + 1], sh_m, sc_m, t + "_pre_m")
        if kind == 0:
            sidecar = ag_sidecar(rest_shards, [by_cols[n] for n in rest]) if i == 0 else None
            y1, ctx, rest_fulls = _fox_fwd(h1, Wt["fox_w_in"][j], fox_b_f[j], Wt["fox_w_out"][j], t + "_fox", sidecar)
            if i == 0:
                w_rest = finish_gather(rest, rest_shards, rest_fulls, "place_rest_")
                for n in rest:
                    Wt[n] += [(w_rest[n], l) for l in range(rest[n][1] - rest[n][0])]
        elif kind == 1:
            y1, ctx = _sgu_fwd(h1, Wt["sgu_w_in"][j], sgu_ln_g[j], sgu_ln_b[j], sgu_w_s[j], sgu_b_s[j], Wt["sgu_w_out"][j], t + "_sgu")
        else:
            y1, ctx = _swa_fwd(h1, Wt["swa_w_in"][j], swa_sinks[j], Wt["swa_w_out"][j], tables, t + "_swa")
        xm = post_fwd(xc, y1, mix_post_g[i:i + 1], g_m, t + "_post_m")
        h2 = pre_fwd(xm, ffn_pre_g[i:i + 1], sh_f, sc_f, t + "_pre_f")
        gu = mm(h2, Wt["ffn_w_gu"][i], "nn", BF16, name=t + "_ffn_gu")
        a = act_fwd(gu, t + "_act")
        y2 = mm(a, Wt["ffn_w_down"][i], "nn", F32, name=t + "_ffn_down")
        xn = post_fwd(xm, y2, ffn_post_g[i:i + 1], g_f, t + "_post_f")
        saved.append((xc, h1, y1, ctx, xm, h2, gu, a, y2))
        xc = xn

    sq, dx = loss_head(xc, target, "loss_head")
    loss = lax.psum(sq[0, 0] * (0.5 / D), ("x", "y", "c"))

    big_g = {n: [None] * P[n].shape[0] for n in BIG}
    small_g = {n: [None] * P[n].shape[0] for n in SMALL}
    def fold(group, tag):
        names = list(group)
        gs = [jnp.stack(big_g[n][group[n][0]:group[n][1]]) for n in names]
        gs = [_pad_shard_columns(g, P[n].shape[2]) if by_cols[n] else g for n, g in zip(names, gs)]
        gm = [geo[n] for n in names]
        from_sibling = sibling_fold(gs, gm, "rs_fold_" + tag)
        return names, gm, [fold_sum(g, r, *m, qc, f"rs_fold_sum_{tag}_{n}") for n, g, r, m in zip(names, gs, from_sibling, gm)]

    for i in reversed(range(L)):
        kind, j = i % n_mix, i // n_mix
        sh_m, sc_m, g_m, sh_f, sc_f, g_f = (mods[i, t] for t in range(6))
        xc, h1, y1, ctx, xm, h2, gu, a, y2 = saved[i]
        t = f"l{i}"
        if i == 0:
            rest_names, rest_geo, rest_part = fold(rest, "rest")
        dy2, dgpost_f, dgate_f = post_bwd(y2, ffn_post_g[i:i + 1], g_f, dx, t + "_dpost_f")
        da = mm(dy2, Wt["ffn_w_down"][i], "nt", F32, name=t + "_da")
        big_g["ffn_w_down"][i] = mm(a, dy2, "tn", BF16, name=t + "_dwdown")
        dgu = act_bwd(gu, da, t + "_dact")
        dh2 = mm(dgu, Wt["ffn_w_gu"][i], "nt", F32, name=t + "_dh2")
        big_g["ffn_w_gu"][i] = mm(h2, dgu, "tn", BF16, name=t + "_dwgu")
        dxm, dgpre_f, dsh_f, dsc_f = pre_bwd(xm, ffn_pre_g[i:i + 1], sh_f, sc_f, dh2, dx, t + "_dpre_f")
        dy1, dgpost_m, dgate_m = post_bwd(y1, mix_post_g[i:i + 1], g_m, dxm, t + "_dpost_m")
        if kind == 0:
            sidecar = exchange_sidecar(rest_part, rest_geo) if i == 0 else None
            dh1, dw_in, dw_out, db, rode = _fox_bwd(dy1, h1, Wt["fox_w_in"][j], fox_b_f[j], Wt["fox_w_out"][j], ctx, t + "_fox", sidecar)
            big_g["fox_w_in"][j], big_g["fox_w_out"][j], small_g["fox_b_f"][j] = dw_in, dw_out, db
            if i == 0:
                rest_from_chips = rode
        elif kind == 1:
            dh1, dw_in, dw_out, dlg, dlb, dws, dbs = _sgu_bwd(dy1, h1, Wt["sgu_w_in"][j], Wt["sgu_w_out"][j], ctx, t + "_sgu")
            big_g["sgu_w_in"][j], big_g["sgu_w_out"][j] = dw_in, dw_out
            small_g["sgu_ln_g"][j], small_g["sgu_ln_b"][j], small_g["sgu_w_s"][j], small_g["sgu_b_s"][j] = dlg, dlb, dws, dbs
        else:
            dh1, dw_in, dw_out, dsk = _swa_bwd(dy1, h1, Wt["swa_w_in"][j], Wt["swa_w_out"][j], ctx, t + "_swa")
            big_g["swa_w_in"][j], big_g["swa_w_out"][j], small_g["swa_sinks"][j] = dw_in, dw_out, dsk
        dx, dgpre_m, dsh_m, dsc_m = pre_bwd(xc, mix_pre_g[i:i + 1], sh_m, sc_m, dh1, dxm, t + "_dpre_m")
        small_g["ada_b"][i] = jnp.concatenate([dsh_m, dsc_m, dgate_m, dsh_f, dsc_f, dgate_f], axis=1)[0]
        small_g["mix_pre_g"][i], small_g["mix_post_g"][i] = dgpre_m[0], dgpost_m[0]
        small_g["ffn_pre_g"][i], small_g["ffn_post_g"][i] = dgpre_f[0], dgpost_f[0]
    grad_x = dx[None]

    shapes = {n: P[n].shape for n in SMALL}
    small_parts = ag_small(_pack_small({n: jnp.stack(small_g[n]) for n in SMALL}), "ag_small_grads")
    sg, sd, sm, sv = adamw(_pack_small({n: P[n] for n in SMALL}), _pack_small({n: P["m_" + n] for n in SMALL}),
                           _pack_small({n: P["v_" + n] for n in SMALL}), [small_parts], "adamw_small", gstack=N_DEV)
    out_g, out_d, out_m, out_v = (_unpack_small(t, shapes) for t in (sg, sd, sm, sv))

    dmod_all = small_parts[:, :L * 6 * D // LANES].reshape(N_DEV, L, 6 * D)
    dmod_mine = lax.dynamic_slice_in_dim(dmod_all, q_me * Nm, Nm, axis=2).transpose(1, 0, 2)
    dmod_mine = jnp.pad(dmod_mine, ((0, 0), (0, 16 - N_DEV), (0, 0)))
    g_ada = ada_wgrad(c_all, dmod_mine, "ada_wgrad")
    r2 = lambda t: t.reshape(-1, t.shape[-1])
    res = adamw(r2(ada_w), r2(m_ada_w), r2(v_ada_w), [r2(g_ada)], "adamw_ada_w", emit_g=False)
    out_g["ada_w"] = g_ada
    out_d["ada_w"], out_m["ada_w"], out_v["ada_w"] = (t.reshape(ada_w.shape) for t in res)

    first_names, first_geo, first_part = fold(first, "first")
    first_from_chips = chip_exchange(first_part, first_geo, "rs_exchange_first")
    pieces = list(zip(["first"] * len(first_names), first_names, first_geo, first_part, first_from_chips)) + list(
        zip(["rest"] * len(rest_names), rest_names, rest_geo, rest_part, rest_from_chips))
    mine = [chip_sum(r, e, *m, qc, f"rs_chip_sum_{tag}_{n}") for tag, n, m, r, e in pieces]
    shared = sibling_share(mine, [m for _, _, m, _, _ in pieces], "rs_share")
    by_name = {n: [] for n in BIG}
    for (tag, n, *_), g in zip(pieces, shared):
        by_name[n].append(g)
    for n in BIG:
        gsh = by_name[n][0] if len(by_name[n]) == 1 else jnp.concatenate(by_name[n], axis=0)
        gsh = gsh[:, :, :P[n].shape[2]]
        res = adamw(r2(P[n]), r2(P["m_" + n]), r2(P["v_" + n]), [r2(gsh)], "adamw_" + n, emit_g=False)
        out_g[n] = gsh
        out_d[n], out_m[n], out_v[n] = (t.reshape(P[n].shape) for t in res)

    return (loss, grad_x, *[out_g[n] for n in WEIGHTS], *[out_d[n] for n in WEIGHTS], *[out_m[n] for n in WEIGHTS],
            *[out_v[n] for n in WEIGHTS])
```

```python
from typing import Callable, NamedTuple

import numpy as np
import jax
import jax.numpy as jnp
from jax import lax
from jax.experimental import pallas as pl
from jax.experimental.pallas import tpu as pltpu

F32 = jnp.float32
BF16 = jnp.bfloat16
MESH = pl.DeviceIdType.MESH

EPS = 1e-6
NEG = -1e30
V7X_VMEM_BYTES = 64 * 1024 * 1024
VMEM_LIMIT = V7X_VMEM_BYTES - 8 * 1024 * 1024
LANES = 128
SEQ_BLOCK = 128
SWA_HEAD_DIM = 64
ROPE_DIM = SWA_HEAD_DIM // 4
ROPE_THETA = 500000.0

ADAM_LR = 0.001
ADAM_B1 = 0.9
ADAM_B2 = 0.999
ADAM_EPS = 1e-08
ADAM_WD = 0.01
ADAM_STEP = 10


def _cp(sem=None, **kw):
    return pltpu.CompilerParams(dimension_semantics=sem, vmem_limit_bytes=VMEM_LIMIT, **kw)


def _tile(dim, pref, mult=LANES):
    if dim <= pref:
        return dim
    t = (pref // mult) * mult
    while t >= mult:
        if dim % t == 0:
            return t
        t -= mult
    return dim


_DIMS = {"nn": (((1,), (0,)), ((), ())), "nt": (((1,), (1,)), ((), ())), "tn": (((0,), (0,)), ((), ()))}


MM_TILES = {"nn": (1024, 512, 2816), "nt": (1024, 1024, 2816), "tn": (1024, 1024, 2048)}


def mm(a, b, mode="nn", out_dtype=F32, add=None, name="mm", b_cols=None, tm=None, tn=None, tk=None, sidecar=None):
    tm, tn, tk = (d if t is None else t for t, d in zip((tm, tn, tk), MM_TILES[mode]))
    b, b_layer = b if isinstance(b, tuple) else (b, None)
    b_shape = b.shape[-2:]
    c0 = 0
    if b_cols is not None:
        c0, csize = b_cols
    if mode == "nn":
        (M, K), (K2, N) = a.shape, b_shape
        if b_cols is not None:
            N = csize
    elif mode == "nt":
        (M, K), (N, K2) = a.shape, b_shape
        if b_cols is not None:
            K2 = csize
    else:
        (K, M), (K2, N) = a.shape, b_shape
        assert b_cols is None
    assert K == K2, (a.shape, b.shape, mode)
    tm = _tile(M, tm, LANES if mode == "tn" else 16)
    tn = _tile(N, tn)
    tk = _tile(K, tk, LANES if mode != "tn" else 16)
    nk = K // tk
    if b_cols is not None:
        assert c0 % (tn if mode == "nn" else tk) == 0, (b_cols, tn, tk)
    bo = c0 // (tn if mode == "nn" else tk)
    dims = _DIMS[mode]
    has_add = add is not None

    def body(a_ref, b_ref, *rest):
        if has_add:
            add_ref, o_ref, acc_ref = rest
        else:
            o_ref, acc_ref = rest
        k = pl.program_id(2)
        p = lax.dot_general(a_ref[...].astype(BF16), b_ref[...].astype(BF16), dims, preferred_element_type=F32)

        @pl.when(k == 0)
        def _():
            acc_ref[...] = p + add_ref[...].astype(F32) if has_add else p

        @pl.when(k > 0)
        def _():
            acc_ref[...] += p

        @pl.when(k == nk - 1)
        def _():
            o_ref[...] = acc_ref[...].astype(o_ref.dtype)

    a_spec = pl.BlockSpec((tk, tm), lambda i, j, k: (k, i)) if mode == "tn" else pl.BlockSpec((tm, tk), lambda i, j, k: (i, k))
    b_blk, b_idx = ((tn, tk), lambda i, j, k: (j, k + bo)) if mode == "nt" else ((tk, tn), lambda i, j, k: (k, j + bo))
    if b_layer is None:
        b_spec = pl.BlockSpec(b_blk, b_idx)
    else:
        b_spec = pl.BlockSpec((None,) + b_blk, lambda i, j, k: (b_layer,) + b_idx(i, j, k))
    o_spec = pl.BlockSpec((tm, tn), lambda i, j, k: (i, j))
    in_specs = [a_spec, b_spec] + ([o_spec] if has_add else [])
    args = (a, b) + ((add,) if has_add else ())
    out_shape = jax.ShapeDtypeStruct((M, N), out_dtype)
    scratch = [pltpu.VMEM((tm, tn), F32)]
    grid = (M // tm, N // tn, nk)
    if sidecar is None:
        return pl.pallas_call(
            body, name=name, grid=grid, in_specs=in_specs, out_specs=o_spec, out_shape=out_shape, scratch_shapes=scratch,
            compiler_params=_cp(("parallel", "parallel", "arbitrary")),
        )(*args)
    first = lambda: (pl.program_id(0) == 0) & (pl.program_id(1) == 0) & (pl.program_id(2) == 0)
    last = lambda: (pl.program_id(0) == grid[0] - 1) & (pl.program_id(1) == grid[1] - 1) & (pl.program_id(2) == grid[2] - 1)
    res = pl.pallas_call(
        _with_sidecar(sidecar, len(args), 1, 1, body, first, last), name=name, grid=grid,
        in_specs=in_specs + _sc_specs(sidecar, False), out_specs=[o_spec] + _sc_specs(sidecar, True),
        out_shape=[out_shape] + _sc_out(sidecar), scratch_shapes=scratch + _sc_sems(sidecar),
        compiler_params=_cp(("arbitrary", "arbitrary", "arbitrary")),
    )(*args, *_sc_arrays(sidecar))
    return res[0], res[1:]


def _rms(x, g):
    return (x * lax.rsqrt(jnp.mean(x * x, axis=-1, keepdims=True) + EPS)) * g


def _pre(x, g, sh, sc):
    return _rms(x, g) * (1 + sc) + sh


def _post(x, y, g, gate):
    return x + gate * _rms(y, g)


ROW_TILE = 256


def _row_spec(tr, d):
    return pl.BlockSpec((tr, d), lambda i: (i, 0))


def _vec_spec(d):
    return pl.BlockSpec((1, d), lambda i: (0, 0))


def pre_fwd(x, g, sh, sc, name):
    S, D = x.shape
    tr = _tile(S, ROW_TILE, 16)

    def body(x_ref, g_ref, sh_ref, sc_ref, h_ref):
        h_ref[...] = _pre(x_ref[...], g_ref[...], sh_ref[...], sc_ref[...]).astype(h_ref.dtype)

    return pl.pallas_call(
        body, name=name, grid=(S // tr,), in_specs=[_row_spec(tr, D)] + [_vec_spec(D)] * 3, out_specs=_row_spec(tr, D),
        out_shape=jax.ShapeDtypeStruct((S, D), BF16), compiler_params=_cp(("parallel",)),
    )(x, g, sh, sc)


def pre_bwd(x, g, sh, sc, dh, dres, name):
    S, D = x.shape
    tr = _tile(S, ROW_TILE, 16)

    def body(x_ref, g_ref, sh_ref, sc_ref, dh_ref, dres_ref, dx_ref, dg_ref, dsh_ref, dsc_ref):
        _, vjp = jax.vjp(_pre, x_ref[...], g_ref[...], sh_ref[...], sc_ref[...])
        dx, dg, dsh, dsc = vjp(dh_ref[...].astype(F32))
        dx_ref[...] = dres_ref[...] + dx

        @pl.when(pl.program_id(0) == 0)
        def _():
            dg_ref[...] = jnp.zeros_like(dg_ref)
            dsh_ref[...] = jnp.zeros_like(dsh_ref)
            dsc_ref[...] = jnp.zeros_like(dsc_ref)

        dg_ref[...] += dg
        dsh_ref[...] += dsh
        dsc_ref[...] += dsc

    vec = jax.ShapeDtypeStruct((1, D), F32)
    return pl.pallas_call(
        body, name=name, grid=(S // tr,), in_specs=[_row_spec(tr, D)] + [_vec_spec(D)] * 3 + [_row_spec(tr, D)] * 2,
        out_specs=[_row_spec(tr, D)] + [_vec_spec(D)] * 3, out_shape=[jax.ShapeDtypeStruct((S, D), F32), vec, vec, vec],
        compiler_params=_cp(("arbitrary",)),
    )(x, g, sh, sc, dh, dres)


def post_fwd(x, y, g, gate, name):
    S, D = x.shape
    tr = _tile(S, ROW_TILE, 16)

    def body(x_ref, y_ref, g_ref, gate_ref, o_ref):
        o_ref[...] = _post(x_ref[...], y_ref[...], g_ref[...], gate_ref[...])

    return pl.pallas_call(
        body, name=name, grid=(S // tr,), in_specs=[_row_spec(tr, D)] * 2 + [_vec_spec(D)] * 2, out_specs=_row_spec(tr, D),
        out_shape=jax.ShapeDtypeStruct((S, D), F32), compiler_params=_cp(("parallel",)),
    )(x, y, g, gate)


def post_bwd(y, g, gate, dxn, name):
    S, D = y.shape
    tr = _tile(S, ROW_TILE, 16)

    def body(y_ref, g_ref, gate_ref, dxn_ref, dy_ref, dg_ref, dgate_ref):
        fn = lambda yy, gg, gt: gt * _rms(yy, gg)
        _, vjp = jax.vjp(fn, y_ref[...], g_ref[...], gate_ref[...])
        dy, dg, dgate = vjp(dxn_ref[...])
        dy_ref[...] = dy.astype(dy_ref.dtype)

        @pl.when(pl.program_id(0) == 0)
        def _():
            dg_ref[...] = jnp.zeros_like(dg_ref)
            dgate_ref[...] = jnp.zeros_like(dgate_ref)

        dg_ref[...] += dg
        dgate_ref[...] += dgate

    vec = jax.ShapeDtypeStruct((1, D), F32)
    return pl.pallas_call(
        body, name=name, grid=(S // tr,), in_specs=[_row_spec(tr, D)] + [_vec_spec(D)] * 2 + [_row_spec(tr, D)],
        out_specs=[_row_spec(tr, D)] + [_vec_spec(D)] * 2, out_shape=[jax.ShapeDtypeStruct((S, D), BF16), vec, vec],
        compiler_params=_cp(("arbitrary",)),
    )(y, g, gate, dxn)


def _swiglu(g, u):
    return jax.nn.silu(g) * u


ACT_ROWS = 256


def act_fwd(gu, name):
    S, F2 = gu.shape
    F = F2 // 2
    tr = _tile(S, ACT_ROWS, 16)

    def body(gu_ref, a_ref):
        a_ref[...] = _swiglu(gu_ref[:, :F].astype(F32), gu_ref[:, F:].astype(F32)).astype(a_ref.dtype)

    return pl.pallas_call(
        body, name=name, grid=(S // tr,), in_specs=[_row_spec(tr, F2)], out_specs=_row_spec(tr, F),
        out_shape=jax.ShapeDtypeStruct((S, F), BF16), compiler_params=_cp(("parallel",)),
    )(gu)


def act_bwd(gu, da, name):
    S, F2 = gu.shape
    F = F2 // 2
    tr = _tile(S, ACT_ROWS, 16)

    def body(gu_ref, da_ref, dgu_ref):
        _, vjp = jax.vjp(_swiglu, gu_ref[:, :F].astype(F32), gu_ref[:, F:].astype(F32))
        dg, du = vjp(da_ref[...].astype(F32))
        dgu_ref[:, :F] = dg.astype(dgu_ref.dtype)
        dgu_ref[:, F:] = du.astype(dgu_ref.dtype)

    return pl.pallas_call(
        body, name=name, grid=(S // tr,), in_specs=[_row_spec(tr, F2), _row_spec(tr, F)], out_specs=_row_spec(tr, F2),
        out_shape=jax.ShapeDtypeStruct((S, F2), BF16), compiler_params=_cp(("parallel",)),
    )(gu, da)


def _tri(upper):
    r = lax.broadcasted_iota(jnp.int32, (LANES, LANES), 0)
    c = lax.broadcasted_iota(jnp.int32, (LANES, LANES), 1)
    return ((r <= c) if upper else (r >= c)).astype(F32)


def _hdot(a, b):
    return jnp.dot(a, b, precision=lax.Precision.HIGHEST, preferred_element_type=F32)


def fox_gate_fwd(fgT, b, name):
    H, S = fgT.shape
    spec = pl.BlockSpec((H, LANES), lambda ch: (0, ch))

    def body(fg_ref, b_ref, cum_ref, carry_ref):
        @pl.when(pl.program_id(0) == 0)
        def _():
            carry_ref[...] = jnp.zeros_like(carry_ref)

        lf = jax.nn.log_sigmoid(fg_ref[...] + b_ref[...])
        cum_ref[...] = _hdot(lf, _tri(True)) + carry_ref[...]
        carry_ref[...] += _hdot(lf, jnp.ones((LANES, LANES), F32))

    return pl.pallas_call(
        body, name=name, grid=(S // LANES,), in_specs=[spec, pl.BlockSpec((H, 1), lambda ch: (0, 0))], out_specs=spec,
        out_shape=jax.ShapeDtypeStruct((H, S), F32), scratch_shapes=[pltpu.VMEM((H, LANES), F32)], compiler_params=_cp(("arbitrary",)),
    )(fgT, b)


def fox_gate_bwd(dcum, fgT, b, name):
    H, S = fgT.shape
    nch = S // LANES
    spec = pl.BlockSpec((H, LANES), lambda t: (0, nch - 1 - t))

    def body(dcum_ref, fg_ref, b_ref, dfg_ref, db_ref, tail_ref):
        @pl.when(pl.program_id(0) == 0)
        def _():
            tail_ref[...] = jnp.zeros_like(tail_ref)
            db_ref[...] = jnp.zeros_like(db_ref)

        dlf = _hdot(dcum_ref[...], _tri(False)) + tail_ref[...]
        dfg = dlf * jax.nn.sigmoid(-(fg_ref[...] + b_ref[...]))
        dfg_ref[...] = dfg
        ones = jnp.ones((LANES, LANES), F32)
        tail_ref[...] += _hdot(dcum_ref[...], ones)
        db_ref[...] += _hdot(dfg, ones)

    return pl.pallas_call(
        body, name=name, grid=(nch,), in_specs=[spec, spec, pl.BlockSpec((H, 1), lambda t: (0, 0))],
        out_specs=[spec, pl.BlockSpec((H, LANES), lambda t: (0, 0))],
        out_shape=[jax.ShapeDtypeStruct((H, S), F32), jax.ShapeDtypeStruct((H, LANES), F32)],
        scratch_shapes=[pltpu.VMEM((H, LANES), F32)], compiler_params=_cp(("arbitrary",)),
    )(dcum, fgT, b)


FOX_TILE = 1024


def _on_and_below_diagonal(i, j, tile):
    @pl.when(j < i)
    def _():
        tile(False)

    @pl.when(j == i)
    def _():
        tile(True)


def _causal_pairs(n, by_key):
    pairs = [(i, j) for i in range(n) for j in range(i + 1)]
    if by_key:
        pairs.sort(key=lambda p: (p[1], p[0]))
    qi = np.asarray([p[0] for p in pairs], np.int32)
    kj = np.asarray([p[1] for p in pairs], np.int32)
    return qi, kj


def _fox_scores(q, k, fq, fk, T, scale, transposed):
    r = lax.broadcasted_iota(jnp.int32, (T, T), 0)
    c = lax.broadcasted_iota(jnp.int32, (T, T), 1)
    if transposed:
        return lax.dot_general(k, q, _DIMS["nt"], preferred_element_type=F32) * scale + (fq - fk), r <= c
    return lax.dot_general(q, k, _DIMS["nt"], preferred_element_type=F32) * scale + (fq - fk), c <= r


class SideCar(NamedTuple):
    arrays: list
    out_shape: list
    semaphores: list
    steps: Callable


def _sc_specs(sc, out):
    return [] if sc is None else [pl.BlockSpec(memory_space=pl.ANY)] * len(sc.out_shape if out else sc.arrays)


def _sc_sems(sc):
    return [] if sc is None else list(sc.semaphores)


def _sc_out(sc):
    return [] if sc is None else list(sc.out_shape)


def _sc_arrays(sc):
    return [] if sc is None else list(sc.arrays)


def _with_sidecar(sc, n_in, n_out, n_scratch, body, first, last):
    if sc is None:
        return body
    a, o = len(sc.arrays), len(sc.out_shape)

    def wrapped(*refs):
        ins, rest = refs[:n_in], refs[n_in:]
        sc_in, rest = rest[:a], rest[a:]
        outs, rest = rest[:n_out], rest[n_out:]
        sc_out, rest = rest[:o], rest[o:]
        scratch, sems = rest[:n_scratch], rest[n_scratch:]
        start, finish = sc.steps(sc_in, sc_out, sems)
        pl.when(first())(start)
        body(*ins, *outs, *scratch)
        pl.when(last())(finish)

    return wrapped


def fox_attn_fwd(qkv, cum_col, cum_row, H, name, sidecar=None):
    S = qkv.shape[0]
    Dh = qkv.shape[1] // (3 * H)
    T = _tile(S, FOX_TILE)
    n = S // T
    qi, kj = _causal_pairs(n, by_key=False)
    scale = Dh ** -0.5

    def body(qi_ref, kj_ref, q_ref, k_ref, v_ref, fq_ref, fk_ref, o_ref, lse_ref, m_ref, l_ref, acc_ref):
        p_id = pl.program_id(1)
        i, j = qi_ref[p_id], kj_ref[p_id]

        @pl.when(j == 0)
        def _():
            m_ref[...] = jnp.full_like(m_ref, NEG)
            l_ref[...] = jnp.zeros_like(l_ref)
            acc_ref[...] = jnp.zeros_like(acc_ref)

        def tile(masked):
            s, mask = _fox_scores(q_ref[...], k_ref[...], fq_ref[0], fk_ref[0], T, scale, False)
            if masked:
                s = jnp.where(mask, s, NEG)
            m_new = jnp.maximum(m_ref[...], jnp.max(s, axis=1, keepdims=True))
            alpha = jnp.exp(m_ref[...] - m_new)
            p = jnp.exp(s - m_new)
            l_ref[...] = alpha * l_ref[...] + jnp.sum(p, axis=1, keepdims=True)
            acc_ref[...] = alpha * acc_ref[...] + jnp.dot(p.astype(BF16), v_ref[...], preferred_element_type=F32)
            m_ref[...] = m_new

        _on_and_below_diagonal(i, j, tile)

        @pl.when(j == i)
        def _():
            o_ref[...] = (acc_ref[...] / l_ref[...]).astype(o_ref.dtype)
            lse_ref[0] = m_ref[...] + jnp.log(l_ref[...])

    grid_spec = pltpu.PrefetchScalarGridSpec(
        num_scalar_prefetch=2, grid=(H, len(qi)),
        in_specs=[
            pl.BlockSpec((T, Dh), lambda h, p, qi, kj: (qi[p], h)),
            pl.BlockSpec((T, Dh), lambda h, p, qi, kj: (kj[p], H + h)),
            pl.BlockSpec((T, Dh), lambda h, p, qi, kj: (kj[p], 2 * H + h)),
            pl.BlockSpec((1, T, 1), lambda h, p, qi, kj: (h, qi[p], 0)),
            pl.BlockSpec((1, 1, T), lambda h, p, qi, kj: (h, 0, kj[p])),
        ] + _sc_specs(sidecar, False),
        out_specs=[
            pl.BlockSpec((T, Dh), lambda h, p, qi, kj: (qi[p], h)),
            pl.BlockSpec((1, T, 1), lambda h, p, qi, kj: (h, qi[p], 0)),
        ] + _sc_specs(sidecar, True),
        scratch_shapes=[pltpu.VMEM((T, 1), F32), pltpu.VMEM((T, 1), F32), pltpu.VMEM((T, Dh), F32)] + _sc_sems(sidecar),
    )
    first = lambda: (pl.program_id(0) == 0) & (pl.program_id(1) == 0)
    last = lambda: (pl.program_id(0) == H - 1) & (pl.program_id(1) == len(qi) - 1)
    res = pl.pallas_call(
        _with_sidecar(sidecar, 7, 2, 3, body, first, last), name=name, grid_spec=grid_spec,
        out_shape=[jax.ShapeDtypeStruct((S, H * Dh), F32), jax.ShapeDtypeStruct((H, S, 1), F32)] + _sc_out(sidecar),
        compiler_params=_cp(("arbitrary", "arbitrary")),
    )(jnp.asarray(qi), jnp.asarray(kj), qkv, qkv, qkv, cum_col, cum_row, *_sc_arrays(sidecar))
    return res[0], res[1], res[2:]


def fox_attn_bwd_dq(qkv, do, o, lse, cum_col, cum_row, H, name, sidecar=None):
    S = qkv.shape[0]
    Dh = qkv.shape[1] // (3 * H)
    T = _tile(S, FOX_TILE)
    n = S // T
    qi, kj = _causal_pairs(n, by_key=False)
    scale = Dh ** -0.5

    def body(qi_ref, kj_ref, q_ref, k_ref, v_ref, do_ref, o_ref, lse_ref, fq_ref, fk_ref, dq_ref, delta_ref, acc_ref, dl_ref,
             rs_ref):
        p_id = pl.program_id(1)
        i, j = qi_ref[p_id], kj_ref[p_id]

        @pl.when(j == 0)
        def _():
            acc_ref[...] = jnp.zeros_like(acc_ref)
            rs_ref[...] = jnp.zeros_like(rs_ref)
            dl_ref[...] = jnp.sum(do_ref[...].astype(F32) * o_ref[...].astype(F32), axis=1, keepdims=True)

        def tile(masked):
            s, mask = _fox_scores(q_ref[...], k_ref[...], fq_ref[0], fk_ref[0], T, scale, False)
            p = jnp.exp(s - lse_ref[0])
            if masked:
                p = jnp.where(mask, p, 0.0)
            dp = lax.dot_general(do_ref[...], v_ref[...], _DIMS["nt"], preferred_element_type=F32)
            ds = p * (dp - dl_ref[...])
            rs_ref[...] += jnp.sum(ds, axis=1, keepdims=True)
            acc_ref[...] += jnp.dot(ds.astype(BF16), k_ref[...], preferred_element_type=F32)

        _on_and_below_diagonal(i, j, tile)

        @pl.when(j == i)
        def _():
            dq_ref[...] = (acc_ref[...] * scale).astype(dq_ref.dtype)
            delta_ref[0] = dl_ref[...] + rs_ref[...]

    qspec = pl.BlockSpec((T, Dh), lambda h, p, qi, kj: (qi[p], h))
    colspec = pl.BlockSpec((1, T, 1), lambda h, p, qi, kj: (h, qi[p], 0))
    grid_spec = pltpu.PrefetchScalarGridSpec(
        num_scalar_prefetch=2, grid=(H, len(qi)),
        in_specs=[
            qspec,
            pl.BlockSpec((T, Dh), lambda h, p, qi, kj: (kj[p], H + h)),
            pl.BlockSpec((T, Dh), lambda h, p, qi, kj: (kj[p], 2 * H + h)),
            qspec, qspec, colspec, colspec,
            pl.BlockSpec((1, 1, T), lambda h, p, qi, kj: (h, 0, kj[p])),
        ] + _sc_specs(sidecar, False),
        out_specs=[qspec, colspec] + _sc_specs(sidecar, True),
        scratch_shapes=[pltpu.VMEM((T, Dh), F32), pltpu.VMEM((T, 1), F32), pltpu.VMEM((T, 1), F32)] + _sc_sems(sidecar),
    )
    first = lambda: (pl.program_id(0) == 0) & (pl.program_id(1) == 0)
    last = lambda: (pl.program_id(0) == H - 1) & (pl.program_id(1) == len(qi) - 1)
    res = pl.pallas_call(
        _with_sidecar(sidecar, 10, 2, 3, body, first, last), name=name, grid_spec=grid_spec,
        out_shape=[jax.ShapeDtypeStruct((S, H * Dh), BF16), jax.ShapeDtypeStruct((H, S, 1), F32)] + _sc_out(sidecar),
        compiler_params=_cp(("arbitrary", "arbitrary")),
    )(jnp.asarray(qi), jnp.asarray(kj), qkv, qkv, qkv, do, o, lse, cum_col, cum_row, *_sc_arrays(sidecar))
    return res[0], res[1], res[2:]


def fox_attn_bwd_dkv(qkv, do, lse_row, delta_row, cum_col, cum_row, H, name, sidecar=None):
    S = qkv.shape[0]
    Dh = qkv.shape[1] // (3 * H)
    T = _tile(S, FOX_TILE)
    n = S // T
    qi, kj = _causal_pairs(n, by_key=True)
    scale = Dh ** -0.5

    def body(qi_ref, kj_ref, q_ref, k_ref, v_ref, do_ref, lse_ref, dl_ref, fq_ref, fk_ref, dk_ref, dv_ref, dcum_ref,
             dk_acc, dv_acc, df_acc):
        p_id = pl.program_id(1)
        i, j = qi_ref[p_id], kj_ref[p_id]

        @pl.when(i == j)
        def _():
            dk_acc[...] = jnp.zeros_like(dk_acc)
            dv_acc[...] = jnp.zeros_like(dv_acc)
            df_acc[...] = jnp.zeros_like(df_acc)

        def tile(masked):
            sT, mask = _fox_scores(q_ref[...], k_ref[...], fq_ref[0], fk_ref[0], T, scale, True)
            pT = jnp.exp(sT - lse_ref[0])
            if masked:
                pT = jnp.where(mask, pT, 0.0)
            dv_acc[...] += jnp.dot(pT.astype(BF16), do_ref[...], preferred_element_type=F32)
            dpT = lax.dot_general(v_ref[...], do_ref[...], _DIMS["nt"], preferred_element_type=F32)
            dsT = pT * (dpT - dl_ref[0])
            dk_acc[...] += jnp.dot(dsT.astype(BF16), q_ref[...], preferred_element_type=F32)
            df_acc[...] -= jnp.sum(dsT, axis=1, keepdims=True)

        _on_and_below_diagonal(i, j, tile)

        @pl.when(i == n - 1)
        def _():
            dk_ref[...] = (dk_acc[...] * scale).astype(dk_ref.dtype)
            dv_ref[...] = dv_acc[...].astype(dv_ref.dtype)
            dcum_ref[0] = df_acc[...]

    qspec = pl.BlockSpec((T, Dh), lambda h, p, qi, kj: (qi[p], h))
    kspec = pl.BlockSpec((T, Dh), lambda h, p, qi, kj: (kj[p], H + h))
    vspec = pl.BlockSpec((T, Dh), lambda h, p, qi, kj: (kj[p], 2 * H + h))
    qrow = pl.BlockSpec((1, 1, T), lambda h, p, qi, kj: (h, 0, qi[p]))
    kcol = pl.BlockSpec((1, T, 1), lambda h, p, qi, kj: (h, kj[p], 0))
    grid_spec = pltpu.PrefetchScalarGridSpec(
        num_scalar_prefetch=2, grid=(H, len(qi)),
        in_specs=[qspec, kspec, vspec, qspec, qrow, qrow, qrow, kcol] + _sc_specs(sidecar, False),
        out_specs=[pl.BlockSpec((T, Dh), lambda h, p, qi, kj: (kj[p], h))] * 2 + [kcol] + _sc_specs(sidecar, True),
        scratch_shapes=[pltpu.VMEM((T, Dh), F32), pltpu.VMEM((T, Dh), F32), pltpu.VMEM((T, 1), F32)] + _sc_sems(sidecar),
    )
    out = jax.ShapeDtypeStruct((S, H * Dh), BF16)
    first = lambda: (pl.program_id(0) == 0) & (pl.program_id(1) == 0)
    last = lambda: (pl.program_id(0) == H - 1) & (pl.program_id(1) == len(qi) - 1)
    res = pl.pallas_call(
        _with_sidecar(sidecar, 10, 3, 3, body, first, last), name=name, grid_spec=grid_spec,
        out_shape=[out, out, jax.ShapeDtypeStruct((H, S, 1), F32)] + _sc_out(sidecar),
        compiler_params=_cp(("arbitrary", "arbitrary")),
    )(jnp.asarray(qi), jnp.asarray(kj), qkv, qkv, qkv, do, lse_row, delta_row, cum_row, cum_col, *_sc_arrays(sidecar))
    return res[0], res[1], res[2], res[3:]


def _sgu_ln(zu, zv, ln_g, ln_b):
    u = jax.nn.gelu(zu)
    v = jax.nn.gelu(zv)
    mu = jnp.mean(v, axis=-1, keepdims=True)
    var = jnp.mean(jnp.square(v - mu), axis=-1, keepdims=True)
    return u, (v - mu) * lax.rsqrt(var + EPS) * ln_g + ln_b


def _tril_mask():
    r = lax.broadcasted_iota(jnp.int32, (SEQ_BLOCK, SEQ_BLOCK), 0)
    c = lax.broadcasted_iota(jnp.int32, (SEQ_BLOCK, SEQ_BLOCK), 1)
    return r >= c


def _sgu_spatial(ws_ref, bsT, selT, vn, G):
    tril = _tril_mask()
    fs = []
    for g in range(G):
        wg = jnp.where(tril, ws_ref[g], 0.0).astype(BF16)
        fs.append(jnp.dot(wg, vn[:, g * SEQ_BLOCK:(g + 1) * SEQ_BLOCK].astype(BF16), preferred_element_type=F32))
    bias = jnp.dot(bsT, selT, precision=lax.Precision.HIGHEST, preferred_element_type=F32)
    return jnp.concatenate(fs, axis=1) + bias


def _sgu_specs(W, G):
    return [
        pl.BlockSpec((SEQ_BLOCK, 2 * W), lambda n: (n, 0)),
        pl.BlockSpec((1, W), lambda n: (0, 0)),
        pl.BlockSpec((1, W), lambda n: (0, 0)),
        pl.BlockSpec((G, SEQ_BLOCK, SEQ_BLOCK), lambda n: (0, 0, 0)),
        pl.BlockSpec((SEQ_BLOCK, G), lambda n: (0, 0)),
        pl.BlockSpec((G, W), lambda n: (0, 0)),
    ]


def sgu_fwd(zp, ln_g, ln_b, ws, bsT, selT, name):
    S, W2 = zp.shape
    W = W2 // 2
    G = ws.shape[0]

    def body(z_ref, lg_ref, lb_ref, ws_ref, bs_ref, sel_ref, o_ref):
        u, vn = _sgu_ln(z_ref[:, :W], z_ref[:, W:], lg_ref[...], lb_ref[...])
        o_ref[...] = (u * _sgu_spatial(ws_ref, bs_ref[...], sel_ref[...], vn, G)).astype(o_ref.dtype)

    return pl.pallas_call(
        body, name=name, grid=(S // SEQ_BLOCK,), in_specs=_sgu_specs(W, G), out_specs=pl.BlockSpec((SEQ_BLOCK, W), lambda n: (n, 0)),
        out_shape=jax.ShapeDtypeStruct((S, W), BF16), compiler_params=_cp(("parallel",)),
    )(zp, ln_g, ln_b, ws, bsT, selT)


def sgu_bwd(zp, ln_g, ln_b, ws, bsT, selT, dgated, name):
    S, W2 = zp.shape
    W = W2 // 2
    G = ws.shape[0]

    def body(z_ref, lg_ref, lb_ref, ws_ref, bs_ref, sel_ref, dgt_ref, dz_ref, dlg_ref, dlb_ref, dws_ref, dbs_ref):
        (u, vn), vjp = jax.vjp(_sgu_ln, z_ref[:, :W], z_ref[:, W:], lg_ref[...], lb_ref[...])
        f = _sgu_spatial(ws_ref, bs_ref[...], sel_ref[...], vn, G)
        dgt = dgt_ref[...].astype(F32)
        du, df = dgt * f, dgt * u

        @pl.when(pl.program_id(0) == 0)
        def _():
            dlg_ref[...] = jnp.zeros_like(dlg_ref)
            dlb_ref[...] = jnp.zeros_like(dlb_ref)
            dws_ref[...] = jnp.zeros_like(dws_ref)
            dbs_ref[...] = jnp.zeros_like(dbs_ref)

        dbs_ref[...] += lax.dot_general(df, sel_ref[...], _DIMS["nt"], precision=lax.Precision.HIGHEST, preferred_element_type=F32)
        tril = _tril_mask()
        dvn = []
        for g in range(G):
            sl = slice(g * SEQ_BLOCK, (g + 1) * SEQ_BLOCK)
            wg = jnp.where(tril, ws_ref[g], 0.0).astype(BF16)
            df_g = df[:, sl].astype(BF16)
            dw = lax.dot_general(df_g, vn[:, sl].astype(BF16), _DIMS["nt"], preferred_element_type=F32)
            dws_ref[g] += jnp.where(tril, dw, 0.0)
            dvn.append(lax.dot_general(wg, df_g, _DIMS["tn"], preferred_element_type=F32))
        dzu, dzv, dlg, dlb = vjp((du, jnp.concatenate(dvn, axis=1)))
        dz_ref[:, :W] = dzu.astype(dz_ref.dtype)
        dz_ref[:, W:] = dzv.astype(dz_ref.dtype)
        dlg_ref[...] += dlg
        dlb_ref[...] += dlb

    vec = jax.ShapeDtypeStruct((1, W), F32)
    return pl.pallas_call(
        body, name=name, grid=(S // SEQ_BLOCK,),
        in_specs=_sgu_specs(W, G) + [pl.BlockSpec((SEQ_BLOCK, W), lambda n: (n, 0))],
        out_specs=[
            pl.BlockSpec((SEQ_BLOCK, 2 * W), lambda n: (n, 0)),
            pl.BlockSpec((1, W), lambda n: (0, 0)),
            pl.BlockSpec((1, W), lambda n: (0, 0)),
            pl.BlockSpec((G, SEQ_BLOCK, SEQ_BLOCK), lambda n: (0, 0, 0)),
            pl.BlockSpec((SEQ_BLOCK, G), lambda n: (0, 0)),
        ],
        out_shape=[jax.ShapeDtypeStruct((S, W2), BF16), vec, vec, jax.ShapeDtypeStruct(ws.shape, F32), jax.ShapeDtypeStruct((SEQ_BLOCK, G), F32)],
        compiler_params=_cp(("arbitrary",)),
    )(zp, ln_g, ln_b, ws, bsT, selT, dgated)


def _rope_matrix():
    half = ROPE_DIM // 2
    R = np.zeros((SWA_HEAD_DIM, SWA_HEAD_DIM), np.float32)
    for j in range(half):
        R[j + half, j] = -1.0
        R[j, j + half] = 1.0
    return R


def _swa_block(q4, kp, kc, vp, vc, sink, Cq, Sq, Cp, Sp, R, n, G):
    B, Dh = SEQ_BLOCK, SWA_HEAD_DIM
    rot = lambda t: jnp.dot(t, R, precision=lax.Precision.HIGHEST, preferred_element_type=F32)
    q = q4.reshape(G * B, Dh)
    Cq4 = jnp.concatenate([Cq] * G, axis=0)
    Sq4 = jnp.concatenate([Sq] * G, axis=0)
    qr = q * Cq4 + rot(q) * Sq4
    kb = jnp.concatenate([kp * Cp + rot(kp) * Sp, kc * Cq + rot(kc) * Sq], axis=0)
    vb = jnp.concatenate([vp, vc], axis=0)
    s = lax.dot_general(qr.astype(BF16), kb.astype(BF16), _DIMS["nt"], preferred_element_type=F32) * (Dh ** -0.5)
    qi = lax.broadcasted_iota(jnp.int32, (G * B, 2 * B), 0) & (B - 1)
    ki = lax.broadcasted_iota(jnp.int32, (G * B, 2 * B), 1) - B
    rel = qi - ki
    valid = (rel >= 0) & (rel < B) & (n * B + ki >= 0)
    s = jnp.where(valid, s, NEG)
    m = lax.stop_gradient(jnp.maximum(jnp.max(s, axis=1, keepdims=True), sink))
    p = jnp.exp(s - m)
    p = p / (jnp.sum(p, axis=1, keepdims=True) + jnp.exp(sink - m))
    o = jnp.dot(p.astype(BF16), vb.astype(BF16), preferred_element_type=F32)
    return o.reshape(G, B, Dh)


def _swa_specs(G):
    B, Dh = SEQ_BLOCK, SWA_HEAD_DIM
    prev = lambda n: jnp.maximum(n - 1, 0)
    return [
        pl.BlockSpec((G, B, Dh), lambda h, n: (h, n, 0)),
        pl.BlockSpec((1, B, Dh), lambda h, n: (h, prev(n), 0)),
        pl.BlockSpec((1, B, Dh), lambda h, n: (h, n, 0)),
        pl.BlockSpec((1, B, Dh), lambda h, n: (h, prev(n), 0)),
        pl.BlockSpec((1, B, Dh), lambda h, n: (h, n, 0)),
        pl.BlockSpec((1, G * B, 1), lambda h, n: (h, 0, 0)),
        pl.BlockSpec((B, Dh), lambda h, n: (n, 0)),
        pl.BlockSpec((B, Dh), lambda h, n: (n, 0)),
        pl.BlockSpec((B, Dh), lambda h, n: (prev(n), 0)),
        pl.BlockSpec((B, Dh), lambda h, n: (prev(n), 0)),
        pl.BlockSpec((Dh, Dh), lambda h, n: (0, 0)),
    ]


def swa_fwd(qh, kh, vh, sink_col, C, Sn, R, name):
    Hq, S, Dh = qh.shape
    Hk = kh.shape[0]
    G = Hq // Hk

    def body(q_ref, kp_ref, kc_ref, vp_ref, vc_ref, sk_ref, cq_ref, sq_ref, cp_ref, sp_ref, r_ref, o_ref):
        o = _swa_block(q_ref[...], kp_ref[0], kc_ref[0], vp_ref[0], vc_ref[0], sk_ref[0], cq_ref[...], sq_ref[...], cp_ref[...],
                       sp_ref[...], r_ref[...], pl.program_id(1), G)
        o_ref[...] = o.astype(o_ref.dtype)

    return pl.pallas_call(
        body, name=name, grid=(Hk, S // SEQ_BLOCK), in_specs=_swa_specs(G),
        out_specs=pl.BlockSpec((G, SEQ_BLOCK, Dh), lambda h, n: (h, n, 0)),
        out_shape=jax.ShapeDtypeStruct((Hq, S, Dh), BF16), compiler_params=_cp(("parallel", "parallel")),
    )(qh, kh, kh, vh, vh, sink_col, C, Sn, C, Sn, R)


def swa_bwd(qh, kh, vh, sink_col, C, Sn, R, doh, name):
    Hq, S, Dh = qh.shape
    Hk = kh.shape[0]
    G = Hq // Hk
    B = SEQ_BLOCK

    def body(q_ref, kp_ref, kc_ref, vp_ref, vc_ref, sk_ref, cq_ref, sq_ref, cp_ref, sp_ref, r_ref, do_ref,
             dq_ref, dkp_ref, dkc_ref, dvp_ref, dvc_ref, dsk_ref):
        n = pl.program_id(1)
        fn = lambda q4, kp, kc, vp, vc, sk: _swa_block(q4, kp, kc, vp, vc, sk, cq_ref[...], sq_ref[...], cp_ref[...], sp_ref[...],
                                                      r_ref[...], n, G)
        _, vjp = jax.vjp(fn, q_ref[...], kp_ref[0], kc_ref[0], vp_ref[0], vc_ref[0], sk_ref[0])
        dq, dkp, dkc, dvp, dvc, dsk = vjp(do_ref[...].astype(F32))
        dq_ref[...] = dq
        dkp_ref[0] = dkp
        dkc_ref[0] = dkc
        dvp_ref[0] = dvp
        dvc_ref[0] = dvc

        @pl.when(n == 0)
        def _():
            dsk_ref[...] = jnp.zeros_like(dsk_ref)

        for g in range(G):
            part = jnp.sum(dsk[g * B:(g + 1) * B], axis=0, keepdims=True)
            dsk_ref[0, g:g + 1, :] += jnp.broadcast_to(part, (1, LANES))

    qspec = pl.BlockSpec((G, B, Dh), lambda h, n: (h, n, 0))
    kspec = pl.BlockSpec((1, B, Dh), lambda h, n: (h, n, 0))
    kshape = jax.ShapeDtypeStruct((Hk, S, Dh), F32)
    return pl.pallas_call(
        body, name=name, grid=(Hk, S // B), in_specs=_swa_specs(G) + [qspec],
        out_specs=[qspec, kspec, kspec, kspec, kspec, pl.BlockSpec((1, G, LANES), lambda h, n: (h, 0, 0))],
        out_shape=[jax.ShapeDtypeStruct((Hq, S, Dh), F32), kshape, kshape, kshape, kshape, jax.ShapeDtypeStruct((Hk, G, LANES), F32)],
        compiler_params=_cp(("parallel", "arbitrary")),
    )(qh, kh, kh, vh, vh, sink_col, C, Sn, C, Sn, R, doh)


def shift_add(cur, prev, name):
    Hk, S, Dh = cur.shape
    B = SEQ_BLOCK
    nb = S // B

    def body(c_ref, p_ref, o_ref):
        last = pl.program_id(1) == nb - 1
        o_ref[...] = c_ref[...] + jnp.where(last, 0.0, p_ref[...])

    spec = pl.BlockSpec((1, B, Dh), lambda h, n: (h, n, 0))
    nxt = pl.BlockSpec((1, B, Dh), lambda h, n: (h, jnp.minimum(n + 1, nb - 1), 0))
    return pl.pallas_call(
        body, name=name, grid=(Hk, nb), in_specs=[spec, nxt], out_specs=spec, out_shape=jax.ShapeDtypeStruct(cur.shape, F32),
        compiler_params=_cp(("parallel", "parallel")),
    )(cur, prev)


def loss_head(y, target, name):
    S, D = y.shape
    tr = _tile(S, ROW_TILE, 16)

    def body(y_ref, t_ref, acc_ref, dy_ref):
        err = y_ref[...] - t_ref[...]
        dy_ref[...] = err * (1.0 / D)

        @pl.when(pl.program_id(0) == 0)
        def _():
            acc_ref[...] = jnp.zeros_like(acc_ref)

        acc_ref[...] += jnp.broadcast_to(jnp.sum(err * err).reshape(1, 1), (1, LANES))

    return pl.pallas_call(
        body, name=name, grid=(S // tr,), in_specs=[_row_spec(tr, D)] * 2,
        out_specs=[pl.BlockSpec((1, LANES), lambda i: (0, 0)), _row_spec(tr, D)],
        out_shape=[jax.ShapeDtypeStruct((1, LANES), F32), jax.ShapeDtypeStruct((S, D), F32)], compiler_params=_cp(("arbitrary",)),
    )(y, target)


def _adam_update(w, g, m, v):
    m = ADAM_B1 * m + (1.0 - ADAM_B1) * g
    v = ADAM_B2 * v + (1.0 - ADAM_B2) * jnp.square(g)
    m_hat = m / (1.0 - ADAM_B1 ** ADAM_STEP)
    v_hat = v / (1.0 - ADAM_B2 ** ADAM_STEP)
    delta = -ADAM_LR * (m_hat / (jnp.sqrt(v_hat) + ADAM_EPS) + ADAM_WD * w)
    return delta, m, v


def adamw(w, m, v, gparts, name, gstack=0, emit_g=True):
    R, C = w.shape
    tr = _tile(R, max(8, (128 * 1024) // C), 8)
    spec = pl.BlockSpec((tr, C), lambda i: (i, 0))
    nplain = len(gparts) - (1 if gstack else 0)
    nout = 4 if emit_g else 3

    def body(w_ref, m_ref, v_ref, *rest):
        g_refs, outs = rest[:len(gparts)], rest[len(gparts):]
        g = None
        for r in g_refs[:nplain]:
            g = r[...].astype(F32) if g is None else g + r[...].astype(F32)
        if gstack:
            for t in range(gstack):
                part = g_refs[-1][t].astype(F32)
                g = part if g is None else g + part
        res = _adam_update(w_ref[...], g, m_ref[...], v_ref[...])
        for o_ref, val in zip(outs, ((g,) if emit_g else ()) + res):
            o_ref[...] = val

    gspecs = [spec] * nplain + ([pl.BlockSpec((gstack, tr, C), lambda i: (0, i, 0))] if gstack else [])
    out = jax.ShapeDtypeStruct((R, C), F32)
    return pl.pallas_call(
        body, name=name, grid=(R // tr,), in_specs=[spec] * 3 + gspecs, out_specs=[spec] * nout, out_shape=[out] * nout,
        compiler_params=_cp(("parallel",)),
    )(w, m, v, *gparts)


def ada_fwd(c_all, ada_w, ada_b, name):
    L, D, N = ada_w.shape
    Bp = c_all.shape[0]
    tn = _tile(N, 512)

    def body(c_ref, w_ref, b_ref, o_ref):
        ca = jax.nn.silu(c_ref[...]).astype(BF16)
        o_ref[0] = jnp.dot(ca, w_ref[0].astype(BF16), preferred_element_type=F32) + b_ref[0]

    return pl.pallas_call(
        body, name=name, grid=(L, N // tn),
        in_specs=[pl.BlockSpec((Bp, D), lambda l, j: (0, 0)), pl.BlockSpec((1, D, tn), lambda l, j: (l, 0, j)),
                  pl.BlockSpec((1, 1, tn), lambda l, j: (l, 0, j))],
        out_specs=pl.BlockSpec((1, Bp, tn), lambda l, j: (l, 0, j)), out_shape=jax.ShapeDtypeStruct((L, Bp, N), F32),
        compiler_params=_cp(("parallel", "parallel")),
    )(c_all, ada_w, ada_b)


def ada_wgrad(c_all, dmod, name):
    L, Bp, N = dmod.shape
    D = c_all.shape[1]
    tn = _tile(N, 512)

    def body(c_ref, d_ref, o_ref):
        ca = jax.nn.silu(c_ref[...]).astype(BF16)
        o_ref[0] = lax.dot_general(ca, d_ref[0].astype(BF16), _DIMS["tn"], preferred_element_type=F32)

    return pl.pallas_call(
        body, name=name, grid=(L, N // tn),
        in_specs=[pl.BlockSpec((Bp, D), lambda l, j: (0, 0)), pl.BlockSpec((1, Bp, tn), lambda l, j: (l, 0, j))],
        out_specs=pl.BlockSpec((1, D, tn), lambda l, j: (l, 0, j)), out_shape=jax.ShapeDtypeStruct((L, D, N), F32),
        compiler_params=_cp(("parallel", "parallel")),
    )(c_all, dmod)


N_DEV = 8
N_CHIP = 4
ANY = pl.BlockSpec(memory_space=pl.ANY)


def _place():
    return lax.axis_index("x"), lax.axis_index("y"), lax.axis_index("c")


def _other_chips(x, y):
    chips = [(1 - x, y), (x, 1 - y), (1 - x, 1 - y)]
    return chips, [2 * cx + cy for cx, cy in chips]


def _rcopy(src, dst, ssem, rsem, to):
    return pltpu.make_async_remote_copy(src_ref=src, dst_ref=dst, send_sem=ssem, recv_sem=rsem, device_id=to, device_id_type=MESH)


def ag_small(xs, name):
    R, Wd = xs.shape

    def body(x_ref, out_ref, send_sems, recv_sems, local_sem):
        x, y, c = _place()
        me, sibling = (x, y, c), (x, y, 1 - c)
        chips, _ = _other_chips(x, y)

        def slot(px, py, pc):
            return out_ref.at[4 * px + 2 * py + pc]

        def copy(k, block, to, src=None):
            return _rcopy(slot(*block) if src is None else src, slot(*block), send_sems.at[k], recv_sems.at[k], to)

        mine = pltpu.make_async_copy(x_ref, slot(*me), local_sem)
        mine.start()
        first = [copy(0, me, sibling, src=x_ref)]
        first += [copy(1 + j, me, (*chip, c), src=x_ref) for j, chip in enumerate(chips)]
        for cp in first:
            cp.start()
        passed = [copy(4 + j, (*chip, c), sibling) for j, chip in enumerate(chips)]
        for j, chip in enumerate(chips):
            copy(1 + j, (*chip, c), me).wait_recv()
            passed[j].start()
        copy(0, sibling, me).wait_recv()
        for j, chip in enumerate(chips):
            copy(4 + j, (*chip, 1 - c), me).wait_recv()
        for cp in first + passed:
            cp.wait_send()
        mine.wait()

    vm = pl.BlockSpec(memory_space=pltpu.VMEM)
    return pl.pallas_call(
        body, name=name, out_shape=jax.ShapeDtypeStruct((N_DEV, R, Wd), xs.dtype), in_specs=[vm], out_specs=vm,
        scratch_shapes=[pltpu.SemaphoreType.DMA((7,)), pltpu.SemaphoreType.DMA((7,)), pltpu.SemaphoreType.DMA],
        compiler_params=_cp(),
    )(xs)


def _half_of_shard(by_cols, A, B, h):
    return (h * (A // 2), A // 2, 0, B) if by_cols else (0, A, h * (B // 2), B // 2)


def _shard_in_full(by_cols, A, B, q):
    return (0, q * B) if by_cols else (q * A, 0)


def _window(ref, r0, nr, c0, nc):
    return ref.at[:, pl.ds(r0, nr), pl.ds(c0, nc)]


def ag_weights(shards, by_cols, name):
    n = len(shards)
    geo, full = _ag_shapes(shards, by_cols)

    def body(*refs):
        start, finish = _ag_steps(geo, refs[:n], refs[n:2 * n], refs[2 * n:])
        start()
        finish()

    return pl.pallas_call(
        body, name=name, out_shape=full, in_specs=[ANY] * n, out_specs=[ANY] * n, scratch_shapes=_ag_semaphores(n),
        compiler_params=_cp(),
    )(*shards)


def _ag_shapes(shards, by_cols):
    geo = [(bc,) + s.shape[1:] for bc, s in zip(by_cols, shards)]
    full = [jax.ShapeDtypeStruct((s.shape[0], A, N_CHIP * B) if bc else (s.shape[0], N_CHIP * A, B), s.dtype)
            for (bc, A, B), s in zip(geo, shards)]
    return geo, full


def _ag_semaphores(n):
    return [pltpu.SemaphoreType.DMA((n, 3)) for _ in range(4)]


def _ag_steps(geo, x_refs, o_refs, sems):
    s_ici, r_ici, s_d2d, r_d2d = sems
    pairs = [(t, j) for t in range(len(geo)) for j in range(3)]

    def copies():
        x, y, c = _place()
        q = 2 * x + y
        chips, qs = _other_chips(x, y)

        def landing(t, chip_q, half):
            r0, nr, c0, nc = _half_of_shard(*geo[t], half)
            ro, co = _shard_in_full(*geo[t], chip_q)
            return _window(o_refs[t], ro + r0, nr, co + c0, nc)

        def ici(t, j, landing_q):
            src = _window(x_refs[t], *_half_of_shard(*geo[t], c))
            return _rcopy(src, landing(t, landing_q, c), s_ici.at[t, j], r_ici.at[t, j], (*chips[j], c))

        def handoff(t, j, half):
            blk = landing(t, qs[j], half)
            return _rcopy(blk, blk, s_d2d.at[t, j], r_d2d.at[t, j], (x, y, 1 - c))

        return c, q, qs, ici, handoff

    def start():
        c, q, qs, ici, handoff = copies()
        for t, j in pairs:
            ici(t, j, q).start()

    def finish():
        c, q, qs, ici, handoff = copies()
        for t, j in pairs:
            ici(t, j, qs[j]).wait_recv()
            handoff(t, j, c).start()
        for t, j in pairs:
            handoff(t, j, 1 - c).wait_recv()
        for t, j in pairs:
            ici(t, j, q).wait_send()
            handoff(t, j, c).wait_send()

    return start, finish


def _half_of_full(by_cols, A, B, h):
    return (h * (A // 2), A // 2, 0, N_CHIP * B) if by_cols else (0, N_CHIP * A, h * (B // 2), B // 2)


def _half_shape(by_cols, L, A, B):
    return (L, A // 2, N_CHIP * B) if by_cols else (L, N_CHIP * A, B // 2)


def _piece_shape(by_cols, L, A, B):
    return (L, A // 2, B) if by_cols else (L, A, B // 2)


def sibling_fold(gs, geo, name):
    n = len(gs)

    def body(*refs):
        x_refs, o_refs, (ssem, rsem) = refs[:n], refs[n:2 * n], refs[2 * n:]
        x, y, c = _place()
        cps = [_rcopy(_window(x_refs[t], *_half_of_full(*geo[t], 1 - c)), o_refs[t], ssem.at[t], rsem.at[t], (x, y, 1 - c))
               for t in range(n)]
        for cp in cps:
            cp.start()
        for cp in cps:
            cp.wait()

    dma = pltpu.SemaphoreType.DMA
    out = [jax.ShapeDtypeStruct(_half_shape(bc, g.shape[0], A, B), g.dtype) for (bc, A, B), g in zip(geo, gs)]
    return pl.pallas_call(
        body, name=name, out_shape=out, in_specs=[ANY] * n, out_specs=[ANY] * n, scratch_shapes=[dma((n,)), dma((n,))],
        compiler_params=_cp(),
    )(*gs)


def chip_exchange(rs, geo, name):
    sc = exchange_sidecar(rs, geo)
    n = len(rs)

    def body(*refs):
        start, finish = sc.steps(refs[:n], refs[n:2 * n], refs[2 * n:])
        start()
        finish()

    return pl.pallas_call(
        body, name=name, out_shape=sc.out_shape, in_specs=[ANY] * n, out_specs=[ANY] * n, scratch_shapes=sc.semaphores,
        compiler_params=_cp(),
    )(*rs)


def exchange_sidecar(rs, geo):
    n = len(rs)

    def steps(x_refs, o_refs, sems):
        ssem, rsem = sems

        def copies():
            x, y, c = _place()
            chips, qs = _other_chips(x, y)

            def part(t, chip_q):
                bc, A, B = geo[t]
                return _window(x_refs[t], 0, A // 2, chip_q * B, B) if bc else _window(x_refs[t], chip_q * A, A, 0, B // 2)

            return [_rcopy(part(t, qs[j]), o_refs[t].at[j], ssem.at[t, j], rsem.at[t, j], (*chips[j], c))
                    for t in range(n) for j in range(3)]

        def start():
            for cp in copies():
                cp.start()

        def finish():
            for cp in copies():
                cp.wait()

        return start, finish

    dma = pltpu.SemaphoreType.DMA
    out = [jax.ShapeDtypeStruct((3,) + _piece_shape(bc, r.shape[0], A, B), r.dtype) for (bc, A, B), r in zip(geo, rs)]
    return SideCar(list(rs), out, [dma((n, 3)), dma((n, 3))], steps)


def ag_sidecar(shards, by_cols):
    geo, full = _ag_shapes(shards, by_cols)
    return SideCar(list(shards), full, _ag_semaphores(len(shards)), lambda ins, outs, sems: _ag_steps(geo, ins, outs, sems))


def sibling_share(fs, geo, name):
    n = len(fs)

    def body(*refs):
        x_refs, o_refs, (ssem, rsem) = refs[:n], refs[n:2 * n], refs[2 * n:]
        x, y, c = _place()
        for t in range(n):
            mine = _window(o_refs[t], *_half_of_shard(*geo[t], c))
            _rcopy(mine, mine, ssem.at[t], rsem.at[t], (x, y, 1 - c)).start()
        for t in range(n):
            mine = _window(o_refs[t], *_half_of_shard(*geo[t], c))
            theirs = _window(o_refs[t], *_half_of_shard(*geo[t], 1 - c))
            _rcopy(mine, theirs, ssem.at[t], rsem.at[t], (x, y, 1 - c)).wait_recv()
            _rcopy(mine, mine, ssem.at[t], rsem.at[t], (x, y, 1 - c)).wait_send()
        del x_refs

    dma = pltpu.SemaphoreType.DMA
    return pl.pallas_call(
        body, name=name, out_shape=[jax.ShapeDtypeStruct(f.shape, f.dtype) for f in fs], in_specs=[ANY] * n, out_specs=[ANY] * n,
        input_output_aliases={t: t for t in range(n)}, scratch_shapes=[dma((n,)), dma((n,))], compiler_params=_cp(),
    )(*fs)


SUM_ROWS = 256


def fold_sum(g, recv, by_cols, A, B, qc_idx, name):
    L = g.shape[0]
    _, hr, hc = _half_shape(by_cols, L, A, B)
    tr, tc = _tile(A // 2 if by_cols else A, SUM_ROWS, 16), (B if by_cols else B // 2)
    ro, co = ((A // 2) // tr, 0) if by_cols else (0, 1)

    def body(qc_ref, g_ref, r_ref, o_ref):
        del qc_ref
        o_ref[...] = (g_ref[...].astype(F32) + r_ref[...].astype(F32)).astype(o_ref.dtype)

    spec = pl.BlockSpec((1, tr, tc), lambda l, i, j, qc: (l, i, j))
    grid_spec = pltpu.PrefetchScalarGridSpec(
        num_scalar_prefetch=1, grid=(L, hr // tr, hc // tc),
        in_specs=[pl.BlockSpec((1, tr, tc), lambda l, i, j, qc: (l, i + qc[1] * ro, j + qc[1] * co)), spec], out_specs=spec,
    )
    return pl.pallas_call(
        body, name=name, grid_spec=grid_spec, out_shape=jax.ShapeDtypeStruct((L, hr, hc), BF16),
        compiler_params=_cp(("parallel", "parallel", "parallel")),
    )(qc_idx, g, recv)


def chip_sum(r, ex, by_cols, A, B, qc_idx, name):
    L = r.shape[0]
    _, wr, wc = _piece_shape(by_cols, L, A, B)
    tr = _tile(wr, SUM_ROWS, 16)
    r_ro, r_co = (0, 1) if by_cols else (A // tr, 0)
    o_ro, o_co = ((A // 2) // tr, 0) if by_cols else (0, 1)

    def body(qc_ref, r_ref, e_ref, o_ref):
        del qc_ref
        o_ref[0] = ((r_ref[0].astype(F32) + e_ref[0, 0].astype(F32)) + e_ref[1, 0].astype(F32)) + e_ref[2, 0].astype(F32)

    grid_spec = pltpu.PrefetchScalarGridSpec(
        num_scalar_prefetch=1, grid=(L, wr // tr),
        in_specs=[pl.BlockSpec((1, tr, wc), lambda l, i, qc: (l, i + qc[0] * r_ro, qc[0] * r_co)),
                  pl.BlockSpec((3, 1, tr, wc), lambda l, i, qc: (0, l, i, 0))],
        out_specs=pl.BlockSpec((1, tr, wc), lambda l, i, qc: (l, i + qc[1] * o_ro, qc[1] * o_co)),
    )
    return pl.pallas_call(
        body, name=name, grid_spec=grid_spec, out_shape=jax.ShapeDtypeStruct((L, A, B), F32),
        compiler_params=_cp(("parallel", "parallel")),
    )(qc_idx, r, ex)


BIG = ("ffn_w_gu", "ffn_w_down", "fox_w_in", "fox_w_out", "sgu_w_in", "sgu_w_out", "swa_w_in", "swa_w_out")
COLUMN_SHARDED = ("ffn_w_gu", "fox_w_in", "sgu_w_in", "swa_w_in")
SMALL = ("ada_b", "mix_pre_g", "mix_post_g", "ffn_pre_g", "ffn_post_g", "fox_b_f", "sgu_ln_g", "sgu_ln_b", "sgu_w_s", "sgu_b_s",
         "swa_sinks")
WEIGHTS = ("ada_w", "ada_b", "mix_pre_g", "mix_post_g", "ffn_pre_g", "ffn_post_g", "ffn_w_gu", "ffn_w_down", "fox_w_in", "fox_b_f",
           "fox_w_out", "sgu_w_in", "sgu_ln_g", "sgu_ln_b", "sgu_w_s", "sgu_b_s", "sgu_w_out", "swa_w_in", "swa_sinks", "swa_w_out")
INPUTS = ("x", "c", "positions") + WEIGHTS + ("loss_target",) + tuple("m_" + n for n in WEIGHTS) + tuple("v_" + n for n in WEIGHTS)


def _lane_pad(n):
    return (-n) % LANES


def _pad_shard_columns(t, B):
    if _lane_pad(B) == 0:
        return t
    L, A, _ = t.shape
    return jnp.pad(t.reshape(L, A, N_CHIP, B), ((0, 0), (0, 0), (0, 0), (0, _lane_pad(B)))).reshape(L, A, -1)


def _unpad_shard_columns(t, B):
    if _lane_pad(B) == 0:
        return t
    L, A, _ = t.shape
    return t.reshape(L, A, N_CHIP, B + _lane_pad(B))[..., :B].reshape(L, A, N_CHIP * B)


def place_shard(full, shard, by_cols, qc_idx, name):
    L, A, B = shard.shape
    tr = _tile(A, SUM_ROWS, 16)
    ro, co = (0, 1) if by_cols else (A // tr, 0)

    def body(qc_ref, s_ref, f_ref, o_ref):
        del qc_ref, f_ref
        o_ref[...] = s_ref[...]

    grid_spec = pltpu.PrefetchScalarGridSpec(
        num_scalar_prefetch=1, grid=(L, A // tr), in_specs=[pl.BlockSpec((1, tr, B), lambda l, i, qc: (l, i, 0)), ANY],
        out_specs=pl.BlockSpec((1, tr, B), lambda l, i, qc: (l, i + qc[0] * ro, qc[0] * co)),
    )
    return pl.pallas_call(
        body, name=name, grid_spec=grid_spec, out_shape=jax.ShapeDtypeStruct(full.shape, full.dtype),
        input_output_aliases={2: 0}, compiler_params=_cp(("parallel", "parallel")),
    )(qc_idx, shard, full)


def _pad_rows(flat1d):
    n = flat1d.shape[0]
    pad = (-n) % (8 * LANES)
    return jnp.pad(flat1d, (0, pad)).reshape(-1, LANES)


def _pack_small(parts):
    return jnp.concatenate([_pad_rows(parts[n].astype(F32).reshape(-1)) for n in SMALL], axis=0)


def _unpack_small(packed, shapes):
    out, off = {}, 0
    for n in SMALL:
        size = int(np.prod(shapes[n]))
        rows = (size + 8 * LANES - 1) // (8 * LANES) * 8
        out[n] = packed[off:off + rows].reshape(-1)[:size].reshape(shapes[n])
        off += rows
    return out


def _fox_fwd(h, w_in, b_f, w_out, tag, ride_qkv=None, ride_attn=None):
    S, D = h.shape
    H = b_f.shape[0]
    qkv = mm(h, w_in, "nn", BF16, name=tag + "_qkv", b_cols=(0, 3 * D), sidecar=ride_qkv)
    qkv, rode_qkv = qkv if ride_qkv is not None else (qkv, ())
    fgp = mm(h, w_in, "nn", F32, name=tag + "_fg", b_cols=(3 * D, LANES))
    fgT = fgp[:, :H].T
    cum = fox_gate_fwd(fgT, b_f.reshape(H, 1), tag + "_gate")
    cum_col, cum_row = cum.reshape(H, S, 1), cum.reshape(H, 1, S)
    o, lse, rode_attn = fox_attn_fwd(qkv, cum_col, cum_row, H, tag + "_attn", ride_attn)
    y = mm(o, w_out, "nn", F32, name=tag + "_out")
    return y, (qkv, fgT, cum_col, cum_row, o, lse), rode_qkv, rode_attn


def _fox_bwd(dy, h, w_in, b_f, w_out, ctx, tag, ride_dq=None, ride_dkv=None):
    qkv, fgT, cum_col, cum_row, o, lse = ctx
    S, D = h.shape
    H = b_f.shape[0]
    do = mm(dy, w_out, "nt", BF16, name=tag + "_do")
    dw_out = mm(o, dy, "tn", BF16, name=tag + "_dwout")
    dq, delta, rode_dq = fox_attn_bwd_dq(qkv, do, o, lse, cum_col, cum_row, H, tag + "_dq", ride_dq)
    dk, dv, dcum, rode_dkv = fox_attn_bwd_dkv(qkv, do, lse.reshape(H, 1, S), delta.reshape(H, 1, S), cum_col, cum_row, H,
                                              tag + "_dkv", ride_dkv)
    dfgT, db = fox_gate_bwd(dcum.reshape(H, S), fgT, b_f.reshape(H, 1), tag + "_dgate")
    dfgp = jnp.pad(dfgT.T, ((0, 0), (0, LANES - H))).astype(BF16)
    dh = mm(dq, w_in, "nt", F32, name=tag + "_dhq", b_cols=(0, D))
    dh = mm(dk, w_in, "nt", F32, add=dh, name=tag + "_dhk", b_cols=(D, D))
    dh = mm(dv, w_in, "nt", F32, add=dh, name=tag + "_dhv", b_cols=(2 * D, D))
    dh = mm(dfgp, w_in, "nt", F32, add=dh, name=tag + "_dhf", b_cols=(3 * D, LANES))
    dw_in = jnp.concatenate(
        [mm(h, dq, "tn", BF16, name=tag + "_dwq"), mm(h, dk, "tn", BF16, name=tag + "_dwk"), mm(h, dv, "tn", BF16, name=tag + "_dwv"),
         mm(h, dfgp, "tn", BF16, name=tag + "_dwf")[:, :H]], axis=1)
    return dh, dw_in, dw_out, db[:, 0], rode_dq, rode_dkv


def _sgu_consts(G, W):
    return jnp.asarray(np.repeat(np.eye(G, dtype=np.float32), W // G, axis=1))


def _sgu_fwd(h, w_in, ln_g, ln_b, w_s, b_s, w_out, tag):
    G, W = w_s.shape[0], ln_g.shape[0]
    zp = mm(h, w_in, "nn", F32, name=tag + "_in")
    args = (zp, ln_g.reshape(1, W), ln_b.reshape(1, W), w_s, b_s.T, _sgu_consts(G, W))
    gated = sgu_fwd(*args, tag + "_core")
    y = mm(gated, w_out, "nn", F32, name=tag + "_out")
    return y, (args, gated)


def _sgu_bwd(dy, h, w_in, w_out, ctx, tag):
    args, gated = ctx
    dgated = mm(dy, w_out, "nt", BF16, name=tag + "_dgated")
    dw_out = mm(gated, dy, "tn", BF16, name=tag + "_dwout")
    dzp, dlg, dlb, dws, dbsT = sgu_bwd(*args, dgated, tag + "_dcore")
    dh = mm(dzp, w_in, "nt", F32, name=tag + "_dh")
    dw_in = mm(h, dzp, "tn", BF16, name=tag + "_dwin")
    return dh, dw_in, dw_out, dlg[0], dlb[0], dws, dbsT.T


def _rope_tables(positions):
    inv = ROPE_THETA ** (-jnp.arange(0, ROPE_DIM, 2, dtype=F32) / ROPE_DIM)
    ang = positions.astype(F32)[:, None] * inv
    S = positions.shape[0]
    rest = SWA_HEAD_DIM - ROPE_DIM
    C = jnp.concatenate([jnp.cos(ang), jnp.cos(ang), jnp.ones((S, rest), F32)], axis=1)
    Sn = jnp.concatenate([jnp.sin(ang), jnp.sin(ang), jnp.zeros((S, rest), F32)], axis=1)
    return C, Sn


def _heads(t, n):
    return t.reshape(t.shape[0], n, SWA_HEAD_DIM).transpose(1, 0, 2)


def _unheads(t):
    return t.transpose(1, 0, 2).reshape(t.shape[1], -1)


def _swa_fwd(h, w_in, sinks, w_out, tables, tag):
    Hq = sinks.shape[0]
    Hk = (w_in[0].shape[-1] // SWA_HEAD_DIM - Hq) // 2
    G = Hq // Hk
    proj = mm(h, w_in, "nn", F32, name=tag + "_in")
    qh = _heads(proj[:, :Hq * SWA_HEAD_DIM], Hq)
    kh = _heads(proj[:, Hq * SWA_HEAD_DIM:(Hq + Hk) * SWA_HEAD_DIM], Hk)
    vh = _heads(proj[:, (Hq + Hk) * SWA_HEAD_DIM:], Hk)
    sink_col = jnp.repeat(sinks.reshape(Hk, G), SEQ_BLOCK, axis=1).reshape(Hk, G * SEQ_BLOCK, 1)
    args = (qh, kh, vh, sink_col, tables[0], tables[1], jnp.asarray(_rope_matrix()))
    o = _unheads(swa_fwd(*args, tag + "_core"))
    y = mm(o, w_out, "nn", F32, name=tag + "_out")
    return y, (args, o)


def _swa_bwd(dy, h, w_in, w_out, ctx, tag):
    args, o = ctx
    Hq = args[0].shape[0]
    do = mm(dy, w_out, "nt", BF16, name=tag + "_do")
    dw_out = mm(o, dy, "tn", BF16, name=tag + "_dwout")
    dqh, dkp, dkc, dvp, dvc, dsk = swa_bwd(*args, _heads(do, Hq), tag + "_dcore")
    dk = shift_add(dkc, dkp, tag + "_dk")
    dv = shift_add(dvc, dvp, tag + "_dv")
    dproj = jnp.concatenate([_unheads(dqh), _unheads(dk), _unheads(dv)], axis=1).astype(BF16)
    dh = mm(dproj, w_in, "nt", F32, name=tag + "_dh")
    dw_in = mm(h, dproj, "tn", BF16, name=tag + "_dwin")
    return dh, dw_in, dw_out, dsk[:, :, 0].reshape(Hq)


def kernel(x, c, positions, ada_w, ada_b, mix_pre_g, mix_post_g, ffn_pre_g, ffn_post_g, ffn_w_gu, ffn_w_down, fox_w_in, fox_b_f, fox_w_out, sgu_w_in, sgu_ln_g, sgu_ln_b, sgu_w_s, sgu_b_s, sgu_w_out, swa_w_in, swa_sinks, swa_w_out, loss_target, m_ada_w, m_ada_b, m_mix_pre_g, m_mix_post_g, m_ffn_pre_g, m_ffn_post_g, m_ffn_w_gu, m_ffn_w_down, m_fox_w_in, m_fox_b_f, m_fox_w_out, m_sgu_w_in, m_sgu_ln_g, m_sgu_ln_b, m_sgu_w_s, m_sgu_b_s, m_sgu_w_out, m_swa_w_in, m_swa_sinks, m_swa_w_out, v_ada_w, v_ada_b, v_mix_pre_g, v_mix_post_g, v_ffn_pre_g, v_ffn_post_g, v_ffn_w_gu, v_ffn_w_down, v_fox_w_in, v_fox_b_f, v_fox_w_out, v_sgu_w_in, v_sgu_ln_g, v_sgu_ln_b, v_sgu_w_s, v_sgu_b_s, v_sgu_w_out, v_swa_w_in, v_swa_sinks, v_swa_w_out):
    P = dict(zip(INPUTS, (x, c, positions, ada_w, ada_b, mix_pre_g, mix_post_g, ffn_pre_g, ffn_post_g, ffn_w_gu, ffn_w_down, fox_w_in, fox_b_f, fox_w_out, sgu_w_in, sgu_ln_g, sgu_ln_b, sgu_w_s, sgu_b_s, sgu_w_out, swa_w_in, swa_sinks, swa_w_out, loss_target, m_ada_w, m_ada_b, m_mix_pre_g, m_mix_post_g, m_ffn_pre_g, m_ffn_post_g, m_ffn_w_gu, m_ffn_w_down, m_fox_w_in, m_fox_b_f, m_fox_w_out, m_sgu_w_in, m_sgu_ln_g, m_sgu_ln_b, m_sgu_w_s, m_sgu_b_s, m_sgu_w_out, m_swa_w_in, m_swa_sinks, m_swa_w_out, v_ada_w, v_ada_b, v_mix_pre_g, v_mix_post_g, v_ffn_pre_g, v_ffn_post_g, v_ffn_w_gu, v_ffn_w_down, v_fox_w_in, v_fox_b_f, v_fox_w_out, v_sgu_w_in, v_sgu_ln_g, v_sgu_ln_b, v_sgu_w_s, v_sgu_b_s, v_sgu_w_out, v_swa_w_in, v_swa_sinks, v_swa_w_out)))
    xs, target, pos = x[0], loss_target[0], positions[0]
    S, D = xs.shape
    L = ada_w.shape[0]
    n_mix = 3
    F = ffn_w_down.shape[1] * N_CHIP
    xi, yi, ci = _place()
    q_me = 2 * xi + yi
    dev = 4 * xi + 2 * yi + ci

    qc = jnp.stack([q_me, ci]).astype(jnp.int32)
    by_cols = {n: n in COLUMN_SHARDED for n in BIG}
    geo = {n: (by_cols[n], P[n].shape[1], P[n].shape[2] + (_lane_pad(P[n].shape[2]) if by_cols[n] else 0)) for n in BIG}
    groups = {
        "fox0": {n: (0, 1) for n in BIG if n.startswith("fox_")},
        "ffn0": {n: (0, 1) for n in BIG if n.startswith("ffn_")},
        "rest": {n: (1 if n.startswith(("fox_", "ffn_")) else 0, P[n].shape[0]) for n in BIG},
    }
    groups["rest"] = {n: r for n, r in groups["rest"].items() if r[1] > r[0]}

    def shard_of(n, lo, hi):
        s = P[n][lo:hi].astype(BF16)
        return jnp.pad(s, ((0, 0), (0, 0), (0, _lane_pad(s.shape[2])))) if by_cols[n] else s

    Wt = {n: [] for n in BIG}

    def finish_gather(g, fulls):
        for n, s, f in zip(groups[g], shards[g], fulls):
            f = place_shard(f, s, by_cols[n], qc, f"place_{g}_{n}")
            f = _unpad_shard_columns(f, P[n].shape[2]) if by_cols[n] else f
            f = jnp.pad(f, ((0, 0), (0, 0), (0, 3 * D + LANES - f.shape[2]))) if n == "fox_w_in" else f
            Wt[n] += [(f, l) for l in range(f.shape[0])]

    shards = {g: [shard_of(n, *r) for n, r in groups[g].items()] for g in groups}
    group_cols = {g: [by_cols[n] for n in groups[g]] for g in groups}
    finish_gather("fox0", ag_weights(shards["fox0"], group_cols["fox0"], "ag_weights_fox0"))

    c_all = ag_small(c.reshape(D // LANES, LANES), "ag_c").reshape(N_DEV, D)
    c_all = jnp.pad(c_all, ((0, 16 - N_DEV), (0, 0)))
    Nm = ada_w.shape[2]
    ada_b_mine = lax.dynamic_slice_in_dim(ada_b, q_me * Nm, Nm, axis=1).reshape(L, 1, Nm)
    modp = ada_fwd(c_all, ada_w, ada_b_mine, "ada_fwd")[:, :N_DEV]
    mod_all = ag_small(modp.reshape(-1, LANES), "ag_mod").reshape(N_DEV, L, N_DEV, Nm)
    mod_mine = lax.dynamic_index_in_dim(mod_all[0::2], dev, axis=2, keepdims=False)
    mods = mod_mine.transpose(1, 0, 2).reshape(L, 6, 1, D)

    tables = _rope_tables(pos)

    saved = []
    xc = xs
    for i in range(L):
        kind, j = i % n_mix, i // n_mix
        sh_m, sc_m, g_m, sh_f, sc_f, g_f = (mods[i, t] for t in range(6))
        t = f"l{i}"
        h1 = pre_fwd(xc, mix_pre_g[i:i + 1], sh_m, sc_m, t + "_pre_m")
        if kind == 0:
            rides = [ag_sidecar(shards[g], group_cols[g]) if i == 0 else None for g in ("ffn0", "rest")]
            y1, ctx, ffn0_fulls, rest_fulls = _fox_fwd(h1, Wt["fox_w_in"][j], fox_b_f[j], Wt["fox_w_out"][j], t + "_fox", *rides)
            if i == 0:
                finish_gather("ffn0", ffn0_fulls)
                finish_gather("rest", rest_fulls)
        elif kind == 1:
            y1, ctx = _sgu_fwd(h1, Wt["sgu_w_in"][j], sgu_ln_g[j], sgu_ln_b[j], sgu_w_s[j], sgu_b_s[j], Wt["sgu_w_out"][j], t + "_sgu")
        else:
            y1, ctx = _swa_fwd(h1, Wt["swa_w_in"][j], swa_sinks[j], Wt["swa_w_out"][j], tables, t + "_swa")
        xm = post_fwd(xc, y1, mix_post_g[i:i + 1], g_m, t + "_post_m")
        h2 = pre_fwd(xm, ffn_pre_g[i:i + 1], sh_f, sc_f, t + "_pre_f")
        gu = mm(h2, Wt["ffn_w_gu"][i], "nn", BF16, name=t + "_ffn_gu")
        a = act_fwd(gu, t + "_act")
        y2 = mm(a, Wt["ffn_w_down"][i], "nn", F32, name=t + "_ffn_down")
        xn = post_fwd(xm, y2, ffn_post_g[i:i + 1], g_f, t + "_post_f")
        saved.append((xc, h1, y1, ctx, xm, h2, gu, a, y2))
        xc = xn

    sq, dx = loss_head(xc, target, "loss_head")
    loss = lax.psum(sq[0, 0] * (0.5 / D), ("x", "y", "c"))

    big_g = {n: [None] * P[n].shape[0] for n in BIG}
    small_g = {n: [None] * P[n].shape[0] for n in SMALL}
    group_geo = {g: [geo[n] for n in groups[g]] for g in groups}
    chip_part, from_chips = {}, {}

    def fold(g):
        gs = [jnp.stack(big_g[n][lo:hi]) for n, (lo, hi) in groups[g].items()]
        gs = [_pad_shard_columns(t, P[n].shape[2]) if by_cols[n] else t for n, t in zip(groups[g], gs)]
        from_sibling = sibling_fold(gs, group_geo[g], "rs_fold_" + g)
        chip_part[g] = [fold_sum(t, r, *m, qc, f"rs_fold_sum_{g}_{n}") for n, t, r, m in zip(groups[g], gs, from_sibling, group_geo[g])]
        return exchange_sidecar(chip_part[g], group_geo[g])

    for i in reversed(range(L)):
        kind, j = i % n_mix, i // n_mix
        sh_m, sc_m, g_m, sh_f, sc_f, g_f = (mods[i, t] for t in range(6))
        xc, h1, y1, ctx, xm, h2, gu, a, y2 = saved[i]
        t = f"l{i}"
        dy2, dgpost_f, dgate_f = post_bwd(y2, ffn_post_g[i:i + 1], g_f, dx, t + "_dpost_f")
        da = mm(dy2, Wt["ffn_w_down"][i], "nt", F32, name=t + "_da")
        big_g["ffn_w_down"][i] = mm(a, dy2, "tn", BF16, name=t + "_dwdown")
        dgu = act_bwd(gu, da, t + "_dact")
        dh2 = mm(dgu, Wt["ffn_w_gu"][i], "nt", F32, name=t + "_dh2")
        big_g["ffn_w_gu"][i] = mm(h2, dgu, "tn", BF16, name=t + "_dwgu")
        dxm, dgpre_f, dsh_f, dsc_f = pre_bwd(xm, ffn_pre_g[i:i + 1], sh_f, sc_f, dh2, dx, t + "_dpre_f")
        dy1, dgpost_m, dgate_m = post_bwd(y1, mix_post_g[i:i + 1], g_m, dxm, t + "_dpost_m")
        if kind == 0:
            rides = [fold(g) if i == 0 else None for g in ("ffn0", "rest")]
            dh1, dw_in, dw_out, db, from_ffn0, from_rest = _fox_bwd(dy1, h1, Wt["fox_w_in"][j], fox_b_f[j], Wt["fox_w_out"][j], ctx,
                                                                    t + "_fox", *rides)
            big_g["fox_w_in"][j], big_g["fox_w_out"][j], small_g["fox_b_f"][j] = dw_in, dw_out, db
            if i == 0:
                from_chips["ffn0"], from_chips["rest"] = from_ffn0, from_rest
        elif kind == 1:
            dh1, dw_in, dw_out, dlg, dlb, dws, dbs = _sgu_bwd(dy1, h1, Wt["sgu_w_in"][j], Wt["sgu_w_out"][j], ctx, t + "_sgu")
            big_g["sgu_w_in"][j], big_g["sgu_w_out"][j] = dw_in, dw_out
            small_g["sgu_ln_g"][j], small_g["sgu_ln_b"][j], small_g["sgu_w_s"][j], small_g["sgu_b_s"][j] = dlg, dlb, dws, dbs
        else:
            dh1, dw_in, dw_out, dsk = _swa_bwd(dy1, h1, Wt["swa_w_in"][j], Wt["swa_w_out"][j], ctx, t + "_swa")
            big_g["swa_w_in"][j], big_g["swa_w_out"][j], small_g["swa_sinks"][j] = dw_in, dw_out, dsk
        dx, dgpre_m, dsh_m, dsc_m = pre_bwd(xc, mix_pre_g[i:i + 1], sh_m, sc_m, dh1, dxm, t + "_dpre_m")
        small_g["ada_b"][i] = jnp.concatenate([dsh_m, dsc_m, dgate_m, dsh_f, dsc_f, dgate_f], axis=1)[0]
        small_g["mix_pre_g"][i], small_g["mix_post_g"][i] = dgpre_m[0], dgpost_m[0]
        small_g["ffn_pre_g"][i], small_g["ffn_post_g"][i] = dgpre_f[0], dgpost_f[0]
    grad_x = dx[None]

    shapes = {n: P[n].shape for n in SMALL}
    small_parts = ag_small(_pack_small({n: jnp.stack(small_g[n]) for n in SMALL}), "ag_small_grads")
    sg, sd, sm, sv = adamw(_pack_small({n: P[n] for n in SMALL}), _pack_small({n: P["m_" + n] for n in SMALL}),
                           _pack_small({n: P["v_" + n] for n in SMALL}), [small_parts], "adamw_small", gstack=N_DEV)
    out_g, out_d, out_m, out_v = (_unpack_small(t, shapes) for t in (sg, sd, sm, sv))

    dmod_all = small_parts[:, :L * 6 * D // LANES].reshape(N_DEV, L, 6 * D)
    dmod_mine = lax.dynamic_slice_in_dim(dmod_all, q_me * Nm, Nm, axis=2).transpose(1, 0, 2)
    dmod_mine = jnp.pad(dmod_mine, ((0, 0), (0, 16 - N_DEV), (0, 0)))
    g_ada = ada_wgrad(c_all, dmod_mine, "ada_wgrad")
    r2 = lambda t: t.reshape(-1, t.shape[-1])
    res = adamw(r2(ada_w), r2(m_ada_w), r2(v_ada_w), [r2(g_ada)], "adamw_ada_w", emit_g=False)
    out_g["ada_w"] = g_ada
    out_d["ada_w"], out_m["ada_w"], out_v["ada_w"] = (t.reshape(ada_w.shape) for t in res)

    fold("fox0")
    from_chips["fox0"] = chip_exchange(chip_part["fox0"], group_geo["fox0"], "rs_exchange_fox0")
    pieces = [(g, n, m, r, e) for g in groups for n, m, r, e in zip(groups[g], group_geo[g], chip_part[g], from_chips[g])]
    mine = [chip_sum(r, e, *m, qc, f"rs_chip_sum_{g}_{n}") for g, n, m, r, e in pieces]
    shared = sibling_share(mine, [m for _, _, m, _, _ in pieces], "rs_share")
    by_name = {n: [] for n in BIG}
    for (g, n, *_), t in zip(pieces, shared):
        by_name[n].append(t)
    for n in BIG:
        gsh = by_name[n][0] if len(by_name[n]) == 1 else jnp.concatenate(by_name[n], axis=0)
        gsh = gsh[:, :, :P[n].shape[2]]
        res = adamw(r2(P[n]), r2(P["m_" + n]), r2(P["v_" + n]), [r2(gsh)], "adamw_" + n, emit_g=False)
        out_g[n] = gsh
        out_d[n], out_m[n], out_v[n] = (t.reshape(P[n].shape) for t in res)

    return (loss, grad_x, *[out_g[n] for n in WEIGHTS], *[out_d[n] for n in WEIGHTS], *[out_m[n] for n in WEIGHTS],
            *[out_v[n] for n in WEIGHTS])
```

```python
from typing import Callable, NamedTuple

import numpy as np
import jax
import jax.numpy as jnp
from jax import lax
from jax.experimental import pallas as pl
from jax.experimental.pallas import tpu as pltpu

F32 = jnp.float32
BF16 = jnp.bfloat16
MESH = pl.DeviceIdType.MESH

EPS = 1e-6
NEG = -1e30
V7X_VMEM_BYTES = 64 * 1024 * 1024
VMEM_LIMIT = V7X_VMEM_BYTES - 8 * 1024 * 1024
LANES = 128
SEQ_BLOCK = 128
SWA_HEAD_DIM = 64
ROPE_DIM = SWA_HEAD_DIM // 4
ROPE_THETA = 500000.0

ADAM_LR = 0.001
ADAM_B1 = 0.9
ADAM_B2 = 0.999
ADAM_EPS = 1e-08
ADAM_WD = 0.01
ADAM_STEP = 10


def _cp(sem=None, **kw):
    return pltpu.CompilerParams(dimension_semantics=sem, vmem_limit_bytes=VMEM_LIMIT, **kw)


def _tile(dim, pref, mult=LANES):
    if dim <= pref:
        return dim
    t = (pref // mult) * mult
    while t >= mult:
        if dim % t == 0:
            return t
        t -= mult
    return dim


_DIMS = {"nn": (((1,), (0,)), ((), ())), "nt": (((1,), (1,)), ((), ())), "tn": (((0,), (0,)), ((), ()))}


MM_TILES = {"nn": (1024, 512, 2816), "nt": (1024, 1024, 2816), "tn": (1024, 1024, 2048)}


def mm(a, b, mode="nn", out_dtype=F32, add=None, name="mm", b_cols=None, tm=None, tn=None, tk=None, sidecar=None):
    tm, tn, tk = (d if t is None else t for t, d in zip((tm, tn, tk), MM_TILES[mode]))
    b, b_layer = b if isinstance(b, tuple) else (b, None)
    b_shape = b.shape[-2:]
    c0 = 0
    if b_cols is not None:
        c0, csize = b_cols
    if mode == "nn":
        (M, K), (K2, N) = a.shape, b_shape
        if b_cols is not None:
            N = csize
    elif mode == "nt":
        (M, K), (N, K2) = a.shape, b_shape
        if b_cols is not None:
            K2 = csize
    else:
        (K, M), (K2, N) = a.shape, b_shape
        assert b_cols is None
    assert K == K2, (a.shape, b.shape, mode)
    tm = _tile(M, tm, LANES if mode == "tn" else 16)
    tn = _tile(N, tn)
    tk = _tile(K, tk, LANES if mode != "tn" else 16)
    nk = K // tk
    if b_cols is not None:
        assert c0 % (tn if mode == "nn" else tk) == 0, (b_cols, tn, tk)
    bo = c0 // (tn if mode == "nn" else tk)
    dims = _DIMS[mode]
    has_add = add is not None

    def body(a_ref, b_ref, *rest):
        if has_add:
            add_ref, o_ref, acc_ref = rest
        else:
            o_ref, acc_ref = rest
        k = pl.program_id(2)
        p = lax.dot_general(a_ref[...].astype(BF16), b_ref[...].astype(BF16), dims, preferred_element_type=F32)

        @pl.when(k == 0)
        def _():
            acc_ref[...] = p + add_ref[...].astype(F32) if has_add else p

        @pl.when(k > 0)
        def _():
            acc_ref[...] += p

        @pl.when(k == nk - 1)
        def _():
            o_ref[...] = acc_ref[...].astype(o_ref.dtype)

    a_spec = pl.BlockSpec((tk, tm), lambda i, j, k: (k, i)) if mode == "tn" else pl.BlockSpec((tm, tk), lambda i, j, k: (i, k))
    b_blk, b_idx = ((tn, tk), lambda i, j, k: (j, k + bo)) if mode == "nt" else ((tk, tn), lambda i, j, k: (k, j + bo))
    if b_layer is None:
        b_spec = pl.BlockSpec(b_blk, b_idx)
    else:
        b_spec = pl.BlockSpec((None,) + b_blk, lambda i, j, k: (b_layer,) + b_idx(i, j, k))
    o_spec = pl.BlockSpec((tm, tn), lambda i, j, k: (i, j))
    in_specs = [a_spec, b_spec] + ([o_spec] if has_add else [])
    args = (a, b) + ((add,) if has_add else ())
    out_shape = jax.ShapeDtypeStruct((M, N), out_dtype)
    scratch = [pltpu.VMEM((tm, tn), F32)]
    grid = (M // tm, N // tn, nk)
    if sidecar is None:
        return pl.pallas_call(
            body, name=name, grid=grid, in_specs=in_specs, out_specs=o_spec, out_shape=out_shape, scratch_shapes=scratch,
            compiler_params=_cp(("parallel", "parallel", "arbitrary")),
        )(*args)
    first = lambda: (pl.program_id(0) == 0) & (pl.program_id(1) == 0) & (pl.program_id(2) == 0)
    last = lambda: (pl.program_id(0) == grid[0] - 1) & (pl.program_id(1) == grid[1] - 1) & (pl.program_id(2) == grid[2] - 1)
    res = pl.pallas_call(
        _with_sidecar(sidecar, len(args), 1, 1, body, first, last), name=name, grid=grid,
        in_specs=in_specs + _sc_specs(sidecar, False), out_specs=[o_spec] + _sc_specs(sidecar, True),
        out_shape=[out_shape] + _sc_out(sidecar), scratch_shapes=scratch + _sc_sems(sidecar),
        compiler_params=_cp(("arbitrary", "arbitrary", "arbitrary")),
    )(*args, *_sc_arrays(sidecar))
    return res[0], res[1:]


def _rms(x, g):
    return (x * lax.rsqrt(jnp.mean(x * x, axis=-1, keepdims=True) + EPS)) * g


def _pre(x, g, sh, sc):
    return _rms(x, g) * (1 + sc) + sh


def _post(x, y, g, gate):
    return x + gate * _rms(y, g)


ROW_TILE = 256


def _row_spec(tr, d):
    return pl.BlockSpec((tr, d), lambda i: (i, 0))


def _vec_spec(d):
    return pl.BlockSpec((1, d), lambda i: (0, 0))


def pre_fwd(x, g, sh, sc, name):
    S, D = x.shape
    tr = _tile(S, ROW_TILE, 16)

    def body(x_ref, g_ref, sh_ref, sc_ref, h_ref):
        h_ref[...] = _pre(x_ref[...], g_ref[...], sh_ref[...], sc_ref[...]).astype(h_ref.dtype)

    return pl.pallas_call(
        body, name=name, grid=(S // tr,), in_specs=[_row_spec(tr, D)] + [_vec_spec(D)] * 3, out_specs=_row_spec(tr, D),
        out_shape=jax.ShapeDtypeStruct((S, D), BF16), compiler_params=_cp(("parallel",)),
    )(x, g, sh, sc)


def pre_bwd(x, g, sh, sc, dh, dres, name):
    S, D = x.shape
    tr = _tile(S, ROW_TILE, 16)

    def body(x_ref, g_ref, sh_ref, sc_ref, dh_ref, dres_ref, dx_ref, dg_ref, dsh_ref, dsc_ref):
        _, vjp = jax.vjp(_pre, x_ref[...], g_ref[...], sh_ref[...], sc_ref[...])
        dx, dg, dsh, dsc = vjp(dh_ref[...].astype(F32))
        dx_ref[...] = dres_ref[...] + dx

        @pl.when(pl.program_id(0) == 0)
        def _():
            dg_ref[...] = jnp.zeros_like(dg_ref)
            dsh_ref[...] = jnp.zeros_like(dsh_ref)
            dsc_ref[...] = jnp.zeros_like(dsc_ref)

        dg_ref[...] += dg
        dsh_ref[...] += dsh
        dsc_ref[...] += dsc

    vec = jax.ShapeDtypeStruct((1, D), F32)
    return pl.pallas_call(
        body, name=name, grid=(S // tr,), in_specs=[_row_spec(tr, D)] + [_vec_spec(D)] * 3 + [_row_spec(tr, D)] * 2,
        out_specs=[_row_spec(tr, D)] + [_vec_spec(D)] * 3, out_shape=[jax.ShapeDtypeStruct((S, D), F32), vec, vec, vec],
        compiler_params=_cp(("arbitrary",)),
    )(x, g, sh, sc, dh, dres)


def post_fwd(x, y, g, gate, name):
    S, D = x.shape
    tr = _tile(S, ROW_TILE, 16)

    def body(x_ref, y_ref, g_ref, gate_ref, o_ref):
        o_ref[...] = _post(x_ref[...], y_ref[...], g_ref[...], gate_ref[...])

    return pl.pallas_call(
        body, name=name, grid=(S // tr,), in_specs=[_row_spec(tr, D)] * 2 + [_vec_spec(D)] * 2, out_specs=_row_spec(tr, D),
        out_shape=jax.ShapeDtypeStruct((S, D), F32), compiler_params=_cp(("parallel",)),
    )(x, y, g, gate)


def post_bwd(y, g, gate, dxn, name):
    S, D = y.shape
    tr = _tile(S, ROW_TILE, 16)

    def body(y_ref, g_ref, gate_ref, dxn_ref, dy_ref, dg_ref, dgate_ref):
        fn = lambda yy, gg, gt: gt * _rms(yy, gg)
        _, vjp = jax.vjp(fn, y_ref[...], g_ref[...], gate_ref[...])
        dy, dg, dgate = vjp(dxn_ref[...])
        dy_ref[...] = dy.astype(dy_ref.dtype)

        @pl.when(pl.program_id(0) == 0)
        def _():
            dg_ref[...] = jnp.zeros_like(dg_ref)
            dgate_ref[...] = jnp.zeros_like(dgate_ref)

        dg_ref[...] += dg
        dgate_ref[...] += dgate

    vec = jax.ShapeDtypeStruct((1, D), F32)
    return pl.pallas_call(
        body, name=name, grid=(S // tr,), in_specs=[_row_spec(tr, D)] + [_vec_spec(D)] * 2 + [_row_spec(tr, D)],
        out_specs=[_row_spec(tr, D)] + [_vec_spec(D)] * 2, out_shape=[jax.ShapeDtypeStruct((S, D), BF16), vec, vec],
        compiler_params=_cp(("arbitrary",)),
    )(y, g, gate, dxn)


def _swiglu(g, u):
    return jax.nn.silu(g) * u


ACT_ROWS = 256


def act_fwd(gu, name):
    S, F2 = gu.shape
    F = F2 // 2
    tr = _tile(S, ACT_ROWS, 16)

    def body(gu_ref, a_ref):
        a_ref[...] = _swiglu(gu_ref[:, :F].astype(F32), gu_ref[:, F:].astype(F32)).astype(a_ref.dtype)

    return pl.pallas_call(
        body, name=name, grid=(S // tr,), in_specs=[_row_spec(tr, F2)], out_specs=_row_spec(tr, F),
        out_shape=jax.ShapeDtypeStruct((S, F), BF16), compiler_params=_cp(("parallel",)),
    )(gu)


def act_bwd(gu, da, name):
    S, F2 = gu.shape
    F = F2 // 2
    tr = _tile(S, ACT_ROWS, 16)

    def body(gu_ref, da_ref, dgu_ref):
        _, vjp = jax.vjp(_swiglu, gu_ref[:, :F].astype(F32), gu_ref[:, F:].astype(F32))
        dg, du = vjp(da_ref[...].astype(F32))
        dgu_ref[:, :F] = dg.astype(dgu_ref.dtype)
        dgu_ref[:, F:] = du.astype(dgu_ref.dtype)

    return pl.pallas_call(
        body, name=name, grid=(S // tr,), in_specs=[_row_spec(tr, F2), _row_spec(tr, F)], out_specs=_row_spec(tr, F2),
        out_shape=jax.ShapeDtypeStruct((S, F2), BF16), compiler_params=_cp(("parallel",)),
    )(gu, da)


def _tri(upper):
    r = lax.broadcasted_iota(jnp.int32, (LANES, LANES), 0)
    c = lax.broadcasted_iota(jnp.int32, (LANES, LANES), 1)
    return ((r <= c) if upper else (r >= c)).astype(F32)


def _hdot(a, b):
    return jnp.dot(a, b, precision=lax.Precision.HIGHEST, preferred_element_type=F32)


def fox_gate_fwd(fgT, b, name):
    H, S = fgT.shape
    spec = pl.BlockSpec((H, LANES), lambda ch: (0, ch))

    def body(fg_ref, b_ref, cum_ref, carry_ref):
        @pl.when(pl.program_id(0) == 0)
        def _():
            carry_ref[...] = jnp.zeros_like(carry_ref)

        lf = jax.nn.log_sigmoid(fg_ref[...] + b_ref[...])
        cum_ref[...] = _hdot(lf, _tri(True)) + carry_ref[...]
        carry_ref[...] += _hdot(lf, jnp.ones((LANES, LANES), F32))

    return pl.pallas_call(
        body, name=name, grid=(S // LANES,), in_specs=[spec, pl.BlockSpec((H, 1), lambda ch: (0, 0))], out_specs=spec,
        out_shape=jax.ShapeDtypeStruct((H, S), F32), scratch_shapes=[pltpu.VMEM((H, LANES), F32)], compiler_params=_cp(("arbitrary",)),
    )(fgT, b)


def fox_gate_bwd(dcum, fgT, b, name):
    H, S = fgT.shape
    nch = S // LANES
    spec = pl.BlockSpec((H, LANES), lambda t: (0, nch - 1 - t))

    def body(dcum_ref, fg_ref, b_ref, dfg_ref, db_ref, tail_ref):
        @pl.when(pl.program_id(0) == 0)
        def _():
            tail_ref[...] = jnp.zeros_like(tail_ref)
            db_ref[...] = jnp.zeros_like(db_ref)

        dlf = _hdot(dcum_ref[...], _tri(False)) + tail_ref[...]
        dfg = dlf * jax.nn.sigmoid(-(fg_ref[...] + b_ref[...]))
        dfg_ref[...] = dfg
        ones = jnp.ones((LANES, LANES), F32)
        tail_ref[...] += _hdot(dcum_ref[...], ones)
        db_ref[...] += _hdot(dfg, ones)

    return pl.pallas_call(
        body, name=name, grid=(nch,), in_specs=[spec, spec, pl.BlockSpec((H, 1), lambda t: (0, 0))],
        out_specs=[spec, pl.BlockSpec((H, LANES), lambda t: (0, 0))],
        out_shape=[jax.ShapeDtypeStruct((H, S), F32), jax.ShapeDtypeStruct((H, LANES), F32)],
        scratch_shapes=[pltpu.VMEM((H, LANES), F32)], compiler_params=_cp(("arbitrary",)),
    )(dcum, fgT, b)


FOX_TILE = 1024


def _on_and_below_diagonal(i, j, tile):
    @pl.when(j < i)
    def _():
        tile(False)

    @pl.when(j == i)
    def _():
        tile(True)


def _causal_pairs(n, by_key):
    pairs = [(i, j) for i in range(n) for j in range(i + 1)]
    if by_key:
        pairs.sort(key=lambda p: (p[1], p[0]))
    qi = np.asarray([p[0] for p in pairs], np.int32)
    kj = np.asarray([p[1] for p in pairs], np.int32)
    return qi, kj


def _fox_scores(q, k, fq, fk, T, scale, transposed):
    r = lax.broadcasted_iota(jnp.int32, (T, T), 0)
    c = lax.broadcasted_iota(jnp.int32, (T, T), 1)
    if transposed:
        return lax.dot_general(k, q, _DIMS["nt"], preferred_element_type=F32) * scale + (fq - fk), r <= c
    return lax.dot_general(q, k, _DIMS["nt"], preferred_element_type=F32) * scale + (fq - fk), c <= r


class SideCar(NamedTuple):
    arrays: list
    out_shape: list
    semaphores: list
    steps: Callable


def _sc_specs(sc, out):
    return [] if sc is None else [pl.BlockSpec(memory_space=pl.ANY)] * len(sc.out_shape if out else sc.arrays)


def _sc_sems(sc):
    return [] if sc is None else list(sc.semaphores)


def _sc_out(sc):
    return [] if sc is None else list(sc.out_shape)


def _sc_arrays(sc):
    return [] if sc is None else list(sc.arrays)


def _with_sidecar(sc, n_in, n_out, n_scratch, body, first, last):
    if sc is None:
        return body
    a, o = len(sc.arrays), len(sc.out_shape)

    def wrapped(*refs):
        ins, rest = refs[:n_in], refs[n_in:]
        sc_in, rest = rest[:a], rest[a:]
        outs, rest = rest[:n_out], rest[n_out:]
        sc_out, rest = rest[:o], rest[o:]
        scratch, sems = rest[:n_scratch], rest[n_scratch:]
        start, finish = sc.steps(sc_in, sc_out, sems)
        pl.when(first())(start)
        body(*ins, *outs, *scratch)
        pl.when(last())(finish)

    return wrapped


def fox_attn_fwd(qkv, cum_col, cum_row, H, name, sidecar=None):
    S = qkv.shape[0]
    Dh = qkv.shape[1] // (3 * H)
    T = _tile(S, FOX_TILE)
    n = S // T
    qi, kj = _causal_pairs(n, by_key=False)
    scale = Dh ** -0.5

    def body(qi_ref, kj_ref, q_ref, k_ref, v_ref, fq_ref, fk_ref, o_ref, lse_ref, m_ref, l_ref, acc_ref):
        p_id = pl.program_id(1)
        i, j = qi_ref[p_id], kj_ref[p_id]

        @pl.when(j == 0)
        def _():
            m_ref[...] = jnp.full_like(m_ref, NEG)
            l_ref[...] = jnp.zeros_like(l_ref)
            acc_ref[...] = jnp.zeros_like(acc_ref)

        def tile(masked):
            s, mask = _fox_scores(q_ref[...], k_ref[...], fq_ref[0], fk_ref[0], T, scale, False)
            if masked:
                s = jnp.where(mask, s, NEG)
            m_new = jnp.maximum(m_ref[...], jnp.max(s, axis=1, keepdims=True))
            alpha = jnp.exp(m_ref[...] - m_new)
            p = jnp.exp(s - m_new)
            l_ref[...] = alpha * l_ref[...] + jnp.sum(p, axis=1, keepdims=True)
            acc_ref[...] = alpha * acc_ref[...] + jnp.dot(p.astype(BF16), v_ref[...], preferred_element_type=F32)
            m_ref[...] = m_new

        _on_and_below_diagonal(i, j, tile)

        @pl.when(j == i)
        def _():
            o_ref[...] = (acc_ref[...] / l_ref[...]).astype(o_ref.dtype)
            lse_ref[0] = m_ref[...] + jnp.log(l_ref[...])

    grid_spec = pltpu.PrefetchScalarGridSpec(
        num_scalar_prefetch=2, grid=(H, len(qi)),
        in_specs=[
            pl.BlockSpec((T, Dh), lambda h, p, qi, kj: (qi[p], h)),
            pl.BlockSpec((T, Dh), lambda h, p, qi, kj: (kj[p], H + h)),
            pl.BlockSpec((T, Dh), lambda h, p, qi, kj: (kj[p], 2 * H + h)),
            pl.BlockSpec((1, T, 1), lambda h, p, qi, kj: (h, qi[p], 0)),
            pl.BlockSpec((1, 1, T), lambda h, p, qi, kj: (h, 0, kj[p])),
        ] + _sc_specs(sidecar, False),
        out_specs=[
            pl.BlockSpec((T, Dh), lambda h, p, qi, kj: (qi[p], h)),
            pl.BlockSpec((1, T, 1), lambda h, p, qi, kj: (h, qi[p], 0)),
        ] + _sc_specs(sidecar, True),
        scratch_shapes=[pltpu.VMEM((T, 1), F32), pltpu.VMEM((T, 1), F32), pltpu.VMEM((T, Dh), F32)] + _sc_sems(sidecar),
    )
    first = lambda: (pl.program_id(0) == 0) & (pl.program_id(1) == 0)
    last = lambda: (pl.program_id(0) == H - 1) & (pl.program_id(1) == len(qi) - 1)
    res = pl.pallas_call(
        _with_sidecar(sidecar, 7, 2, 3, body, first, last), name=name, grid_spec=grid_spec,
        out_shape=[jax.ShapeDtypeStruct((S, H * Dh), F32), jax.ShapeDtypeStruct((H, S, 1), F32)] + _sc_out(sidecar),
        compiler_params=_cp(("arbitrary", "arbitrary")),
    )(jnp.asarray(qi), jnp.asarray(kj), qkv, qkv, qkv, cum_col, cum_row, *_sc_arrays(sidecar))
    return res[0], res[1], res[2:]


def fox_attn_bwd_dq(qkv, do, o, lse, cum_col, cum_row, H, name, sidecar=None):
    S = qkv.shape[0]
    Dh = qkv.shape[1] // (3 * H)
    T = _tile(S, FOX_TILE)
    n = S // T
    qi, kj = _causal_pairs(n, by_key=False)
    scale = Dh ** -0.5

    def body(qi_ref, kj_ref, q_ref, k_ref, v_ref, do_ref, o_ref, lse_ref, fq_ref, fk_ref, dq_ref, delta_ref, acc_ref, dl_ref,
             rs_ref):
        p_id = pl.program_id(1)
        i, j = qi_ref[p_id], kj_ref[p_id]

        @pl.when(j == 0)
        def _():
            acc_ref[...] = jnp.zeros_like(acc_ref)
            rs_ref[...] = jnp.zeros_like(rs_ref)
            dl_ref[...] = jnp.sum(do_ref[...].astype(F32) * o_ref[...].astype(F32), axis=1, keepdims=True)

        def tile(masked):
            s, mask = _fox_scores(q_ref[...], k_ref[...], fq_ref[0], fk_ref[0], T, scale, False)
            p = jnp.exp(s - lse_ref[0])
            if masked:
                p = jnp.where(mask, p, 0.0)
            dp = lax.dot_general(do_ref[...], v_ref[...], _DIMS["nt"], preferred_element_type=F32)
            ds = p * (dp - dl_ref[...])
            rs_ref[...] += jnp.sum(ds, axis=1, keepdims=True)
            acc_ref[...] += jnp.dot(ds.astype(BF16), k_ref[...], preferred_element_type=F32)

        _on_and_below_diagonal(i, j, tile)

        @pl.when(j == i)
        def _():
            dq_ref[...] = (acc_ref[...] * scale).astype(dq_ref.dtype)
            delta_ref[0] = dl_ref[...] + rs_ref[...]

    qspec = pl.BlockSpec((T, Dh), lambda h, p, qi, kj: (qi[p], h))
    colspec = pl.BlockSpec((1, T, 1), lambda h, p, qi, kj: (h, qi[p], 0))
    grid_spec = pltpu.PrefetchScalarGridSpec(
        num_scalar_prefetch=2, grid=(H, len(qi)),
        in_specs=[
            qspec,
            pl.BlockSpec((T, Dh), lambda h, p, qi, kj: (kj[p], H + h)),
            pl.BlockSpec((T, Dh), lambda h, p, qi, kj: (kj[p], 2 * H + h)),
            qspec, qspec, colspec, colspec,
            pl.BlockSpec((1, 1, T), lambda h, p, qi, kj: (h, 0, kj[p])),
        ] + _sc_specs(sidecar, False),
        out_specs=[qspec, colspec] + _sc_specs(sidecar, True),
        scratch_shapes=[pltpu.VMEM((T, Dh), F32), pltpu.VMEM((T, 1), F32), pltpu.VMEM((T, 1), F32)] + _sc_sems(sidecar),
    )
    first = lambda: (pl.program_id(0) == 0) & (pl.program_id(1) == 0)
    last = lambda: (pl.program_id(0) == H - 1) & (pl.program_id(1) == len(qi) - 1)
    res = pl.pallas_call(
        _with_sidecar(sidecar, 10, 2, 3, body, first, last), name=name, grid_spec=grid_spec,
        out_shape=[jax.ShapeDtypeStruct((S, H * Dh), BF16), jax.ShapeDtypeStruct((H, S, 1), F32)] + _sc_out(sidecar),
        compiler_params=_cp(("arbitrary", "arbitrary")),
    )(jnp.asarray(qi), jnp.asarray(kj), qkv, qkv, qkv, do, o, lse, cum_col, cum_row, *_sc_arrays(sidecar))
    return res[0], res[1], res[2:]


def fox_attn_bwd_dkv(qkv, do, lse_row, delta_row, cum_col, cum_row, H, name, sidecar=None):
    S = qkv.shape[0]
    Dh = qkv.shape[1] // (3 * H)
    T = _tile(S, FOX_TILE)
    n = S // T
    qi, kj = _causal_pairs(n, by_key=True)
    scale = Dh ** -0.5

    def body(qi_ref, kj_ref, q_ref, k_ref, v_ref, do_ref, lse_ref, dl_ref, fq_ref, fk_ref, dk_ref, dv_ref, dcum_ref,
             dk_acc, dv_acc, df_acc):
        p_id = pl.program_id(1)
        i, j = qi_ref[p_id], kj_ref[p_id]

        @pl.when(i == j)
        def _():
            dk_acc[...] = jnp.zeros_like(dk_acc)
            dv_acc[...] = jnp.zeros_like(dv_acc)
            df_acc[...] = jnp.zeros_like(df_acc)

        def tile(masked):
            sT, mask = _fox_scores(q_ref[...], k_ref[...], fq_ref[0], fk_ref[0], T, scale, True)
            pT = jnp.exp(sT - lse_ref[0])
            if masked:
                pT = jnp.where(mask, pT, 0.0)
            dv_acc[...] += jnp.dot(pT.astype(BF16), do_ref[...], preferred_element_type=F32)
            dpT = lax.dot_general(v_ref[...], do_ref[...], _DIMS["nt"], preferred_element_type=F32)
            dsT = pT * (dpT - dl_ref[0])
            dk_acc[...] += jnp.dot(dsT.astype(BF16), q_ref[...], preferred_element_type=F32)
            df_acc[...] -= jnp.sum(dsT, axis=1, keepdims=True)

        _on_and_below_diagonal(i, j, tile)

        @pl.when(i == n - 1)
        def _():
            dk_ref[...] = (dk_acc[...] * scale).astype(dk_ref.dtype)
            dv_ref[...] = dv_acc[...].astype(dv_ref.dtype)
            dcum_ref[0] = df_acc[...]

    qspec = pl.BlockSpec((T, Dh), lambda h, p, qi, kj: (qi[p], h))
    kspec = pl.BlockSpec((T, Dh), lambda h, p, qi, kj: (kj[p], H + h))
    vspec = pl.BlockSpec((T, Dh), lambda h, p, qi, kj: (kj[p], 2 * H + h))
    qrow = pl.BlockSpec((1, 1, T), lambda h, p, qi, kj: (h, 0, qi[p]))
    kcol = pl.BlockSpec((1, T, 1), lambda h, p, qi, kj: (h, kj[p], 0))
    grid_spec = pltpu.PrefetchScalarGridSpec(
        num_scalar_prefetch=2, grid=(H, len(qi)),
        in_specs=[qspec, kspec, vspec, qspec, qrow, qrow, qrow, kcol] + _sc_specs(sidecar, False),
        out_specs=[pl.BlockSpec((T, Dh), lambda h, p, qi, kj: (kj[p], h))] * 2 + [kcol] + _sc_specs(sidecar, True),
        scratch_shapes=[pltpu.VMEM((T, Dh), F32), pltpu.VMEM((T, Dh), F32), pltpu.VMEM((T, 1), F32)] + _sc_sems(sidecar),
    )
    out = jax.ShapeDtypeStruct((S, H * Dh), BF16)
    first = lambda: (pl.program_id(0) == 0) & (pl.program_id(1) == 0)
    last = lambda: (pl.program_id(0) == H - 1) & (pl.program_id(1) == len(qi) - 1)
    res = pl.pallas_call(
        _with_sidecar(sidecar, 10, 3, 3, body, first, last), name=name, grid_spec=grid_spec,
        out_shape=[out, out, jax.ShapeDtypeStruct((H, S, 1), F32)] + _sc_out(sidecar),
        compiler_params=_cp(("arbitrary", "arbitrary")),
    )(jnp.asarray(qi), jnp.asarray(kj), qkv, qkv, qkv, do, lse_row, delta_row, cum_row, cum_col, *_sc_arrays(sidecar))
    return res[0], res[1], res[2], res[3:]


def _sgu_ln(zu, zv, ln_g, ln_b):
    u = jax.nn.gelu(zu)
    v = jax.nn.gelu(zv)
    mu = jnp.mean(v, axis=-1, keepdims=True)
    var = jnp.mean(jnp.square(v - mu), axis=-1, keepdims=True)
    return u, (v - mu) * lax.rsqrt(var + EPS) * ln_g + ln_b


def _tril_mask():
    r = lax.broadcasted_iota(jnp.int32, (SEQ_BLOCK, SEQ_BLOCK), 0)
    c = lax.broadcasted_iota(jnp.int32, (SEQ_BLOCK, SEQ_BLOCK), 1)
    return r >= c


def _sgu_spatial(ws_ref, bsT, selT, vn, G):
    tril = _tril_mask()
    fs = []
    for g in range(G):
        wg = jnp.where(tril, ws_ref[g], 0.0).astype(BF16)
        fs.append(jnp.dot(wg, vn[:, g * SEQ_BLOCK:(g + 1) * SEQ_BLOCK].astype(BF16), preferred_element_type=F32))
    bias = jnp.dot(bsT, selT, precision=lax.Precision.HIGHEST, preferred_element_type=F32)
    return jnp.concatenate(fs, axis=1) + bias


def _sgu_specs(W, G):
    return [
        pl.BlockSpec((SEQ_BLOCK, 2 * W), lambda n: (n, 0)),
        pl.BlockSpec((1, W), lambda n: (0, 0)),
        pl.BlockSpec((1, W), lambda n: (0, 0)),
        pl.BlockSpec((G, SEQ_BLOCK, SEQ_BLOCK), lambda n: (0, 0, 0)),
        pl.BlockSpec((SEQ_BLOCK, G), lambda n: (0, 0)),
        pl.BlockSpec((G, W), lambda n: (0, 0)),
    ]


def sgu_fwd(zp, ln_g, ln_b, ws, bsT, selT, name):
    S, W2 = zp.shape
    W = W2 // 2
    G = ws.shape[0]

    def body(z_ref, lg_ref, lb_ref, ws_ref, bs_ref, sel_ref, o_ref):
        u, vn = _sgu_ln(z_ref[:, :W], z_ref[:, W:], lg_ref[...], lb_ref[...])
        o_ref[...] = (u * _sgu_spatial(ws_ref, bs_ref[...], sel_ref[...], vn, G)).astype(o_ref.dtype)

    return pl.pallas_call(
        body, name=name, grid=(S // SEQ_BLOCK,), in_specs=_sgu_specs(W, G), out_specs=pl.BlockSpec((SEQ_BLOCK, W), lambda n: (n, 0)),
        out_shape=jax.ShapeDtypeStruct((S, W), BF16), compiler_params=_cp(("parallel",)),
    )(zp, ln_g, ln_b, ws, bsT, selT)


def sgu_bwd(zp, ln_g, ln_b, ws, bsT, selT, dgated, name):
    S, W2 = zp.shape
    W = W2 // 2
    G = ws.shape[0]

    def body(z_ref, lg_ref, lb_ref, ws_ref, bs_ref, sel_ref, dgt_ref, dz_ref, dlg_ref, dlb_ref, dws_ref, dbs_ref):
        (u, vn), vjp = jax.vjp(_sgu_ln, z_ref[:, :W], z_ref[:, W:], lg_ref[...], lb_ref[...])
        f = _sgu_spatial(ws_ref, bs_ref[...], sel_ref[...], vn, G)
        dgt = dgt_ref[...].astype(F32)
        du, df = dgt * f, dgt * u

        @pl.when(pl.program_id(0) == 0)
        def _():
            dlg_ref[...] = jnp.zeros_like(dlg_ref)
            dlb_ref[...] = jnp.zeros_like(dlb_ref)
            dws_ref[...] = jnp.zeros_like(dws_ref)
            dbs_ref[...] = jnp.zeros_like(dbs_ref)

        dbs_ref[...] += lax.dot_general(df, sel_ref[...], _DIMS["nt"], precision=lax.Precision.HIGHEST, preferred_element_type=F32)
        tril = _tril_mask()
        dvn = []
        for g in range(G):
            sl = slice(g * SEQ_BLOCK, (g + 1) * SEQ_BLOCK)
            wg = jnp.where(tril, ws_ref[g], 0.0).astype(BF16)
            df_g = df[:, sl].astype(BF16)
            dw = lax.dot_general(df_g, vn[:, sl].astype(BF16), _DIMS["nt"], preferred_element_type=F32)
            dws_ref[g] += jnp.where(tril, dw, 0.0)
            dvn.append(lax.dot_general(wg, df_g, _DIMS["tn"], preferred_element_type=F32))
        dzu, dzv, dlg, dlb = vjp((du, jnp.concatenate(dvn, axis=1)))
        dz_ref[:, :W] = dzu.astype(dz_ref.dtype)
        dz_ref[:, W:] = dzv.astype(dz_ref.dtype)
        dlg_ref[...] += dlg
        dlb_ref[...] += dlb

    vec = jax.ShapeDtypeStruct((1, W), F32)
    return pl.pallas_call(
        body, name=name, grid=(S // SEQ_BLOCK,),
        in_specs=_sgu_specs(W, G) + [pl.BlockSpec((SEQ_BLOCK, W), lambda n: (n, 0))],
        out_specs=[
            pl.BlockSpec((SEQ_BLOCK, 2 * W), lambda n: (n, 0)),
            pl.BlockSpec((1, W), lambda n: (0, 0)),
            pl.BlockSpec((1, W), lambda n: (0, 0)),
            pl.BlockSpec((G, SEQ_BLOCK, SEQ_BLOCK), lambda n: (0, 0, 0)),
            pl.BlockSpec((SEQ_BLOCK, G), lambda n: (0, 0)),
        ],
        out_shape=[jax.ShapeDtypeStruct((S, W2), BF16), vec, vec, jax.ShapeDtypeStruct(ws.shape, F32), jax.ShapeDtypeStruct((SEQ_BLOCK, G), F32)],
        compiler_params=_cp(("arbitrary",)),
    )(zp, ln_g, ln_b, ws, bsT, selT, dgated)


def _rope_matrix():
    half = ROPE_DIM // 2
    R = np.zeros((SWA_HEAD_DIM, SWA_HEAD_DIM), np.float32)
    for j in range(half):
        R[j + half, j] = -1.0
        R[j, j + half] = 1.0
    return R


def _swa_mask_bias(G):
    B = SEQ_BLOCK
    qi = np.arange(G * B)[:, None] % B
    ki = np.arange(2 * B)[None, :] - B
    rel = qi - ki
    valid = (rel >= 0) & (rel < B)
    return np.where(np.stack([valid & (ki >= 0), valid]), 0.0, NEG).astype(np.float32)


def _rot3(t, r_bf16):
    hi = t.astype(BF16)
    rest = t - hi.astype(F32)
    mid = rest.astype(BF16)
    lo = (rest - mid.astype(F32)).astype(BF16)
    d = lambda piece: jnp.dot(piece, r_bf16, preferred_element_type=F32)
    return (d(hi) + d(mid)) + d(lo)


@jax.custom_vjp
def _rope_rot(t, r_bf16):
    return _rot3(t, r_bf16)


def _rope_rot_fwd(t, r_bf16):
    return _rot3(t, r_bf16), r_bf16


def _rope_rot_bwd(r_bf16, ct):
    return -_rot3(ct, r_bf16), jnp.zeros_like(r_bf16)


_rope_rot.defvjp(_rope_rot_fwd, _rope_rot_bwd)


def _swa_block(q4, kp, kc, vp, vc, sink, Cq, Sq, Cp, Sp, R, bias, G):
    B, Dh = SEQ_BLOCK, SWA_HEAD_DIM
    r_bf16 = R.astype(BF16)
    rot = lambda t: _rope_rot(t, r_bf16)
    q = q4.reshape(G * B, Dh)
    Cq4 = jnp.concatenate([Cq] * G, axis=0)
    Sq4 = jnp.concatenate([Sq] * G, axis=0)
    qr = q * Cq4 + rot(q) * Sq4
    kb = jnp.concatenate([kp * Cp + rot(kp) * Sp, kc * Cq + rot(kc) * Sq], axis=0)
    vb = jnp.concatenate([vp, vc], axis=0)
    s = lax.dot_general(qr.astype(BF16), kb.astype(BF16), _DIMS["nt"], preferred_element_type=F32) * (Dh ** -0.5) + bias
    m = lax.stop_gradient(jnp.maximum(jnp.max(s, axis=1, keepdims=True), sink))
    p = jnp.exp(s - m)
    p = p / (jnp.sum(p, axis=1, keepdims=True) + jnp.exp(sink - m))
    o = jnp.dot(p.astype(BF16), vb.astype(BF16), preferred_element_type=F32)
    return o.reshape(G, B, Dh)


SWA_HEADS_PER_STEP = 2


def _swa_specs(G, HP):
    B, Dh = SEQ_BLOCK, SWA_HEAD_DIM
    prev = lambda n: jnp.maximum(n - 1, 0)
    return [
        pl.BlockSpec((HP * G, B, Dh), lambda h, n: (h, n, 0)),
        pl.BlockSpec((HP, B, Dh), lambda h, n: (h, prev(n), 0)),
        pl.BlockSpec((HP, B, Dh), lambda h, n: (h, n, 0)),
        pl.BlockSpec((HP, B, Dh), lambda h, n: (h, prev(n), 0)),
        pl.BlockSpec((HP, B, Dh), lambda h, n: (h, n, 0)),
        pl.BlockSpec((HP, G * B, 1), lambda h, n: (h, 0, 0)),
        pl.BlockSpec((B, Dh), lambda h, n: (n, 0)),
        pl.BlockSpec((B, Dh), lambda h, n: (n, 0)),
        pl.BlockSpec((B, Dh), lambda h, n: (prev(n), 0)),
        pl.BlockSpec((B, Dh), lambda h, n: (prev(n), 0)),
        pl.BlockSpec((Dh, Dh), lambda h, n: (0, 0)),
        pl.BlockSpec((1, G * B, 2 * B), lambda h, n: (jnp.minimum(n, 1), 0, 0)),
    ]


def swa_fwd(qh, kh, vh, sink_col, C, Sn, R, bias, name):
    Hq, S, Dh = qh.shape
    Hk = kh.shape[0]
    G = Hq // Hk
    HP = _tile(Hk, SWA_HEADS_PER_STEP, 1)

    def body(q_ref, kp_ref, kc_ref, vp_ref, vc_ref, sk_ref, cq_ref, sq_ref, cp_ref, sp_ref, r_ref, b_ref, o_ref):
        for hp in range(HP):
            qs = slice(hp * G, (hp + 1) * G)
            o = _swa_block(q_ref[qs], kp_ref[hp], kc_ref[hp], vp_ref[hp], vc_ref[hp], sk_ref[hp], cq_ref[...], sq_ref[...],
                           cp_ref[...], sp_ref[...], r_ref[...], b_ref[0], G)
            o_ref[qs] = o.astype(o_ref.dtype)

    return pl.pallas_call(
        body, name=name, grid=(Hk // HP, S // SEQ_BLOCK), in_specs=_swa_specs(G, HP),
        out_specs=pl.BlockSpec((HP * G, SEQ_BLOCK, Dh), lambda h, n: (h, n, 0)),
        out_shape=jax.ShapeDtypeStruct((Hq, S, Dh), BF16), compiler_params=_cp(("parallel", "parallel")),
    )(qh, kh, kh, vh, vh, sink_col, C, Sn, C, Sn, R, bias)


def swa_bwd(qh, kh, vh, sink_col, C, Sn, R, bias, doh, name):
    Hq, S, Dh = qh.shape
    Hk = kh.shape[0]
    G = Hq // Hk
    B = SEQ_BLOCK
    HP = _tile(Hk, SWA_HEADS_PER_STEP, 1)

    def body(q_ref, kp_ref, kc_ref, vp_ref, vc_ref, sk_ref, cq_ref, sq_ref, cp_ref, sp_ref, r_ref, b_ref, do_ref,
             dq_ref, dkp_ref, dkc_ref, dvp_ref, dvc_ref, dsk_ref):
        @pl.when(pl.program_id(1) == 0)
        def _():
            dsk_ref[...] = jnp.zeros_like(dsk_ref)

        fn = lambda q4, kp, kc, vp, vc, sk: _swa_block(q4, kp, kc, vp, vc, sk, cq_ref[...], sq_ref[...], cp_ref[...], sp_ref[...],
                                                      r_ref[...], b_ref[0], G)
        for hp in range(HP):
            qs = slice(hp * G, (hp + 1) * G)
            _, vjp = jax.vjp(fn, q_ref[qs], kp_ref[hp], kc_ref[hp], vp_ref[hp], vc_ref[hp], sk_ref[hp])
            dq, dkp, dkc, dvp, dvc, dsk = vjp(do_ref[qs].astype(F32))
            dq_ref[qs] = dq
            dkp_ref[hp] = dkp
            dkc_ref[hp] = dkc
            dvp_ref[hp] = dvp
            dvc_ref[hp] = dvc
            for g in range(G):
                part = jnp.sum(dsk[g * B:(g + 1) * B], axis=0, keepdims=True)
                dsk_ref[hp, g:g + 1, :] += jnp.broadcast_to(part, (1, LANES))

    qspec = pl.BlockSpec((HP * G, B, Dh), lambda h, n: (h, n, 0))
    kspec = pl.BlockSpec((HP, B, Dh), lambda h, n: (h, n, 0))
    kshape = jax.ShapeDtypeStruct((Hk, S, Dh), F32)
    return pl.pallas_call(
        body, name=name, grid=(Hk // HP, S // B), in_specs=_swa_specs(G, HP) + [qspec],
        out_specs=[qspec, kspec, kspec, kspec, kspec, pl.BlockSpec((HP, G, LANES), lambda h, n: (h, 0, 0))],
        out_shape=[jax.ShapeDtypeStruct((Hq, S, Dh), F32), kshape, kshape, kshape, kshape, jax.ShapeDtypeStruct((Hk, G, LANES), F32)],
        compiler_params=_cp(("parallel", "arbitrary")),
    )(qh, kh, kh, vh, vh, sink_col, C, Sn, C, Sn, R, bias, doh)


def shift_add(cur, prev, name):
    Hk, S, Dh = cur.shape
    B = SEQ_BLOCK
    nb = S // B

    def body(c_ref, p_ref, o_ref):
        last = pl.program_id(1) == nb - 1
        o_ref[...] = c_ref[...] + jnp.where(last, 0.0, p_ref[...])

    spec = pl.BlockSpec((1, B, Dh), lambda h, n: (h, n, 0))
    nxt = pl.BlockSpec((1, B, Dh), lambda h, n: (h, jnp.minimum(n + 1, nb - 1), 0))
    return pl.pallas_call(
        body, name=name, grid=(Hk, nb), in_specs=[spec, nxt], out_specs=spec, out_shape=jax.ShapeDtypeStruct(cur.shape, F32),
        compiler_params=_cp(("parallel", "parallel")),
    )(cur, prev)


def loss_head(y, target, name):
    S, D = y.shape
    tr = _tile(S, ROW_TILE, 16)

    def body(y_ref, t_ref, acc_ref, dy_ref):
        err = y_ref[...] - t_ref[...]
        dy_ref[...] = err * (1.0 / D)

        @pl.when(pl.program_id(0) == 0)
        def _():
            acc_ref[...] = jnp.zeros_like(acc_ref)

        acc_ref[...] += jnp.broadcast_to(jnp.sum(err * err).reshape(1, 1), (1, LANES))

    return pl.pallas_call(
        body, name=name, grid=(S // tr,), in_specs=[_row_spec(tr, D)] * 2,
        out_specs=[pl.BlockSpec((1, LANES), lambda i: (0, 0)), _row_spec(tr, D)],
        out_shape=[jax.ShapeDtypeStruct((1, LANES), F32), jax.ShapeDtypeStruct((S, D), F32)], compiler_params=_cp(("arbitrary",)),
    )(y, target)


def _adam_update(w, g, m, v):
    m = ADAM_B1 * m + (1.0 - ADAM_B1) * g
    v = ADAM_B2 * v + (1.0 - ADAM_B2) * jnp.square(g)
    m_hat = m / (1.0 - ADAM_B1 ** ADAM_STEP)
    v_hat = v / (1.0 - ADAM_B2 ** ADAM_STEP)
    delta = -ADAM_LR * (m_hat / (jnp.sqrt(v_hat) + ADAM_EPS) + ADAM_WD * w)
    return delta, m, v


def adamw(w, m, v, gparts, name, gstack=0, emit_g=True):
    R, C = w.shape
    tr = _tile(R, max(8, (128 * 1024) // C), 8)
    spec = pl.BlockSpec((tr, C), lambda i: (i, 0))
    nplain = len(gparts) - (1 if gstack else 0)
    nout = 4 if emit_g else 3

    def body(w_ref, m_ref, v_ref, *rest):
        g_refs, outs = rest[:len(gparts)], rest[len(gparts):]
        g = None
        for r in g_refs[:nplain]:
            g = r[...].astype(F32) if g is None else g + r[...].astype(F32)
        if gstack:
            for t in range(gstack):
                part = g_refs[-1][t].astype(F32)
                g = part if g is None else g + part
        res = _adam_update(w_ref[...], g, m_ref[...], v_ref[...])
        for o_ref, val in zip(outs, ((g,) if emit_g else ()) + res):
            o_ref[...] = val

    gspecs = [spec] * nplain + ([pl.BlockSpec((gstack, tr, C), lambda i: (0, i, 0))] if gstack else [])
    out = jax.ShapeDtypeStruct((R, C), F32)
    return pl.pallas_call(
        body, name=name, grid=(R // tr,), in_specs=[spec] * 3 + gspecs, out_specs=[spec] * nout, out_shape=[out] * nout,
        compiler_params=_cp(("parallel",)),
    )(w, m, v, *gparts)


def ada_fwd(c_all, ada_w, ada_b, name):
    L, D, N = ada_w.shape
    Bp = c_all.shape[0]
    tn = _tile(N, 512)

    def body(c_ref, w_ref, b_ref, o_ref):
        ca = jax.nn.silu(c_ref[...]).astype(BF16)
        o_ref[0] = jnp.dot(ca, w_ref[0].astype(BF16), preferred_element_type=F32) + b_ref[0]

    return pl.pallas_call(
        body, name=name, grid=(L, N // tn),
        in_specs=[pl.BlockSpec((Bp, D), lambda l, j: (0, 0)), pl.BlockSpec((1, D, tn), lambda l, j: (l, 0, j)),
                  pl.BlockSpec((1, 1, tn), lambda l, j: (l, 0, j))],
        out_specs=pl.BlockSpec((1, Bp, tn), lambda l, j: (l, 0, j)), out_shape=jax.ShapeDtypeStruct((L, Bp, N), F32),
        compiler_params=_cp(("parallel", "parallel")),
    )(c_all, ada_w, ada_b)


def ada_wgrad(c_all, dmod, name):
    L, Bp, N = dmod.shape
    D = c_all.shape[1]
    tn = _tile(N, 512)

    def body(c_ref, d_ref, o_ref):
        ca = jax.nn.silu(c_ref[...]).astype(BF16)
        o_ref[0] = lax.dot_general(ca, d_ref[0].astype(BF16), _DIMS["tn"], preferred_element_type=F32)

    return pl.pallas_call(
        body, name=name, grid=(L, N // tn),
        in_specs=[pl.BlockSpec((Bp, D), lambda l, j: (0, 0)), pl.BlockSpec((1, Bp, tn), lambda l, j: (l, 0, j))],
        out_specs=pl.BlockSpec((1, D, tn), lambda l, j: (l, 0, j)), out_shape=jax.ShapeDtypeStruct((L, D, N), F32),
        compiler_params=_cp(("parallel", "parallel")),
    )(c_all, dmod)


N_DEV = 8
N_CHIP = 4
ANY = pl.BlockSpec(memory_space=pl.ANY)


def _place():
    return lax.axis_index("x"), lax.axis_index("y"), lax.axis_index("c")


def _other_chips(x, y):
    chips = [(1 - x, y), (x, 1 - y), (1 - x, 1 - y)]
    return chips, [2 * cx + cy for cx, cy in chips]


def _rcopy(src, dst, ssem, rsem, to):
    return pltpu.make_async_remote_copy(src_ref=src, dst_ref=dst, send_sem=ssem, recv_sem=rsem, device_id=to, device_id_type=MESH)


def ag_small(xs, name):
    R, Wd = xs.shape

    def body(x_ref, out_ref, send_sems, recv_sems, local_sem):
        x, y, c = _place()
        me, sibling = (x, y, c), (x, y, 1 - c)
        chips, _ = _other_chips(x, y)

        def slot(px, py, pc):
            return out_ref.at[4 * px + 2 * py + pc]

        def copy(k, block, to, src=None):
            return _rcopy(slot(*block) if src is None else src, slot(*block), send_sems.at[k], recv_sems.at[k], to)

        mine = pltpu.make_async_copy(x_ref, slot(*me), local_sem)
        mine.start()
        first = [copy(0, me, sibling, src=x_ref)]
        first += [copy(1 + j, me, (*chip, c), src=x_ref) for j, chip in enumerate(chips)]
        for cp in first:
            cp.start()
        passed = [copy(4 + j, (*chip, c), sibling) for j, chip in enumerate(chips)]
        for j, chip in enumerate(chips):
            copy(1 + j, (*chip, c), me).wait_recv()
            passed[j].start()
        copy(0, sibling, me).wait_recv()
        for j, chip in enumerate(chips):
            copy(4 + j, (*chip, 1 - c), me).wait_recv()
        for cp in first + passed:
            cp.wait_send()
        mine.wait()

    vm = pl.BlockSpec(memory_space=pltpu.VMEM)
    return pl.pallas_call(
        body, name=name, out_shape=jax.ShapeDtypeStruct((N_DEV, R, Wd), xs.dtype), in_specs=[vm], out_specs=vm,
        scratch_shapes=[pltpu.SemaphoreType.DMA((7,)), pltpu.SemaphoreType.DMA((7,)), pltpu.SemaphoreType.DMA],
        compiler_params=_cp(),
    )(xs)


def _half_of_shard(by_cols, A, B, h):
    return (h * (A // 2), A // 2, 0, B) if by_cols else (0, A, h * (B // 2), B // 2)


def _shard_in_full(by_cols, A, B, q):
    return (0, q * B) if by_cols else (q * A, 0)


def _window(ref, r0, nr, c0, nc):
    return ref.at[:, pl.ds(r0, nr), pl.ds(c0, nc)]


def ag_weights(shards, by_cols, name):
    n = len(shards)
    geo, full = _ag_shapes(shards, by_cols)

    def body(*refs):
        start, finish = _ag_steps(geo, refs[:n], refs[n:2 * n], refs[2 * n:])
        start()
        finish()

    return pl.pallas_call(
        body, name=name, out_shape=full, in_specs=[ANY] * n, out_specs=[ANY] * n, scratch_shapes=_ag_semaphores(n),
        compiler_params=_cp(),
    )(*shards)


def _ag_shapes(shards, by_cols):
    geo = [(bc,) + s.shape[1:] for bc, s in zip(by_cols, shards)]
    full = [jax.ShapeDtypeStruct((s.shape[0], A, N_CHIP * B) if bc else (s.shape[0], N_CHIP * A, B), s.dtype)
            for (bc, A, B), s in zip(geo, shards)]
    return geo, full


def _ag_semaphores(n):
    return [pltpu.SemaphoreType.DMA((n, 3)) for _ in range(4)]


def _ag_steps(geo, x_refs, o_refs, sems):
    s_ici, r_ici, s_d2d, r_d2d = sems
    pairs = [(t, j) for t in range(len(geo)) for j in range(3)]

    def copies():
        x, y, c = _place()
        q = 2 * x + y
        chips, qs = _other_chips(x, y)

        def landing(t, chip_q, half):
            r0, nr, c0, nc = _half_of_shard(*geo[t], half)
            ro, co = _shard_in_full(*geo[t], chip_q)
            return _window(o_refs[t], ro + r0, nr, co + c0, nc)

        def ici(t, j, landing_q):
            src = _window(x_refs[t], *_half_of_shard(*geo[t], c))
            return _rcopy(src, landing(t, landing_q, c), s_ici.at[t, j], r_ici.at[t, j], (*chips[j], c))

        def handoff(t, j, half):
            blk = landing(t, qs[j], half)
            return _rcopy(blk, blk, s_d2d.at[t, j], r_d2d.at[t, j], (x, y, 1 - c))

        return c, q, qs, ici, handoff

    def start():
        c, q, qs, ici, handoff = copies()
        for t, j in pairs:
            ici(t, j, q).start()

    def finish():
        c, q, qs, ici, handoff = copies()
        for t, j in pairs:
            ici(t, j, qs[j]).wait_recv()
            handoff(t, j, c).start()
        for t, j in pairs:
            handoff(t, j, 1 - c).wait_recv()
        for t, j in pairs:
            ici(t, j, q).wait_send()
            handoff(t, j, c).wait_send()

    return start, finish


def _half_of_full(by_cols, A, B, h):
    return (h * (A // 2), A // 2, 0, N_CHIP * B) if by_cols else (0, N_CHIP * A, h * (B // 2), B // 2)


def _half_shape(by_cols, L, A, B):
    return (L, A // 2, N_CHIP * B) if by_cols else (L, N_CHIP * A, B // 2)


def _piece_shape(by_cols, L, A, B):
    return (L, A // 2, B) if by_cols else (L, A, B // 2)


def sibling_fold(gs, geo, name):
    sc = fold_sidecar(gs, geo)
    n = len(gs)

    def body(*refs):
        start, finish = sc.steps(refs[:n], refs[n:2 * n], refs[2 * n:])
        start()
        finish()

    return pl.pallas_call(
        body, name=name, out_shape=sc.out_shape, in_specs=[ANY] * n, out_specs=[ANY] * n, scratch_shapes=sc.semaphores,
        compiler_params=_cp(),
    )(*gs)


def fold_sidecar(gs, geo):
    n = len(gs)

    def steps(x_refs, o_refs, sems):
        ssem, rsem = sems

        def copies():
            x, y, c = _place()
            return [_rcopy(_window(x_refs[t], *_half_of_full(*geo[t], 1 - c)), o_refs[t], ssem.at[t], rsem.at[t], (x, y, 1 - c))
                    for t in range(n)]

        def start():
            for cp in copies():
                cp.start()

        def finish():
            for cp in copies():
                cp.wait()

        return start, finish

    dma = pltpu.SemaphoreType.DMA
    out = [jax.ShapeDtypeStruct(_half_shape(bc, g.shape[0], A, B), g.dtype) for (bc, A, B), g in zip(geo, gs)]
    return SideCar(list(gs), out, [dma((n,)), dma((n,))], steps)


def chip_exchange(rs, geo, name):
    sc = exchange_sidecar(rs, geo)
    n = len(rs)

    def body(*refs):
        start, finish = sc.steps(refs[:n], refs[n:2 * n], refs[2 * n:])
        start()
        finish()

    return pl.pallas_call(
        body, name=name, out_shape=sc.out_shape, in_specs=[ANY] * n, out_specs=[ANY] * n, scratch_shapes=sc.semaphores,
        compiler_params=_cp(),
    )(*rs)


def exchange_sidecar(rs, geo):
    n = len(rs)

    def steps(x_refs, o_refs, sems):
        ssem, rsem = sems

        def copies():
            x, y, c = _place()
            chips, qs = _other_chips(x, y)

            def part(t, chip_q):
                bc, A, B = geo[t]
                return _window(x_refs[t], 0, A // 2, chip_q * B, B) if bc else _window(x_refs[t], chip_q * A, A, 0, B // 2)

            return [_rcopy(part(t, qs[j]), o_refs[t].at[j], ssem.at[t, j], rsem.at[t, j], (*chips[j], c))
                    for t in range(n) for j in range(3)]

        def start():
            for cp in copies():
                cp.start()

        def finish():
            for cp in copies():
                cp.wait()

        return start, finish

    dma = pltpu.SemaphoreType.DMA
    out = [jax.ShapeDtypeStruct((3,) + _piece_shape(bc, r.shape[0], A, B), r.dtype) for (bc, A, B), r in zip(geo, rs)]
    return SideCar(list(rs), out, [dma((n, 3)), dma((n, 3))], steps)


def ag_sidecar(shards, by_cols):
    geo, full = _ag_shapes(shards, by_cols)
    return SideCar(list(shards), full, _ag_semaphores(len(shards)), lambda ins, outs, sems: _ag_steps(geo, ins, outs, sems))


def sibling_share(fs, geo, name):
    n = len(fs)

    def body(*refs):
        x_refs, o_refs, (ssem, rsem) = refs[:n], refs[n:2 * n], refs[2 * n:]
        x, y, c = _place()
        for t in range(n):
            mine = _window(o_refs[t], *_half_of_shard(*geo[t], c))
            _rcopy(mine, mine, ssem.at[t], rsem.at[t], (x, y, 1 - c)).start()
        for t in range(n):
            mine = _window(o_refs[t], *_half_of_shard(*geo[t], c))
            theirs = _window(o_refs[t], *_half_of_shard(*geo[t], 1 - c))
            _rcopy(mine, theirs, ssem.at[t], rsem.at[t], (x, y, 1 - c)).wait_recv()
            _rcopy(mine, mine, ssem.at[t], rsem.at[t], (x, y, 1 - c)).wait_send()
        del x_refs

    dma = pltpu.SemaphoreType.DMA
    return pl.pallas_call(
        body, name=name, out_shape=[jax.ShapeDtypeStruct(f.shape, f.dtype) for f in fs], in_specs=[ANY] * n, out_specs=[ANY] * n,
        input_output_aliases={t: t for t in range(n)}, scratch_shapes=[dma((n,)), dma((n,))], compiler_params=_cp(),
    )(*fs)


SUM_ROWS = 256


def fold_sum(g, recv, by_cols, A, B, qc_idx, name):
    L = g.shape[0]
    _, hr, hc = _half_shape(by_cols, L, A, B)
    tr, tc = _tile(A // 2 if by_cols else A, SUM_ROWS, 16), (B if by_cols else B // 2)
    ro, co = ((A // 2) // tr, 0) if by_cols else (0, 1)

    def body(qc_ref, g_ref, r_ref, o_ref):
        del qc_ref
        o_ref[...] = (g_ref[...].astype(F32) + r_ref[...].astype(F32)).astype(o_ref.dtype)

    spec = pl.BlockSpec((1, tr, tc), lambda l, i, j, qc: (l, i, j))
    grid_spec = pltpu.PrefetchScalarGridSpec(
        num_scalar_prefetch=1, grid=(L, hr // tr, hc // tc),
        in_specs=[pl.BlockSpec((1, tr, tc), lambda l, i, j, qc: (l, i + qc[1] * ro, j + qc[1] * co)), spec], out_specs=spec,
    )
    return pl.pallas_call(
        body, name=name, grid_spec=grid_spec, out_shape=jax.ShapeDtypeStruct((L, hr, hc), BF16),
        compiler_params=_cp(("parallel", "parallel", "parallel")),
    )(qc_idx, g, recv)


def chip_sum(r, ex, by_cols, A, B, qc_idx, name):
    L = r.shape[0]
    _, wr, wc = _piece_shape(by_cols, L, A, B)
    tr = _tile(wr, SUM_ROWS, 16)
    r_ro, r_co = (0, 1) if by_cols else (A // tr, 0)
    o_ro, o_co = ((A // 2) // tr, 0) if by_cols else (0, 1)

    def body(qc_ref, r_ref, e_ref, o_ref):
        del qc_ref
        o_ref[0] = ((r_ref[0].astype(F32) + e_ref[0, 0].astype(F32)) + e_ref[1, 0].astype(F32)) + e_ref[2, 0].astype(F32)

    grid_spec = pltpu.PrefetchScalarGridSpec(
        num_scalar_prefetch=1, grid=(L, wr // tr),
        in_specs=[pl.BlockSpec((1, tr, wc), lambda l, i, qc: (l, i + qc[0] * r_ro, qc[0] * r_co)),
                  pl.BlockSpec((3, 1, tr, wc), lambda l, i, qc: (0, l, i, 0))],
        out_specs=pl.BlockSpec((1, tr, wc), lambda l, i, qc: (l, i + qc[1] * o_ro, qc[1] * o_co)),
    )
    return pl.pallas_call(
        body, name=name, grid_spec=grid_spec, out_shape=jax.ShapeDtypeStruct((L, A, B), F32),
        compiler_params=_cp(("parallel", "parallel")),
    )(qc_idx, r, ex)


BIG = ("ffn_w_gu", "ffn_w_down", "fox_w_in", "fox_w_out", "sgu_w_in", "sgu_w_out", "swa_w_in", "swa_w_out")
COLUMN_SHARDED = ("ffn_w_gu", "fox_w_in", "sgu_w_in", "swa_w_in")
SMALL = ("ada_b", "mix_pre_g", "mix_post_g", "ffn_pre_g", "ffn_post_g", "fox_b_f", "sgu_ln_g", "sgu_ln_b", "sgu_w_s", "sgu_b_s",
         "swa_sinks")
WEIGHTS = ("ada_w", "ada_b", "mix_pre_g", "mix_post_g", "ffn_pre_g", "ffn_post_g", "ffn_w_gu", "ffn_w_down", "fox_w_in", "fox_b_f",
           "fox_w_out", "sgu_w_in", "sgu_ln_g", "sgu_ln_b", "sgu_w_s", "sgu_b_s", "sgu_w_out", "swa_w_in", "swa_sinks", "swa_w_out")
INPUTS = ("x", "c", "positions") + WEIGHTS + ("loss_target",) + tuple("m_" + n for n in WEIGHTS) + tuple("v_" + n for n in WEIGHTS)


def _lane_pad(n):
    return (-n) % LANES


def _pad_shard_columns(t, B):
    if _lane_pad(B) == 0:
        return t
    L, A, _ = t.shape
    return jnp.pad(t.reshape(L, A, N_CHIP, B), ((0, 0), (0, 0), (0, 0), (0, _lane_pad(B)))).reshape(L, A, -1)


def _unpad_shard_columns(t, B):
    if _lane_pad(B) == 0:
        return t
    L, A, _ = t.shape
    return t.reshape(L, A, N_CHIP, B + _lane_pad(B))[..., :B].reshape(L, A, N_CHIP * B)


def place_shard(full, shard, by_cols, qc_idx, name):
    L, A, B = shard.shape
    tr = _tile(A, SUM_ROWS, 16)
    ro, co = (0, 1) if by_cols else (A // tr, 0)

    def body(qc_ref, s_ref, f_ref, o_ref):
        del qc_ref, f_ref
        o_ref[...] = s_ref[...]

    grid_spec = pltpu.PrefetchScalarGridSpec(
        num_scalar_prefetch=1, grid=(L, A // tr), in_specs=[pl.BlockSpec((1, tr, B), lambda l, i, qc: (l, i, 0)), ANY],
        out_specs=pl.BlockSpec((1, tr, B), lambda l, i, qc: (l, i + qc[0] * ro, qc[0] * co)),
    )
    return pl.pallas_call(
        body, name=name, grid_spec=grid_spec, out_shape=jax.ShapeDtypeStruct(full.shape, full.dtype),
        input_output_aliases={2: 0}, compiler_params=_cp(("parallel", "parallel")),
    )(qc_idx, shard, full)


def _pad_rows(flat1d):
    n = flat1d.shape[0]
    pad = (-n) % (8 * LANES)
    return jnp.pad(flat1d, (0, pad)).reshape(-1, LANES)


def _pack_small(parts):
    return jnp.concatenate([_pad_rows(parts[n].astype(F32).reshape(-1)) for n in SMALL], axis=0)


def _unpack_small(packed, shapes):
    out, off = {}, 0
    for n in SMALL:
        size = int(np.prod(shapes[n]))
        rows = (size + 8 * LANES - 1) // (8 * LANES) * 8
        out[n] = packed[off:off + rows].reshape(-1)[:size].reshape(shapes[n])
        off += rows
    return out


def _fox_fwd(h, w_in, b_f, w_out, tag, ride_qkv=None, ride_attn=None):
    S, D = h.shape
    H = b_f.shape[0]
    qkv = mm(h, w_in, "nn", BF16, name=tag + "_qkv", b_cols=(0, 3 * D), sidecar=ride_qkv)
    qkv, rode_qkv = qkv if ride_qkv is not None else (qkv, ())
    fgp = mm(h, w_in, "nn", F32, name=tag + "_fg", b_cols=(3 * D, LANES))
    fgT = fgp[:, :H].T
    cum = fox_gate_fwd(fgT, b_f.reshape(H, 1), tag + "_gate")
    cum_col, cum_row = cum.reshape(H, S, 1), cum.reshape(H, 1, S)
    o, lse, rode_attn = fox_attn_fwd(qkv, cum_col, cum_row, H, tag + "_attn", ride_attn)
    y = mm(o, w_out, "nn", F32, name=tag + "_out")
    return y, (qkv, fgT, cum_col, cum_row, o, lse), rode_qkv, rode_attn


def _fox_bwd(dy, h, w_in, b_f, w_out, ctx, tag, ride_dq=None, ride_dkv=None):
    qkv, fgT, cum_col, cum_row, o, lse = ctx
    S, D = h.shape
    H = b_f.shape[0]
    do = mm(dy, w_out, "nt", BF16, name=tag + "_do")
    dw_out = mm(o, dy, "tn", BF16, name=tag + "_dwout")
    dq, delta, rode_dq = fox_attn_bwd_dq(qkv, do, o, lse, cum_col, cum_row, H, tag + "_dq", ride_dq)
    dk, dv, dcum, rode_dkv = fox_attn_bwd_dkv(qkv, do, lse.reshape(H, 1, S), delta.reshape(H, 1, S), cum_col, cum_row, H,
                                              tag + "_dkv", ride_dkv)
    dfgT, db = fox_gate_bwd(dcum.reshape(H, S), fgT, b_f.reshape(H, 1), tag + "_dgate")
    dfgp = jnp.pad(dfgT.T, ((0, 0), (0, LANES - H))).astype(BF16)
    dh = mm(dq, w_in, "nt", F32, name=tag + "_dhq", b_cols=(0, D))
    dh = mm(dk, w_in, "nt", F32, add=dh, name=tag + "_dhk", b_cols=(D, D))
    dh = mm(dv, w_in, "nt", F32, add=dh, name=tag + "_dhv", b_cols=(2 * D, D))
    dh = mm(dfgp, w_in, "nt", F32, add=dh, name=tag + "_dhf", b_cols=(3 * D, LANES))
    dw_in = jnp.concatenate(
        [mm(h, dq, "tn", BF16, name=tag + "_dwq"), mm(h, dk, "tn", BF16, name=tag + "_dwk"), mm(h, dv, "tn", BF16, name=tag + "_dwv"),
         mm(h, dfgp, "tn", BF16, name=tag + "_dwf")[:, :H]], axis=1)
    return dh, dw_in, dw_out, db[:, 0], rode_dq, rode_dkv


def _sgu_consts(G, W):
    return jnp.asarray(np.repeat(np.eye(G, dtype=np.float32), W // G, axis=1))


def _sgu_fwd(h, w_in, ln_g, ln_b, w_s, b_s, w_out, tag):
    G, W = w_s.shape[0], ln_g.shape[0]
    zp = mm(h, w_in, "nn", F32, name=tag + "_in")
    args = (zp, ln_g.reshape(1, W), ln_b.reshape(1, W), w_s, b_s.T, _sgu_consts(G, W))
    gated = sgu_fwd(*args, tag + "_core")
    y = mm(gated, w_out, "nn", F32, name=tag + "_out")
    return y, (args, gated)


def _sgu_bwd(dy, h, w_in, w_out, ctx, tag):
    args, gated = ctx
    dgated = mm(dy, w_out, "nt", BF16, name=tag + "_dgated")
    dw_out = mm(gated, dy, "tn", BF16, name=tag + "_dwout")
    dzp, dlg, dlb, dws, dbsT = sgu_bwd(*args, dgated, tag + "_dcore")
    dh = mm(dzp, w_in, "nt", F32, name=tag + "_dh")
    dw_in = mm(h, dzp, "tn", BF16, name=tag + "_dwin")
    return dh, dw_in, dw_out, dlg[0], dlb[0], dws, dbsT.T


def _rope_tables(positions):
    inv = ROPE_THETA ** (-jnp.arange(0, ROPE_DIM, 2, dtype=F32) / ROPE_DIM)
    ang = positions.astype(F32)[:, None] * inv
    S = positions.shape[0]
    rest = SWA_HEAD_DIM - ROPE_DIM
    C = jnp.concatenate([jnp.cos(ang), jnp.cos(ang), jnp.ones((S, rest), F32)], axis=1)
    Sn = jnp.concatenate([jnp.sin(ang), jnp.sin(ang), jnp.zeros((S, rest), F32)], axis=1)
    return C, Sn


def _heads(t, n):
    return t.reshape(t.shape[0], n, SWA_HEAD_DIM).transpose(1, 0, 2)


def _unheads(t):
    return t.transpose(1, 0, 2).reshape(t.shape[1], -1)


def _swa_fwd(h, w_in, sinks, w_out, tables, tag):
    Hq = sinks.shape[0]
    Hk = (w_in[0].shape[-1] // SWA_HEAD_DIM - Hq) // 2
    G = Hq // Hk
    proj = mm(h, w_in, "nn", F32, name=tag + "_in")
    qh = _heads(proj[:, :Hq * SWA_HEAD_DIM], Hq)
    kh = _heads(proj[:, Hq * SWA_HEAD_DIM:(Hq + Hk) * SWA_HEAD_DIM], Hk)
    vh = _heads(proj[:, (Hq + Hk) * SWA_HEAD_DIM:], Hk)
    sink_col = jnp.repeat(sinks.reshape(Hk, G), SEQ_BLOCK, axis=1).reshape(Hk, G * SEQ_BLOCK, 1)
    args = (qh, kh, vh, sink_col, tables[0], tables[1], jnp.asarray(_rope_matrix()), jnp.asarray(_swa_mask_bias(G)))
    o = _unheads(swa_fwd(*args, tag + "_core"))
    y = mm(o, w_out, "nn", F32, name=tag + "_out")
    return y, (args, o)


def _swa_bwd(dy, h, w_in, w_out, ctx, tag):
    args, o = ctx
    Hq = args[0].shape[0]
    do = mm(dy, w_out, "nt", BF16, name=tag + "_do")
    dw_out = mm(o, dy, "tn", BF16, name=tag + "_dwout")
    dqh, dkp, dkc, dvp, dvc, dsk = swa_bwd(*args, _heads(do, Hq), tag + "_dcore")
    dk = shift_add(dkc, dkp, tag + "_dk")
    dv = shift_add(dvc, dvp, tag + "_dv")
    dproj = jnp.concatenate([_unheads(dqh), _unheads(dk), _unheads(dv)], axis=1).astype(BF16)
    dh = mm(dproj, w_in, "nt", F32, name=tag + "_dh")
    dw_in = mm(h, dproj, "tn", BF16, name=tag + "_dwin")
    return dh, dw_in, dw_out, dsk[:, :, 0].reshape(Hq)


def kernel(x, c, positions, ada_w, ada_b, mix_pre_g, mix_post_g, ffn_pre_g, ffn_post_g, ffn_w_gu, ffn_w_down, fox_w_in, fox_b_f, fox_w_out, sgu_w_in, sgu_ln_g, sgu_ln_b, sgu_w_s, sgu_b_s, sgu_w_out, swa_w_in, swa_sinks, swa_w_out, loss_target, m_ada_w, m_ada_b, m_mix_pre_g, m_mix_post_g, m_ffn_pre_g, m_ffn_post_g, m_ffn_w_gu, m_ffn_w_down, m_fox_w_in, m_fox_b_f, m_fox_w_out, m_sgu_w_in, m_sgu_ln_g, m_sgu_ln_b, m_sgu_w_s, m_sgu_b_s, m_sgu_w_out, m_swa_w_in, m_swa_sinks, m_swa_w_out, v_ada_w, v_ada_b, v_mix_pre_g, v_mix_post_g, v_ffn_pre_g, v_ffn_post_g, v_ffn_w_gu, v_ffn_w_down, v_fox_w_in, v_fox_b_f, v_fox_w_out, v_sgu_w_in, v_sgu_ln_g, v_sgu_ln_b, v_sgu_w_s, v_sgu_b_s, v_sgu_w_out, v_swa_w_in, v_swa_sinks, v_swa_w_out):
    P = dict(zip(INPUTS, (x, c, positions, ada_w, ada_b, mix_pre_g, mix_post_g, ffn_pre_g, ffn_post_g, ffn_w_gu, ffn_w_down, fox_w_in, fox_b_f, fox_w_out, sgu_w_in, sgu_ln_g, sgu_ln_b, sgu_w_s, sgu_b_s, sgu_w_out, swa_w_in, swa_sinks, swa_w_out, loss_target, m_ada_w, m_ada_b, m_mix_pre_g, m_mix_post_g, m_ffn_pre_g, m_ffn_post_g, m_ffn_w_gu, m_ffn_w_down, m_fox_w_in, m_fox_b_f, m_fox_w_out, m_sgu_w_in, m_sgu_ln_g, m_sgu_ln_b, m_sgu_w_s, m_sgu_b_s, m_sgu_w_out, m_swa_w_in, m_swa_sinks, m_swa_w_out, v_ada_w, v_ada_b, v_mix_pre_g, v_mix_post_g, v_ffn_pre_g, v_ffn_post_g, v_ffn_w_gu, v_ffn_w_down, v_fox_w_in, v_fox_b_f, v_fox_w_out, v_sgu_w_in, v_sgu_ln_g, v_sgu_ln_b, v_sgu_w_s, v_sgu_b_s, v_sgu_w_out, v_swa_w_in, v_swa_sinks, v_swa_w_out)))
    xs, target, pos = x[0], loss_target[0], positions[0]
    S, D = xs.shape
    L = ada_w.shape[0]
    n_mix = 3
    F = ffn_w_down.shape[1] * N_CHIP
    xi, yi, ci = _place()
    q_me = 2 * xi + yi
    dev = 4 * xi + 2 * yi + ci

    qc = jnp.stack([q_me, ci]).astype(jnp.int32)
    by_cols = {n: n in COLUMN_SHARDED for n in BIG}
    geo = {n: (by_cols[n], P[n].shape[1], P[n].shape[2] + (_lane_pad(P[n].shape[2]) if by_cols[n] else 0)) for n in BIG}
    groups = {
        "fox0": {n: (0, 1) for n in BIG if n.startswith("fox_")},
        "ffn0": {n: (0, 1) for n in BIG if n.startswith("ffn_")},
        "rest": {n: (1 if n.startswith(("fox_", "ffn_")) else 0, P[n].shape[0]) for n in BIG},
    }
    groups["rest"] = {n: r for n, r in groups["rest"].items() if r[1] > r[0]}

    def shard_of(n, lo, hi):
        s = P[n][lo:hi].astype(BF16)
        return jnp.pad(s, ((0, 0), (0, 0), (0, _lane_pad(s.shape[2])))) if by_cols[n] else s

    Wt = {n: [] for n in BIG}

    def finish_gather(g, fulls):
        for n, s, f in zip(groups[g], shards[g], fulls):
            f = place_shard(f, s, by_cols[n], qc, f"place_{g}_{n}")
            f = _unpad_shard_columns(f, P[n].shape[2]) if by_cols[n] else f
            f = jnp.pad(f, ((0, 0), (0, 0), (0, 3 * D + LANES - f.shape[2]))) if n == "fox_w_in" else f
            Wt[n] += [(f, l) for l in range(f.shape[0])]

    shards = {g: [shard_of(n, *r) for n, r in groups[g].items()] for g in groups}
    group_cols = {g: [by_cols[n] for n in groups[g]] for g in groups}
    finish_gather("fox0", ag_weights(shards["fox0"], group_cols["fox0"], "ag_weights_fox0"))

    c_all = ag_small(c.reshape(D // LANES, LANES), "ag_c").reshape(N_DEV, D)
    c_all = jnp.pad(c_all, ((0, 16 - N_DEV), (0, 0)))
    Nm = ada_w.shape[2]
    ada_b_mine = lax.dynamic_slice_in_dim(ada_b, q_me * Nm, Nm, axis=1).reshape(L, 1, Nm)
    modp = ada_fwd(c_all, ada_w, ada_b_mine, "ada_fwd")[:, :N_DEV]
    mod_all = ag_small(modp.reshape(-1, LANES), "ag_mod").reshape(N_DEV, L, N_DEV, Nm)
    mod_mine = lax.dynamic_index_in_dim(mod_all[0::2], dev, axis=2, keepdims=False)
    mods = mod_mine.transpose(1, 0, 2).reshape(L, 6, 1, D)

    tables = _rope_tables(pos)

    saved = []
    xc = xs
    for i in range(L):
        kind, j = i % n_mix, i // n_mix
        sh_m, sc_m, g_m, sh_f, sc_f, g_f = (mods[i, t] for t in range(6))
        t = f"l{i}"
        h1 = pre_fwd(xc, mix_pre_g[i:i + 1], sh_m, sc_m, t + "_pre_m")
        if kind == 0:
            rides = [ag_sidecar(shards[g], group_cols[g]) if i == 0 else None for g in ("ffn0", "rest")]
            y1, ctx, ffn0_fulls, rest_fulls = _fox_fwd(h1, Wt["fox_w_in"][j], fox_b_f[j], Wt["fox_w_out"][j], t + "_fox", *rides)
            if i == 0:
                finish_gather("ffn0", ffn0_fulls)
                finish_gather("rest", rest_fulls)
        elif kind == 1:
            y1, ctx = _sgu_fwd(h1, Wt["sgu_w_in"][j], sgu_ln_g[j], sgu_ln_b[j], sgu_w_s[j], sgu_b_s[j], Wt["sgu_w_out"][j], t + "_sgu")
        else:
            y1, ctx = _swa_fwd(h1, Wt["swa_w_in"][j], swa_sinks[j], Wt["swa_w_out"][j], tables, t + "_swa")
        xm = post_fwd(xc, y1, mix_post_g[i:i + 1], g_m, t + "_post_m")
        h2 = pre_fwd(xm, ffn_pre_g[i:i + 1], sh_f, sc_f, t + "_pre_f")
        gu = mm(h2, Wt["ffn_w_gu"][i], "nn", BF16, name=t + "_ffn_gu")
        a = act_fwd(gu, t + "_act")
        y2 = mm(a, Wt["ffn_w_down"][i], "nn", F32, name=t + "_ffn_down")
        xn = post_fwd(xm, y2, ffn_post_g[i:i + 1], g_f, t + "_post_f")
        saved.append((xc, h1, y1, ctx, xm, h2, gu, a, y2))
        xc = xn

    sq, dx = loss_head(xc, target, "loss_head")
    loss = lax.psum(sq[0, 0] * (0.5 / D), ("x", "y", "c"))

    big_g = {n: [None] * P[n].shape[0] for n in BIG}
    small_g = {n: [None] * P[n].shape[0] for n in SMALL}
    group_geo = {g: [geo[n] for n in groups[g]] for g in groups}
    group_grads, from_sibling, chip_part, from_chips = {}, {}, {}, {}

    def grads_of(g):
        if g not in group_grads:
            gs = [jnp.stack(big_g[n][lo:hi]) for n, (lo, hi) in groups[g].items()]
            group_grads[g] = [_pad_shard_columns(t, P[n].shape[2]) if by_cols[n] else t for n, t in zip(groups[g], gs)]
        return group_grads[g]

    def fold(g):
        if g not in from_sibling:
            from_sibling[g] = sibling_fold(grads_of(g), group_geo[g], "rs_fold_" + g)
        chip_part[g] = [fold_sum(t, r, *m, qc, f"rs_fold_sum_{g}_{n}")
                        for n, t, r, m in zip(groups[g], grads_of(g), from_sibling[g], group_geo[g])]
        return exchange_sidecar(chip_part[g], group_geo[g])

    for i in reversed(range(L)):
        kind, j = i % n_mix, i // n_mix
        sh_m, sc_m, g_m, sh_f, sc_f, g_f = (mods[i, t] for t in range(6))
        xc, h1, y1, ctx, xm, h2, gu, a, y2 = saved[i]
        t = f"l{i}"
        dy2, dgpost_f, dgate_f = post_bwd(y2, ffn_post_g[i:i + 1], g_f, dx, t + "_dpost_f")
        if i == 0:
            da, from_sibling["rest"] = mm(dy2, Wt["ffn_w_down"][i], "nt", F32, name=t + "_da",
                                          sidecar=fold_sidecar(grads_of("rest"), group_geo["rest"]))
        else:
            da = mm(dy2, Wt["ffn_w_down"][i], "nt", F32, name=t + "_da")
        big_g["ffn_w_down"][i] = mm(a, dy2, "tn", BF16, name=t + "_dwdown")
        dgu = act_bwd(gu, da, t + "_dact")
        dh2 = mm(dgu, Wt["ffn_w_gu"][i], "nt", F32, name=t + "_dh2")
        big_g["ffn_w_gu"][i] = mm(h2, dgu, "tn", BF16, name=t + "_dwgu")
        dxm, dgpre_f, dsh_f, dsc_f = pre_bwd(xm, ffn_pre_g[i:i + 1], sh_f, sc_f, dh2, dx, t + "_dpre_f")
        dy1, dgpost_m, dgate_m = post_bwd(y1, mix_post_g[i:i + 1], g_m, dxm, t + "_dpost_m")
        if kind == 0:
            rides = [fold(g) if i == 0 else None for g in ("ffn0", "rest")]
            dh1, dw_in, dw_out, db, from_ffn0, from_rest = _fox_bwd(dy1, h1, Wt["fox_w_in"][j], fox_b_f[j], Wt["fox_w_out"][j], ctx,
                                                                    t + "_fox", *rides)
            big_g["fox_w_in"][j], big_g["fox_w_out"][j], small_g["fox_b_f"][j] = dw_in, dw_out, db
            if i == 0:
                from_chips["ffn0"], from_chips["rest"] = from_ffn0, from_rest
        elif kind == 1:
            dh1, dw_in, dw_out, dlg, dlb, dws, dbs = _sgu_bwd(dy1, h1, Wt["sgu_w_in"][j], Wt["sgu_w_out"][j], ctx, t + "_sgu")
            big_g["sgu_w_in"][j], big_g["sgu_w_out"][j] = dw_in, dw_out
            small_g["sgu_ln_g"][j], small_g["sgu_ln_b"][j], small_g["sgu_w_s"][j], small_g["sgu_b_s"][j] = dlg, dlb, dws, dbs
        else:
            dh1, dw_in, dw_out, dsk = _swa_bwd(dy1, h1, Wt["swa_w_in"][j], Wt["swa_w_out"][j], ctx, t + "_swa")
            big_g["swa_w_in"][j], big_g["swa_w_out"][j], small_g["swa_sinks"][j] = dw_in, dw_out, dsk
        dx, dgpre_m, dsh_m, dsc_m = pre_bwd(xc, mix_pre_g[i:i + 1], sh_m, sc_m, dh1, dxm, t + "_dpre_m")
        small_g["ada_b"][i] = jnp.concatenate([dsh_m, dsc_m, dgate_m, dsh_f, dsc_f, dgate_f], axis=1)[0]
        small_g["mix_pre_g"][i], small_g["mix_post_g"][i] = dgpre_m[0], dgpost_m[0]
        small_g["ffn_pre_g"][i], small_g["ffn_post_g"][i] = dgpre_f[0], dgpost_f[0]
    grad_x = dx[None]

    shapes = {n: P[n].shape for n in SMALL}
    small_parts = ag_small(_pack_small({n: jnp.stack(small_g[n]) for n in SMALL}), "ag_small_grads")
    sg, sd, sm, sv = adamw(_pack_small({n: P[n] for n in SMALL}), _pack_small({n: P["m_" + n] for n in SMALL}),
                           _pack_small({n: P["v_" + n] for n in SMALL}), [small_parts], "adamw_small", gstack=N_DEV)
    out_g, out_d, out_m, out_v = (_unpack_small(t, shapes) for t in (sg, sd, sm, sv))

    dmod_all = small_parts[:, :L * 6 * D // LANES].reshape(N_DEV, L, 6 * D)
    dmod_mine = lax.dynamic_slice_in_dim(dmod_all, q_me * Nm, Nm, axis=2).transpose(1, 0, 2)
    dmod_mine = jnp.pad(dmod_mine, ((0, 0), (0, 16 - N_DEV), (0, 0)))
    g_ada = ada_wgrad(c_all, dmod_mine, "ada_wgrad")
    r2 = lambda t: t.reshape(-1, t.shape[-1])
    res = adamw(r2(ada_w), r2(m_ada_w), r2(v_ada_w), [r2(g_ada)], "adamw_ada_w", emit_g=False)
    out_g["ada_w"] = g_ada
    out_d["ada_w"], out_m["ada_w"], out_v["ada_w"] = (t.reshape(ada_w.shape) for t in res)

    fold("fox0")
    from_chips["fox0"] = chip_exchange(chip_part["fox0"], group_geo["fox0"], "rs_exchange_fox0")
    pieces = [(g, n, m, r, e) for g in groups for n, m, r, e in zip(groups[g], group_geo[g], chip_part[g], from_chips[g])]
    mine = [chip_sum(r, e, *m, qc, f"rs_chip_sum_{g}_{n}") for g, n, m, r, e in pieces]
    shared = sibling_share(mine, [m for _, _, m, _, _ in pieces], "rs_share")
    by_name = {n: [] for n in BIG}
    for (g, n, *_), t in zip(pieces, shared):
        by_name[n].append(t)
    for n in BIG:
        gsh = by_name[n][0] if len(by_name[n]) == 1 else jnp.concatenate(by_name[n], axis=0)
        gsh = gsh[:, :, :P[n].shape[2]]
        res = adamw(r2(P[n]), r2(P["m_" + n]), r2(P["v_" + n]), [r2(gsh)], "adamw_" + n, emit_g=False)
        out_g[n] = gsh
        out_d[n], out_m[n], out_v[n] = (t.reshape(P[n].shape) for t in res)

    return (loss, grad_x, *[out_g[n] for n in WEIGHTS], *[out_d[n] for n in WEIGHTS], *[out_m[n] for n in WEIGHTS],
            *[out_v[n] for n in WEIGHTS])
```

```python
from typing import Callable, NamedTuple

import numpy as np
import jax
import jax.numpy as jnp
from jax import lax
from jax.experimental import pallas as pl
from jax.experimental.pallas import tpu as pltpu

F32 = jnp.float32
BF16 = jnp.bfloat16
MESH = pl.DeviceIdType.MESH

EPS = 1e-6
NEG = -1e30
V7X_VMEM_BYTES = 64 * 1024 * 1024
VMEM_LIMIT = V7X_VMEM_BYTES - 8 * 1024 * 1024
LANES = 128
SEQ_BLOCK = 128
SWA_HEAD_DIM = 64
ROPE_DIM = SWA_HEAD_DIM // 4
ROPE_THETA = 500000.0

ADAM_LR = 0.001
ADAM_B1 = 0.9
ADAM_B2 = 0.999
ADAM_EPS = 1e-08
ADAM_WD = 0.01
ADAM_STEP = 10


def _cp(sem=None, **kw):
    return pltpu.CompilerParams(dimension_semantics=sem, vmem_limit_bytes=VMEM_LIMIT, **kw)


def _tile(dim, pref, mult=LANES):
    if dim <= pref:
        return dim
    t = (pref // mult) * mult
    while t >= mult:
        if dim % t == 0:
            return t
        t -= mult
    return dim


_DIMS = {"nn": (((1,), (0,)), ((), ())), "nt": (((1,), (1,)), ((), ())), "tn": (((0,), (0,)), ((), ()))}


MM_TILES = {"nn": (1024, 512, 2816), "nt": (1024, 1024, 2816), "tn": (1024, 1024, 2048)}


def mm(a, b, mode="nn", out_dtype=F32, add=None, name="mm", b_cols=None, tm=None, tn=None, tk=None, sidecar=None):
    tm, tn, tk = (d if t is None else t for t, d in zip((tm, tn, tk), MM_TILES[mode]))
    b, b_layer = b if isinstance(b, tuple) else (b, None)
    b_shape = b.shape[-2:]
    c0 = 0
    if b_cols is not None:
        c0, csize = b_cols
    if mode == "nn":
        (M, K), (K2, N) = a.shape, b_shape
        if b_cols is not None:
            N = csize
    elif mode == "nt":
        (M, K), (N, K2) = a.shape, b_shape
        if b_cols is not None:
            K2 = csize
    else:
        (K, M), (K2, N) = a.shape, b_shape
        assert b_cols is None
    assert K == K2, (a.shape, b.shape, mode)
    tm = _tile(M, tm, LANES if mode == "tn" else 16)
    tn = _tile(N, tn)
    tk = _tile(K, tk, LANES if mode != "tn" else 16)
    nk = K // tk
    if b_cols is not None:
        assert c0 % (tn if mode == "nn" else tk) == 0, (b_cols, tn, tk)
    bo = c0 // (tn if mode == "nn" else tk)
    dims = _DIMS[mode]
    has_add = add is not None

    def body(a_ref, b_ref, *rest):
        if has_add:
            add_ref, o_ref, acc_ref = rest
        else:
            o_ref, acc_ref = rest
        k = pl.program_id(2)
        p = lax.dot_general(a_ref[...].astype(BF16), b_ref[...].astype(BF16), dims, preferred_element_type=F32)

        @pl.when(k == 0)
        def _():
            acc_ref[...] = p + add_ref[...].astype(F32) if has_add else p

        @pl.when(k > 0)
        def _():
            acc_ref[...] += p

        @pl.when(k == nk - 1)
        def _():
            o_ref[...] = acc_ref[...].astype(o_ref.dtype)

    a_spec = pl.BlockSpec((tk, tm), lambda i, j, k: (k, i)) if mode == "tn" else pl.BlockSpec((tm, tk), lambda i, j, k: (i, k))
    b_blk, b_idx = ((tn, tk), lambda i, j, k: (j, k + bo)) if mode == "nt" else ((tk, tn), lambda i, j, k: (k, j + bo))
    if b_layer is None:
        b_spec = pl.BlockSpec(b_blk, b_idx)
    else:
        b_spec = pl.BlockSpec((None,) + b_blk, lambda i, j, k: (b_layer,) + b_idx(i, j, k))
    o_spec = pl.BlockSpec((tm, tn), lambda i, j, k: (i, j))
    in_specs = [a_spec, b_spec] + ([o_spec] if has_add else [])
    args = (a, b) + ((add,) if has_add else ())
    out_shape = jax.ShapeDtypeStruct((M, N), out_dtype)
    scratch = [pltpu.VMEM((tm, tn), F32)]
    grid = (M // tm, N // tn, nk)
    if sidecar is None:
        return pl.pallas_call(
            body, name=name, grid=grid, in_specs=in_specs, out_specs=o_spec, out_shape=out_shape, scratch_shapes=scratch,
            compiler_params=_cp(("parallel", "parallel", "arbitrary")),
        )(*args)
    first = lambda: (pl.program_id(0) == 0) & (pl.program_id(1) == 0) & (pl.program_id(2) == 0)
    last = lambda: (pl.program_id(0) == grid[0] - 1) & (pl.program_id(1) == grid[1] - 1) & (pl.program_id(2) == grid[2] - 1)
    res = pl.pallas_call(
        _with_sidecar(sidecar, len(args), 1, 1, body, first, last), name=name, grid=grid,
        in_specs=in_specs + _sc_specs(sidecar, False), out_specs=[o_spec] + _sc_specs(sidecar, True),
        out_shape=[out_shape] + _sc_out(sidecar), scratch_shapes=scratch + _sc_sems(sidecar),
        compiler_params=_cp(("arbitrary", "arbitrary", "arbitrary")),
    )(*args, *_sc_arrays(sidecar))
    return res[0], res[1:]


def _rms(x, g):
    return (x * lax.rsqrt(jnp.mean(x * x, axis=-1, keepdims=True) + EPS)) * g


def _pre(x, g, sh, sc):
    return _rms(x, g) * (1 + sc) + sh


def _post(x, y, g, gate):
    return x + gate * _rms(y, g)


ROW_TILE = 256


def _row_spec(tr, d):
    return pl.BlockSpec((tr, d), lambda i: (i, 0))


def _vec_spec(d):
    return pl.BlockSpec((1, d), lambda i: (0, 0))


def pre_fwd(x, g, sh, sc, name):
    S, D = x.shape
    tr = _tile(S, ROW_TILE, 16)

    def body(x_ref, g_ref, sh_ref, sc_ref, h_ref):
        h_ref[...] = _pre(x_ref[...], g_ref[...], sh_ref[...], sc_ref[...]).astype(h_ref.dtype)

    return pl.pallas_call(
        body, name=name, grid=(S // tr,), in_specs=[_row_spec(tr, D)] + [_vec_spec(D)] * 3, out_specs=_row_spec(tr, D),
        out_shape=jax.ShapeDtypeStruct((S, D), BF16), compiler_params=_cp(("parallel",)),
    )(x, g, sh, sc)


def pre_bwd(x, g, sh, sc, dh, dres, name):
    S, D = x.shape
    tr = _tile(S, ROW_TILE, 16)

    def body(x_ref, g_ref, sh_ref, sc_ref, dh_ref, dres_ref, dx_ref, dg_ref, dsh_ref, dsc_ref):
        _, vjp = jax.vjp(_pre, x_ref[...], g_ref[...], sh_ref[...], sc_ref[...])
        dx, dg, dsh, dsc = vjp(dh_ref[...].astype(F32))
        dx_ref[...] = dres_ref[...] + dx

        @pl.when(pl.program_id(0) == 0)
        def _():
            dg_ref[...] = jnp.zeros_like(dg_ref)
            dsh_ref[...] = jnp.zeros_like(dsh_ref)
            dsc_ref[...] = jnp.zeros_like(dsc_ref)

        dg_ref[...] += dg
        dsh_ref[...] += dsh
        dsc_ref[...] += dsc

    vec = jax.ShapeDtypeStruct((1, D), F32)
    return pl.pallas_call(
        body, name=name, grid=(S // tr,), in_specs=[_row_spec(tr, D)] + [_vec_spec(D)] * 3 + [_row_spec(tr, D)] * 2,
        out_specs=[_row_spec(tr, D)] + [_vec_spec(D)] * 3, out_shape=[jax.ShapeDtypeStruct((S, D), F32), vec, vec, vec],
        compiler_params=_cp(("arbitrary",)),
    )(x, g, sh, sc, dh, dres)


def post_fwd(x, y, g, gate, name):
    S, D = x.shape
    tr = _tile(S, ROW_TILE, 16)

    def body(x_ref, y_ref, g_ref, gate_ref, o_ref):
        o_ref[...] = _post(x_ref[...], y_ref[...], g_ref[...], gate_ref[...])

    return pl.pallas_call(
        body, name=name, grid=(S // tr,), in_specs=[_row_spec(tr, D)] * 2 + [_vec_spec(D)] * 2, out_specs=_row_spec(tr, D),
        out_shape=jax.ShapeDtypeStruct((S, D), F32), compiler_params=_cp(("parallel",)),
    )(x, y, g, gate)


def post_bwd(y, g, gate, dxn, name):
    S, D = y.shape
    tr = _tile(S, ROW_TILE, 16)

    def body(y_ref, g_ref, gate_ref, dxn_ref, dy_ref, dg_ref, dgate_ref):
        fn = lambda yy, gg, gt: gt * _rms(yy, gg)
        _, vjp = jax.vjp(fn, y_ref[...], g_ref[...], gate_ref[...])
        dy, dg, dgate = vjp(dxn_ref[...])
        dy_ref[...] = dy.astype(dy_ref.dtype)

        @pl.when(pl.program_id(0) == 0)
        def _():
            dg_ref[...] = jnp.zeros_like(dg_ref)
            dgate_ref[...] = jnp.zeros_like(dgate_ref)

        dg_ref[...] += dg
        dgate_ref[...] += dgate

    vec = jax.ShapeDtypeStruct((1, D), F32)
    return pl.pallas_call(
        body, name=name, grid=(S // tr,), in_specs=[_row_spec(tr, D)] + [_vec_spec(D)] * 2 + [_row_spec(tr, D)],
        out_specs=[_row_spec(tr, D)] + [_vec_spec(D)] * 2, out_shape=[jax.ShapeDtypeStruct((S, D), BF16), vec, vec],
        compiler_params=_cp(("arbitrary",)),
    )(y, g, gate, dxn)


def _swiglu(g, u):
    return jax.nn.silu(g) * u


ACT_ROWS = 256


def act_fwd(gu, name):
    S, F2 = gu.shape
    F = F2 // 2
    tr = _tile(S, ACT_ROWS, 16)

    def body(gu_ref, a_ref):
        a_ref[...] = _swiglu(gu_ref[:, :F].astype(F32), gu_ref[:, F:].astype(F32)).astype(a_ref.dtype)

    return pl.pallas_call(
        body, name=name, grid=(S // tr,), in_specs=[_row_spec(tr, F2)], out_specs=_row_spec(tr, F),
        out_shape=jax.ShapeDtypeStruct((S, F), BF16), compiler_params=_cp(("parallel",)),
    )(gu)


def act_bwd(gu, da, name):
    S, F2 = gu.shape
    F = F2 // 2
    tr = _tile(S, ACT_ROWS, 16)

    def body(gu_ref, da_ref, dgu_ref):
        _, vjp = jax.vjp(_swiglu, gu_ref[:, :F].astype(F32), gu_ref[:, F:].astype(F32))
        dg, du = vjp(da_ref[...].astype(F32))
        dgu_ref[:, :F] = dg.astype(dgu_ref.dtype)
        dgu_ref[:, F:] = du.astype(dgu_ref.dtype)

    return pl.pallas_call(
        body, name=name, grid=(S // tr,), in_specs=[_row_spec(tr, F2), _row_spec(tr, F)], out_specs=_row_spec(tr, F2),
        out_shape=jax.ShapeDtypeStruct((S, F2), BF16), compiler_params=_cp(("parallel",)),
    )(gu, da)


def _tri(upper):
    r = lax.broadcasted_iota(jnp.int32, (LANES, LANES), 0)
    c = lax.broadcasted_iota(jnp.int32, (LANES, LANES), 1)
    return ((r <= c) if upper else (r >= c)).astype(F32)


def _hdot(a, b):
    return jnp.dot(a, b, precision=lax.Precision.HIGHEST, preferred_element_type=F32)


def fox_gate_fwd(fgT, b, name):
    H, S = fgT.shape
    spec = pl.BlockSpec((H, LANES), lambda ch: (0, ch))

    def body(fg_ref, b_ref, cum_ref, carry_ref):
        @pl.when(pl.program_id(0) == 0)
        def _():
            carry_ref[...] = jnp.zeros_like(carry_ref)

        lf = jax.nn.log_sigmoid(fg_ref[...] + b_ref[...])
        cum_ref[...] = _hdot(lf, _tri(True)) + carry_ref[...]
        carry_ref[...] += _hdot(lf, jnp.ones((LANES, LANES), F32))

    return pl.pallas_call(
        body, name=name, grid=(S // LANES,), in_specs=[spec, pl.BlockSpec((H, 1), lambda ch: (0, 0))], out_specs=spec,
        out_shape=jax.ShapeDtypeStruct((H, S), F32), scratch_shapes=[pltpu.VMEM((H, LANES), F32)], compiler_params=_cp(("arbitrary",)),
    )(fgT, b)


def fox_gate_bwd(dcum, fgT, b, name):
    H, S = fgT.shape
    nch = S // LANES
    spec = pl.BlockSpec((H, LANES), lambda t: (0, nch - 1 - t))

    def body(dcum_ref, fg_ref, b_ref, dfg_ref, db_ref, tail_ref):
        @pl.when(pl.program_id(0) == 0)
        def _():
            tail_ref[...] = jnp.zeros_like(tail_ref)
            db_ref[...] = jnp.zeros_like(db_ref)

        dlf = _hdot(dcum_ref[...], _tri(False)) + tail_ref[...]
        dfg = dlf * jax.nn.sigmoid(-(fg_ref[...] + b_ref[...]))
        dfg_ref[...] = dfg
        ones = jnp.ones((LANES, LANES), F32)
        tail_ref[...] += _hdot(dcum_ref[...], ones)
        db_ref[...] += _hdot(dfg, ones)

    return pl.pallas_call(
        body, name=name, grid=(nch,), in_specs=[spec, spec, pl.BlockSpec((H, 1), lambda t: (0, 0))],
        out_specs=[spec, pl.BlockSpec((H, LANES), lambda t: (0, 0))],
        out_shape=[jax.ShapeDtypeStruct((H, S), F32), jax.ShapeDtypeStruct((H, LANES), F32)],
        scratch_shapes=[pltpu.VMEM((H, LANES), F32)], compiler_params=_cp(("arbitrary",)),
    )(dcum, fgT, b)


FOX_TILE = 1024


def _on_and_below_diagonal(i, j, tile):
    @pl.when(j < i)
    def _():
        tile(False)

    @pl.when(j == i)
    def _():
        tile(True)


def _causal_pairs(n, by_key):
    pairs = [(i, j) for i in range(n) for j in range(i + 1)]
    if by_key:
        pairs.sort(key=lambda p: (p[1], p[0]))
    qi = np.asarray([p[0] for p in pairs], np.int32)
    kj = np.asarray([p[1] for p in pairs], np.int32)
    return qi, kj


def _fox_scores(q, k, fq, fk, T, scale, transposed):
    r = lax.broadcasted_iota(jnp.int32, (T, T), 0)
    c = lax.broadcasted_iota(jnp.int32, (T, T), 1)
    if transposed:
        return lax.dot_general(k, q, _DIMS["nt"], preferred_element_type=F32) * scale + (fq - fk), r <= c
    return lax.dot_general(q, k, _DIMS["nt"], preferred_element_type=F32) * scale + (fq - fk), c <= r


class SideCar(NamedTuple):
    arrays: list
    out_shape: list
    semaphores: list
    steps: Callable


def _sc_specs(sc, out):
    return [] if sc is None else [pl.BlockSpec(memory_space=pl.ANY)] * len(sc.out_shape if out else sc.arrays)


def _sc_sems(sc):
    return [] if sc is None else list(sc.semaphores)


def _sc_out(sc):
    return [] if sc is None else list(sc.out_shape)


def _sc_arrays(sc):
    return [] if sc is None else list(sc.arrays)


def _with_sidecar(sc, n_in, n_out, n_scratch, body, first, last):
    if sc is None:
        return body
    a, o = len(sc.arrays), len(sc.out_shape)

    def wrapped(*refs):
        ins, rest = refs[:n_in], refs[n_in:]
        sc_in, rest = rest[:a], rest[a:]
        outs, rest = rest[:n_out], rest[n_out:]
        sc_out, rest = rest[:o], rest[o:]
        scratch, sems = rest[:n_scratch], rest[n_scratch:]
        start, finish = sc.steps(sc_in, sc_out, sems)
        pl.when(first())(start)
        body(*ins, *outs, *scratch)
        pl.when(last())(finish)

    return wrapped


def fox_attn_fwd(qkv, cum_col, cum_row, H, name, sidecar=None):
    S = qkv.shape[0]
    Dh = qkv.shape[1] // (3 * H)
    T = _tile(S, FOX_TILE)
    n = S // T
    qi, kj = _causal_pairs(n, by_key=False)
    scale = Dh ** -0.5

    def body(qi_ref, kj_ref, q_ref, k_ref, v_ref, fq_ref, fk_ref, o_ref, lse_ref, m_ref, l_ref, acc_ref):
        p_id = pl.program_id(1)
        i, j = qi_ref[p_id], kj_ref[p_id]

        @pl.when(j == 0)
        def _():
            m_ref[...] = jnp.full_like(m_ref, NEG)
            l_ref[...] = jnp.zeros_like(l_ref)
            acc_ref[...] = jnp.zeros_like(acc_ref)

        def tile(masked):
            s, mask = _fox_scores(q_ref[...], k_ref[...], fq_ref[0], fk_ref[0], T, scale, False)
            if masked:
                s = jnp.where(mask, s, NEG)
            m_new = jnp.maximum(m_ref[...], jnp.max(s, axis=1, keepdims=True))
            alpha = jnp.exp(m_ref[...] - m_new)
            p = jnp.exp(s - m_new)
            l_ref[...] = alpha * l_ref[...] + jnp.sum(p, axis=1, keepdims=True)
            acc_ref[...] = alpha * acc_ref[...] + jnp.dot(p.astype(BF16), v_ref[...], preferred_element_type=F32)
            m_ref[...] = m_new

        _on_and_below_diagonal(i, j, tile)

        @pl.when(j == i)
        def _():
            o_ref[...] = (acc_ref[...] / l_ref[...]).astype(o_ref.dtype)
            lse_ref[0] = m_ref[...] + jnp.log(l_ref[...])

    grid_spec = pltpu.PrefetchScalarGridSpec(
        num_scalar_prefetch=2, grid=(H, len(qi)),
        in_specs=[
            pl.BlockSpec((T, Dh), lambda h, p, qi, kj: (qi[p], h)),
            pl.BlockSpec((T, Dh), lambda h, p, qi, kj: (kj[p], H + h)),
            pl.BlockSpec((T, Dh), lambda h, p, qi, kj: (kj[p], 2 * H + h)),
            pl.BlockSpec((1, T, 1), lambda h, p, qi, kj: (h, qi[p], 0)),
            pl.BlockSpec((1, 1, T), lambda h, p, qi, kj: (h, 0, kj[p])),
        ] + _sc_specs(sidecar, False),
        out_specs=[
            pl.BlockSpec((T, Dh), lambda h, p, qi, kj: (qi[p], h)),
            pl.BlockSpec((1, T, 1), lambda h, p, qi, kj: (h, qi[p], 0)),
        ] + _sc_specs(sidecar, True),
        scratch_shapes=[pltpu.VMEM((T, 1), F32), pltpu.VMEM((T, 1), F32), pltpu.VMEM((T, Dh), F32)] + _sc_sems(sidecar),
    )
    first = lambda: (pl.program_id(0) == 0) & (pl.program_id(1) == 0)
    last = lambda: (pl.program_id(0) == H - 1) & (pl.program_id(1) == len(qi) - 1)
    res = pl.pallas_call(
        _with_sidecar(sidecar, 7, 2, 3, body, first, last), name=name, grid_spec=grid_spec,
        out_shape=[jax.ShapeDtypeStruct((S, H * Dh), F32), jax.ShapeDtypeStruct((H, S, 1), F32)] + _sc_out(sidecar),
        compiler_params=_cp(("arbitrary", "arbitrary")),
    )(jnp.asarray(qi), jnp.asarray(kj), qkv, qkv, qkv, cum_col, cum_row, *_sc_arrays(sidecar))
    return res[0], res[1], res[2:]


def fox_attn_bwd_dq(qkv, do, o, lse, cum_col, cum_row, H, name, sidecar=None):
    S = qkv.shape[0]
    Dh = qkv.shape[1] // (3 * H)
    T = _tile(S, FOX_TILE)
    n = S // T
    qi, kj = _causal_pairs(n, by_key=False)
    scale = Dh ** -0.5

    def body(qi_ref, kj_ref, q_ref, k_ref, v_ref, do_ref, o_ref, lse_ref, fq_ref, fk_ref, dq_ref, delta_ref, acc_ref, dl_ref,
             rs_ref):
        p_id = pl.program_id(1)
        i, j = qi_ref[p_id], kj_ref[p_id]

        @pl.when(j == 0)
        def _():
            acc_ref[...] = jnp.zeros_like(acc_ref)
            rs_ref[...] = jnp.zeros_like(rs_ref)
            dl_ref[...] = jnp.sum(do_ref[...].astype(F32) * o_ref[...].astype(F32), axis=1, keepdims=True)

        def tile(masked):
            s, mask = _fox_scores(q_ref[...], k_ref[...], fq_ref[0], fk_ref[0], T, scale, False)
            p = jnp.exp(s - lse_ref[0])
            if masked:
                p = jnp.where(mask, p, 0.0)
            dp = lax.dot_general(do_ref[...], v_ref[...], _DIMS["nt"], preferred_element_type=F32)
            ds = p * (dp - dl_ref[...])
            rs_ref[...] += jnp.sum(ds, axis=1, keepdims=True)
            acc_ref[...] += jnp.dot(ds.astype(BF16), k_ref[...], preferred_element_type=F32)

        _on_and_below_diagonal(i, j, tile)

        @pl.when(j == i)
        def _():
            dq_ref[...] = (acc_ref[...] * scale).astype(dq_ref.dtype)
            delta_ref[0] = dl_ref[...] + rs_ref[...]

    qspec = pl.BlockSpec((T, Dh), lambda h, p, qi, kj: (qi[p], h))
    colspec = pl.BlockSpec((1, T, 1), lambda h, p, qi, kj: (h, qi[p], 0))
    grid_spec = pltpu.PrefetchScalarGridSpec(
        num_scalar_prefetch=2, grid=(H, len(qi)),
        in_specs=[
            qspec,
            pl.BlockSpec((T, Dh), lambda h, p, qi, kj: (kj[p], H + h)),
            pl.BlockSpec((T, Dh), lambda h, p, qi, kj: (kj[p], 2 * H + h)),
            qspec, qspec, colspec, colspec,
            pl.BlockSpec((1, 1, T), lambda h, p, qi, kj: (h, 0, kj[p])),
        ] + _sc_specs(sidecar, False),
        out_specs=[qspec, colspec] + _sc_specs(sidecar, True),
        scratch_shapes=[pltpu.VMEM((T, Dh), F32), pltpu.VMEM((T, 1), F32), pltpu.VMEM((T, 1), F32)] + _sc_sems(sidecar),
    )
    first = lambda: (pl.program_id(0) == 0) & (pl.program_id(1) == 0)
    last = lambda: (pl.program_id(0) == H - 1) & (pl.program_id(1) == len(qi) - 1)
    res = pl.pallas_call(
        _with_sidecar(sidecar, 10, 2, 3, body, first, last), name=name, grid_spec=grid_spec,
        out_shape=[jax.ShapeDtypeStruct((S, H * Dh), BF16), jax.ShapeDtypeStruct((H, S, 1), F32)] + _sc_out(sidecar),
        compiler_params=_cp(("arbitrary", "arbitrary")),
    )(jnp.asarray(qi), jnp.asarray(kj), qkv, qkv, qkv, do, o, lse, cum_col, cum_row, *_sc_arrays(sidecar))
    return res[0], res[1], res[2:]


def fox_attn_bwd_dkv(qkv, do, lse_row, delta_row, cum_col, cum_row, H, name, sidecar=None):
    S = qkv.shape[0]
    Dh = qkv.shape[1] // (3 * H)
    T = _tile(S, FOX_TILE)
    n = S // T
    qi, kj = _causal_pairs(n, by_key=True)
    scale = Dh ** -0.5

    def body(qi_ref, kj_ref, q_ref, k_ref, v_ref, do_ref, lse_ref, dl_ref, fq_ref, fk_ref, dk_ref, dv_ref, dcum_ref,
             dk_acc, dv_acc, df_acc):
        p_id = pl.program_id(1)
        i, j = qi_ref[p_id], kj_ref[p_id]

        @pl.when(i == j)
        def _():
            dk_acc[...] = jnp.zeros_like(dk_acc)
            dv_acc[...] = jnp.zeros_like(dv_acc)
            df_acc[...] = jnp.zeros_like(df_acc)

        def tile(masked):
            sT, mask = _fox_scores(q_ref[...], k_ref[...], fq_ref[0], fk_ref[0], T, scale, True)
            pT = jnp.exp(sT - lse_ref[0])
            if masked:
                pT = jnp.where(mask, pT, 0.0)
            dv_acc[...] += jnp.dot(pT.astype(BF16), do_ref[...], preferred_element_type=F32)
            dpT = lax.dot_general(v_ref[...], do_ref[...], _DIMS["nt"], preferred_element_type=F32)
            dsT = pT * (dpT - dl_ref[0])
            dk_acc[...] += jnp.dot(dsT.astype(BF16), q_ref[...], preferred_element_type=F32)
            df_acc[...] -= jnp.sum(dsT, axis=1, keepdims=True)

        _on_and_below_diagonal(i, j, tile)

        @pl.when(i == n - 1)
        def _():
            dk_ref[...] = (dk_acc[...] * scale).astype(dk_ref.dtype)
            dv_ref[...] = dv_acc[...].astype(dv_ref.dtype)
            dcum_ref[0] = df_acc[...]

    qspec = pl.BlockSpec((T, Dh), lambda h, p, qi, kj: (qi[p], h))
    kspec = pl.BlockSpec((T, Dh), lambda h, p, qi, kj: (kj[p], H + h))
    vspec = pl.BlockSpec((T, Dh), lambda h, p, qi, kj: (kj[p], 2 * H + h))
    qrow = pl.BlockSpec((1, 1, T), lambda h, p, qi, kj: (h, 0, qi[p]))
    kcol = pl.BlockSpec((1, T, 1), lambda h, p, qi, kj: (h, kj[p], 0))
    grid_spec = pltpu.PrefetchScalarGridSpec(
        num_scalar_prefetch=2, grid=(H, len(qi)),
        in_specs=[qspec, kspec, vspec, qspec, qrow, qrow, qrow, kcol] + _sc_specs(sidecar, False),
        out_specs=[pl.BlockSpec((T, Dh), lambda h, p, qi, kj: (kj[p], h))] * 2 + [kcol] + _sc_specs(sidecar, True),
        scratch_shapes=[pltpu.VMEM((T, Dh), F32), pltpu.VMEM((T, Dh), F32), pltpu.VMEM((T, 1), F32)] + _sc_sems(sidecar),
    )
    out = jax.ShapeDtypeStruct((S, H * Dh), BF16)
    first = lambda: (pl.program_id(0) == 0) & (pl.program_id(1) == 0)
    last = lambda: (pl.program_id(0) == H - 1) & (pl.program_id(1) == len(qi) - 1)
    res = pl.pallas_call(
        _with_sidecar(sidecar, 10, 3, 3, body, first, last), name=name, grid_spec=grid_spec,
        out_shape=[out, out, jax.ShapeDtypeStruct((H, S, 1), F32)] + _sc_out(sidecar),
        compiler_params=_cp(("arbitrary", "arbitrary")),
    )(jnp.asarray(qi), jnp.asarray(kj), qkv, qkv, qkv, do, lse_row, delta_row, cum_row, cum_col, *_sc_arrays(sidecar))
    return res[0], res[1], res[2], res[3:]


def _sgu_ln(zu, zv, ln_g, ln_b):
    u = jax.nn.gelu(zu)
    v = jax.nn.gelu(zv)
    mu = jnp.mean(v, axis=-1, keepdims=True)
    var = jnp.mean(jnp.square(v - mu), axis=-1, keepdims=True)
    return u, (v - mu) * lax.rsqrt(var + EPS) * ln_g + ln_b


def _tril_mask():
    r = lax.broadcasted_iota(jnp.int32, (SEQ_BLOCK, SEQ_BLOCK), 0)
    c = lax.broadcasted_iota(jnp.int32, (SEQ_BLOCK, SEQ_BLOCK), 1)
    return r >= c


def _sgu_spatial(ws_ref, bsT, selT, vn, G):
    tril = _tril_mask()
    fs = []
    for g in range(G):
        wg = jnp.where(tril, ws_ref[g], 0.0).astype(BF16)
        fs.append(jnp.dot(wg, vn[:, g * SEQ_BLOCK:(g + 1) * SEQ_BLOCK].astype(BF16), preferred_element_type=F32))
    bias = jnp.dot(bsT, selT, precision=lax.Precision.HIGHEST, preferred_element_type=F32)
    return jnp.concatenate(fs, axis=1) + bias


def _sgu_specs(W, G):
    return [
        pl.BlockSpec((SEQ_BLOCK, 2 * W), lambda n: (n, 0)),
        pl.BlockSpec((1, W), lambda n: (0, 0)),
        pl.BlockSpec((1, W), lambda n: (0, 0)),
        pl.BlockSpec((G, SEQ_BLOCK, SEQ_BLOCK), lambda n: (0, 0, 0)),
        pl.BlockSpec((SEQ_BLOCK, G), lambda n: (0, 0)),
        pl.BlockSpec((G, W), lambda n: (0, 0)),
    ]


def sgu_fwd(zp, ln_g, ln_b, ws, bsT, selT, name):
    S, W2 = zp.shape
    W = W2 // 2
    G = ws.shape[0]

    def body(z_ref, lg_ref, lb_ref, ws_ref, bs_ref, sel_ref, o_ref):
        u, vn = _sgu_ln(z_ref[:, :W], z_ref[:, W:], lg_ref[...], lb_ref[...])
        o_ref[...] = (u * _sgu_spatial(ws_ref, bs_ref[...], sel_ref[...], vn, G)).astype(o_ref.dtype)

    return pl.pallas_call(
        body, name=name, grid=(S // SEQ_BLOCK,), in_specs=_sgu_specs(W, G), out_specs=pl.BlockSpec((SEQ_BLOCK, W), lambda n: (n, 0)),
        out_shape=jax.ShapeDtypeStruct((S, W), BF16), compiler_params=_cp(("parallel",)),
    )(zp, ln_g, ln_b, ws, bsT, selT)


def sgu_bwd(zp, ln_g, ln_b, ws, bsT, selT, dgated, name):
    S, W2 = zp.shape
    W = W2 // 2
    G = ws.shape[0]

    def body(z_ref, lg_ref, lb_ref, ws_ref, bs_ref, sel_ref, dgt_ref, dz_ref, dlg_ref, dlb_ref, dws_ref, dbs_ref):
        (u, vn), vjp = jax.vjp(_sgu_ln, z_ref[:, :W], z_ref[:, W:], lg_ref[...], lb_ref[...])
        f = _sgu_spatial(ws_ref, bs_ref[...], sel_ref[...], vn, G)
        dgt = dgt_ref[...].astype(F32)
        du, df = dgt * f, dgt * u

        @pl.when(pl.program_id(0) == 0)
        def _():
            dlg_ref[...] = jnp.zeros_like(dlg_ref)
            dlb_ref[...] = jnp.zeros_like(dlb_ref)
            dws_ref[...] = jnp.zeros_like(dws_ref)
            dbs_ref[...] = jnp.zeros_like(dbs_ref)

        dbs_ref[...] += lax.dot_general(df, sel_ref[...], _DIMS["nt"], precision=lax.Precision.HIGHEST, preferred_element_type=F32)
        tril = _tril_mask()
        dvn = []
        for g in range(G):
            sl = slice(g * SEQ_BLOCK, (g + 1) * SEQ_BLOCK)
            wg = jnp.where(tril, ws_ref[g], 0.0).astype(BF16)
            df_g = df[:, sl].astype(BF16)
            dw = lax.dot_general(df_g, vn[:, sl].astype(BF16), _DIMS["nt"], preferred_element_type=F32)
            dws_ref[g] += jnp.where(tril, dw, 0.0)
            dvn.append(lax.dot_general(wg, df_g, _DIMS["tn"], preferred_element_type=F32))
        dzu, dzv, dlg, dlb = vjp((du, jnp.concatenate(dvn, axis=1)))
        dz_ref[:, :W] = dzu.astype(dz_ref.dtype)
        dz_ref[:, W:] = dzv.astype(dz_ref.dtype)
        dlg_ref[...] += dlg
        dlb_ref[...] += dlb

    vec = jax.ShapeDtypeStruct((1, W), F32)
    return pl.pallas_call(
        body, name=name, grid=(S // SEQ_BLOCK,),
        in_specs=_sgu_specs(W, G) + [pl.BlockSpec((SEQ_BLOCK, W), lambda n: (n, 0))],
        out_specs=[
            pl.BlockSpec((SEQ_BLOCK, 2 * W), lambda n: (n, 0)),
            pl.BlockSpec((1, W), lambda n: (0, 0)),
            pl.BlockSpec((1, W), lambda n: (0, 0)),
            pl.BlockSpec((G, SEQ_BLOCK, SEQ_BLOCK), lambda n: (0, 0, 0)),
            pl.BlockSpec((SEQ_BLOCK, G), lambda n: (0, 0)),
        ],
        out_shape=[jax.ShapeDtypeStruct((S, W2), BF16), vec, vec, jax.ShapeDtypeStruct(ws.shape, F32), jax.ShapeDtypeStruct((SEQ_BLOCK, G), F32)],
        compiler_params=_cp(("arbitrary",)),
    )(zp, ln_g, ln_b, ws, bsT, selT, dgated)


def _rope_matrix():
    half = ROPE_DIM // 2
    R = np.zeros((SWA_HEAD_DIM, SWA_HEAD_DIM), np.float32)
    for j in range(half):
        R[j + half, j] = -1.0
        R[j, j + half] = 1.0
    return R


def _swa_mask_bias(G):
    B = SEQ_BLOCK
    qi = np.arange(G * B)[:, None] % B
    ki = np.arange(2 * B)[None, :] - B
    rel = qi - ki
    valid = (rel >= 0) & (rel < B)
    return np.where(np.stack([valid & (ki >= 0), valid]), 0.0, NEG).astype(np.float32)


def _rot3(t, r_bf16):
    hi = t.astype(BF16)
    rest = t - hi.astype(F32)
    mid = rest.astype(BF16)
    lo = (rest - mid.astype(F32)).astype(BF16)
    d = lambda piece: jnp.dot(piece, r_bf16, preferred_element_type=F32)
    return (d(hi) + d(mid)) + d(lo)


@jax.custom_vjp
def _rope_rot(t, r_bf16):
    return _rot3(t, r_bf16)


def _rope_rot_fwd(t, r_bf16):
    return _rot3(t, r_bf16), r_bf16


def _rope_rot_bwd(r_bf16, ct):
    return -_rot3(ct, r_bf16), jnp.zeros_like(r_bf16)


_rope_rot.defvjp(_rope_rot_fwd, _rope_rot_bwd)


def _swa_block(q4, kp, kc, vp, vc, sink, Cq, Sq, Cp, Sp, R, bias, G):
    B, Dh = SEQ_BLOCK, SWA_HEAD_DIM
    r_bf16 = R.astype(BF16)
    rot = lambda t: _rope_rot(t, r_bf16)
    q = q4.reshape(G * B, Dh)
    Cq4 = jnp.concatenate([Cq] * G, axis=0)
    Sq4 = jnp.concatenate([Sq] * G, axis=0)
    qr = q * Cq4 + rot(q) * Sq4
    kb = jnp.concatenate([kp * Cp + rot(kp) * Sp, kc * Cq + rot(kc) * Sq], axis=0)
    vb = jnp.concatenate([vp, vc], axis=0)
    s = lax.dot_general(qr.astype(BF16), kb.astype(BF16), _DIMS["nt"], preferred_element_type=F32) * (Dh ** -0.5) + bias
    m = lax.stop_gradient(jnp.maximum(jnp.max(s, axis=1, keepdims=True), sink))
    p = jnp.exp(s - m)
    p = p / (jnp.sum(p, axis=1, keepdims=True) + jnp.exp(sink - m))
    o = jnp.dot(p.astype(BF16), vb.astype(BF16), preferred_element_type=F32)
    return o.reshape(G, B, Dh)


SWA_HEADS_PER_STEP = 2


def _swa_specs(G, HP):
    B, Dh = SEQ_BLOCK, SWA_HEAD_DIM
    prev = lambda n: jnp.maximum(n - 1, 0)
    return [
        pl.BlockSpec((HP * G, B, Dh), lambda h, n: (h, n, 0)),
        pl.BlockSpec((HP, B, Dh), lambda h, n: (h, prev(n), 0)),
        pl.BlockSpec((HP, B, Dh), lambda h, n: (h, n, 0)),
        pl.BlockSpec((HP, B, Dh), lambda h, n: (h, prev(n), 0)),
        pl.BlockSpec((HP, B, Dh), lambda h, n: (h, n, 0)),
        pl.BlockSpec((HP, G * B, 1), lambda h, n: (h, 0, 0)),
        pl.BlockSpec((B, Dh), lambda h, n: (n, 0)),
        pl.BlockSpec((B, Dh), lambda h, n: (n, 0)),
        pl.BlockSpec((B, Dh), lambda h, n: (prev(n), 0)),
        pl.BlockSpec((B, Dh), lambda h, n: (prev(n), 0)),
        pl.BlockSpec((Dh, Dh), lambda h, n: (0, 0)),
        pl.BlockSpec((1, G * B, 2 * B), lambda h, n: (jnp.minimum(n, 1), 0, 0)),
    ]


def swa_fwd(qh, kh, vh, sink_col, C, Sn, R, bias, name):
    Hq, S, Dh = qh.shape
    Hk = kh.shape[0]
    G = Hq // Hk
    HP = _tile(Hk, SWA_HEADS_PER_STEP, 1)

    def body(q_ref, kp_ref, kc_ref, vp_ref, vc_ref, sk_ref, cq_ref, sq_ref, cp_ref, sp_ref, r_ref, b_ref, o_ref):
        for hp in range(HP):
            qs = slice(hp * G, (hp + 1) * G)
            o = _swa_block(q_ref[qs], kp_ref[hp], kc_ref[hp], vp_ref[hp], vc_ref[hp], sk_ref[hp], cq_ref[...], sq_ref[...],
                           cp_ref[...], sp_ref[...], r_ref[...], b_ref[0], G)
            o_ref[qs] = o.astype(o_ref.dtype)

    return pl.pallas_call(
        body, name=name, grid=(Hk // HP, S // SEQ_BLOCK), in_specs=_swa_specs(G, HP),
        out_specs=pl.BlockSpec((HP * G, SEQ_BLOCK, Dh), lambda h, n: (h, n, 0)),
        out_shape=jax.ShapeDtypeStruct((Hq, S, Dh), BF16), compiler_params=_cp(("parallel", "parallel")),
    )(qh, kh, kh, vh, vh, sink_col, C, Sn, C, Sn, R, bias)


def swa_bwd(qh, kh, vh, sink_col, C, Sn, R, bias, doh, name):
    Hq, S, Dh = qh.shape
    Hk = kh.shape[0]
    G = Hq // Hk
    B = SEQ_BLOCK
    HP = _tile(Hk, SWA_HEADS_PER_STEP, 1)

    def body(q_ref, kp_ref, kc_ref, vp_ref, vc_ref, sk_ref, cq_ref, sq_ref, cp_ref, sp_ref, r_ref, b_ref, do_ref,
             dq_ref, dkp_ref, dkc_ref, dvp_ref, dvc_ref, dsk_ref):
        @pl.when(pl.program_id(1) == 0)
        def _():
            dsk_ref[...] = jnp.zeros_like(dsk_ref)

        fn = lambda q4, kp, kc, vp, vc, sk: _swa_block(q4, kp, kc, vp, vc, sk, cq_ref[...], sq_ref[...], cp_ref[...], sp_ref[...],
                                                      r_ref[...], b_ref[0], G)
        for hp in range(HP):
            qs = slice(hp * G, (hp + 1) * G)
            _, vjp = jax.vjp(fn, q_ref[qs], kp_ref[hp], kc_ref[hp], vp_ref[hp], vc_ref[hp], sk_ref[hp])
            dq, dkp, dkc, dvp, dvc, dsk = vjp(do_ref[qs].astype(F32))
            dq_ref[qs] = dq
            dkp_ref[hp] = dkp
            dkc_ref[hp] = dkc
            dvp_ref[hp] = dvp
            dvc_ref[hp] = dvc
            for g in range(G):
                part = jnp.sum(dsk[g * B:(g + 1) * B], axis=0, keepdims=True)
                dsk_ref[hp, g:g + 1, :] += jnp.broadcast_to(part, (1, LANES))

    qspec = pl.BlockSpec((HP * G, B, Dh), lambda h, n: (h, n, 0))
    kspec = pl.BlockSpec((HP, B, Dh), lambda h, n: (h, n, 0))
    kshape = jax.ShapeDtypeStruct((Hk, S, Dh), F32)
    return pl.pallas_call(
        body, name=name, grid=(Hk // HP, S // B), in_specs=_swa_specs(G, HP) + [qspec],
        out_specs=[qspec, kspec, kspec, kspec, kspec, pl.BlockSpec((HP, G, LANES), lambda h, n: (h, 0, 0))],
        out_shape=[jax.ShapeDtypeStruct((Hq, S, Dh), F32), kshape, kshape, kshape, kshape, jax.ShapeDtypeStruct((Hk, G, LANES), F32)],
        compiler_params=_cp(("parallel", "arbitrary")),
    )(qh, kh, kh, vh, vh, sink_col, C, Sn, C, Sn, R, bias, doh)


def shift_add(cur, prev, name):
    Hk, S, Dh = cur.shape
    B = SEQ_BLOCK

    def body(c_ref, p_ref, o_ref):
        o_ref[0, :S - B] = c_ref[0, :S - B] + p_ref[0, B:]
        o_ref[0, S - B:] = c_ref[0, S - B:]

    spec = pl.BlockSpec((1, S, Dh), lambda h: (h, 0, 0))
    return pl.pallas_call(
        body, name=name, grid=(Hk,), in_specs=[spec, spec], out_specs=spec, out_shape=jax.ShapeDtypeStruct(cur.shape, F32),
        compiler_params=_cp(("parallel",)),
    )(cur, prev)


def loss_head(y, target, name):
    S, D = y.shape
    tr = _tile(S, ROW_TILE, 16)

    def body(y_ref, t_ref, acc_ref, dy_ref):
        err = y_ref[...] - t_ref[...]
        dy_ref[...] = err * (1.0 / D)

        @pl.when(pl.program_id(0) == 0)
        def _():
            acc_ref[...] = jnp.zeros_like(acc_ref)

        acc_ref[...] += jnp.broadcast_to(jnp.sum(err * err).reshape(1, 1), (1, LANES))

    return pl.pallas_call(
        body, name=name, grid=(S // tr,), in_specs=[_row_spec(tr, D)] * 2,
        out_specs=[pl.BlockSpec((1, LANES), lambda i: (0, 0)), _row_spec(tr, D)],
        out_shape=[jax.ShapeDtypeStruct((1, LANES), F32), jax.ShapeDtypeStruct((S, D), F32)], compiler_params=_cp(("arbitrary",)),
    )(y, target)


def _adam_update(w, g, m, v):
    m = ADAM_B1 * m + (1.0 - ADAM_B1) * g
    v = ADAM_B2 * v + (1.0 - ADAM_B2) * jnp.square(g)
    m_hat = m / (1.0 - ADAM_B1 ** ADAM_STEP)
    v_hat = v / (1.0 - ADAM_B2 ** ADAM_STEP)
    delta = -ADAM_LR * (m_hat / (jnp.sqrt(v_hat) + ADAM_EPS) + ADAM_WD * w)
    return delta, m, v


def adamw(w, m, v, gparts, name, gstack=0, emit_g=True):
    R, C = w.shape
    tr = _tile(R, max(8, (128 * 1024) // C), 8)
    spec = pl.BlockSpec((tr, C), lambda i: (i, 0))
    nplain = len(gparts) - (1 if gstack else 0)
    nout = 4 if emit_g else 3

    def body(w_ref, m_ref, v_ref, *rest):
        g_refs, outs = rest[:len(gparts)], rest[len(gparts):]
        g = None
        for r in g_refs[:nplain]:
            g = r[...].astype(F32) if g is None else g + r[...].astype(F32)
        if gstack:
            for t in range(gstack):
                part = g_refs[-1][t].astype(F32)
                g = part if g is None else g + part
        res = _adam_update(w_ref[...], g, m_ref[...], v_ref[...])
        for o_ref, val in zip(outs, ((g,) if emit_g else ()) + res):
            o_ref[...] = val

    gspecs = [spec] * nplain + ([pl.BlockSpec((gstack, tr, C), lambda i: (0, i, 0))] if gstack else [])
    out = jax.ShapeDtypeStruct((R, C), F32)
    return pl.pallas_call(
        body, name=name, grid=(R // tr,), in_specs=[spec] * 3 + gspecs, out_specs=[spec] * nout, out_shape=[out] * nout,
        compiler_params=_cp(("parallel",)),
    )(w, m, v, *gparts)


def ada_fwd(c_all, ada_w, ada_b, name):
    L, D, N = ada_w.shape
    Bp = c_all.shape[0]
    tn = _tile(N, 512)

    def body(c_ref, w_ref, b_ref, o_ref):
        ca = jax.nn.silu(c_ref[...]).astype(BF16)
        o_ref[0] = jnp.dot(ca, w_ref[0].astype(BF16), preferred_element_type=F32) + b_ref[0]

    return pl.pallas_call(
        body, name=name, grid=(L, N // tn),
        in_specs=[pl.BlockSpec((Bp, D), lambda l, j: (0, 0)), pl.BlockSpec((1, D, tn), lambda l, j: (l, 0, j)),
                  pl.BlockSpec((1, 1, tn), lambda l, j: (l, 0, j))],
        out_specs=pl.BlockSpec((1, Bp, tn), lambda l, j: (l, 0, j)), out_shape=jax.ShapeDtypeStruct((L, Bp, N), F32),
        compiler_params=_cp(("parallel", "parallel")),
    )(c_all, ada_w, ada_b)


def ada_wgrad(c_all, dmod, name):
    L, Bp, N = dmod.shape
    D = c_all.shape[1]
    tn = _tile(N, 512)

    def body(c_ref, d_ref, o_ref):
        ca = jax.nn.silu(c_ref[...]).astype(BF16)
        o_ref[0] = lax.dot_general(ca, d_ref[0].astype(BF16), _DIMS["tn"], preferred_element_type=F32)

    return pl.pallas_call(
        body, name=name, grid=(L, N // tn),
        in_specs=[pl.BlockSpec((Bp, D), lambda l, j: (0, 0)), pl.BlockSpec((1, Bp, tn), lambda l, j: (l, 0, j))],
        out_specs=pl.BlockSpec((1, D, tn), lambda l, j: (l, 0, j)), out_shape=jax.ShapeDtypeStruct((L, D, N), F32),
        compiler_params=_cp(("parallel", "parallel")),
    )(c_all, dmod)


N_DEV = 8
N_CHIP = 4
ANY = pl.BlockSpec(memory_space=pl.ANY)


def _place():
    return lax.axis_index("x"), lax.axis_index("y"), lax.axis_index("c")


def _other_chips(x, y):
    chips = [(1 - x, y), (x, 1 - y), (1 - x, 1 - y)]
    return chips, [2 * cx + cy for cx, cy in chips]


def _rcopy(src, dst, ssem, rsem, to):
    return pltpu.make_async_remote_copy(src_ref=src, dst_ref=dst, send_sem=ssem, recv_sem=rsem, device_id=to, device_id_type=MESH)


def ag_small(xs, name):
    R, Wd = xs.shape

    def body(x_ref, out_ref, send_sems, recv_sems, local_sem):
        x, y, c = _place()
        me, sibling = (x, y, c), (x, y, 1 - c)
        chips, _ = _other_chips(x, y)

        def slot(px, py, pc):
            return out_ref.at[4 * px + 2 * py + pc]

        def copy(k, block, to, src=None):
            return _rcopy(slot(*block) if src is None else src, slot(*block), send_sems.at[k], recv_sems.at[k], to)

        mine = pltpu.make_async_copy(x_ref, slot(*me), local_sem)
        mine.start()
        first = [copy(0, me, sibling, src=x_ref)]
        first += [copy(1 + j, me, (*chip, c), src=x_ref) for j, chip in enumerate(chips)]
        for cp in first:
            cp.start()
        passed = [copy(4 + j, (*chip, c), sibling) for j, chip in enumerate(chips)]
        for j, chip in enumerate(chips):
            copy(1 + j, (*chip, c), me).wait_recv()
            passed[j].start()
        copy(0, sibling, me).wait_recv()
        for j, chip in enumerate(chips):
            copy(4 + j, (*chip, 1 - c), me).wait_recv()
        for cp in first + passed:
            cp.wait_send()
        mine.wait()

    vm = pl.BlockSpec(memory_space=pltpu.VMEM)
    return pl.pallas_call(
        body, name=name, out_shape=jax.ShapeDtypeStruct((N_DEV, R, Wd), xs.dtype), in_specs=[vm], out_specs=vm,
        scratch_shapes=[pltpu.SemaphoreType.DMA((7,)), pltpu.SemaphoreType.DMA((7,)), pltpu.SemaphoreType.DMA],
        compiler_params=_cp(),
    )(xs)


def _half_of_shard(by_cols, A, B, h):
    return (h * (A // 2), A // 2, 0, B) if by_cols else (0, A, h * (B // 2), B // 2)


def _shard_in_full(by_cols, A, B, q):
    return (0, q * B) if by_cols else (q * A, 0)


def _window(ref, r0, nr, c0, nc):
    return ref.at[:, pl.ds(r0, nr), pl.ds(c0, nc)]


def ag_weights(shards, by_cols, name):
    n = len(shards)
    geo, full = _ag_shapes(shards, by_cols)

    def body(*refs):
        start, finish = _ag_steps(geo, refs[:n], refs[n:2 * n], refs[2 * n:])
        start()
        finish()

    return pl.pallas_call(
        body, name=name, out_shape=full, in_specs=[ANY] * n, out_specs=[ANY] * n, scratch_shapes=_ag_semaphores(n),
        compiler_params=_cp(),
    )(*shards)


def _ag_shapes(shards, by_cols):
    geo = [(bc,) + s.shape[1:] for bc, s in zip(by_cols, shards)]
    full = [jax.ShapeDtypeStruct((s.shape[0], A, N_CHIP * B) if bc else (s.shape[0], N_CHIP * A, B), s.dtype)
            for (bc, A, B), s in zip(geo, shards)]
    return geo, full


def _ag_semaphores(n):
    return [pltpu.SemaphoreType.DMA((n, 3)) for _ in range(4)]


def _ag_steps(geo, x_refs, o_refs, sems):
    s_ici, r_ici, s_d2d, r_d2d = sems
    pairs = [(t, j) for t in range(len(geo)) for j in range(3)]

    def copies():
        x, y, c = _place()
        q = 2 * x + y
        chips, qs = _other_chips(x, y)

        def landing(t, chip_q, half):
            r0, nr, c0, nc = _half_of_shard(*geo[t], half)
            ro, co = _shard_in_full(*geo[t], chip_q)
            return _window(o_refs[t], ro + r0, nr, co + c0, nc)

        def ici(t, j, landing_q):
            src = _window(x_refs[t], *_half_of_shard(*geo[t], c))
            return _rcopy(src, landing(t, landing_q, c), s_ici.at[t, j], r_ici.at[t, j], (*chips[j], c))

        def handoff(t, j, half):
            blk = landing(t, qs[j], half)
            return _rcopy(blk, blk, s_d2d.at[t, j], r_d2d.at[t, j], (x, y, 1 - c))

        return c, q, qs, ici, handoff

    def start():
        c, q, qs, ici, handoff = copies()
        for t, j in pairs:
            ici(t, j, q).start()

    def finish():
        c, q, qs, ici, handoff = copies()
        for t, j in pairs:
            ici(t, j, qs[j]).wait_recv()
            handoff(t, j, c).start()
        for t, j in pairs:
            handoff(t, j, 1 - c).wait_recv()
        for t, j in pairs:
            ici(t, j, q).wait_send()
            handoff(t, j, c).wait_send()

    return start, finish


def _half_of_full(by_cols, A, B, h):
    return (h * (A // 2), A // 2, 0, N_CHIP * B) if by_cols else (0, N_CHIP * A, h * (B // 2), B // 2)


def _half_shape(by_cols, L, A, B):
    return (L, A // 2, N_CHIP * B) if by_cols else (L, N_CHIP * A, B // 2)


def _piece_shape(by_cols, L, A, B):
    return (L, A // 2, B) if by_cols else (L, A, B // 2)


def sibling_fold(gs, geo, name):
    sc = fold_sidecar(gs, geo)
    n = len(gs)

    def body(*refs):
        start, finish = sc.steps(refs[:n], refs[n:2 * n], refs[2 * n:])
        start()
        finish()

    return pl.pallas_call(
        body, name=name, out_shape=sc.out_shape, in_specs=[ANY] * n, out_specs=[ANY] * n, scratch_shapes=sc.semaphores,
        compiler_params=_cp(),
    )(*gs)


def fold_sidecar(gs, geo):
    n = len(gs)

    def steps(x_refs, o_refs, sems):
        ssem, rsem = sems

        def copies():
            x, y, c = _place()
            return [_rcopy(_window(x_refs[t], *_half_of_full(*geo[t], 1 - c)), o_refs[t], ssem.at[t], rsem.at[t], (x, y, 1 - c))
                    for t in range(n)]

        def start():
            for cp in copies():
                cp.start()

        def finish():
            for cp in copies():
                cp.wait()

        return start, finish

    dma = pltpu.SemaphoreType.DMA
    out = [jax.ShapeDtypeStruct(_half_shape(bc, g.shape[0], A, B), g.dtype) for (bc, A, B), g in zip(geo, gs)]
    return SideCar(list(gs), out, [dma((n,)), dma((n,))], steps)


def chip_exchange(rs, geo, name):
    sc = exchange_sidecar(rs, geo)
    n = len(rs)

    def body(*refs):
        start, finish = sc.steps(refs[:n], refs[n:2 * n], refs[2 * n:])
        start()
        finish()

    return pl.pallas_call(
        body, name=name, out_shape=sc.out_shape, in_specs=[ANY] * n, out_specs=[ANY] * n, scratch_shapes=sc.semaphores,
        compiler_params=_cp(),
    )(*rs)


def exchange_sidecar(rs, geo):
    n = len(rs)

    def steps(x_refs, o_refs, sems):
        ssem, rsem = sems

        def copies():
            x, y, c = _place()
            chips, qs = _other_chips(x, y)

            def part(t, chip_q):
                bc, A, B = geo[t]
                return _window(x_refs[t], 0, A // 2, chip_q * B, B) if bc else _window(x_refs[t], chip_q * A, A, 0, B // 2)

            return [_rcopy(part(t, qs[j]), o_refs[t].at[j], ssem.at[t, j], rsem.at[t, j], (*chips[j], c))
                    for t in range(n) for j in range(3)]

        def start():
            for cp in copies():
                cp.start()

        def finish():
            for cp in copies():
                cp.wait()

        return start, finish

    dma = pltpu.SemaphoreType.DMA
    out = [jax.ShapeDtypeStruct((3,) + _piece_shape(bc, r.shape[0], A, B), r.dtype) for (bc, A, B), r in zip(geo, rs)]
    return SideCar(list(rs), out, [dma((n, 3)), dma((n, 3))], steps)


def ag_sidecar(shards, by_cols):
    geo, full = _ag_shapes(shards, by_cols)
    return SideCar(list(shards), full, _ag_semaphores(len(shards)), lambda ins, outs, sems: _ag_steps(geo, ins, outs, sems))


def sibling_share(fs, geo, name):
    n = len(fs)

    def body(*refs):
        x_refs, o_refs, (ssem, rsem) = refs[:n], refs[n:2 * n], refs[2 * n:]
        x, y, c = _place()
        for t in range(n):
            mine = _window(o_refs[t], *_half_of_shard(*geo[t], c))
            _rcopy(mine, mine, ssem.at[t], rsem.at[t], (x, y, 1 - c)).start()
        for t in range(n):
            mine = _window(o_refs[t], *_half_of_shard(*geo[t], c))
            theirs = _window(o_refs[t], *_half_of_shard(*geo[t], 1 - c))
            _rcopy(mine, theirs, ssem.at[t], rsem.at[t], (x, y, 1 - c)).wait_recv()
            _rcopy(mine, mine, ssem.at[t], rsem.at[t], (x, y, 1 - c)).wait_send()
        del x_refs

    dma = pltpu.SemaphoreType.DMA
    return pl.pallas_call(
        body, name=name, out_shape=[jax.ShapeDtypeStruct(f.shape, f.dtype) for f in fs], in_specs=[ANY] * n, out_specs=[ANY] * n,
        input_output_aliases={t: t for t in range(n)}, scratch_shapes=[dma((n,)), dma((n,))], compiler_params=_cp(),
    )(*fs)


SUM_ROWS = 256


def fold_sum(g, recv, by_cols, A, B, qc_idx, name):
    L = g.shape[0]
    _, hr, hc = _half_shape(by_cols, L, A, B)
    tr, tc = _tile(A // 2 if by_cols else A, SUM_ROWS, 16), (B if by_cols else B // 2)
    ro, co = ((A // 2) // tr, 0) if by_cols else (0, 1)

    def body(qc_ref, g_ref, r_ref, o_ref):
        del qc_ref
        o_ref[...] = (g_ref[...].astype(F32) + r_ref[...].astype(F32)).astype(o_ref.dtype)

    spec = pl.BlockSpec((1, tr, tc), lambda l, i, j, qc: (l, i, j))
    grid_spec = pltpu.PrefetchScalarGridSpec(
        num_scalar_prefetch=1, grid=(L, hr // tr, hc // tc),
        in_specs=[pl.BlockSpec((1, tr, tc), lambda l, i, j, qc: (l, i + qc[1] * ro, j + qc[1] * co)), spec], out_specs=spec,
    )
    return pl.pallas_call(
        body, name=name, grid_spec=grid_spec, out_shape=jax.ShapeDtypeStruct((L, hr, hc), BF16),
        compiler_params=_cp(("parallel", "parallel", "parallel")),
    )(qc_idx, g, recv)


def chip_sum(r, ex, by_cols, A, B, qc_idx, name):
    L = r.shape[0]
    _, wr, wc = _piece_shape(by_cols, L, A, B)
    tr = _tile(wr, SUM_ROWS, 16)
    r_ro, r_co = (0, 1) if by_cols else (A // tr, 0)
    o_ro, o_co = ((A // 2) // tr, 0) if by_cols else (0, 1)

    def body(qc_ref, r_ref, e_ref, o_ref):
        del qc_ref
        o_ref[0] = ((r_ref[0].astype(F32) + e_ref[0, 0].astype(F32)) + e_ref[1, 0].astype(F32)) + e_ref[2, 0].astype(F32)

    grid_spec = pltpu.PrefetchScalarGridSpec(
        num_scalar_prefetch=1, grid=(L, wr // tr),
        in_specs=[pl.BlockSpec((1, tr, wc), lambda l, i, qc: (l, i + qc[0] * r_ro, qc[0] * r_co)),
                  pl.BlockSpec((3, 1, tr, wc), lambda l, i, qc: (0, l, i, 0))],
        out_specs=pl.BlockSpec((1, tr, wc), lambda l, i, qc: (l, i + qc[1] * o_ro, qc[1] * o_co)),
    )
    return pl.pallas_call(
        body, name=name, grid_spec=grid_spec, out_shape=jax.ShapeDtypeStruct((L, A, B), F32),
        compiler_params=_cp(("parallel", "parallel")),
    )(qc_idx, r, ex)


BIG = ("ffn_w_gu", "ffn_w_down", "fox_w_in", "fox_w_out", "sgu_w_in", "sgu_w_out", "swa_w_in", "swa_w_out")
COLUMN_SHARDED = ("ffn_w_gu", "fox_w_in", "sgu_w_in", "swa_w_in")
SMALL = ("ada_b", "mix_pre_g", "mix_post_g", "ffn_pre_g", "ffn_post_g", "fox_b_f", "sgu_ln_g", "sgu_ln_b", "sgu_w_s", "sgu_b_s",
         "swa_sinks")
WEIGHTS = ("ada_w", "ada_b", "mix_pre_g", "mix_post_g", "ffn_pre_g", "ffn_post_g", "ffn_w_gu", "ffn_w_down", "fox_w_in", "fox_b_f",
           "fox_w_out", "sgu_w_in", "sgu_ln_g", "sgu_ln_b", "sgu_w_s", "sgu_b_s", "sgu_w_out", "swa_w_in", "swa_sinks", "swa_w_out")
INPUTS = ("x", "c", "positions") + WEIGHTS + ("loss_target",) + tuple("m_" + n for n in WEIGHTS) + tuple("v_" + n for n in WEIGHTS)


def _lane_pad(n):
    return (-n) % LANES


def _pad_shard_columns(t, B):
    if _lane_pad(B) == 0:
        return t
    L, A, _ = t.shape
    return jnp.pad(t.reshape(L, A, N_CHIP, B), ((0, 0), (0, 0), (0, 0), (0, _lane_pad(B)))).reshape(L, A, -1)


def _unpad_shard_columns(t, B):
    if _lane_pad(B) == 0:
        return t
    L, A, _ = t.shape
    return t.reshape(L, A, N_CHIP, B + _lane_pad(B))[..., :B].reshape(L, A, N_CHIP * B)


def place_shard(full, shard, by_cols, qc_idx, name):
    L, A, B = shard.shape
    tr = _tile(A, SUM_ROWS, 16)
    ro, co = (0, 1) if by_cols else (A // tr, 0)

    def body(qc_ref, s_ref, f_ref, o_ref):
        del qc_ref, f_ref
        o_ref[...] = s_ref[...]

    grid_spec = pltpu.PrefetchScalarGridSpec(
        num_scalar_prefetch=1, grid=(L, A // tr), in_specs=[pl.BlockSpec((1, tr, B), lambda l, i, qc: (l, i, 0)), ANY],
        out_specs=pl.BlockSpec((1, tr, B), lambda l, i, qc: (l, i + qc[0] * ro, qc[0] * co)),
    )
    return pl.pallas_call(
        body, name=name, grid_spec=grid_spec, out_shape=jax.ShapeDtypeStruct(full.shape, full.dtype),
        input_output_aliases={2: 0}, compiler_params=_cp(("parallel", "parallel")),
    )(qc_idx, shard, full)


def _pad_rows(flat1d):
    n = flat1d.shape[0]
    pad = (-n) % (8 * LANES)
    return jnp.pad(flat1d, (0, pad)).reshape(-1, LANES)


def _pack_small(parts):
    return jnp.concatenate([_pad_rows(parts[n].astype(F32).reshape(-1)) for n in SMALL], axis=0)


def _unpack_small(packed, shapes):
    out, off = {}, 0
    for n in SMALL:
        size = int(np.prod(shapes[n]))
        rows = (size + 8 * LANES - 1) // (8 * LANES) * 8
        out[n] = packed[off:off + rows].reshape(-1)[:size].reshape(shapes[n])
        off += rows
    return out


def _fox_fwd(h, w_in, b_f, w_out, tag, ride_qkv=None, ride_attn=None):
    S, D = h.shape
    H = b_f.shape[0]
    qkv = mm(h, w_in, "nn", BF16, name=tag + "_qkv", b_cols=(0, 3 * D), sidecar=ride_qkv)
    qkv, rode_qkv = qkv if ride_qkv is not None else (qkv, ())
    fgp = mm(h, w_in, "nn", F32, name=tag + "_fg", b_cols=(3 * D, LANES))
    fgT = fgp[:, :H].T
    cum = fox_gate_fwd(fgT, b_f.reshape(H, 1), tag + "_gate")
    cum_col, cum_row = cum.reshape(H, S, 1), cum.reshape(H, 1, S)
    o, lse, rode_attn = fox_attn_fwd(qkv, cum_col, cum_row, H, tag + "_attn", ride_attn)
    y = mm(o, w_out, "nn", F32, name=tag + "_out")
    return y, (qkv, fgT, cum_col, cum_row, o, lse), rode_qkv, rode_attn


def _fox_bwd(dy, h, w_in, b_f, w_out, ctx, tag, ride_dq=None, ride_dkv=None):
    qkv, fgT, cum_col, cum_row, o, lse = ctx
    S, D = h.shape
    H = b_f.shape[0]
    do = mm(dy, w_out, "nt", BF16, name=tag + "_do")
    dw_out = mm(o, dy, "tn", BF16, name=tag + "_dwout")
    dq, delta, rode_dq = fox_attn_bwd_dq(qkv, do, o, lse, cum_col, cum_row, H, tag + "_dq", ride_dq)
    dk, dv, dcum, rode_dkv = fox_attn_bwd_dkv(qkv, do, lse.reshape(H, 1, S), delta.reshape(H, 1, S), cum_col, cum_row, H,
                                              tag + "_dkv", ride_dkv)
    dfgT, db = fox_gate_bwd(dcum.reshape(H, S), fgT, b_f.reshape(H, 1), tag + "_dgate")
    dfgp = jnp.pad(dfgT.T, ((0, 0), (0, LANES - H))).astype(BF16)
    dh = mm(dq, w_in, "nt", F32, name=tag + "_dhq", b_cols=(0, D))
    dh = mm(dk, w_in, "nt", F32, add=dh, name=tag + "_dhk", b_cols=(D, D))
    dh = mm(dv, w_in, "nt", F32, add=dh, name=tag + "_dhv", b_cols=(2 * D, D))
    dh = mm(dfgp, w_in, "nt", F32, add=dh, name=tag + "_dhf", b_cols=(3 * D, LANES))
    dw_in = jnp.concatenate(
        [mm(h, dq, "tn", BF16, name=tag + "_dwq"), mm(h, dk, "tn", BF16, name=tag + "_dwk"), mm(h, dv, "tn", BF16, name=tag + "_dwv"),
         mm(h, dfgp, "tn", BF16, name=tag + "_dwf")[:, :H]], axis=1)
    return dh, dw_in, dw_out, db[:, 0], rode_dq, rode_dkv


def _sgu_consts(G, W):
    return jnp.asarray(np.repeat(np.eye(G, dtype=np.float32), W // G, axis=1))


def _sgu_fwd(h, w_in, ln_g, ln_b, w_s, b_s, w_out, tag):
    G, W = w_s.shape[0], ln_g.shape[0]
    zp = mm(h, w_in, "nn", F32, name=tag + "_in")
    args = (zp, ln_g.reshape(1, W), ln_b.reshape(1, W), w_s, b_s.T, _sgu_consts(G, W))
    gated = sgu_fwd(*args, tag + "_core")
    y = mm(gated, w_out, "nn", F32, name=tag + "_out")
    return y, (args, gated)


def _sgu_bwd(dy, h, w_in, w_out, ctx, tag):
    args, gated = ctx
    dgated = mm(dy, w_out, "nt", BF16, name=tag + "_dgated")
    dw_out = mm(gated, dy, "tn", BF16, name=tag + "_dwout")
    dzp, dlg, dlb, dws, dbsT = sgu_bwd(*args, dgated, tag + "_dcore")
    dh = mm(dzp, w_in, "nt", F32, name=tag + "_dh")
    dw_in = mm(h, dzp, "tn", BF16, name=tag + "_dwin")
    return dh, dw_in, dw_out, dlg[0], dlb[0], dws, dbsT.T


def _rope_tables(positions):
    inv = ROPE_THETA ** (-jnp.arange(0, ROPE_DIM, 2, dtype=F32) / ROPE_DIM)
    ang = positions.astype(F32)[:, None] * inv
    S = positions.shape[0]
    rest = SWA_HEAD_DIM - ROPE_DIM
    C = jnp.concatenate([jnp.cos(ang), jnp.cos(ang), jnp.ones((S, rest), F32)], axis=1)
    Sn = jnp.concatenate([jnp.sin(ang), jnp.sin(ang), jnp.zeros((S, rest), F32)], axis=1)
    return C, Sn


def _heads(t, n):
    return t.reshape(t.shape[0], n, SWA_HEAD_DIM).transpose(1, 0, 2)


def _unheads(t):
    return t.transpose(1, 0, 2).reshape(t.shape[1], -1)


def _swa_fwd(h, w_in, sinks, w_out, tables, tag):
    Hq = sinks.shape[0]
    Hk = (w_in[0].shape[-1] // SWA_HEAD_DIM - Hq) // 2
    G = Hq // Hk
    proj = mm(h, w_in, "nn", F32, name=tag + "_in")
    qh = _heads(proj[:, :Hq * SWA_HEAD_DIM], Hq)
    kh = _heads(proj[:, Hq * SWA_HEAD_DIM:(Hq + Hk) * SWA_HEAD_DIM], Hk)
    vh = _heads(proj[:, (Hq + Hk) * SWA_HEAD_DIM:], Hk)
    sink_col = jnp.repeat(sinks.reshape(Hk, G), SEQ_BLOCK, axis=1).reshape(Hk, G * SEQ_BLOCK, 1)
    args = (qh, kh, vh, sink_col, tables[0], tables[1], jnp.asarray(_rope_matrix()), jnp.asarray(_swa_mask_bias(G)))
    o = _unheads(swa_fwd(*args, tag + "_core"))
    y = mm(o, w_out, "nn", F32, name=tag + "_out")
    return y, (args, o)


def _swa_bwd(dy, h, w_in, w_out, ctx, tag):
    args, o = ctx
    Hq = args[0].shape[0]
    do = mm(dy, w_out, "nt", BF16, name=tag + "_do")
    dw_out = mm(o, dy, "tn", BF16, name=tag + "_dwout")
    dqh, dkp, dkc, dvp, dvc, dsk = swa_bwd(*args, _heads(do, Hq), tag + "_dcore")
    dk = shift_add(dkc, dkp, tag + "_dk")
    dv = shift_add(dvc, dvp, tag + "_dv")
    dproj = jnp.concatenate([_unheads(dqh), _unheads(dk), _unheads(dv)], axis=1).astype(BF16)
    dh = mm(dproj, w_in, "nt", F32, name=tag + "_dh")
    dw_in = mm(h, dproj, "tn", BF16, name=tag + "_dwin")
    return dh, dw_in, dw_out, dsk[:, :, 0].reshape(Hq)


def kernel(x, c, positions, ada_w, ada_b, mix_pre_g, mix_post_g, ffn_pre_g, ffn_post_g, ffn_w_gu, ffn_w_down, fox_w_in, fox_b_f, fox_w_out, sgu_w_in, sgu_ln_g, sgu_ln_b, sgu_w_s, sgu_b_s, sgu_w_out, swa_w_in, swa_sinks, swa_w_out, loss_target, m_ada_w, m_ada_b, m_mix_pre_g, m_mix_post_g, m_ffn_pre_g, m_ffn_post_g, m_ffn_w_gu, m_ffn_w_down, m_fox_w_in, m_fox_b_f, m_fox_w_out, m_sgu_w_in, m_sgu_ln_g, m_sgu_ln_b, m_sgu_w_s, m_sgu_b_s, m_sgu_w_out, m_swa_w_in, m_swa_sinks, m_swa_w_out, v_ada_w, v_ada_b, v_mix_pre_g, v_mix_post_g, v_ffn_pre_g, v_ffn_post_g, v_ffn_w_gu, v_ffn_w_down, v_fox_w_in, v_fox_b_f, v_fox_w_out, v_sgu_w_in, v_sgu_ln_g, v_sgu_ln_b, v_sgu_w_s, v_sgu_b_s, v_sgu_w_out, v_swa_w_in, v_swa_sinks, v_swa_w_out):
    P = dict(zip(INPUTS, (x, c, positions, ada_w, ada_b, mix_pre_g, mix_post_g, ffn_pre_g, ffn_post_g, ffn_w_gu, ffn_w_down, fox_w_in, fox_b_f, fox_w_out, sgu_w_in, sgu_ln_g, sgu_ln_b, sgu_w_s, sgu_b_s, sgu_w_out, swa_w_in, swa_sinks, swa_w_out, loss_target, m_ada_w, m_ada_b, m_mix_pre_g, m_mix_post_g, m_ffn_pre_g, m_ffn_post_g, m_ffn_w_gu, m_ffn_w_down, m_fox_w_in, m_fox_b_f, m_fox_w_out, m_sgu_w_in, m_sgu_ln_g, m_sgu_ln_b, m_sgu_w_s, m_sgu_b_s, m_sgu_w_out, m_swa_w_in, m_swa_sinks, m_swa_w_out, v_ada_w, v_ada_b, v_mix_pre_g, v_mix_post_g, v_ffn_pre_g, v_ffn_post_g, v_ffn_w_gu, v_ffn_w_down, v_fox_w_in, v_fox_b_f, v_fox_w_out, v_sgu_w_in, v_sgu_ln_g, v_sgu_ln_b, v_sgu_w_s, v_sgu_b_s, v_sgu_w_out, v_swa_w_in, v_swa_sinks, v_swa_w_out)))
    xs, target, pos = x[0], loss_target[0], positions[0]
    S, D = xs.shape
    L = ada_w.shape[0]
    n_mix = 3
    F = ffn_w_down.shape[1] * N_CHIP
    xi, yi, ci = _place()
    q_me = 2 * xi + yi
    dev = 4 * xi + 2 * yi + ci

    qc = jnp.stack([q_me, ci]).astype(jnp.int32)
    by_cols = {n: n in COLUMN_SHARDED for n in BIG}
    geo = {n: (by_cols[n], P[n].shape[1], P[n].shape[2] + (_lane_pad(P[n].shape[2]) if by_cols[n] else 0)) for n in BIG}
    groups = {
        "fox0": {n: (0, 1) for n in BIG if n.startswith("fox_")},
        "ffn0": {n: (0, 1) for n in BIG if n.startswith("ffn_")},
        "rest": {n: (1 if n.startswith(("fox_", "ffn_")) else 0, P[n].shape[0]) for n in BIG},
    }
    groups["rest"] = {n: r for n, r in groups["rest"].items() if r[1] > r[0]}

    def shard_of(n, lo, hi):
        s = P[n][lo:hi].astype(BF16)
        return jnp.pad(s, ((0, 0), (0, 0), (0, _lane_pad(s.shape[2])))) if by_cols[n] else s

    Wt = {n: [] for n in BIG}

    def finish_gather(g, fulls):
        for n, s, f in zip(groups[g], shards[g], fulls):
            f = place_shard(f, s, by_cols[n], qc, f"place_{g}_{n}")
            f = _unpad_shard_columns(f, P[n].shape[2]) if by_cols[n] else f
            f = jnp.pad(f, ((0, 0), (0, 0), (0, 3 * D + LANES - f.shape[2]))) if n == "fox_w_in" else f
            Wt[n] += [(f, l) for l in range(f.shape[0])]

    shards = {g: [shard_of(n, *r) for n, r in groups[g].items()] for g in groups}
    group_cols = {g: [by_cols[n] for n in groups[g]] for g in groups}
    finish_gather("fox0", ag_weights(shards["fox0"], group_cols["fox0"], "ag_weights_fox0"))

    c_all = ag_small(c.reshape(D // LANES, LANES), "ag_c").reshape(N_DEV, D)
    c_all = jnp.pad(c_all, ((0, 16 - N_DEV), (0, 0)))
    Nm = ada_w.shape[2]
    ada_b_mine = lax.dynamic_slice_in_dim(ada_b, q_me * Nm, Nm, axis=1).reshape(L, 1, Nm)
    modp = ada_fwd(c_all, ada_w, ada_b_mine, "ada_fwd")[:, :N_DEV]
    mod_all = ag_small(modp.reshape(-1, LANES), "ag_mod").reshape(N_DEV, L, N_DEV, Nm)
    mod_mine = lax.dynamic_index_in_dim(mod_all[0::2], dev, axis=2, keepdims=False)
    mods = mod_mine.transpose(1, 0, 2).reshape(L, 6, 1, D)

    tables = _rope_tables(pos)

    saved = []
    xc = xs
    for i in range(L):
        kind, j = i % n_mix, i // n_mix
        sh_m, sc_m, g_m, sh_f, sc_f, g_f = (mods[i, t] for t in range(6))
        t = f"l{i}"
        h1 = pre_fwd(xc, mix_pre_g[i:i + 1], sh_m, sc_m, t + "_pre_m")
        if kind == 0:
            rides = [ag_sidecar(shards[g], group_cols[g]) if i == 0 else None for g in ("ffn0", "rest")]
            y1, ctx, ffn0_fulls, rest_fulls = _fox_fwd(h1, Wt["fox_w_in"][j], fox_b_f[j], Wt["fox_w_out"][j], t + "_fox", *rides)
            if i == 0:
                finish_gather("ffn0", ffn0_fulls)
                finish_gather("rest", rest_fulls)
        elif kind == 1:
            y1, ctx = _sgu_fwd(h1, Wt["sgu_w_in"][j], sgu_ln_g[j], sgu_ln_b[j], sgu_w_s[j], sgu_b_s[j], Wt["sgu_w_out"][j], t + "_sgu")
        else:
            y1, ctx = _swa_fwd(h1, Wt["swa_w_in"][j], swa_sinks[j], Wt["swa_w_out"][j], tables, t + "_swa")
        xm = post_fwd(xc, y1, mix_post_g[i:i + 1], g_m, t + "_post_m")
        h2 = pre_fwd(xm, ffn_pre_g[i:i + 1], sh_f, sc_f, t + "_pre_f")
        gu = mm(h2, Wt["ffn_w_gu"][i], "nn", BF16, name=t + "_ffn_gu")
        a = act_fwd(gu, t + "_act")
        y2 = mm(a, Wt["ffn_w_down"][i], "nn", F32, name=t + "_ffn_down")
        xn = post_fwd(xm, y2, ffn_post_g[i:i + 1], g_f, t + "_post_f")
        saved.append((xc, h1, y1, ctx, xm, h2, gu, a, y2))
        xc = xn

    sq, dx = loss_head(xc, target, "loss_head")
    loss = lax.psum(sq[0, 0] * (0.5 / D), ("x", "y", "c"))

    big_g = {n: [None] * P[n].shape[0] for n in BIG}
    small_g = {n: [None] * P[n].shape[0] for n in SMALL}
    group_geo = {g: [geo[n] for n in groups[g]] for g in groups}
    group_grads, from_sibling, chip_part, from_chips = {}, {}, {}, {}

    def grads_of(g):
        if g not in group_grads:
            gs = [jnp.stack(big_g[n][lo:hi]) for n, (lo, hi) in groups[g].items()]
            group_grads[g] = [_pad_shard_columns(t, P[n].shape[2]) if by_cols[n] else t for n, t in zip(groups[g], gs)]
        return group_grads[g]

    def fold(g):
        if g not in from_sibling:
            from_sibling[g] = sibling_fold(grads_of(g), group_geo[g], "rs_fold_" + g)
        chip_part[g] = [fold_sum(t, r, *m, qc, f"rs_fold_sum_{g}_{n}")
                        for n, t, r, m in zip(groups[g], grads_of(g), from_sibling[g], group_geo[g])]
        return exchange_sidecar(chip_part[g], group_geo[g])

    for i in reversed(range(L)):
        kind, j = i % n_mix, i // n_mix
        sh_m, sc_m, g_m, sh_f, sc_f, g_f = (mods[i, t] for t in range(6))
        xc, h1, y1, ctx, xm, h2, gu, a, y2 = saved[i]
        t = f"l{i}"
        dy2, dgpost_f, dgate_f = post_bwd(y2, ffn_post_g[i:i + 1], g_f, dx, t + "_dpost_f")
        if i == 0:
            da, from_sibling["rest"] = mm(dy2, Wt["ffn_w_down"][i], "nt", F32, name=t + "_da",
                                          sidecar=fold_sidecar(grads_of("rest"), group_geo["rest"]))
        else:
            da = mm(dy2, Wt["ffn_w_down"][i], "nt", F32, name=t + "_da")
        big_g["ffn_w_down"][i] = mm(a, dy2, "tn", BF16, name=t + "_dwdown")
        dgu = act_bwd(gu, da, t + "_dact")
        dh2 = mm(dgu, Wt["ffn_w_gu"][i], "nt", F32, name=t + "_dh2")
        big_g["ffn_w_gu"][i] = mm(h2, dgu, "tn", BF16, name=t + "_dwgu")
        dxm, dgpre_f, dsh_f, dsc_f = pre_bwd(xm, ffn_pre_g[i:i + 1], sh_f, sc_f, dh2, dx, t + "_dpre_f")
        dy1, dgpost_m, dgate_m = post_bwd(y1, mix_post_g[i:i + 1], g_m, dxm, t + "_dpost_m")
        if kind == 0:
            rides = [fold(g) if i == 0 else None for g in ("ffn0", "rest")]
            dh1, dw_in, dw_out, db, from_ffn0, from_rest = _fox_bwd(dy1, h1, Wt["fox_w_in"][j], fox_b_f[j], Wt["fox_w_out"][j], ctx,
                                                                    t + "_fox", *rides)
            big_g["fox_w_in"][j], big_g["fox_w_out"][j], small_g["fox_b_f"][j] = dw_in, dw_out, db
            if i == 0:
                from_chips["ffn0"], from_chips["rest"] = from_ffn0, from_rest
        elif kind == 1:
            dh1, dw_in, dw_out, dlg, dlb, dws, dbs = _sgu_bwd(dy1, h1, Wt["sgu_w_in"][j], Wt["sgu_w_out"][j], ctx, t + "_sgu")
            big_g["sgu_w_in"][j], big_g["sgu_w_out"][j] = dw_in, dw_out
            small_g["sgu_ln_g"][j], small_g["sgu_ln_b"][j], small_g["sgu_w_s"][j], small_g["sgu_b_s"][j] = dlg, dlb, dws, dbs
        else:
            dh1, dw_in, dw_out, dsk = _swa_bwd(dy1, h1, Wt["swa_w_in"][j], Wt["swa_w_out"][j], ctx, t + "_swa")
            big_g["swa_w_in"][j], big_g["swa_w_out"][j], small_g["swa_sinks"][j] = dw_in, dw_out, dsk
        dx, dgpre_m, dsh_m, dsc_m = pre_bwd(xc, mix_pre_g[i:i + 1], sh_m, sc_m, dh1, dxm, t + "_dpre_m")
        small_g["ada_b"][i] = jnp.concatenate([dsh_m, dsc_m, dgate_m, dsh_f, dsc_f, dgate_f], axis=1)[0]
        small_g["mix_pre_g"][i], small_g["mix_post_g"][i] = dgpre_m[0], dgpost_m[0]
        small_g["ffn_pre_g"][i], small_g["ffn_post_g"][i] = dgpre_f[0], dgpost_f[0]
    grad_x = dx[None]

    shapes = {n: P[n].shape for n in SMALL}
    small_parts = ag_small(_pack_small({n: jnp.stack(small_g[n]) for n in SMALL}), "ag_small_grads")
    sg, sd, sm, sv = adamw(_pack_small({n: P[n] for n in SMALL}), _pack_small({n: P["m_" + n] for n in SMALL}),
                           _pack_small({n: P["v_" + n] for n in SMALL}), [small_parts], "adamw_small", gstack=N_DEV)
    out_g, out_d, out_m, out_v = (_unpack_small(t, shapes) for t in (sg, sd, sm, sv))

    dmod_all = small_parts[:, :L * 6 * D // LANES].reshape(N_DEV, L, 6 * D)
    dmod_mine = lax.dynamic_slice_in_dim(dmod_all, q_me * Nm, Nm, axis=2).transpose(1, 0, 2)
    dmod_mine = jnp.pad(dmod_mine, ((0, 0), (0, 16 - N_DEV), (0, 0)))
    g_ada = ada_wgrad(c_all, dmod_mine, "ada_wgrad")
    r2 = lambda t: t.reshape(-1, t.shape[-1])
    res = adamw(r2(ada_w), r2(m_ada_w), r2(v_ada_w), [r2(g_ada)], "adamw_ada_w", emit_g=False)
    out_g["ada_w"] = g_ada
    out_d["ada_w"], out_m["ada_w"], out_v["ada_w"] = (t.reshape(ada_w.shape) for t in res)

    fold("fox0")
    from_chips["fox0"] = chip_exchange(chip_part["fox0"], group_geo["fox0"], "rs_exchange_fox0")
    pieces = [(g, n, m, r, e) for g in groups for n, m, r, e in zip(groups[g], group_geo[g], chip_part[g], from_chips[g])]
    mine = [chip_sum(r, e, *m, qc, f"rs_chip_sum_{g}_{n}") for g, n, m, r, e in pieces]
    shared = sibling_share(mine, [m for _, _, m, _, _ in pieces], "rs_share")
    by_name = {n: [] for n in BIG}
    for (g, n, *_), t in zip(pieces, shared):
        by_name[n].append(t)
    for n in BIG:
        gsh = by_name[n][0] if len(by_name[n]) == 1 else jnp.concatenate(by_name[n], axis=0)
        gsh = gsh[:, :, :P[n].shape[2]]
        res = adamw(r2(P[n]), r2(P["m_" + n]), r2(P["v_" + n]), [r2(gsh)], "adamw_" + n, emit_g=False)
        out_g[n] = gsh
        out_d[n], out_m[n], out_v[n] = (t.reshape(P[n].shape) for t in res)

    return (loss, grad_x, *[out_g[n] for n in WEIGHTS], *[out_d[n] for n in WEIGHTS], *[out_m[n] for n in WEIGHTS],
            *[out_v[n] for n in WEIGHTS])
```

```python
from typing import Callable, NamedTuple

import numpy as np
import jax
import jax.numpy as jnp
from jax import lax
from jax.experimental import pallas as pl
from jax.experimental.pallas import tpu as pltpu

F32 = jnp.float32
BF16 = jnp.bfloat16
MESH = pl.DeviceIdType.MESH

EPS = 1e-6
NEG = -1e30
V7X_VMEM_BYTES = 64 * 1024 * 1024
VMEM_LIMIT = V7X_VMEM_BYTES - 8 * 1024 * 1024
LANES = 128
SEQ_BLOCK = 128
SWA_HEAD_DIM = 64
ROPE_DIM = SWA_HEAD_DIM // 4
ROPE_THETA = 500000.0

ADAM_LR = 0.001
ADAM_B1 = 0.9
ADAM_B2 = 0.999
ADAM_EPS = 1e-08
ADAM_WD = 0.01
ADAM_STEP = 10


def _cp(sem=None, **kw):
    return pltpu.CompilerParams(dimension_semantics=sem, vmem_limit_bytes=VMEM_LIMIT, **kw)


def _tile(dim, pref, mult=LANES):
    if dim <= pref:
        return dim
    t = (pref // mult) * mult
    while t >= mult:
        if dim % t == 0:
            return t
        t -= mult
    return dim


_DIMS = {"nn": (((1,), (0,)), ((), ())), "nt": (((1,), (1,)), ((), ())), "tn": (((0,), (0,)), ((), ()))}


MM_TILES = {"nn": (1024, 512, 2816), "nt": (1024, 1024, 2816), "tn": (1024, 1024, 2048)}


def mm(a, b, mode="nn", out_dtype=F32, add=None, name="mm", b_cols=None, tm=None, tn=None, tk=None, sidecar=None):
    tm, tn, tk = (d if t is None else t for t, d in zip((tm, tn, tk), MM_TILES[mode]))
    b, b_layer = b if isinstance(b, tuple) else (b, None)
    b_shape = b.shape[-2:]
    c0 = 0
    if b_cols is not None:
        c0, csize = b_cols
    if mode == "nn":
        (M, K), (K2, N) = a.shape, b_shape
        if b_cols is not None:
            N = csize
    elif mode == "nt":
        (M, K), (N, K2) = a.shape, b_shape
        if b_cols is not None:
            K2 = csize
    else:
        (K, M), (K2, N) = a.shape, b_shape
        assert b_cols is None
    assert K == K2, (a.shape, b.shape, mode)
    tm = _tile(M, tm, LANES if mode == "tn" else 16)
    tn = _tile(N, tn)
    tk = _tile(K, tk, LANES if mode != "tn" else 16)
    nk = K // tk
    if b_cols is not None:
        assert c0 % (tn if mode == "nn" else tk) == 0, (b_cols, tn, tk)
    bo = c0 // (tn if mode == "nn" else tk)
    dims = _DIMS[mode]
    has_add = add is not None

    def body(a_ref, b_ref, *rest):
        if has_add:
            add_ref, o_ref, acc_ref = rest
        else:
            o_ref, acc_ref = rest
        k = pl.program_id(2)
        p = lax.dot_general(a_ref[...].astype(BF16), b_ref[...].astype(BF16), dims, preferred_element_type=F32)

        @pl.when(k == 0)
        def _():
            acc_ref[...] = p + add_ref[...].astype(F32) if has_add else p

        @pl.when(k > 0)
        def _():
            acc_ref[...] += p

        @pl.when(k == nk - 1)
        def _():
            o_ref[...] = acc_ref[...].astype(o_ref.dtype)

    a_spec = pl.BlockSpec((tk, tm), lambda i, j, k: (k, i)) if mode == "tn" else pl.BlockSpec((tm, tk), lambda i, j, k: (i, k))
    b_blk, b_idx = ((tn, tk), lambda i, j, k: (j, k + bo)) if mode == "nt" else ((tk, tn), lambda i, j, k: (k, j + bo))
    if b_layer is None:
        b_spec = pl.BlockSpec(b_blk, b_idx)
    else:
        b_spec = pl.BlockSpec((None,) + b_blk, lambda i, j, k: (b_layer,) + b_idx(i, j, k))
    o_spec = pl.BlockSpec((tm, tn), lambda i, j, k: (i, j))
    in_specs = [a_spec, b_spec] + ([o_spec] if has_add else [])
    args = (a, b) + ((add,) if has_add else ())
    out_shape = jax.ShapeDtypeStruct((M, N), out_dtype)
    scratch = [pltpu.VMEM((tm, tn), F32)]
    grid = (M // tm, N // tn, nk)
    if sidecar is None:
        return pl.pallas_call(
            body, name=name, grid=grid, in_specs=in_specs, out_specs=o_spec, out_shape=out_shape, scratch_shapes=scratch,
            compiler_params=_cp(("parallel", "parallel", "arbitrary")),
        )(*args)
    first = lambda: (pl.program_id(0) == 0) & (pl.program_id(1) == 0) & (pl.program_id(2) == 0)
    last = lambda: (pl.program_id(0) == grid[0] - 1) & (pl.program_id(1) == grid[1] - 1) & (pl.program_id(2) == grid[2] - 1)
    res = pl.pallas_call(
        _with_sidecar(sidecar, len(args), 1, 1, body, first, last), name=name, grid=grid,
        in_specs=in_specs + _sc_specs(sidecar, False), out_specs=[o_spec] + _sc_specs(sidecar, True),
        out_shape=[out_shape] + _sc_out(sidecar), scratch_shapes=scratch + _sc_sems(sidecar),
        compiler_params=_cp(("arbitrary", "arbitrary", "arbitrary")),
    )(*args, *_sc_arrays(sidecar))
    return res[0], res[1:]


def _rms(x, g):
    return (x * lax.rsqrt(jnp.mean(x * x, axis=-1, keepdims=True) + EPS)) * g


def _pre(x, g, sh, sc):
    return _rms(x, g) * (1 + sc) + sh


def _post(x, y, g, gate):
    return x + gate * _rms(y, g)


ROW_TILE = 256


def _row_spec(tr, d):
    return pl.BlockSpec((tr, d), lambda i: (i, 0))


def _vec_spec(d):
    return pl.BlockSpec((1, d), lambda i: (0, 0))


def pre_fwd(x, g, sh, sc, name):
    S, D = x.shape
    tr = _tile(S, ROW_TILE, 16)

    def body(x_ref, g_ref, sh_ref, sc_ref, h_ref):
        h_ref[...] = _pre(x_ref[...], g_ref[...], sh_ref[...], sc_ref[...]).astype(h_ref.dtype)

    return pl.pallas_call(
        body, name=name, grid=(S // tr,), in_specs=[_row_spec(tr, D)] + [_vec_spec(D)] * 3, out_specs=_row_spec(tr, D),
        out_shape=jax.ShapeDtypeStruct((S, D), BF16), compiler_params=_cp(("parallel",)),
    )(x, g, sh, sc)


def pre_bwd(x, g, sh, sc, dh, dres, name):
    S, D = x.shape
    tr = _tile(S, ROW_TILE, 16)

    def body(x_ref, g_ref, sh_ref, sc_ref, dh_ref, dres_ref, dx_ref, dg_ref, dsh_ref, dsc_ref):
        _, vjp = jax.vjp(_pre, x_ref[...], g_ref[...], sh_ref[...], sc_ref[...])
        dx, dg, dsh, dsc = vjp(dh_ref[...].astype(F32))
        dx_ref[...] = dres_ref[...] + dx

        @pl.when(pl.program_id(0) == 0)
        def _():
            dg_ref[...] = jnp.zeros_like(dg_ref)
            dsh_ref[...] = jnp.zeros_like(dsh_ref)
            dsc_ref[...] = jnp.zeros_like(dsc_ref)

        dg_ref[...] += dg
        dsh_ref[...] += dsh
        dsc_ref[...] += dsc

    vec = jax.ShapeDtypeStruct((1, D), F32)
    return pl.pallas_call(
        body, name=name, grid=(S // tr,), in_specs=[_row_spec(tr, D)] + [_vec_spec(D)] * 3 + [_row_spec(tr, D)] * 2,
        out_specs=[_row_spec(tr, D)] + [_vec_spec(D)] * 3, out_shape=[jax.ShapeDtypeStruct((S, D), F32), vec, vec, vec],
        compiler_params=_cp(("arbitrary",)),
    )(x, g, sh, sc, dh, dres)


def post_fwd(x, y, g, gate, name):
    S, D = x.shape
    tr = _tile(S, ROW_TILE, 16)

    def body(x_ref, y_ref, g_ref, gate_ref, o_ref):
        o_ref[...] = _post(x_ref[...], y_ref[...], g_ref[...], gate_ref[...])

    return pl.pallas_call(
        body, name=name, grid=(S // tr,), in_specs=[_row_spec(tr, D)] * 2 + [_vec_spec(D)] * 2, out_specs=_row_spec(tr, D),
        out_shape=jax.ShapeDtypeStruct((S, D), F32), compiler_params=_cp(("parallel",)),
    )(x, y, g, gate)


def post_bwd(y, g, gate, dxn, name):
    S, D = y.shape
    tr = _tile(S, ROW_TILE, 16)

    def body(y_ref, g_ref, gate_ref, dxn_ref, dy_ref, dg_ref, dgate_ref):
        fn = lambda yy, gg, gt: gt * _rms(yy, gg)
        _, vjp = jax.vjp(fn, y_ref[...], g_ref[...], gate_ref[...])
        dy, dg, dgate = vjp(dxn_ref[...])
        dy_ref[...] = dy.astype(dy_ref.dtype)

        @pl.when(pl.program_id(0) == 0)
        def _():
            dg_ref[...] = jnp.zeros_like(dg_ref)
            dgate_ref[...] = jnp.zeros_like(dgate_ref)

        dg_ref[...] += dg
        dgate_ref[...] += dgate

    vec = jax.ShapeDtypeStruct((1, D), F32)
    return pl.pallas_call(
        body, name=name, grid=(S // tr,), in_specs=[_row_spec(tr, D)] + [_vec_spec(D)] * 2 + [_row_spec(tr, D)],
        out_specs=[_row_spec(tr, D)] + [_vec_spec(D)] * 2, out_shape=[jax.ShapeDtypeStruct((S, D), BF16), vec, vec],
        compiler_params=_cp(("arbitrary",)),
    )(y, g, gate, dxn)


def _swiglu(g, u):
    return jax.nn.silu(g) * u


ACT_ROWS = 256


def act_fwd(gu, name):
    S, F2 = gu.shape
    F = F2 // 2
    tr = _tile(S, ACT_ROWS, 16)

    def body(gu_ref, a_ref):
        a_ref[...] = _swiglu(gu_ref[:, :F].astype(F32), gu_ref[:, F:].astype(F32)).astype(a_ref.dtype)

    return pl.pallas_call(
        body, name=name, grid=(S // tr,), in_specs=[_row_spec(tr, F2)], out_specs=_row_spec(tr, F),
        out_shape=jax.ShapeDtypeStruct((S, F), BF16), compiler_params=_cp(("parallel",)),
    )(gu)


def act_bwd(gu, da, name):
    S, F2 = gu.shape
    F = F2 // 2
    tr = _tile(S, ACT_ROWS, 16)

    def body(gu_ref, da_ref, dgu_ref):
        _, vjp = jax.vjp(_swiglu, gu_ref[:, :F].astype(F32), gu_ref[:, F:].astype(F32))
        dg, du = vjp(da_ref[...].astype(F32))
        dgu_ref[:, :F] = dg.astype(dgu_ref.dtype)
        dgu_ref[:, F:] = du.astype(dgu_ref.dtype)

    return pl.pallas_call(
        body, name=name, grid=(S // tr,), in_specs=[_row_spec(tr, F2), _row_spec(tr, F)], out_specs=_row_spec(tr, F2),
        out_shape=jax.ShapeDtypeStruct((S, F2), BF16), compiler_params=_cp(("parallel",)),
    )(gu, da)


GATE_CHUNK = 512


def _tri(upper):
    r = lax.broadcasted_iota(jnp.int32, (GATE_CHUNK, GATE_CHUNK), 0)
    c = lax.broadcasted_iota(jnp.int32, (GATE_CHUNK, GATE_CHUNK), 1)
    return ((r <= c) if upper else (r >= c)).astype(F32)


def _hdot(a, b):
    return jnp.dot(a, b, precision=lax.Precision.HIGHEST, preferred_element_type=F32)


def fox_gate_fwd(fgT, b, name):
    H, S = fgT.shape
    C = _tile(S, GATE_CHUNK)
    assert C == GATE_CHUNK, (S, C)
    spec = pl.BlockSpec((H, C), lambda ch: (0, ch))

    def body(fg_ref, b_ref, cum_ref, carry_ref):
        @pl.when(pl.program_id(0) == 0)
        def _():
            carry_ref[...] = jnp.zeros_like(carry_ref)

        lf = jax.nn.log_sigmoid(fg_ref[...] + b_ref[...])
        cum_ref[...] = _hdot(lf, _tri(True)) + carry_ref[...]
        carry_ref[...] += _hdot(lf, jnp.ones((C, C), F32))

    return pl.pallas_call(
        body, name=name, grid=(S // C,), in_specs=[spec, pl.BlockSpec((H, 1), lambda ch: (0, 0))], out_specs=spec,
        out_shape=jax.ShapeDtypeStruct((H, S), F32), scratch_shapes=[pltpu.VMEM((H, C), F32)], compiler_params=_cp(("arbitrary",)),
    )(fgT, b)


def fox_gate_bwd(dcum, fgT, b, name):
    H, S = fgT.shape
    C = _tile(S, GATE_CHUNK)
    assert C == GATE_CHUNK, (S, C)
    nch = S // C
    spec = pl.BlockSpec((H, C), lambda t: (0, nch - 1 - t))

    def body(dcum_ref, fg_ref, b_ref, dfg_ref, db_ref, tail_ref):
        @pl.when(pl.program_id(0) == 0)
        def _():
            tail_ref[...] = jnp.zeros_like(tail_ref)
            db_ref[...] = jnp.zeros_like(db_ref)

        dlf = _hdot(dcum_ref[...], _tri(False)) + tail_ref[...]
        dfg = dlf * jax.nn.sigmoid(-(fg_ref[...] + b_ref[...]))
        dfg_ref[...] = dfg
        tail_ref[...] += _hdot(dcum_ref[...], jnp.ones((C, C), F32))
        db_ref[...] += _hdot(dfg, jnp.ones((C, LANES), F32))

    return pl.pallas_call(
        body, name=name, grid=(nch,), in_specs=[spec, spec, pl.BlockSpec((H, 1), lambda t: (0, 0))],
        out_specs=[spec, pl.BlockSpec((H, LANES), lambda t: (0, 0))],
        out_shape=[jax.ShapeDtypeStruct((H, S), F32), jax.ShapeDtypeStruct((H, LANES), F32)],
        scratch_shapes=[pltpu.VMEM((H, C), F32)], compiler_params=_cp(("arbitrary",)),
    )(dcum, fgT, b)


FOX_TILE = 1024


def _on_and_below_diagonal(i, j, tile):
    @pl.when(j < i)
    def _():
        tile(False)

    @pl.when(j == i)
    def _():
        tile(True)


def _causal_pairs(n, by_key):
    pairs = [(i, j) for i in range(n) for j in range(i + 1)]
    if by_key:
        pairs.sort(key=lambda p: (p[1], p[0]))
    qi = np.asarray([p[0] for p in pairs], np.int32)
    kj = np.asarray([p[1] for p in pairs], np.int32)
    return qi, kj


def _fox_scores(q, k, fq, fk, T, scale, transposed):
    r = lax.broadcasted_iota(jnp.int32, (T, T), 0)
    c = lax.broadcasted_iota(jnp.int32, (T, T), 1)
    if transposed:
        return lax.dot_general(k, q, _DIMS["nt"], preferred_element_type=F32) * scale + (fq - fk), r <= c
    return lax.dot_general(q, k, _DIMS["nt"], preferred_element_type=F32) * scale + (fq - fk), c <= r


class SideCar(NamedTuple):
    arrays: list
    out_shape: list
    semaphores: list
    steps: Callable


def _sc_specs(sc, out):
    return [] if sc is None else [pl.BlockSpec(memory_space=pl.ANY)] * len(sc.out_shape if out else sc.arrays)


def _sc_sems(sc):
    return [] if sc is None else list(sc.semaphores)


def _sc_out(sc):
    return [] if sc is None else list(sc.out_shape)


def _sc_arrays(sc):
    return [] if sc is None else list(sc.arrays)


def _with_sidecar(sc, n_in, n_out, n_scratch, body, first, last):
    if sc is None:
        return body
    a, o = len(sc.arrays), len(sc.out_shape)

    def wrapped(*refs):
        ins, rest = refs[:n_in], refs[n_in:]
        sc_in, rest = rest[:a], rest[a:]
        outs, rest = rest[:n_out], rest[n_out:]
        sc_out, rest = rest[:o], rest[o:]
        scratch, sems = rest[:n_scratch], rest[n_scratch:]
        start, finish = sc.steps(sc_in, sc_out, sems)
        pl.when(first())(start)
        body(*ins, *outs, *scratch)
        pl.when(last())(finish)

    return wrapped


def fox_attn_fwd(qkv, cum_col, cum_row, H, name, sidecar=None):
    S = qkv.shape[0]
    Dh = qkv.shape[1] // (3 * H)
    T = _tile(S, FOX_TILE)
    n = S // T
    qi, kj = _causal_pairs(n, by_key=False)
    scale = Dh ** -0.5

    def body(qi_ref, kj_ref, q_ref, k_ref, v_ref, fq_ref, fk_ref, o_ref, lse_ref, m_ref, l_ref, acc_ref):
        p_id = pl.program_id(1)
        i, j = qi_ref[p_id], kj_ref[p_id]

        @pl.when(j == 0)
        def _():
            m_ref[...] = jnp.full_like(m_ref, NEG)
            l_ref[...] = jnp.zeros_like(l_ref)
            acc_ref[...] = jnp.zeros_like(acc_ref)

        def tile(masked):
            s, mask = _fox_scores(q_ref[...], k_ref[...], fq_ref[0], fk_ref[0], T, scale, False)
            if masked:
                s = jnp.where(mask, s, NEG)
            m_new = jnp.maximum(m_ref[...], jnp.max(s, axis=1, keepdims=True))
            alpha = jnp.exp(m_ref[...] - m_new)
            p = jnp.exp(s - m_new)
            l_ref[...] = alpha * l_ref[...] + jnp.sum(p, axis=1, keepdims=True)
            acc_ref[...] = alpha * acc_ref[...] + jnp.dot(p.astype(BF16), v_ref[...], preferred_element_type=F32)
            m_ref[...] = m_new

        _on_and_below_diagonal(i, j, tile)

        @pl.when(j == i)
        def _():
            o_ref[...] = (acc_ref[...] / l_ref[...]).astype(o_ref.dtype)
            lse_ref[0] = m_ref[...] + jnp.log(l_ref[...])

    grid_spec = pltpu.PrefetchScalarGridSpec(
        num_scalar_prefetch=2, grid=(H, len(qi)),
        in_specs=[
            pl.BlockSpec((T, Dh), lambda h, p, qi, kj: (qi[p], h)),
            pl.BlockSpec((T, Dh), lambda h, p, qi, kj: (kj[p], H + h)),
            pl.BlockSpec((T, Dh), lambda h, p, qi, kj: (kj[p], 2 * H + h)),
            pl.BlockSpec((1, T, 1), lambda h, p, qi, kj: (h, qi[p], 0)),
            pl.BlockSpec((1, 1, T), lambda h, p, qi, kj: (h, 0, kj[p])),
        ] + _sc_specs(sidecar, False),
        out_specs=[
            pl.BlockSpec((T, Dh), lambda h, p, qi, kj: (qi[p], h)),
            pl.BlockSpec((1, T, 1), lambda h, p, qi, kj: (h, qi[p], 0)),
        ] + _sc_specs(sidecar, True),
        scratch_shapes=[pltpu.VMEM((T, 1), F32), pltpu.VMEM((T, 1), F32), pltpu.VMEM((T, Dh), F32)] + _sc_sems(sidecar),
    )
    first = lambda: (pl.program_id(0) == 0) & (pl.program_id(1) == 0)
    last = lambda: (pl.program_id(0) == H - 1) & (pl.program_id(1) == len(qi) - 1)
    res = pl.pallas_call(
        _with_sidecar(sidecar, 7, 2, 3, body, first, last), name=name, grid_spec=grid_spec,
        out_shape=[jax.ShapeDtypeStruct((S, H * Dh), F32), jax.ShapeDtypeStruct((H, S, 1), F32)] + _sc_out(sidecar),
        compiler_params=_cp(("arbitrary", "arbitrary")),
    )(jnp.asarray(qi), jnp.asarray(kj), qkv, qkv, qkv, cum_col, cum_row, *_sc_arrays(sidecar))
    return res[0], res[1], res[2:]


def fox_attn_bwd_dq(qkv, do, o, lse, cum_col, cum_row, H, name, sidecar=None):
    S = qkv.shape[0]
    Dh = qkv.shape[1] // (3 * H)
    T = _tile(S, FOX_TILE)
    n = S // T
    qi, kj = _causal_pairs(n, by_key=False)
    scale = Dh ** -0.5

    def body(qi_ref, kj_ref, q_ref, k_ref, v_ref, do_ref, o_ref, lse_ref, fq_ref, fk_ref, dq_ref, delta_ref, acc_ref, dl_ref,
             rs_ref):
        p_id = pl.program_id(1)
        i, j = qi_ref[p_id], kj_ref[p_id]

        @pl.when(j == 0)
        def _():
            acc_ref[...] = jnp.zeros_like(acc_ref)
            rs_ref[...] = jnp.zeros_like(rs_ref)
            dl_ref[...] = jnp.sum(do_ref[...].astype(F32) * o_ref[...].astype(F32), axis=1, keepdims=True)

        def tile(masked):
            s, mask = _fox_scores(q_ref[...], k_ref[...], fq_ref[0], fk_ref[0], T, scale, False)
            p = jnp.exp(s - lse_ref[0])
            if masked:
                p = jnp.where(mask, p, 0.0)
            dp = lax.dot_general(do_ref[...], v_ref[...], _DIMS["nt"], preferred_element_type=F32)
            ds = p * (dp - dl_ref[...])
            rs_ref[...] += jnp.sum(ds, axis=1, keepdims=True)
            acc_ref[...] += jnp.dot(ds.astype(BF16), k_ref[...], preferred_element_type=F32)

        _on_and_below_diagonal(i, j, tile)

        @pl.when(j == i)
        def _():
            dq_ref[...] = (acc_ref[...] * scale).astype(dq_ref.dtype)
            delta_ref[0] = dl_ref[...] + rs_ref[...]

    qspec = pl.BlockSpec((T, Dh), lambda h, p, qi, kj: (qi[p], h))
    colspec = pl.BlockSpec((1, T, 1), lambda h, p, qi, kj: (h, qi[p], 0))
    grid_spec = pltpu.PrefetchScalarGridSpec(
        num_scalar_prefetch=2, grid=(H, len(qi)),
        in_specs=[
            qspec,
            pl.BlockSpec((T, Dh), lambda h, p, qi, kj: (kj[p], H + h)),
            pl.BlockSpec((T, Dh), lambda h, p, qi, kj: (kj[p], 2 * H + h)),
            qspec, qspec, colspec, colspec,
            pl.BlockSpec((1, 1, T), lambda h, p, qi, kj: (h, 0, kj[p])),
        ] + _sc_specs(sidecar, False),
        out_specs=[qspec, colspec] + _sc_specs(sidecar, True),
        scratch_shapes=[pltpu.VMEM((T, Dh), F32), pltpu.VMEM((T, 1), F32), pltpu.VMEM((T, 1), F32)] + _sc_sems(sidecar),
    )
    first = lambda: (pl.program_id(0) == 0) & (pl.program_id(1) == 0)
    last = lambda: (pl.program_id(0) == H - 1) & (pl.program_id(1) == len(qi) - 1)
    res = pl.pallas_call(
        _with_sidecar(sidecar, 10, 2, 3, body, first, last), name=name, grid_spec=grid_spec,
        out_shape=[jax.ShapeDtypeStruct((S, H * Dh), BF16), jax.ShapeDtypeStruct((H, S, 1), F32)] + _sc_out(sidecar),
        compiler_params=_cp(("arbitrary", "arbitrary")),
    )(jnp.asarray(qi), jnp.asarray(kj), qkv, qkv, qkv, do, o, lse, cum_col, cum_row, *_sc_arrays(sidecar))
    return res[0], res[1], res[2:]


def fox_attn_bwd_dkv(qkv, do, lse_row, delta_row, cum_col, cum_row, H, name, sidecar=None):
    S = qkv.shape[0]
    Dh = qkv.shape[1] // (3 * H)
    T = _tile(S, FOX_TILE)
    n = S // T
    qi, kj = _causal_pairs(n, by_key=True)
    scale = Dh ** -0.5

    def body(qi_ref, kj_ref, q_ref, k_ref, v_ref, do_ref, lse_ref, dl_ref, fq_ref, fk_ref, dk_ref, dv_ref, dcum_ref,
             dk_acc, dv_acc, df_acc):
        p_id = pl.program_id(1)
        i, j = qi_ref[p_id], kj_ref[p_id]

        @pl.when(i == j)
        def _():
            dk_acc[...] = jnp.zeros_like(dk_acc)
            dv_acc[...] = jnp.zeros_like(dv_acc)
            df_acc[...] = jnp.zeros_like(df_acc)

        def tile(masked):
            sT, mask = _fox_scores(q_ref[...], k_ref[...], fq_ref[0], fk_ref[0], T, scale, True)
            pT = jnp.exp(sT - lse_ref[0])
            if masked:
                pT = jnp.where(mask, pT, 0.0)
            dv_acc[...] += jnp.dot(pT.astype(BF16), do_ref[...], preferred_element_type=F32)
            dpT = lax.dot_general(v_ref[...], do_ref[...], _DIMS["nt"], preferred_element_type=F32)
            dsT = pT * (dpT - dl_ref[0])
            dk_acc[...] += jnp.dot(dsT.astype(BF16), q_ref[...], preferred_element_type=F32)
            df_acc[...] -= jnp.sum(dsT, axis=1, keepdims=True)

        _on_and_below_diagonal(i, j, tile)

        @pl.when(i == n - 1)
        def _():
            dk_ref[...] = (dk_acc[...] * scale).astype(dk_ref.dtype)
            dv_ref[...] = dv_acc[...].astype(dv_ref.dtype)
            dcum_ref[0] = df_acc[...]

    qspec = pl.BlockSpec((T, Dh), lambda h, p, qi, kj: (qi[p], h))
    kspec = pl.BlockSpec((T, Dh), lambda h, p, qi, kj: (kj[p], H + h))
    vspec = pl.BlockSpec((T, Dh), lambda h, p, qi, kj: (kj[p], 2 * H + h))
    qrow = pl.BlockSpec((1, 1, T), lambda h, p, qi, kj: (h, 0, qi[p]))
    kcol = pl.BlockSpec((1, T, 1), lambda h, p, qi, kj: (h, kj[p], 0))
    grid_spec = pltpu.PrefetchScalarGridSpec(
        num_scalar_prefetch=2, grid=(H, len(qi)),
        in_specs=[qspec, kspec, vspec, qspec, qrow, qrow, qrow, kcol] + _sc_specs(sidecar, False),
        out_specs=[pl.BlockSpec((T, Dh), lambda h, p, qi, kj: (kj[p], h))] * 2 + [kcol] + _sc_specs(sidecar, True),
        scratch_shapes=[pltpu.VMEM((T, Dh), F32), pltpu.VMEM((T, Dh), F32), pltpu.VMEM((T, 1), F32)] + _sc_sems(sidecar),
    )
    out = jax.ShapeDtypeStruct((S, H * Dh), BF16)
    first = lambda: (pl.program_id(0) == 0) & (pl.program_id(1) == 0)
    last = lambda: (pl.program_id(0) == H - 1) & (pl.program_id(1) == len(qi) - 1)
    res = pl.pallas_call(
        _with_sidecar(sidecar, 10, 3, 3, body, first, last), name=name, grid_spec=grid_spec,
        out_shape=[out, out, jax.ShapeDtypeStruct((H, S, 1), F32)] + _sc_out(sidecar),
        compiler_params=_cp(("arbitrary", "arbitrary")),
    )(jnp.asarray(qi), jnp.asarray(kj), qkv, qkv, qkv, do, lse_row, delta_row, cum_row, cum_col, *_sc_arrays(sidecar))
    return res[0], res[1], res[2], res[3:]


def _sgu_ln(zu, zv, ln_g, ln_b):
    u = jax.nn.gelu(zu)
    v = jax.nn.gelu(zv)
    mu = jnp.mean(v, axis=-1, keepdims=True)
    var = jnp.mean(jnp.square(v - mu), axis=-1, keepdims=True)
    return u, (v - mu) * lax.rsqrt(var + EPS) * ln_g + ln_b


def _tril_mask():
    r = lax.broadcasted_iota(jnp.int32, (SEQ_BLOCK, SEQ_BLOCK), 0)
    c = lax.broadcasted_iota(jnp.int32, (SEQ_BLOCK, SEQ_BLOCK), 1)
    return r >= c


def _sgu_spatial(ws_ref, bsT, selT, vn, G):
    tril = _tril_mask()
    fs = []
    for g in range(G):
        wg = jnp.where(tril, ws_ref[g], 0.0).astype(BF16)
        fs.append(jnp.dot(wg, vn[:, g * SEQ_BLOCK:(g + 1) * SEQ_BLOCK].astype(BF16), preferred_element_type=F32))
    bias = jnp.dot(bsT, selT, precision=lax.Precision.HIGHEST, preferred_element_type=F32)
    return jnp.concatenate(fs, axis=1) + bias


def _sgu_specs(W, G):
    return [
        pl.BlockSpec((SEQ_BLOCK, 2 * W), lambda n: (n, 0)),
        pl.BlockSpec((1, W), lambda n: (0, 0)),
        pl.BlockSpec((1, W), lambda n: (0, 0)),
        pl.BlockSpec((G, SEQ_BLOCK, SEQ_BLOCK), lambda n: (0, 0, 0)),
        pl.BlockSpec((SEQ_BLOCK, G), lambda n: (0, 0)),
        pl.BlockSpec((G, W), lambda n: (0, 0)),
    ]


def sgu_fwd(zp, ln_g, ln_b, ws, bsT, selT, name):
    S, W2 = zp.shape
    W = W2 // 2
    G = ws.shape[0]

    def body(z_ref, lg_ref, lb_ref, ws_ref, bs_ref, sel_ref, o_ref):
        u, vn = _sgu_ln(z_ref[:, :W], z_ref[:, W:], lg_ref[...], lb_ref[...])
        o_ref[...] = (u * _sgu_spatial(ws_ref, bs_ref[...], sel_ref[...], vn, G)).astype(o_ref.dtype)

    return pl.pallas_call(
        body, name=name, grid=(S // SEQ_BLOCK,), in_specs=_sgu_specs(W, G), out_specs=pl.BlockSpec((SEQ_BLOCK, W), lambda n: (n, 0)),
        out_shape=jax.ShapeDtypeStruct((S, W), BF16), compiler_params=_cp(("parallel",)),
    )(zp, ln_g, ln_b, ws, bsT, selT)


def sgu_bwd(zp, ln_g, ln_b, ws, bsT, selT, dgated, name):
    S, W2 = zp.shape
    W = W2 // 2
    G = ws.shape[0]

    def body(z_ref, lg_ref, lb_ref, ws_ref, bs_ref, sel_ref, dgt_ref, dz_ref, dlg_ref, dlb_ref, dws_ref, dbs_ref):
        (u, vn), vjp = jax.vjp(_sgu_ln, z_ref[:, :W], z_ref[:, W:], lg_ref[...], lb_ref[...])
        f = _sgu_spatial(ws_ref, bs_ref[...], sel_ref[...], vn, G)
        dgt = dgt_ref[...].astype(F32)
        du, df = dgt * f, dgt * u

        @pl.when(pl.program_id(0) == 0)
        def _():
            dlg_ref[...] = jnp.zeros_like(dlg_ref)
            dlb_ref[...] = jnp.zeros_like(dlb_ref)
            dws_ref[...] = jnp.zeros_like(dws_ref)
            dbs_ref[...] = jnp.zeros_like(dbs_ref)

        dbs_ref[...] += lax.dot_general(df, sel_ref[...], _DIMS["nt"], precision=lax.Precision.HIGHEST, preferred_element_type=F32)
        tril = _tril_mask()
        dvn = []
        for g in range(G):
            sl = slice(g * SEQ_BLOCK, (g + 1) * SEQ_BLOCK)
            wg = jnp.where(tril, ws_ref[g], 0.0).astype(BF16)
            df_g = df[:, sl].astype(BF16)
            dw = lax.dot_general(df_g, vn[:, sl].astype(BF16), _DIMS["nt"], preferred_element_type=F32)
            dws_ref[g] += jnp.where(tril, dw, 0.0)
            dvn.append(lax.dot_general(wg, df_g, _DIMS["tn"], preferred_element_type=F32))
        dzu, dzv, dlg, dlb = vjp((du, jnp.concatenate(dvn, axis=1)))
        dz_ref[:, :W] = dzu.astype(dz_ref.dtype)
        dz_ref[:, W:] = dzv.astype(dz_ref.dtype)
        dlg_ref[...] += dlg
        dlb_ref[...] += dlb

    vec = jax.ShapeDtypeStruct((1, W), F32)
    return pl.pallas_call(
        body, name=name, grid=(S // SEQ_BLOCK,),
        in_specs=_sgu_specs(W, G) + [pl.BlockSpec((SEQ_BLOCK, W), lambda n: (n, 0))],
        out_specs=[
            pl.BlockSpec((SEQ_BLOCK, 2 * W), lambda n: (n, 0)),
            pl.BlockSpec((1, W), lambda n: (0, 0)),
            pl.BlockSpec((1, W), lambda n: (0, 0)),
            pl.BlockSpec((G, SEQ_BLOCK, SEQ_BLOCK), lambda n: (0, 0, 0)),
            pl.BlockSpec((SEQ_BLOCK, G), lambda n: (0, 0)),
        ],
        out_shape=[jax.ShapeDtypeStruct((S, W2), BF16), vec, vec, jax.ShapeDtypeStruct(ws.shape, F32), jax.ShapeDtypeStruct((SEQ_BLOCK, G), F32)],
        compiler_params=_cp(("arbitrary",)),
    )(zp, ln_g, ln_b, ws, bsT, selT, dgated)


def _rope_matrix():
    half = ROPE_DIM // 2
    R = np.zeros((SWA_HEAD_DIM, SWA_HEAD_DIM), np.float32)
    for j in range(half):
        R[j + half, j] = -1.0
        R[j, j + half] = 1.0
    return R


def _swa_mask_bias(G):
    B = SEQ_BLOCK
    qi = np.arange(G * B)[:, None] % B
    ki = np.arange(2 * B)[None, :] - B
    rel = qi - ki
    valid = (rel >= 0) & (rel < B)
    return np.where(np.stack([valid & (ki >= 0), valid]), 0.0, NEG).astype(np.float32)


def _rot3(t, r_bf16):
    hi = t.astype(BF16)
    rest = t - hi.astype(F32)
    mid = rest.astype(BF16)
    lo = (rest - mid.astype(F32)).astype(BF16)
    d = lambda piece: jnp.dot(piece, r_bf16, preferred_element_type=F32)
    return (d(hi) + d(mid)) + d(lo)


@jax.custom_vjp
def _rope_rot(t, r_bf16):
    return _rot3(t, r_bf16)


def _rope_rot_fwd(t, r_bf16):
    return _rot3(t, r_bf16), r_bf16


def _rope_rot_bwd(r_bf16, ct):
    return -_rot3(ct, r_bf16), jnp.zeros_like(r_bf16)


_rope_rot.defvjp(_rope_rot_fwd, _rope_rot_bwd)


def _swa_block(q4, kp, kc, vp, vc, sink, Cq, Sq, Cp, Sp, R, bias, G):
    B, Dh = SEQ_BLOCK, SWA_HEAD_DIM
    r_bf16 = R.astype(BF16)
    rot = lambda t: _rope_rot(t, r_bf16)
    q = q4.reshape(G * B, Dh)
    Cq4 = jnp.concatenate([Cq] * G, axis=0)
    Sq4 = jnp.concatenate([Sq] * G, axis=0)
    qr = q * Cq4 + rot(q) * Sq4
    kb = jnp.concatenate([kp * Cp + rot(kp) * Sp, kc * Cq + rot(kc) * Sq], axis=0)
    vb = jnp.concatenate([vp, vc], axis=0)
    s = lax.dot_general(qr.astype(BF16), kb.astype(BF16), _DIMS["nt"], preferred_element_type=F32) * (Dh ** -0.5) + bias
    m = lax.stop_gradient(jnp.maximum(jnp.max(s, axis=1, keepdims=True), sink))
    p = jnp.exp(s - m)
    p = p / (jnp.sum(p, axis=1, keepdims=True) + jnp.exp(sink - m))
    o = jnp.dot(p.astype(BF16), vb.astype(BF16), preferred_element_type=F32)
    return o.reshape(G, B, Dh)


SWA_HEADS_PER_STEP = 2


def _swa_specs(G, HP):
    B, Dh = SEQ_BLOCK, SWA_HEAD_DIM
    prev = lambda n: jnp.maximum(n - 1, 0)
    return [
        pl.BlockSpec((HP * G, B, Dh), lambda h, n: (h, n, 0)),
        pl.BlockSpec((HP, B, Dh), lambda h, n: (h, prev(n), 0)),
        pl.BlockSpec((HP, B, Dh), lambda h, n: (h, n, 0)),
        pl.BlockSpec((HP, B, Dh), lambda h, n: (h, prev(n), 0)),
        pl.BlockSpec((HP, B, Dh), lambda h, n: (h, n, 0)),
        pl.BlockSpec((HP, G * B, 1), lambda h, n: (h, 0, 0)),
        pl.BlockSpec((B, Dh), lambda h, n: (n, 0)),
        pl.BlockSpec((B, Dh), lambda h, n: (n, 0)),
        pl.BlockSpec((B, Dh), lambda h, n: (prev(n), 0)),
        pl.BlockSpec((B, Dh), lambda h, n: (prev(n), 0)),
        pl.BlockSpec((Dh, Dh), lambda h, n: (0, 0)),
        pl.BlockSpec((1, G * B, 2 * B), lambda h, n: (jnp.minimum(n, 1), 0, 0)),
    ]


def swa_fwd(qh, kh, vh, sink_col, C, Sn, R, bias, name):
    Hq, S, Dh = qh.shape
    Hk = kh.shape[0]
    G = Hq // Hk
    HP = _tile(Hk, SWA_HEADS_PER_STEP, 1)

    def body(q_ref, kp_ref, kc_ref, vp_ref, vc_ref, sk_ref, cq_ref, sq_ref, cp_ref, sp_ref, r_ref, b_ref, o_ref):
        for hp in range(HP):
            qs = slice(hp * G, (hp + 1) * G)
            o = _swa_block(q_ref[qs], kp_ref[hp], kc_ref[hp], vp_ref[hp], vc_ref[hp], sk_ref[hp], cq_ref[...], sq_ref[...],
                           cp_ref[...], sp_ref[...], r_ref[...], b_ref[0], G)
            o_ref[qs] = o.astype(o_ref.dtype)

    return pl.pallas_call(
        body, name=name, grid=(Hk // HP, S // SEQ_BLOCK), in_specs=_swa_specs(G, HP),
        out_specs=pl.BlockSpec((HP * G, SEQ_BLOCK, Dh), lambda h, n: (h, n, 0)),
        out_shape=jax.ShapeDtypeStruct((Hq, S, Dh), BF16), compiler_params=_cp(("parallel", "parallel")),
    )(qh, kh, kh, vh, vh, sink_col, C, Sn, C, Sn, R, bias)


def swa_bwd(qh, kh, vh, sink_col, C, Sn, R, bias, doh, name):
    Hq, S, Dh = qh.shape
    Hk = kh.shape[0]
    G = Hq // Hk
    B = SEQ_BLOCK
    HP = _tile(Hk, SWA_HEADS_PER_STEP, 1)

    def body(q_ref, kp_ref, kc_ref, vp_ref, vc_ref, sk_ref, cq_ref, sq_ref, cp_ref, sp_ref, r_ref, b_ref, do_ref,
             dq_ref, dkp_ref, dkc_ref, dvp_ref, dvc_ref, dsk_ref):
        @pl.when(pl.program_id(1) == 0)
        def _():
            dsk_ref[...] = jnp.zeros_like(dsk_ref)

        fn = lambda q4, kp, kc, vp, vc, sk: _swa_block(q4, kp, kc, vp, vc, sk, cq_ref[...], sq_ref[...], cp_ref[...], sp_ref[...],
                                                      r_ref[...], b_ref[0], G)
        for hp in range(HP):
            qs = slice(hp * G, (hp + 1) * G)
            _, vjp = jax.vjp(fn, q_ref[qs], kp_ref[hp], kc_ref[hp], vp_ref[hp], vc_ref[hp], sk_ref[hp])
            dq, dkp, dkc, dvp, dvc, dsk = vjp(do_ref[qs].astype(F32))
            dq_ref[qs] = dq
            dkp_ref[hp] = dkp
            dkc_ref[hp] = dkc
            dvp_ref[hp] = dvp
            dvc_ref[hp] = dvc
            for g in range(G):
                part = jnp.sum(dsk[g * B:(g + 1) * B], axis=0, keepdims=True)
                dsk_ref[hp, g:g + 1, :] += jnp.broadcast_to(part, (1, LANES))

    qspec = pl.BlockSpec((HP * G, B, Dh), lambda h, n: (h, n, 0))
    kspec = pl.BlockSpec((HP, B, Dh), lambda h, n: (h, n, 0))
    kshape = jax.ShapeDtypeStruct((Hk, S, Dh), F32)
    return pl.pallas_call(
        body, name=name, grid=(Hk // HP, S // B), in_specs=_swa_specs(G, HP) + [qspec],
        out_specs=[qspec, kspec, kspec, kspec, kspec, pl.BlockSpec((HP, G, LANES), lambda h, n: (h, 0, 0))],
        out_shape=[jax.ShapeDtypeStruct((Hq, S, Dh), F32), kshape, kshape, kshape, kshape, jax.ShapeDtypeStruct((Hk, G, LANES), F32)],
        compiler_params=_cp(("parallel", "arbitrary")),
    )(qh, kh, kh, vh, vh, sink_col, C, Sn, C, Sn, R, bias, doh)


def shift_add(cur, prev, name):
    Hk, S, Dh = cur.shape
    B = SEQ_BLOCK

    def body(c_ref, p_ref, o_ref):
        o_ref[0, :S - B] = c_ref[0, :S - B] + p_ref[0, B:]
        o_ref[0, S - B:] = c_ref[0, S - B:]

    spec = pl.BlockSpec((1, S, Dh), lambda h: (h, 0, 0))
    return pl.pallas_call(
        body, name=name, grid=(Hk,), in_specs=[spec, spec], out_specs=spec, out_shape=jax.ShapeDtypeStruct(cur.shape, F32),
        compiler_params=_cp(("parallel",)),
    )(cur, prev)


def loss_head(y, target, name):
    S, D = y.shape
    tr = _tile(S, ROW_TILE, 16)

    def body(y_ref, t_ref, acc_ref, dy_ref):
        err = y_ref[...] - t_ref[...]
        dy_ref[...] = err * (1.0 / D)

        @pl.when(pl.program_id(0) == 0)
        def _():
            acc_ref[...] = jnp.zeros_like(acc_ref)

        acc_ref[...] += jnp.broadcast_to(jnp.sum(err * err).reshape(1, 1), (1, LANES))

    return pl.pallas_call(
        body, name=name, grid=(S // tr,), in_specs=[_row_spec(tr, D)] * 2,
        out_specs=[pl.BlockSpec((1, LANES), lambda i: (0, 0)), _row_spec(tr, D)],
        out_shape=[jax.ShapeDtypeStruct((1, LANES), F32), jax.ShapeDtypeStruct((S, D), F32)], compiler_params=_cp(("arbitrary",)),
    )(y, target)


def _adam_update(w, g, m, v):
    m = ADAM_B1 * m + (1.0 - ADAM_B1) * g
    v = ADAM_B2 * v + (1.0 - ADAM_B2) * jnp.square(g)
    m_hat = m / (1.0 - ADAM_B1 ** ADAM_STEP)
    v_hat = v / (1.0 - ADAM_B2 ** ADAM_STEP)
    delta = -ADAM_LR * (m_hat / (jnp.sqrt(v_hat) + ADAM_EPS) + ADAM_WD * w)
    return delta, m, v


def adamw(w, m, v, gparts, name, gstack=0, emit_g=True):
    R, C = w.shape
    tr = _tile(R, max(8, (128 * 1024) // C), 8)
    spec = pl.BlockSpec((tr, C), lambda i: (i, 0))
    nplain = len(gparts) - (1 if gstack else 0)
    nout = 4 if emit_g else 3

    def body(w_ref, m_ref, v_ref, *rest):
        g_refs, outs = rest[:len(gparts)], rest[len(gparts):]
        g = None
        for r in g_refs[:nplain]:
            g = r[...].astype(F32) if g is None else g + r[...].astype(F32)
        if gstack:
            for t in range(gstack):
                part = g_refs[-1][t].astype(F32)
                g = part if g is None else g + part
        res = _adam_update(w_ref[...], g, m_ref[...], v_ref[...])
        for o_ref, val in zip(outs, ((g,) if emit_g else ()) + res):
            o_ref[...] = val

    gspecs = [spec] * nplain + ([pl.BlockSpec((gstack, tr, C), lambda i: (0, i, 0))] if gstack else [])
    out = jax.ShapeDtypeStruct((R, C), F32)
    return pl.pallas_call(
        body, name=name, grid=(R // tr,), in_specs=[spec] * 3 + gspecs, out_specs=[spec] * nout, out_shape=[out] * nout,
        compiler_params=_cp(("parallel",)),
    )(w, m, v, *gparts)


def ada_fwd(c_all, ada_w, ada_b, name):
    L, D, N = ada_w.shape
    Bp = c_all.shape[0]
    tn = _tile(N, 512)

    def body(c_ref, w_ref, b_ref, o_ref):
        ca = jax.nn.silu(c_ref[...]).astype(BF16)
        o_ref[0] = jnp.dot(ca, w_ref[0].astype(BF16), preferred_element_type=F32) + b_ref[0]

    return pl.pallas_call(
        body, name=name, grid=(L, N // tn),
        in_specs=[pl.BlockSpec((Bp, D), lambda l, j: (0, 0)), pl.BlockSpec((1, D, tn), lambda l, j: (l, 0, j)),
                  pl.BlockSpec((1, 1, tn), lambda l, j: (l, 0, j))],
        out_specs=pl.BlockSpec((1, Bp, tn), lambda l, j: (l, 0, j)), out_shape=jax.ShapeDtypeStruct((L, Bp, N), F32),
        compiler_params=_cp(("parallel", "parallel")),
    )(c_all, ada_w, ada_b)


def ada_wgrad(c_all, dmod, name):
    L, Bp, N = dmod.shape
    D = c_all.shape[1]
    tn = _tile(N, 512)

    def body(c_ref, d_ref, o_ref):
        ca = jax.nn.silu(c_ref[...]).astype(BF16)
        o_ref[0] = lax.dot_general(ca, d_ref[0].astype(BF16), _DIMS["tn"], preferred_element_type=F32)

    return pl.pallas_call(
        body, name=name, grid=(L, N // tn),
        in_specs=[pl.BlockSpec((Bp, D), lambda l, j: (0, 0)), pl.BlockSpec((1, Bp, tn), lambda l, j: (l, 0, j))],
        out_specs=pl.BlockSpec((1, D, tn), lambda l, j: (l, 0, j)), out_shape=jax.ShapeDtypeStruct((L, D, N), F32),
        compiler_params=_cp(("parallel", "parallel")),
    )(c_all, dmod)


N_DEV = 8
N_CHIP = 4
ANY = pl.BlockSpec(memory_space=pl.ANY)


def _place():
    return lax.axis_index("x"), lax.axis_index("y"), lax.axis_index("c")


def _other_chips(x, y):
    chips = [(1 - x, y), (x, 1 - y), (1 - x, 1 - y)]
    return chips, [2 * cx + cy for cx, cy in chips]


def _rcopy(src, dst, ssem, rsem, to):
    return pltpu.make_async_remote_copy(src_ref=src, dst_ref=dst, send_sem=ssem, recv_sem=rsem, device_id=to, device_id_type=MESH)


def ag_small(xs, name):
    R, Wd = xs.shape

    def body(x_ref, out_ref, send_sems, recv_sems, local_sem):
        x, y, c = _place()
        me, sibling = (x, y, c), (x, y, 1 - c)
        chips, _ = _other_chips(x, y)

        def slot(px, py, pc):
            return out_ref.at[4 * px + 2 * py + pc]

        def copy(k, block, to, src=None):
            return _rcopy(slot(*block) if src is None else src, slot(*block), send_sems.at[k], recv_sems.at[k], to)

        mine = pltpu.make_async_copy(x_ref, slot(*me), local_sem)
        mine.start()
        first = [copy(0, me, sibling, src=x_ref)]
        first += [copy(1 + j, me, (*chip, c), src=x_ref) for j, chip in enumerate(chips)]
        for cp in first:
            cp.start()
        passed = [copy(4 + j, (*chip, c), sibling) for j, chip in enumerate(chips)]
        for j, chip in enumerate(chips):
            copy(1 + j, (*chip, c), me).wait_recv()
            passed[j].start()
        copy(0, sibling, me).wait_recv()
        for j, chip in enumerate(chips):
            copy(4 + j, (*chip, 1 - c), me).wait_recv()
        for cp in first + passed:
            cp.wait_send()
        mine.wait()

    vm = pl.BlockSpec(memory_space=pltpu.VMEM)
    return pl.pallas_call(
        body, name=name, out_shape=jax.ShapeDtypeStruct((N_DEV, R, Wd), xs.dtype), in_specs=[vm], out_specs=vm,
        scratch_shapes=[pltpu.SemaphoreType.DMA((7,)), pltpu.SemaphoreType.DMA((7,)), pltpu.SemaphoreType.DMA],
        compiler_params=_cp(),
    )(xs)


def _half_of_shard(by_cols, A, B, h):
    return (h * (A // 2), A // 2, 0, B) if by_cols else (0, A, h * (B // 2), B // 2)


def _shard_in_full(by_cols, A, B, q):
    return (0, q * B) if by_cols else (q * A, 0)


def _window(ref, r0, nr, c0, nc):
    return ref.at[:, pl.ds(r0, nr), pl.ds(c0, nc)]


def ag_weights(shards, by_cols, name):
    n = len(shards)
    geo, full = _ag_shapes(shards, by_cols)

    def body(*refs):
        start, finish = _ag_steps(geo, refs[:n], refs[n:2 * n], refs[2 * n:])
        start()
        finish()

    return pl.pallas_call(
        body, name=name, out_shape=full, in_specs=[ANY] * n, out_specs=[ANY] * n, scratch_shapes=_ag_semaphores(n),
        compiler_params=_cp(),
    )(*shards)


def _ag_shapes(shards, by_cols):
    geo = [(bc,) + s.shape[1:] for bc, s in zip(by_cols, shards)]
    full = [jax.ShapeDtypeStruct((s.shape[0], A, N_CHIP * B) if bc else (s.shape[0], N_CHIP * A, B), s.dtype)
            for (bc, A, B), s in zip(geo, shards)]
    return geo, full


def _ag_semaphores(n):
    return [pltpu.SemaphoreType.DMA((n, 3)) for _ in range(4)]


def _ag_steps(geo, x_refs, o_refs, sems):
    s_ici, r_ici, s_d2d, r_d2d = sems
    pairs = [(t, j) for t in range(len(geo)) for j in range(3)]

    def copies():
        x, y, c = _place()
        q = 2 * x + y
        chips, qs = _other_chips(x, y)

        def landing(t, chip_q, half):
            r0, nr, c0, nc = _half_of_shard(*geo[t], half)
            ro, co = _shard_in_full(*geo[t], chip_q)
            return _window(o_refs[t], ro + r0, nr, co + c0, nc)

        def ici(t, j, landing_q):
            src = _window(x_refs[t], *_half_of_shard(*geo[t], c))
            return _rcopy(src, landing(t, landing_q, c), s_ici.at[t, j], r_ici.at[t, j], (*chips[j], c))

        def handoff(t, j, half):
            blk = landing(t, qs[j], half)
            return _rcopy(blk, blk, s_d2d.at[t, j], r_d2d.at[t, j], (x, y, 1 - c))

        return c, q, qs, ici, handoff

    def start():
        c, q, qs, ici, handoff = copies()
        for t, j in pairs:
            ici(t, j, q).start()

    def finish():
        c, q, qs, ici, handoff = copies()
        for t, j in pairs:
            ici(t, j, qs[j]).wait_recv()
            handoff(t, j, c).start()
        for t, j in pairs:
            handoff(t, j, 1 - c).wait_recv()
        for t, j in pairs:
            ici(t, j, q).wait_send()
            handoff(t, j, c).wait_send()

    return start, finish


def _half_of_full(by_cols, A, B, h):
    return (h * (A // 2), A // 2, 0, N_CHIP * B) if by_cols else (0, N_CHIP * A, h * (B // 2), B // 2)


def _half_shape(by_cols, L, A, B):
    return (L, A // 2, N_CHIP * B) if by_cols else (L, N_CHIP * A, B // 2)


def _piece_shape(by_cols, L, A, B):
    return (L, A // 2, B) if by_cols else (L, A, B // 2)


def sibling_fold(gs, geo, name):
    sc = fold_sidecar(gs, geo)
    n = len(gs)

    def body(*refs):
        start, finish = sc.steps(refs[:n], refs[n:2 * n], refs[2 * n:])
        start()
        finish()

    return pl.pallas_call(
        body, name=name, out_shape=sc.out_shape, in_specs=[ANY] * n, out_specs=[ANY] * n, scratch_shapes=sc.semaphores,
        compiler_params=_cp(),
    )(*gs)


def fold_sidecar(gs, geo):
    n = len(gs)

    def steps(x_refs, o_refs, sems):
        ssem, rsem = sems

        def copies():
            x, y, c = _place()
            return [_rcopy(_window(x_refs[t], *_half_of_full(*geo[t], 1 - c)), o_refs[t], ssem.at[t], rsem.at[t], (x, y, 1 - c))
                    for t in range(n)]

        def start():
            for cp in copies():
                cp.start()

        def finish():
            for cp in copies():
                cp.wait()

        return start, finish

    dma = pltpu.SemaphoreType.DMA
    out = [jax.ShapeDtypeStruct(_half_shape(bc, g.shape[0], A, B), g.dtype) for (bc, A, B), g in zip(geo, gs)]
    return SideCar(list(gs), out, [dma((n,)), dma((n,))], steps)


def chip_exchange(rs, geo, name):
    sc = exchange_sidecar(rs, geo)
    n = len(rs)

    def body(*refs):
        start, finish = sc.steps(refs[:n], refs[n:2 * n], refs[2 * n:])
        start()
        finish()

    return pl.pallas_call(
        body, name=name, out_shape=sc.out_shape, in_specs=[ANY] * n, out_specs=[ANY] * n, scratch_shapes=sc.semaphores,
        compiler_params=_cp(),
    )(*rs)


def exchange_sidecar(rs, geo):
    n = len(rs)

    def steps(x_refs, o_refs, sems):
        ssem, rsem = sems

        def copies():
            x, y, c = _place()
            chips, qs = _other_chips(x, y)

            def part(t, chip_q):
                bc, A, B = geo[t]
                return _window(x_refs[t], 0, A // 2, chip_q * B, B) if bc else _window(x_refs[t], chip_q * A, A, 0, B // 2)

            return [_rcopy(part(t, qs[j]), o_refs[t].at[j], ssem.at[t, j], rsem.at[t, j], (*chips[j], c))
                    for t in range(n) for j in range(3)]

        def start():
            for cp in copies():
                cp.start()

        def finish():
            for cp in copies():
                cp.wait()

        return start, finish

    dma = pltpu.SemaphoreType.DMA
    out = [jax.ShapeDtypeStruct((3,) + _piece_shape(bc, r.shape[0], A, B), r.dtype) for (bc, A, B), r in zip(geo, rs)]
    return SideCar(list(rs), out, [dma((n, 3)), dma((n, 3))], steps)


def ag_sidecar(shards, by_cols):
    geo, full = _ag_shapes(shards, by_cols)
    return SideCar(list(shards), full, _ag_semaphores(len(shards)), lambda ins, outs, sems: _ag_steps(geo, ins, outs, sems))


def sibling_share(fs, geo, name):
    n = len(fs)

    def body(*refs):
        x_refs, o_refs, (ssem, rsem) = refs[:n], refs[n:2 * n], refs[2 * n:]
        x, y, c = _place()
        for t in range(n):
            mine = _window(o_refs[t], *_half_of_shard(*geo[t], c))
            _rcopy(mine, mine, ssem.at[t], rsem.at[t], (x, y, 1 - c)).start()
        for t in range(n):
            mine = _window(o_refs[t], *_half_of_shard(*geo[t], c))
            theirs = _window(o_refs[t], *_half_of_shard(*geo[t], 1 - c))
            _rcopy(mine, theirs, ssem.at[t], rsem.at[t], (x, y, 1 - c)).wait_recv()
            _rcopy(mine, mine, ssem.at[t], rsem.at[t], (x, y, 1 - c)).wait_send()
        del x_refs

    dma = pltpu.SemaphoreType.DMA
    return pl.pallas_call(
        body, name=name, out_shape=[jax.ShapeDtypeStruct(f.shape, f.dtype) for f in fs], in_specs=[ANY] * n, out_specs=[ANY] * n,
        input_output_aliases={t: t for t in range(n)}, scratch_shapes=[dma((n,)), dma((n,))], compiler_params=_cp(),
    )(*fs)


SUM_ROWS = 256


def fold_sum(g, recv, by_cols, A, B, qc_idx, name):
    L = g.shape[0]
    _, hr, hc = _half_shape(by_cols, L, A, B)
    tr, tc = _tile(A // 2 if by_cols else A, SUM_ROWS, 16), (B if by_cols else B // 2)
    ro, co = ((A // 2) // tr, 0) if by_cols else (0, 1)

    def body(qc_ref, g_ref, r_ref, o_ref):
        del qc_ref
        o_ref[...] = (g_ref[...].astype(F32) + r_ref[...].astype(F32)).astype(o_ref.dtype)

    spec = pl.BlockSpec((1, tr, tc), lambda l, i, j, qc: (l, i, j))
    grid_spec = pltpu.PrefetchScalarGridSpec(
        num_scalar_prefetch=1, grid=(L, hr // tr, hc // tc),
        in_specs=[pl.BlockSpec((1, tr, tc), lambda l, i, j, qc: (l, i + qc[1] * ro, j + qc[1] * co)), spec], out_specs=spec,
    )
    return pl.pallas_call(
        body, name=name, grid_spec=grid_spec, out_shape=jax.ShapeDtypeStruct((L, hr, hc), BF16),
        compiler_params=_cp(("parallel", "parallel", "parallel")),
    )(qc_idx, g, recv)


def chip_sum(r, ex, by_cols, A, B, qc_idx, name):
    L = r.shape[0]
    _, wr, wc = _piece_shape(by_cols, L, A, B)
    tr = _tile(wr, SUM_ROWS, 16)
    r_ro, r_co = (0, 1) if by_cols else (A // tr, 0)
    o_ro, o_co = ((A // 2) // tr, 0) if by_cols else (0, 1)

    def body(qc_ref, r_ref, e_ref, o_ref):
        del qc_ref
        o_ref[0] = ((r_ref[0].astype(F32) + e_ref[0, 0].astype(F32)) + e_ref[1, 0].astype(F32)) + e_ref[2, 0].astype(F32)

    grid_spec = pltpu.PrefetchScalarGridSpec(
        num_scalar_prefetch=1, grid=(L, wr // tr),
        in_specs=[pl.BlockSpec((1, tr, wc), lambda l, i, qc: (l, i + qc[0] * r_ro, qc[0] * r_co)),
                  pl.BlockSpec((3, 1, tr, wc), lambda l, i, qc: (0, l, i, 0))],
        out_specs=pl.BlockSpec((1, tr, wc), lambda l, i, qc: (l, i + qc[1] * o_ro, qc[1] * o_co)),
    )
    return pl.pallas_call(
        body, name=name, grid_spec=grid_spec, out_shape=jax.ShapeDtypeStruct((L, A, B), F32),
        compiler_params=_cp(("parallel", "parallel")),
    )(qc_idx, r, ex)


BIG = ("ffn_w_gu", "ffn_w_down", "fox_w_in", "fox_w_out", "sgu_w_in", "sgu_w_out", "swa_w_in", "swa_w_out")
COLUMN_SHARDED = ("ffn_w_gu", "fox_w_in", "sgu_w_in", "swa_w_in")
SMALL = ("ada_b", "mix_pre_g", "mix_post_g", "ffn_pre_g", "ffn_post_g", "fox_b_f", "sgu_ln_g", "sgu_ln_b", "sgu_w_s", "sgu_b_s",
         "swa_sinks")
WEIGHTS = ("ada_w", "ada_b", "mix_pre_g", "mix_post_g", "ffn_pre_g", "ffn_post_g", "ffn_w_gu", "ffn_w_down", "fox_w_in", "fox_b_f",
           "fox_w_out", "sgu_w_in", "sgu_ln_g", "sgu_ln_b", "sgu_w_s", "sgu_b_s", "sgu_w_out", "swa_w_in", "swa_sinks", "swa_w_out")
INPUTS = ("x", "c", "positions") + WEIGHTS + ("loss_target",) + tuple("m_" + n for n in WEIGHTS) + tuple("v_" + n for n in WEIGHTS)


def _lane_pad(n):
    return (-n) % LANES


def _pad_shard_columns(t, B):
    if _lane_pad(B) == 0:
        return t
    L, A, _ = t.shape
    return jnp.pad(t.reshape(L, A, N_CHIP, B), ((0, 0), (0, 0), (0, 0), (0, _lane_pad(B)))).reshape(L, A, -1)


def _unpad_shard_columns(t, B):
    if _lane_pad(B) == 0:
        return t
    L, A, _ = t.shape
    return t.reshape(L, A, N_CHIP, B + _lane_pad(B))[..., :B].reshape(L, A, N_CHIP * B)


def place_shard(full, shard, by_cols, qc_idx, name):
    L, A, B = shard.shape
    tr = _tile(A, SUM_ROWS, 16)
    ro, co = (0, 1) if by_cols else (A // tr, 0)

    def body(qc_ref, s_ref, f_ref, o_ref):
        del qc_ref, f_ref
        o_ref[...] = s_ref[...]

    grid_spec = pltpu.PrefetchScalarGridSpec(
        num_scalar_prefetch=1, grid=(L, A // tr), in_specs=[pl.BlockSpec((1, tr, B), lambda l, i, qc: (l, i, 0)), ANY],
        out_specs=pl.BlockSpec((1, tr, B), lambda l, i, qc: (l, i + qc[0] * ro, qc[0] * co)),
    )
    return pl.pallas_call(
        body, name=name, grid_spec=grid_spec, out_shape=jax.ShapeDtypeStruct(full.shape, full.dtype),
        input_output_aliases={2: 0}, compiler_params=_cp(("parallel", "parallel")),
    )(qc_idx, shard, full)


def _pad_rows(flat1d):
    n = flat1d.shape[0]
    pad = (-n) % (8 * LANES)
    return jnp.pad(flat1d, (0, pad)).reshape(-1, LANES)


def _pack_small(parts):
    return jnp.concatenate([_pad_rows(parts[n].astype(F32).reshape(-1)) for n in SMALL], axis=0)


def _unpack_small(packed, shapes):
    out, off = {}, 0
    for n in SMALL:
        size = int(np.prod(shapes[n]))
        rows = (size + 8 * LANES - 1) // (8 * LANES) * 8
        out[n] = packed[off:off + rows].reshape(-1)[:size].reshape(shapes[n])
        off += rows
    return out


def _fox_fwd(h, w_in, b_f, w_out, tag, ride_qkv=None, ride_attn=None):
    S, D = h.shape
    H = b_f.shape[0]
    qkv = mm(h, w_in, "nn", BF16, name=tag + "_qkv", b_cols=(0, 3 * D), sidecar=ride_qkv)
    qkv, rode_qkv = qkv if ride_qkv is not None else (qkv, ())
    fgp = mm(h, w_in, "nn", F32, name=tag + "_fg", b_cols=(3 * D, LANES))
    fgT = fgp[:, :H].T
    cum = fox_gate_fwd(fgT, b_f.reshape(H, 1), tag + "_gate")
    cum_col, cum_row = cum.reshape(H, S, 1), cum.reshape(H, 1, S)
    o, lse, rode_attn = fox_attn_fwd(qkv, cum_col, cum_row, H, tag + "_attn", ride_attn)
    y = mm(o, w_out, "nn", F32, name=tag + "_out")
    return y, (qkv, fgT, cum_col, cum_row, o, lse), rode_qkv, rode_attn


def _fox_bwd(dy, h, w_in, b_f, w_out, ctx, tag, ride_dq=None, ride_dkv=None):
    qkv, fgT, cum_col, cum_row, o, lse = ctx
    S, D = h.shape
    H = b_f.shape[0]
    do = mm(dy, w_out, "nt", BF16, name=tag + "_do")
    dw_out = mm(o, dy, "tn", BF16, name=tag + "_dwout")
    dq, delta, rode_dq = fox_attn_bwd_dq(qkv, do, o, lse, cum_col, cum_row, H, tag + "_dq", ride_dq)
    dk, dv, dcum, rode_dkv = fox_attn_bwd_dkv(qkv, do, lse.reshape(H, 1, S), delta.reshape(H, 1, S), cum_col, cum_row, H,
                                              tag + "_dkv", ride_dkv)
    dfgT, db = fox_gate_bwd(dcum.reshape(H, S), fgT, b_f.reshape(H, 1), tag + "_dgate")
    dfgp = jnp.pad(dfgT.T, ((0, 0), (0, LANES - H))).astype(BF16)
    dh = mm(dq, w_in, "nt", F32, name=tag + "_dhq", b_cols=(0, D))
    dh = mm(dk, w_in, "nt", F32, add=dh, name=tag + "_dhk", b_cols=(D, D))
    dh = mm(dv, w_in, "nt", F32, add=dh, name=tag + "_dhv", b_cols=(2 * D, D))
    dh = mm(dfgp, w_in, "nt", F32, add=dh, name=tag + "_dhf", b_cols=(3 * D, LANES))
    dw_in = jnp.concatenate(
        [mm(h, dq, "tn", BF16, name=tag + "_dwq"), mm(h, dk, "tn", BF16, name=tag + "_dwk"), mm(h, dv, "tn", BF16, name=tag + "_dwv"),
         mm(h, dfgp, "tn", BF16, name=tag + "_dwf")[:, :H]], axis=1)
    return dh, dw_in, dw_out, db[:, 0], rode_dq, rode_dkv


def _sgu_consts(G, W):
    return jnp.asarray(np.repeat(np.eye(G, dtype=np.float32), W // G, axis=1))


def _sgu_fwd(h, w_in, ln_g, ln_b, w_s, b_s, w_out, tag):
    G, W = w_s.shape[0], ln_g.shape[0]
    zp = mm(h, w_in, "nn", F32, name=tag + "_in")
    args = (zp, ln_g.reshape(1, W), ln_b.reshape(1, W), w_s, b_s.T, _sgu_consts(G, W))
    gated = sgu_fwd(*args, tag + "_core")
    y = mm(gated, w_out, "nn", F32, name=tag + "_out")
    return y, (args, gated)


def _sgu_bwd(dy, h, w_in, w_out, ctx, tag):
    args, gated = ctx
    dgated = mm(dy, w_out, "nt", BF16, name=tag + "_dgated")
    dw_out = mm(gated, dy, "tn", BF16, name=tag + "_dwout")
    dzp, dlg, dlb, dws, dbsT = sgu_bwd(*args, dgated, tag + "_dcore")
    dh = mm(dzp, w_in, "nt", F32, name=tag + "_dh")
    dw_in = mm(h, dzp, "tn", BF16, name=tag + "_dwin")
    return dh, dw_in, dw_out, dlg[0], dlb[0], dws, dbsT.T


def _rope_tables(positions):
    inv = ROPE_THETA ** (-jnp.arange(0, ROPE_DIM, 2, dtype=F32) / ROPE_DIM)
    ang = positions.astype(F32)[:, None] * inv
    S = positions.shape[0]
    rest = SWA_HEAD_DIM - ROPE_DIM
    C = jnp.concatenate([jnp.cos(ang), jnp.cos(ang), jnp.ones((S, rest), F32)], axis=1)
    Sn = jnp.concatenate([jnp.sin(ang), jnp.sin(ang), jnp.zeros((S, rest), F32)], axis=1)
    return C, Sn


def _heads(t, n):
    return t.reshape(t.shape[0], n, SWA_HEAD_DIM).transpose(1, 0, 2)


def _unheads(t):
    return t.transpose(1, 0, 2).reshape(t.shape[1], -1)


def _swa_fwd(h, w_in, sinks, w_out, tables, tag):
    Hq = sinks.shape[0]
    Hk = (w_in[0].shape[-1] // SWA_HEAD_DIM - Hq) // 2
    G = Hq // Hk
    proj = mm(h, w_in, "nn", F32, name=tag + "_in")
    qh = _heads(proj[:, :Hq * SWA_HEAD_DIM], Hq)
    kh = _heads(proj[:, Hq * SWA_HEAD_DIM:(Hq + Hk) * SWA_HEAD_DIM], Hk)
    vh = _heads(proj[:, (Hq + Hk) * SWA_HEAD_DIM:], Hk)
    sink_col = jnp.repeat(sinks.reshape(Hk, G), SEQ_BLOCK, axis=1).reshape(Hk, G * SEQ_BLOCK, 1)
    args = (qh, kh, vh, sink_col, tables[0], tables[1], jnp.asarray(_rope_matrix()), jnp.asarray(_swa_mask_bias(G)))
    o = _unheads(swa_fwd(*args, tag + "_core"))
    y = mm(o, w_out, "nn", F32, name=tag + "_out")
    return y, (args, o)


def _swa_bwd(dy, h, w_in, w_out, ctx, tag):
    args, o = ctx
    Hq = args[0].shape[0]
    do = mm(dy, w_out, "nt", BF16, name=tag + "_do")
    dw_out = mm(o, dy, "tn", BF16, name=tag + "_dwout")
    dqh, dkp, dkc, dvp, dvc, dsk = swa_bwd(*args, _heads(do, Hq), tag + "_dcore")
    dk = shift_add(dkc, dkp, tag + "_dk")
    dv = shift_add(dvc, dvp, tag + "_dv")
    dproj = jnp.concatenate([_unheads(dqh), _unheads(dk), _unheads(dv)], axis=1).astype(BF16)
    dh = mm(dproj, w_in, "nt", F32, name=tag + "_dh")
    dw_in = mm(h, dproj, "tn", BF16, name=tag + "_dwin")
    return dh, dw_in, dw_out, dsk[:, :, 0].reshape(Hq)


def kernel(x, c, positions, ada_w, ada_b, mix_pre_g, mix_post_g, ffn_pre_g, ffn_post_g, ffn_w_gu, ffn_w_down, fox_w_in, fox_b_f, fox_w_out, sgu_w_in, sgu_ln_g, sgu_ln_b, sgu_w_s, sgu_b_s, sgu_w_out, swa_w_in, swa_sinks, swa_w_out, loss_target, m_ada_w, m_ada_b, m_mix_pre_g, m_mix_post_g, m_ffn_pre_g, m_ffn_post_g, m_ffn_w_gu, m_ffn_w_down, m_fox_w_in, m_fox_b_f, m_fox_w_out, m_sgu_w_in, m_sgu_ln_g, m_sgu_ln_b, m_sgu_w_s, m_sgu_b_s, m_sgu_w_out, m_swa_w_in, m_swa_sinks, m_swa_w_out, v_ada_w, v_ada_b, v_mix_pre_g, v_mix_post_g, v_ffn_pre_g, v_ffn_post_g, v_ffn_w_gu, v_ffn_w_down, v_fox_w_in, v_fox_b_f, v_fox_w_out, v_sgu_w_in, v_sgu_ln_g, v_sgu_ln_b, v_sgu_w_s, v_sgu_b_s, v_sgu_w_out, v_swa_w_in, v_swa_sinks, v_swa_w_out):
    P = dict(zip(INPUTS, (x, c, positions, ada_w, ada_b, mix_pre_g, mix_post_g, ffn_pre_g, ffn_post_g, ffn_w_gu, ffn_w_down, fox_w_in, fox_b_f, fox_w_out, sgu_w_in, sgu_ln_g, sgu_ln_b, sgu_w_s, sgu_b_s, sgu_w_out, swa_w_in, swa_sinks, swa_w_out, loss_target, m_ada_w, m_ada_b, m_mix_pre_g, m_mix_post_g, m_ffn_pre_g, m_ffn_post_g, m_ffn_w_gu, m_ffn_w_down, m_fox_w_in, m_fox_b_f, m_fox_w_out, m_sgu_w_in, m_sgu_ln_g, m_sgu_ln_b, m_sgu_w_s, m_sgu_b_s, m_sgu_w_out, m_swa_w_in, m_swa_sinks, m_swa_w_out, v_ada_w, v_ada_b, v_mix_pre_g, v_mix_post_g, v_ffn_pre_g, v_ffn_post_g, v_ffn_w_gu, v_ffn_w_down, v_fox_w_in, v_fox_b_f, v_fox_w_out, v_sgu_w_in, v_sgu_ln_g, v_sgu_ln_b, v_sgu_w_s, v_sgu_b_s, v_sgu_w_out, v_swa_w_in, v_swa_sinks, v_swa_w_out)))
    xs, target, pos = x[0], loss_target[0], positions[0]
    S, D = xs.shape
    L = ada_w.shape[0]
    n_mix = 3
    F = ffn_w_down.shape[1] * N_CHIP
    xi, yi, ci = _place()
    q_me = 2 * xi + yi
    dev = 4 * xi + 2 * yi + ci

    qc = jnp.stack([q_me, ci]).astype(jnp.int32)
    by_cols = {n: n in COLUMN_SHARDED for n in BIG}
    geo = {n: (by_cols[n], P[n].shape[1], P[n].shape[2] + (_lane_pad(P[n].shape[2]) if by_cols[n] else 0)) for n in BIG}
    groups = {
        "fox0": {n: (0, 1) for n in BIG if n.startswith("fox_")},
        "ffn0": {n: (0, 1) for n in BIG if n.startswith("ffn_")},
        "rest": {n: (1 if n.startswith(("fox_", "ffn_")) else 0, P[n].shape[0]) for n in BIG},
    }
    groups["rest"] = {n: r for n, r in groups["rest"].items() if r[1] > r[0]}

    def shard_of(n, lo, hi):
        s = P[n][lo:hi].astype(BF16)
        return jnp.pad(s, ((0, 0), (0, 0), (0, _lane_pad(s.shape[2])))) if by_cols[n] else s

    Wt = {n: [] for n in BIG}

    def finish_gather(g, fulls):
        for n, s, f in zip(groups[g], shards[g], fulls):
            f = place_shard(f, s, by_cols[n], qc, f"place_{g}_{n}")
            f = _unpad_shard_columns(f, P[n].shape[2]) if by_cols[n] else f
            f = jnp.pad(f, ((0, 0), (0, 0), (0, 3 * D + LANES - f.shape[2]))) if n == "fox_w_in" else f
            Wt[n] += [(f, l) for l in range(f.shape[0])]

    shards = {g: [shard_of(n, *r) for n, r in groups[g].items()] for g in groups}
    group_cols = {g: [by_cols[n] for n in groups[g]] for g in groups}
    finish_gather("fox0", ag_weights(shards["fox0"], group_cols["fox0"], "ag_weights_fox0"))

    c_all = ag_small(c.reshape(D // LANES, LANES), "ag_c").reshape(N_DEV, D)
    c_all = jnp.pad(c_all, ((0, 16 - N_DEV), (0, 0)))
    Nm = ada_w.shape[2]
    ada_b_mine = lax.dynamic_slice_in_dim(ada_b, q_me * Nm, Nm, axis=1).reshape(L, 1, Nm)
    modp = ada_fwd(c_all, ada_w, ada_b_mine, "ada_fwd")[:, :N_DEV]
    mod_all = ag_small(modp.reshape(-1, LANES), "ag_mod").reshape(N_DEV, L, N_DEV, Nm)
    mod_mine = lax.dynamic_index_in_dim(mod_all[0::2], dev, axis=2, keepdims=False)
    mods = mod_mine.transpose(1, 0, 2).reshape(L, 6, 1, D)

    tables = _rope_tables(pos)

    saved = []
    xc = xs
    for i in range(L):
        kind, j = i % n_mix, i // n_mix
        sh_m, sc_m, g_m, sh_f, sc_f, g_f = (mods[i, t] for t in range(6))
        t = f"l{i}"
        h1 = pre_fwd(xc, mix_pre_g[i:i + 1], sh_m, sc_m, t + "_pre_m")
        if kind == 0:
            rides = [ag_sidecar(shards[g], group_cols[g]) if i == 0 else None for g in ("ffn0", "rest")]
            y1, ctx, ffn0_fulls, rest_fulls = _fox_fwd(h1, Wt["fox_w_in"][j], fox_b_f[j], Wt["fox_w_out"][j], t + "_fox", *rides)
            if i == 0:
                finish_gather("ffn0", ffn0_fulls)
                finish_gather("rest", rest_fulls)
        elif kind == 1:
            y1, ctx = _sgu_fwd(h1, Wt["sgu_w_in"][j], sgu_ln_g[j], sgu_ln_b[j], sgu_w_s[j], sgu_b_s[j], Wt["sgu_w_out"][j], t + "_sgu")
        else:
            y1, ctx = _swa_fwd(h1, Wt["swa_w_in"][j], swa_sinks[j], Wt["swa_w_out"][j], tables, t + "_swa")
        xm = post_fwd(xc, y1, mix_post_g[i:i + 1], g_m, t + "_post_m")
        h2 = pre_fwd(xm, ffn_pre_g[i:i + 1], sh_f, sc_f, t + "_pre_f")
        gu = mm(h2, Wt["ffn_w_gu"][i], "nn", BF16, name=t + "_ffn_gu")
        a = act_fwd(gu, t + "_act")
        y2 = mm(a, Wt["ffn_w_down"][i], "nn", F32, name=t + "_ffn_down")
        xn = post_fwd(xm, y2, ffn_post_g[i:i + 1], g_f, t + "_post_f")
        saved.append((xc, h1, y1, ctx, xm, h2, gu, a, y2))
        xc = xn

    sq, dx = loss_head(xc, target, "loss_head")
    loss = lax.psum(sq[0, 0] * (0.5 / D), ("x", "y", "c"))

    big_g = {n: [None] * P[n].shape[0] for n in BIG}
    small_g = {n: [None] * P[n].shape[0] for n in SMALL}
    group_geo = {g: [geo[n] for n in groups[g]] for g in groups}
    group_grads, from_sibling, chip_part, from_chips = {}, {}, {}, {}

    def grads_of(g):
        if g not in group_grads:
            gs = [jnp.stack(big_g[n][lo:hi]) for n, (lo, hi) in groups[g].items()]
            group_grads[g] = [_pad_shard_columns(t, P[n].shape[2]) if by_cols[n] else t for n, t in zip(groups[g], gs)]
        return group_grads[g]

    def fold(g):
        if g not in from_sibling:
            from_sibling[g] = sibling_fold(grads_of(g), group_geo[g], "rs_fold_" + g)
        chip_part[g] = [fold_sum(t, r, *m, qc, f"rs_fold_sum_{g}_{n}")
                        for n, t, r, m in zip(groups[g], grads_of(g), from_sibling[g], group_geo[g])]
        return exchange_sidecar(chip_part[g], group_geo[g])

    for i in reversed(range(L)):
        kind, j = i % n_mix, i // n_mix
        sh_m, sc_m, g_m, sh_f, sc_f, g_f = (mods[i, t] for t in range(6))
        xc, h1, y1, ctx, xm, h2, gu, a, y2 = saved[i]
        t = f"l{i}"
        dy2, dgpost_f, dgate_f = post_bwd(y2, ffn_post_g[i:i + 1], g_f, dx, t + "_dpost_f")
        if i == 0:
            da, from_sibling["rest"] = mm(dy2, Wt["ffn_w_down"][i], "nt", F32, name=t + "_da",
                                          sidecar=fold_sidecar(grads_of("rest"), group_geo["rest"]))
        else:
            da = mm(dy2, Wt["ffn_w_down"][i], "nt", F32, name=t + "_da")
        big_g["ffn_w_down"][i] = mm(a, dy2, "tn", BF16, name=t + "_dwdown")
        dgu = act_bwd(gu, da, t + "_dact")
        dh2 = mm(dgu, Wt["ffn_w_gu"][i], "nt", F32, name=t + "_dh2")
        big_g["ffn_w_gu"][i] = mm(h2, dgu, "tn", BF16, name=t + "_dwgu")
        dxm, dgpre_f, dsh_f, dsc_f = pre_bwd(xm, ffn_pre_g[i:i + 1], sh_f, sc_f, dh2, dx, t + "_dpre_f")
        dy1, dgpost_m, dgate_m = post_bwd(y1, mix_post_g[i:i + 1], g_m, dxm, t + "_dpost_m")
        if kind == 0:
            rides = [fold(g) if i == 0 else None for g in ("ffn0", "rest")]
            dh1, dw_in, dw_out, db, from_ffn0, from_rest = _fox_bwd(dy1, h1, Wt["fox_w_in"][j], fox_b_f[j], Wt["fox_w_out"][j], ctx,
                                                                    t + "_fox", *rides)
            big_g["fox_w_in"][j], big_g["fox_w_out"][j], small_g["fox_b_f"][j] = dw_in, dw_out, db
            if i == 0:
                from_chips["ffn0"], from_chips["rest"] = from_ffn0, from_rest
        elif kind == 1:
            dh1, dw_in, dw_out, dlg, dlb, dws, dbs = _sgu_bwd(dy1, h1, Wt["sgu_w_in"][j], Wt["sgu_w_out"][j], ctx, t + "_sgu")
            big_g["sgu_w_in"][j], big_g["sgu_w_out"][j] = dw_in, dw_out
            small_g["sgu_ln_g"][j], small_g["sgu_ln_b"][j], small_g["sgu_w_s"][j], small_g["sgu_b_s"][j] = dlg, dlb, dws, dbs
        else:
            dh1, dw_in, dw_out, dsk = _swa_bwd(dy1, h1, Wt["swa_w_in"][j], Wt["swa_w_out"][j], ctx, t + "_swa")
            big_g["swa_w_in"][j], big_g["swa_w_out"][j], small_g["swa_sinks"][j] = dw_in, dw_out, dsk
        dx, dgpre_m, dsh_m, dsc_m = pre_bwd(xc, mix_pre_g[i:i + 1], sh_m, sc_m, dh1, dxm, t + "_dpre_m")
        small_g["ada_b"][i] = jnp.concatenate([dsh_m, dsc_m, dgate_m, dsh_f, dsc_f, dgate_f], axis=1)[0]
        small_g["mix_pre_g"][i], small_g["mix_post_g"][i] = dgpre_m[0], dgpost_m[0]
        small_g["ffn_pre_g"][i], small_g["ffn_post_g"][i] = dgpre_f[0], dgpost_f[0]
    grad_x = dx[None]

    shapes = {n: P[n].shape for n in SMALL}
    small_parts = ag_small(_pack_small({n: jnp.stack(small_g[n]) for n in SMALL}), "ag_small_grads")
    sg, sd, sm, sv = adamw(_pack_small({n: P[n] for n in SMALL}), _pack_small({n: P["m_" + n] for n in SMALL}),
                           _pack_small({n: P["v_" + n] for n in SMALL}), [small_parts], "adamw_small", gstack=N_DEV)
    out_g, out_d, out_m, out_v = (_unpack_small(t, shapes) for t in (sg, sd, sm, sv))

    dmod_all = small_parts[:, :L * 6 * D // LANES].reshape(N_DEV, L, 6 * D)
    dmod_mine = lax.dynamic_slice_in_dim(dmod_all, q_me * Nm, Nm, axis=2).transpose(1, 0, 2)
    dmod_mine = jnp.pad(dmod_mine, ((0, 0), (0, 16 - N_DEV), (0, 0)))
    g_ada = ada_wgrad(c_all, dmod_mine, "ada_wgrad")
    r2 = lambda t: t.reshape(-1, t.shape[-1])
    res = adamw(r2(ada_w), r2(m_ada_w), r2(v_ada_w), [r2(g_ada)], "adamw_ada_w", emit_g=False)
    out_g["ada_w"] = g_ada
    out_d["ada_w"], out_m["ada_w"], out_v["ada_w"] = (t.reshape(ada_w.shape) for t in res)

    fold("fox0")
    from_chips["fox0"] = chip_exchange(chip_part["fox0"], group_geo["fox0"], "rs_exchange_fox0")
    pieces = [(g, n, m, r, e) for g in groups for n, m, r, e in zip(groups[g], group_geo[g], chip_part[g], from_chips[g])]
    mine = [chip_sum(r, e, *m, qc, f"rs_chip_sum_{g}_{n}") for g, n, m, r, e in pieces]
    shared = sibling_share(mine, [m for _, _, m, _, _ in pieces], "rs_share")
    by_name = {n: [] for n in BIG}
    for (g, n, *_), t in zip(pieces, shared):
        by_name[n].append(t)
    for n in BIG:
        gsh = by_name[n][0] if len(by_name[n]) == 1 else jnp.concatenate(by_name[n], axis=0)
        gsh = gsh[:, :, :P[n].shape[2]]
        res = adamw(r2(P[n]), r2(P["m_" + n]), r2(P["v_" + n]), [r2(gsh)], "adamw_" + n, emit_g=False)
        out_g[n] = gsh
        out_d[n], out_m[n], out_v[n] = (t.reshape(P[n].shape) for t in res)

    return (loss, grad_x, *[out_g[n] for n in WEIGHTS], *[out_d[n] for n in WEIGHTS], *[out_m[n] for n in WEIGHTS],
            *[out_v[n] for n in WEIGHTS])
```

```python
from typing import Callable, NamedTuple

import numpy as np
import jax
import jax.numpy as jnp
from jax import lax
from jax.experimental import pallas as pl
from jax.experimental.pallas import tpu as pltpu

F32 = jnp.float32
BF16 = jnp.bfloat16
MESH = pl.DeviceIdType.MESH

EPS = 1e-6
NEG = -1e30
V7X_VMEM_BYTES = 64 * 1024 * 1024
VMEM_LIMIT = V7X_VMEM_BYTES - 8 * 1024 * 1024
LANES = 128
SEQ_BLOCK = 128
SWA_HEAD_DIM = 64
ROPE_DIM = SWA_HEAD_DIM // 4
ROPE_THETA = 500000.0

ADAM_LR = 0.001
ADAM_B1 = 0.9
ADAM_B2 = 0.999
ADAM_EPS = 1e-08
ADAM_WD = 0.01
ADAM_STEP = 10


def _cp(sem=None, **kw):
    return pltpu.CompilerParams(dimension_semantics=sem, vmem_limit_bytes=VMEM_LIMIT, **kw)


def _tile(dim, pref, mult=LANES):
    if dim <= pref:
        return dim
    t = (pref // mult) * mult
    while t >= mult:
        if dim % t == 0:
            return t
        t -= mult
    return dim


_DIMS = {"nn": (((1,), (0,)), ((), ())), "nt": (((1,), (1,)), ((), ())), "tn": (((0,), (0,)), ((), ()))}


MM_TILES = {"nn": (1024, 512, 2816), "nt": (1024, 1024, 2816), "tn": (1024, 1024, 2048)}


def mm(a, b, mode="nn", out_dtype=F32, add=None, name="mm", b_cols=None, tm=None, tn=None, tk=None, sidecar=None):
    tm, tn, tk = (d if t is None else t for t, d in zip((tm, tn, tk), MM_TILES[mode]))
    b, b_layer = b if isinstance(b, tuple) else (b, None)
    b_shape = b.shape[-2:]
    c0 = 0
    if b_cols is not None:
        c0, csize = b_cols
    if mode == "nn":
        (M, K), (K2, N) = a.shape, b_shape
        if b_cols is not None:
            N = csize
    elif mode == "nt":
        (M, K), (N, K2) = a.shape, b_shape
        if b_cols is not None:
            K2 = csize
    else:
        (K, M), (K2, N) = a.shape, b_shape
        assert b_cols is None
    assert K == K2, (a.shape, b.shape, mode)
    tm = _tile(M, tm, LANES if mode == "tn" else 16)
    tn = _tile(N, tn)
    tk = _tile(K, tk, LANES if mode != "tn" else 16)
    nk = K // tk
    if b_cols is not None:
        assert c0 % (tn if mode == "nn" else tk) == 0, (b_cols, tn, tk)
    bo = c0 // (tn if mode == "nn" else tk)
    dims = _DIMS[mode]
    has_add = add is not None

    def body(a_ref, b_ref, *rest):
        if has_add:
            add_ref, o_ref, acc_ref = rest
        else:
            o_ref, acc_ref = rest
        k = pl.program_id(2)
        p = lax.dot_general(a_ref[...].astype(BF16), b_ref[...].astype(BF16), dims, preferred_element_type=F32)

        @pl.when(k == 0)
        def _():
            acc_ref[...] = p + add_ref[...].astype(F32) if has_add else p

        @pl.when(k > 0)
        def _():
            acc_ref[...] += p

        @pl.when(k == nk - 1)
        def _():
            o_ref[...] = acc_ref[...].astype(o_ref.dtype)

    a_spec = pl.BlockSpec((tk, tm), lambda i, j, k: (k, i)) if mode == "tn" else pl.BlockSpec((tm, tk), lambda i, j, k: (i, k))
    b_blk, b_idx = ((tn, tk), lambda i, j, k: (j, k + bo)) if mode == "nt" else ((tk, tn), lambda i, j, k: (k, j + bo))
    if b_layer is None:
        b_spec = pl.BlockSpec(b_blk, b_idx)
    else:
        b_spec = pl.BlockSpec((None,) + b_blk, lambda i, j, k: (b_layer,) + b_idx(i, j, k))
    o_spec = pl.BlockSpec((tm, tn), lambda i, j, k: (i, j))
    in_specs = [a_spec, b_spec] + ([o_spec] if has_add else [])
    args = (a, b) + ((add,) if has_add else ())
    out_shape = jax.ShapeDtypeStruct((M, N), out_dtype)
    scratch = [pltpu.VMEM((tm, tn), F32)]
    grid = (M // tm, N // tn, nk)
    if sidecar is None:
        return pl.pallas_call(
            body, name=name, grid=grid, in_specs=in_specs, out_specs=o_spec, out_shape=out_shape, scratch_shapes=scratch,
            compiler_params=_cp(("parallel", "parallel", "arbitrary")),
        )(*args)
    first = lambda: (pl.program_id(0) == 0) & (pl.program_id(1) == 0) & (pl.program_id(2) == 0)
    last = lambda: (pl.program_id(0) == grid[0] - 1) & (pl.program_id(1) == grid[1] - 1) & (pl.program_id(2) == grid[2] - 1)
    res = pl.pallas_call(
        _with_sidecar(sidecar, len(args), 1, 1, body, first, last), name=name, grid=grid,
        in_specs=in_specs + _sc_specs(sidecar, False), out_specs=[o_spec] + _sc_specs(sidecar, True),
        out_shape=[out_shape] + _sc_out(sidecar), scratch_shapes=scratch + _sc_sems(sidecar),
        compiler_params=_cp(("arbitrary", "arbitrary", "arbitrary")),
    )(*args, *_sc_arrays(sidecar))
    return res[0], res[1:]


def _rms(x, g):
    return (x * lax.rsqrt(jnp.mean(x * x, axis=-1, keepdims=True) + EPS)) * g


def _pre(x, g, sh, sc):
    return _rms(x, g) * (1 + sc) + sh


def _post(x, y, g, gate):
    return x + gate * _rms(y, g)


ROW_TILE = 256


def _row_spec(tr, d):
    return pl.BlockSpec((tr, d), lambda i: (i, 0))


def _vec_spec(d):
    return pl.BlockSpec((1, d), lambda i: (0, 0))


def pre_fwd(x, g, sh, sc, name):
    S, D = x.shape
    tr = _tile(S, ROW_TILE, 16)

    def body(x_ref, g_ref, sh_ref, sc_ref, h_ref):
        h_ref[...] = _pre(x_ref[...], g_ref[...], sh_ref[...], sc_ref[...]).astype(h_ref.dtype)

    return pl.pallas_call(
        body, name=name, grid=(S // tr,), in_specs=[_row_spec(tr, D)] + [_vec_spec(D)] * 3, out_specs=_row_spec(tr, D),
        out_shape=jax.ShapeDtypeStruct((S, D), BF16), compiler_params=_cp(("parallel",)),
    )(x, g, sh, sc)


def pre_bwd(x, g, sh, sc, dh, dres, name):
    S, D = x.shape
    tr = _tile(S, ROW_TILE, 16)

    def body(x_ref, g_ref, sh_ref, sc_ref, dh_ref, dres_ref, dx_ref, dg_ref, dsh_ref, dsc_ref):
        _, vjp = jax.vjp(_pre, x_ref[...], g_ref[...], sh_ref[...], sc_ref[...])
        dx, dg, dsh, dsc = vjp(dh_ref[...].astype(F32))
        dx_ref[...] = dres_ref[...] + dx

        @pl.when(pl.program_id(0) == 0)
        def _():
            dg_ref[...] = jnp.zeros_like(dg_ref)
            dsh_ref[...] = jnp.zeros_like(dsh_ref)
            dsc_ref[...] = jnp.zeros_like(dsc_ref)

        dg_ref[...] += dg
        dsh_ref[...] += dsh
        dsc_ref[...] += dsc

    vec = jax.ShapeDtypeStruct((1, D), F32)
    return pl.pallas_call(
        body, name=name, grid=(S // tr,), in_specs=[_row_spec(tr, D)] + [_vec_spec(D)] * 3 + [_row_spec(tr, D)] * 2,
        out_specs=[_row_spec(tr, D)] + [_vec_spec(D)] * 3, out_shape=[jax.ShapeDtypeStruct((S, D), F32), vec, vec, vec],
        compiler_params=_cp(("arbitrary",)),
    )(x, g, sh, sc, dh, dres)


def post_fwd(x, y, g, gate, name):
    S, D = x.shape
    tr = _tile(S, ROW_TILE, 16)

    def body(x_ref, y_ref, g_ref, gate_ref, o_ref):
        o_ref[...] = _post(x_ref[...], y_ref[...], g_ref[...], gate_ref[...])

    return pl.pallas_call(
        body, name=name, grid=(S // tr,), in_specs=[_row_spec(tr, D)] * 2 + [_vec_spec(D)] * 2, out_specs=_row_spec(tr, D),
        out_shape=jax.ShapeDtypeStruct((S, D), F32), compiler_params=_cp(("parallel",)),
    )(x, y, g, gate)


def post_bwd(y, g, gate, dxn, name):
    S, D = y.shape
    tr = _tile(S, ROW_TILE, 16)

    def body(y_ref, g_ref, gate_ref, dxn_ref, dy_ref, dg_ref, dgate_ref):
        fn = lambda yy, gg, gt: gt * _rms(yy, gg)
        _, vjp = jax.vjp(fn, y_ref[...], g_ref[...], gate_ref[...])
        dy, dg, dgate = vjp(dxn_ref[...])
        dy_ref[...] = dy.astype(dy_ref.dtype)

        @pl.when(pl.program_id(0) == 0)
        def _():
            dg_ref[...] = jnp.zeros_like(dg_ref)
            dgate_ref[...] = jnp.zeros_like(dgate_ref)

        dg_ref[...] += dg
        dgate_ref[...] += dgate

    vec = jax.ShapeDtypeStruct((1, D), F32)
    return pl.pallas_call(
        body, name=name, grid=(S // tr,), in_specs=[_row_spec(tr, D)] + [_vec_spec(D)] * 2 + [_row_spec(tr, D)],
        out_specs=[_row_spec(tr, D)] + [_vec_spec(D)] * 2, out_shape=[jax.ShapeDtypeStruct((S, D), BF16), vec, vec],
        compiler_params=_cp(("arbitrary",)),
    )(y, g, gate, dxn)


def _swiglu(g, u):
    return jax.nn.silu(g) * u


ACT_ROWS = 256


def act_fwd(gu, name):
    S, F2 = gu.shape
    F = F2 // 2
    tr = _tile(S, ACT_ROWS, 16)

    def body(gu_ref, a_ref):
        a_ref[...] = _swiglu(gu_ref[:, :F].astype(F32), gu_ref[:, F:].astype(F32)).astype(a_ref.dtype)

    return pl.pallas_call(
        body, name=name, grid=(S // tr,), in_specs=[_row_spec(tr, F2)], out_specs=_row_spec(tr, F),
        out_shape=jax.ShapeDtypeStruct((S, F), BF16), compiler_params=_cp(("parallel",)),
    )(gu)


def act_bwd(gu, da, name):
    S, F2 = gu.shape
    F = F2 // 2
    tr = _tile(S, ACT_ROWS, 16)

    def body(gu_ref, da_ref, dgu_ref):
        _, vjp = jax.vjp(_swiglu, gu_ref[:, :F].astype(F32), gu_ref[:, F:].astype(F32))
        dg, du = vjp(da_ref[...].astype(F32))
        dgu_ref[:, :F] = dg.astype(dgu_ref.dtype)
        dgu_ref[:, F:] = du.astype(dgu_ref.dtype)

    return pl.pallas_call(
        body, name=name, grid=(S // tr,), in_specs=[_row_spec(tr, F2), _row_spec(tr, F)], out_specs=_row_spec(tr, F2),
        out_shape=jax.ShapeDtypeStruct((S, F2), BF16), compiler_params=_cp(("parallel",)),
    )(gu, da)


GATE_CHUNK = 512


def _tri(upper):
    r = lax.broadcasted_iota(jnp.int32, (GATE_CHUNK, GATE_CHUNK), 0)
    c = lax.broadcasted_iota(jnp.int32, (GATE_CHUNK, GATE_CHUNK), 1)
    return ((r <= c) if upper else (r >= c)).astype(F32)


def _hdot(a, b):
    return jnp.dot(a, b, precision=lax.Precision.HIGHEST, preferred_element_type=F32)


def fox_gate_fwd(fgT, b, name):
    H, S = fgT.shape
    C = _tile(S, GATE_CHUNK)
    assert C == GATE_CHUNK, (S, C)
    spec = pl.BlockSpec((H, C), lambda ch: (0, ch))

    def body(fg_ref, b_ref, cum_ref, carry_ref):
        @pl.when(pl.program_id(0) == 0)
        def _():
            carry_ref[...] = jnp.zeros_like(carry_ref)

        lf = jax.nn.log_sigmoid(fg_ref[...] + b_ref[...])
        cum_ref[...] = _hdot(lf, _tri(True)) + carry_ref[...]
        carry_ref[...] += _hdot(lf, jnp.ones((C, C), F32))

    return pl.pallas_call(
        body, name=name, grid=(S // C,), in_specs=[spec, pl.BlockSpec((H, 1), lambda ch: (0, 0))], out_specs=spec,
        out_shape=jax.ShapeDtypeStruct((H, S), F32), scratch_shapes=[pltpu.VMEM((H, C), F32)], compiler_params=_cp(("arbitrary",)),
    )(fgT, b)


def fox_gate_bwd(dcum, fgT, b, name):
    H, S = fgT.shape
    C = _tile(S, GATE_CHUNK)
    assert C == GATE_CHUNK, (S, C)
    nch = S // C
    spec = pl.BlockSpec((H, C), lambda t: (0, nch - 1 - t))

    def body(dcum_ref, fg_ref, b_ref, dfg_ref, db_ref, tail_ref):
        @pl.when(pl.program_id(0) == 0)
        def _():
            tail_ref[...] = jnp.zeros_like(tail_ref)
            db_ref[...] = jnp.zeros_like(db_ref)

        dlf = _hdot(dcum_ref[...], _tri(False)) + tail_ref[...]
        dfg = dlf * jax.nn.sigmoid(-(fg_ref[...] + b_ref[...]))
        dfg_ref[...] = dfg
        tail_ref[...] += _hdot(dcum_ref[...], jnp.ones((C, C), F32))
        db_ref[...] += _hdot(dfg, jnp.ones((C, LANES), F32))

    return pl.pallas_call(
        body, name=name, grid=(nch,), in_specs=[spec, spec, pl.BlockSpec((H, 1), lambda t: (0, 0))],
        out_specs=[spec, pl.BlockSpec((H, LANES), lambda t: (0, 0))],
        out_shape=[jax.ShapeDtypeStruct((H, S), F32), jax.ShapeDtypeStruct((H, LANES), F32)],
        scratch_shapes=[pltpu.VMEM((H, C), F32)], compiler_params=_cp(("arbitrary",)),
    )(dcum, fgT, b)


FOX_TILE = 1024
FOX_FWD_STRIP = 256


def _on_and_below_diagonal(i, j, tile):
    @pl.when(j < i)
    def _():
        tile(False)

    @pl.when(j == i)
    def _():
        tile(True)


def _causal_pairs(n, by_key):
    pairs = [(i, j) for i in range(n) for j in range(i + 1)]
    if by_key:
        pairs.sort(key=lambda p: (p[1], p[0]))
    qi = np.asarray([p[0] for p in pairs], np.int32)
    kj = np.asarray([p[1] for p in pairs], np.int32)
    return qi, kj


def _fox_scores(q, k, fq, fk, T, scale, transposed):
    r = lax.broadcasted_iota(jnp.int32, (T, T), 0)
    c = lax.broadcasted_iota(jnp.int32, (T, T), 1)
    if transposed:
        return lax.dot_general(k, q, _DIMS["nt"], preferred_element_type=F32) * scale + (fq - fk), r <= c
    return lax.dot_general(q, k, _DIMS["nt"], preferred_element_type=F32) * scale + (fq - fk), c <= r


class SideCar(NamedTuple):
    arrays: list
    out_shape: list
    semaphores: list
    steps: Callable


def _sc_specs(sc, out):
    return [] if sc is None else [pl.BlockSpec(memory_space=pl.ANY)] * len(sc.out_shape if out else sc.arrays)


def _sc_sems(sc):
    return [] if sc is None else list(sc.semaphores)


def _sc_out(sc):
    return [] if sc is None else list(sc.out_shape)


def _sc_arrays(sc):
    return [] if sc is None else list(sc.arrays)


def _with_sidecar(sc, n_in, n_out, n_scratch, body, first, last):
    if sc is None:
        return body
    a, o = len(sc.arrays), len(sc.out_shape)

    def wrapped(*refs):
        ins, rest = refs[:n_in], refs[n_in:]
        sc_in, rest = rest[:a], rest[a:]
        outs, rest = rest[:n_out], rest[n_out:]
        sc_out, rest = rest[:o], rest[o:]
        scratch, sems = rest[:n_scratch], rest[n_scratch:]
        start, finish = sc.steps(sc_in, sc_out, sems)
        pl.when(first())(start)
        body(*ins, *outs, *scratch)
        pl.when(last())(finish)

    return wrapped


def fox_attn_fwd(qkv, cum_col, cum_row, H, name, sidecar=None):
    S = qkv.shape[0]
    Dh = qkv.shape[1] // (3 * H)
    T = _tile(S, FOX_TILE)
    n = S // T
    qi, kj = _causal_pairs(n, by_key=False)
    scale = Dh ** -0.5

    def body(qi_ref, kj_ref, q_ref, k_ref, v_ref, fq_ref, fk_ref, o_ref, lse_ref, m_ref, l_ref, acc_ref):
        p_id = pl.program_id(1)
        i, j = qi_ref[p_id], kj_ref[p_id]

        @pl.when(j == 0)
        def _():
            m_ref[...] = jnp.full_like(m_ref, NEG)
            l_ref[...] = jnp.zeros_like(l_ref)
            acc_ref[...] = jnp.zeros_like(acc_ref)

        def tile(masked):
            for r0 in range(0, T, FOX_FWD_STRIP):
                rows = slice(r0, r0 + FOX_FWD_STRIP)
                s = lax.dot_general(q_ref[rows], k_ref[...], _DIMS["nt"], preferred_element_type=F32) * scale + (
                    fq_ref[0, rows] - fk_ref[0])
                if masked:
                    r = r0 + lax.broadcasted_iota(jnp.int32, (FOX_FWD_STRIP, T), 0)
                    c = lax.broadcasted_iota(jnp.int32, (FOX_FWD_STRIP, T), 1)
                    s = jnp.where(c <= r, s, NEG)
                m_old = m_ref[rows]
                m_new = jnp.maximum(m_old, jnp.max(s, axis=1, keepdims=True))
                alpha = jnp.exp(m_old - m_new)
                p = jnp.exp(s - m_new)
                l_ref[rows] = alpha * l_ref[rows] + jnp.sum(p, axis=1, keepdims=True)
                acc_ref[rows] = alpha * acc_ref[rows] + jnp.dot(p.astype(BF16), v_ref[...], preferred_element_type=F32)
                m_ref[rows] = m_new

        _on_and_below_diagonal(i, j, tile)

        @pl.when(j == i)
        def _():
            o_ref[...] = (acc_ref[...] / l_ref[...]).astype(o_ref.dtype)
            lse_ref[0] = m_ref[...] + jnp.log(l_ref[...])

    grid_spec = pltpu.PrefetchScalarGridSpec(
        num_scalar_prefetch=2, grid=(H, len(qi)),
        in_specs=[
            pl.BlockSpec((T, Dh), lambda h, p, qi, kj: (qi[p], h)),
            pl.BlockSpec((T, Dh), lambda h, p, qi, kj: (kj[p], H + h)),
            pl.BlockSpec((T, Dh), lambda h, p, qi, kj: (kj[p], 2 * H + h)),
            pl.BlockSpec((1, T, 1), lambda h, p, qi, kj: (h, qi[p], 0)),
            pl.BlockSpec((1, 1, T), lambda h, p, qi, kj: (h, 0, kj[p])),
        ] + _sc_specs(sidecar, False),
        out_specs=[
            pl.BlockSpec((T, Dh), lambda h, p, qi, kj: (qi[p], h)),
            pl.BlockSpec((1, T, 1), lambda h, p, qi, kj: (h, qi[p], 0)),
        ] + _sc_specs(sidecar, True),
        scratch_shapes=[pltpu.VMEM((T, 1), F32), pltpu.VMEM((T, 1), F32), pltpu.VMEM((T, Dh), F32)] + _sc_sems(sidecar),
    )
    first = lambda: (pl.program_id(0) == 0) & (pl.program_id(1) == 0)
    last = lambda: (pl.program_id(0) == H - 1) & (pl.program_id(1) == len(qi) - 1)
    res = pl.pallas_call(
        _with_sidecar(sidecar, 7, 2, 3, body, first, last), name=name, grid_spec=grid_spec,
        out_shape=[jax.ShapeDtypeStruct((S, H * Dh), F32), jax.ShapeDtypeStruct((H, S, 1), F32)] + _sc_out(sidecar),
        compiler_params=_cp(("arbitrary", "arbitrary")),
    )(jnp.asarray(qi), jnp.asarray(kj), qkv, qkv, qkv, cum_col, cum_row, *_sc_arrays(sidecar))
    return res[0], res[1], res[2:]


def fox_attn_bwd_dq(qkv, do, o, lse, cum_col, cum_row, H, name, sidecar=None):
    S = qkv.shape[0]
    Dh = qkv.shape[1] // (3 * H)
    T = _tile(S, FOX_TILE)
    n = S // T
    qi, kj = _causal_pairs(n, by_key=False)
    scale = Dh ** -0.5

    def body(qi_ref, kj_ref, q_ref, k_ref, v_ref, do_ref, o_ref, lse_ref, fq_ref, fk_ref, dq_ref, delta_ref, acc_ref, dl_ref,
             rs_ref):
        p_id = pl.program_id(1)
        i, j = qi_ref[p_id], kj_ref[p_id]

        @pl.when(j == 0)
        def _():
            acc_ref[...] = jnp.zeros_like(acc_ref)
            rs_ref[...] = jnp.zeros_like(rs_ref)
            dl_ref[...] = jnp.sum(do_ref[...].astype(F32) * o_ref[...].astype(F32), axis=1, keepdims=True)

        def tile(masked):
            s, mask = _fox_scores(q_ref[...], k_ref[...], fq_ref[0], fk_ref[0], T, scale, False)
            p = jnp.exp(s - lse_ref[0])
            if masked:
                p = jnp.where(mask, p, 0.0)
            dp = lax.dot_general(do_ref[...], v_ref[...], _DIMS["nt"], preferred_element_type=F32)
            ds = p * (dp - dl_ref[...])
            rs_ref[...] += jnp.sum(ds, axis=1, keepdims=True)
            acc_ref[...] += jnp.dot(ds.astype(BF16), k_ref[...], preferred_element_type=F32)

        _on_and_below_diagonal(i, j, tile)

        @pl.when(j == i)
        def _():
            dq_ref[...] = (acc_ref[...] * scale).astype(dq_ref.dtype)
            delta_ref[0] = dl_ref[...] + rs_ref[...]

    qspec = pl.BlockSpec((T, Dh), lambda h, p, qi, kj: (qi[p], h))
    colspec = pl.BlockSpec((1, T, 1), lambda h, p, qi, kj: (h, qi[p], 0))
    grid_spec = pltpu.PrefetchScalarGridSpec(
        num_scalar_prefetch=2, grid=(H, len(qi)),
        in_specs=[
            qspec,
            pl.BlockSpec((T, Dh), lambda h, p, qi, kj: (kj[p], H + h)),
            pl.BlockSpec((T, Dh), lambda h, p, qi, kj: (kj[p], 2 * H + h)),
            qspec, qspec, colspec, colspec,
            pl.BlockSpec((1, 1, T), lambda h, p, qi, kj: (h, 0, kj[p])),
        ] + _sc_specs(sidecar, False),
        out_specs=[qspec, colspec] + _sc_specs(sidecar, True),
        scratch_shapes=[pltpu.VMEM((T, Dh), F32), pltpu.VMEM((T, 1), F32), pltpu.VMEM((T, 1), F32)] + _sc_sems(sidecar),
    )
    first = lambda: (pl.program_id(0) == 0) & (pl.program_id(1) == 0)
    last = lambda: (pl.program_id(0) == H - 1) & (pl.program_id(1) == len(qi) - 1)
    res = pl.pallas_call(
        _with_sidecar(sidecar, 10, 2, 3, body, first, last), name=name, grid_spec=grid_spec,
        out_shape=[jax.ShapeDtypeStruct((S, H * Dh), BF16), jax.ShapeDtypeStruct((H, S, 1), F32)] + _sc_out(sidecar),
        compiler_params=_cp(("arbitrary", "arbitrary")),
    )(jnp.asarray(qi), jnp.asarray(kj), qkv, qkv, qkv, do, o, lse, cum_col, cum_row, *_sc_arrays(sidecar))
    return res[0], res[1], res[2:]


def fox_attn_bwd_dkv(qkv, do, lse_row, delta_row, cum_col, cum_row, H, name, sidecar=None):
    S = qkv.shape[0]
    Dh = qkv.shape[1] // (3 * H)
    T = _tile(S, FOX_TILE)
    n = S // T
    qi, kj = _causal_pairs(n, by_key=True)
    scale = Dh ** -0.5

    def body(qi_ref, kj_ref, q_ref, k_ref, v_ref, do_ref, lse_ref, dl_ref, fq_ref, fk_ref, dk_ref, dv_ref, dcum_ref,
             dk_acc, dv_acc, df_acc):
        p_id = pl.program_id(1)
        i, j = qi_ref[p_id], kj_ref[p_id]

        @pl.when(i == j)
        def _():
            dk_acc[...] = jnp.zeros_like(dk_acc)
            dv_acc[...] = jnp.zeros_like(dv_acc)
            df_acc[...] = jnp.zeros_like(df_acc)

        def tile(masked):
            sT, mask = _fox_scores(q_ref[...], k_ref[...], fq_ref[0], fk_ref[0], T, scale, True)
            pT = jnp.exp(sT - lse_ref[0])
            if masked:
                pT = jnp.where(mask, pT, 0.0)
            dv_acc[...] += jnp.dot(pT.astype(BF16), do_ref[...], preferred_element_type=F32)
            dpT = lax.dot_general(v_ref[...], do_ref[...], _DIMS["nt"], preferred_element_type=F32)
            dsT = pT * (dpT - dl_ref[0])
            dk_acc[...] += jnp.dot(dsT.astype(BF16), q_ref[...], preferred_element_type=F32)
            df_acc[...] -= jnp.sum(dsT, axis=1, keepdims=True)

        _on_and_below_diagonal(i, j, tile)

        @pl.when(i == n - 1)
        def _():
            dk_ref[...] = (dk_acc[...] * scale).astype(dk_ref.dtype)
            dv_ref[...] = dv_acc[...].astype(dv_ref.dtype)
            dcum_ref[0] = df_acc[...]

    qspec = pl.BlockSpec((T, Dh), lambda h, p, qi, kj: (qi[p], h))
    kspec = pl.BlockSpec((T, Dh), lambda h, p, qi, kj: (kj[p], H + h))
    vspec = pl.BlockSpec((T, Dh), lambda h, p, qi, kj: (kj[p], 2 * H + h))
    qrow = pl.BlockSpec((1, 1, T), lambda h, p, qi, kj: (h, 0, qi[p]))
    kcol = pl.BlockSpec((1, T, 1), lambda h, p, qi, kj: (h, kj[p], 0))
    grid_spec = pltpu.PrefetchScalarGridSpec(
        num_scalar_prefetch=2, grid=(H, len(qi)),
        in_specs=[qspec, kspec, vspec, qspec, qrow, qrow, qrow, kcol] + _sc_specs(sidecar, False),
        out_specs=[pl.BlockSpec((T, Dh), lambda h, p, qi, kj: (kj[p], h))] * 2 + [kcol] + _sc_specs(sidecar, True),
        scratch_shapes=[pltpu.VMEM((T, Dh), F32), pltpu.VMEM((T, Dh), F32), pltpu.VMEM((T, 1), F32)] + _sc_sems(sidecar),
    )
    out = jax.ShapeDtypeStruct((S, H * Dh), BF16)
    first = lambda: (pl.program_id(0) == 0) & (pl.program_id(1) == 0)
    last = lambda: (pl.program_id(0) == H - 1) & (pl.program_id(1) == len(qi) - 1)
    res = pl.pallas_call(
        _with_sidecar(sidecar, 10, 3, 3, body, first, last), name=name, grid_spec=grid_spec,
        out_shape=[out, out, jax.ShapeDtypeStruct((H, S, 1), F32)] + _sc_out(sidecar),
        compiler_params=_cp(("arbitrary", "arbitrary")),
    )(jnp.asarray(qi), jnp.asarray(kj), qkv, qkv, qkv, do, lse_row, delta_row, cum_row, cum_col, *_sc_arrays(sidecar))
    return res[0], res[1], res[2], res[3:]


def _sgu_ln(zu, zv, ln_g, ln_b):
    u = jax.nn.gelu(zu)
    v = jax.nn.gelu(zv)
    mu = jnp.mean(v, axis=-1, keepdims=True)
    var = jnp.mean(jnp.square(v - mu), axis=-1, keepdims=True)
    return u, (v - mu) * lax.rsqrt(var + EPS) * ln_g + ln_b


def _tril_mask():
    r = lax.broadcasted_iota(jnp.int32, (SEQ_BLOCK, SEQ_BLOCK), 0)
    c = lax.broadcasted_iota(jnp.int32, (SEQ_BLOCK, SEQ_BLOCK), 1)
    return r >= c


def _sgu_spatial(ws_ref, bsT, selT, vn, G):
    tril = _tril_mask()
    fs = []
    for g in range(G):
        wg = jnp.where(tril, ws_ref[g], 0.0).astype(BF16)
        fs.append(jnp.dot(wg, vn[:, g * SEQ_BLOCK:(g + 1) * SEQ_BLOCK].astype(BF16), preferred_element_type=F32))
    bias = jnp.dot(bsT, selT, precision=lax.Precision.HIGHEST, preferred_element_type=F32)
    return jnp.concatenate(fs, axis=1) + bias


def _sgu_specs(W, G):
    return [
        pl.BlockSpec((SEQ_BLOCK, 2 * W), lambda n: (n, 0)),
        pl.BlockSpec((1, W), lambda n: (0, 0)),
        pl.BlockSpec((1, W), lambda n: (0, 0)),
        pl.BlockSpec((G, SEQ_BLOCK, SEQ_BLOCK), lambda n: (0, 0, 0)),
        pl.BlockSpec((SEQ_BLOCK, G), lambda n: (0, 0)),
        pl.BlockSpec((G, W), lambda n: (0, 0)),
    ]


def sgu_fwd(zp, ln_g, ln_b, ws, bsT, selT, name):
    S, W2 = zp.shape
    W = W2 // 2
    G = ws.shape[0]

    def body(z_ref, lg_ref, lb_ref, ws_ref, bs_ref, sel_ref, o_ref):
        u, vn = _sgu_ln(z_ref[:, :W], z_ref[:, W:], lg_ref[...], lb_ref[...])
        o_ref[...] = (u * _sgu_spatial(ws_ref, bs_ref[...], sel_ref[...], vn, G)).astype(o_ref.dtype)

    return pl.pallas_call(
        body, name=name, grid=(S // SEQ_BLOCK,), in_specs=_sgu_specs(W, G), out_specs=pl.BlockSpec((SEQ_BLOCK, W), lambda n: (n, 0)),
        out_shape=jax.ShapeDtypeStruct((S, W), BF16), compiler_params=_cp(("parallel",)),
    )(zp, ln_g, ln_b, ws, bsT, selT)


def sgu_bwd(zp, ln_g, ln_b, ws, bsT, selT, dgated, name):
    S, W2 = zp.shape
    W = W2 // 2
    G = ws.shape[0]

    def body(z_ref, lg_ref, lb_ref, ws_ref, bs_ref, sel_ref, dgt_ref, dz_ref, dlg_ref, dlb_ref, dws_ref, dbs_ref):
        (u, vn), vjp = jax.vjp(_sgu_ln, z_ref[:, :W], z_ref[:, W:], lg_ref[...], lb_ref[...])
        f = _sgu_spatial(ws_ref, bs_ref[...], sel_ref[...], vn, G)
        dgt = dgt_ref[...].astype(F32)
        du, df = dgt * f, dgt * u

        @pl.when(pl.program_id(0) == 0)
        def _():
            dlg_ref[...] = jnp.zeros_like(dlg_ref)
            dlb_ref[...] = jnp.zeros_like(dlb_ref)
            dws_ref[...] = jnp.zeros_like(dws_ref)
            dbs_ref[...] = jnp.zeros_like(dbs_ref)

        dbs_ref[...] += lax.dot_general(df, sel_ref[...], _DIMS["nt"], precision=lax.Precision.HIGHEST, preferred_element_type=F32)
        tril = _tril_mask()
        dvn = []
        for g in range(G):
            sl = slice(g * SEQ_BLOCK, (g + 1) * SEQ_BLOCK)
            wg = jnp.where(tril, ws_ref[g], 0.0).astype(BF16)
            df_g = df[:, sl].astype(BF16)
            dw = lax.dot_general(df_g, vn[:, sl].astype(BF16), _DIMS["nt"], preferred_element_type=F32)
            dws_ref[g] += jnp.where(tril, dw, 0.0)
            dvn.append(lax.dot_general(wg, df_g, _DIMS["tn"], preferred_element_type=F32))
        dzu, dzv, dlg, dlb = vjp((du, jnp.concatenate(dvn, axis=1)))
        dz_ref[:, :W] = dzu.astype(dz_ref.dtype)
        dz_ref[:, W:] = dzv.astype(dz_ref.dtype)
        dlg_ref[...] += dlg
        dlb_ref[...] += dlb

    vec = jax.ShapeDtypeStruct((1, W), F32)
    return pl.pallas_call(
        body, name=name, grid=(S // SEQ_BLOCK,),
        in_specs=_sgu_specs(W, G) + [pl.BlockSpec((SEQ_BLOCK, W), lambda n: (n, 0))],
        out_specs=[
            pl.BlockSpec((SEQ_BLOCK, 2 * W), lambda n: (n, 0)),
            pl.BlockSpec((1, W), lambda n: (0, 0)),
            pl.BlockSpec((1, W), lambda n: (0, 0)),
            pl.BlockSpec((G, SEQ_BLOCK, SEQ_BLOCK), lambda n: (0, 0, 0)),
            pl.BlockSpec((SEQ_BLOCK, G), lambda n: (0, 0)),
        ],
        out_shape=[jax.ShapeDtypeStruct((S, W2), BF16), vec, vec, jax.ShapeDtypeStruct(ws.shape, F32), jax.ShapeDtypeStruct((SEQ_BLOCK, G), F32)],
        compiler_params=_cp(("arbitrary",)),
    )(zp, ln_g, ln_b, ws, bsT, selT, dgated)


def _rope_matrix():
    half = ROPE_DIM // 2
    R = np.zeros((SWA_HEAD_DIM, SWA_HEAD_DIM), np.float32)
    for j in range(half):
        R[j + half, j] = -1.0
        R[j, j + half] = 1.0
    return R


def _swa_mask_bias(G):
    B = SEQ_BLOCK
    qi = np.arange(G * B)[:, None] % B
    ki = np.arange(2 * B)[None, :] - B
    rel = qi - ki
    valid = (rel >= 0) & (rel < B)
    return np.where(np.stack([valid & (ki >= 0), valid]), 0.0, NEG).astype(np.float32)


def _rot3(t, r_bf16):
    hi = t.astype(BF16)
    rest = t - hi.astype(F32)
    mid = rest.astype(BF16)
    lo = (rest - mid.astype(F32)).astype(BF16)
    d = lambda piece: jnp.dot(piece, r_bf16, preferred_element_type=F32)
    return (d(hi) + d(mid)) + d(lo)


@jax.custom_vjp
def _rope_rot(t, r_bf16):
    return _rot3(t, r_bf16)


def _rope_rot_fwd(t, r_bf16):
    return _rot3(t, r_bf16), r_bf16


def _rope_rot_bwd(r_bf16, ct):
    return -_rot3(ct, r_bf16), jnp.zeros_like(r_bf16)


_rope_rot.defvjp(_rope_rot_fwd, _rope_rot_bwd)


def _swa_block(q4, kp, kc, vp, vc, sink, Cq, Sq, Cp, Sp, R, bias, G):
    B, Dh = SEQ_BLOCK, SWA_HEAD_DIM
    r_bf16 = R.astype(BF16)
    rot = lambda t: _rope_rot(t, r_bf16)
    q = q4.reshape(G * B, Dh)
    Cq4 = jnp.concatenate([Cq] * G, axis=0)
    Sq4 = jnp.concatenate([Sq] * G, axis=0)
    qr = q * Cq4 + rot(q) * Sq4
    kb = jnp.concatenate([kp * Cp + rot(kp) * Sp, kc * Cq + rot(kc) * Sq], axis=0)
    vb = jnp.concatenate([vp, vc], axis=0)
    s = lax.dot_general(qr.astype(BF16), kb.astype(BF16), _DIMS["nt"], preferred_element_type=F32) * (Dh ** -0.5) + bias
    m = lax.stop_gradient(jnp.maximum(jnp.max(s, axis=1, keepdims=True), sink))
    p = jnp.exp(s - m)
    p = p / (jnp.sum(p, axis=1, keepdims=True) + jnp.exp(sink - m))
    o = jnp.dot(p.astype(BF16), vb.astype(BF16), preferred_element_type=F32)
    return o.reshape(G, B, Dh)


SWA_HEADS_PER_STEP = 2


def _swa_specs(G, HP):
    B, Dh = SEQ_BLOCK, SWA_HEAD_DIM
    prev = lambda n: jnp.maximum(n - 1, 0)
    return [
        pl.BlockSpec((HP * G, B, Dh), lambda h, n: (h, n, 0)),
        pl.BlockSpec((HP, B, Dh), lambda h, n: (h, prev(n), 0)),
        pl.BlockSpec((HP, B, Dh), lambda h, n: (h, n, 0)),
        pl.BlockSpec((HP, B, Dh), lambda h, n: (h, prev(n), 0)),
        pl.BlockSpec((HP, B, Dh), lambda h, n: (h, n, 0)),
        pl.BlockSpec((HP, G * B, 1), lambda h, n: (h, 0, 0)),
        pl.BlockSpec((B, Dh), lambda h, n: (n, 0)),
        pl.BlockSpec((B, Dh), lambda h, n: (n, 0)),
        pl.BlockSpec((B, Dh), lambda h, n: (prev(n), 0)),
        pl.BlockSpec((B, Dh), lambda h, n: (prev(n), 0)),
        pl.BlockSpec((Dh, Dh), lambda h, n: (0, 0)),
        pl.BlockSpec((1, G * B, 2 * B), lambda h, n: (jnp.minimum(n, 1), 0, 0)),
    ]


def swa_fwd(qh, kh, vh, sink_col, C, Sn, R, bias, name):
    Hq, S, Dh = qh.shape
    Hk = kh.shape[0]
    G = Hq // Hk
    HP = _tile(Hk, SWA_HEADS_PER_STEP, 1)

    def body(q_ref, kp_ref, kc_ref, vp_ref, vc_ref, sk_ref, cq_ref, sq_ref, cp_ref, sp_ref, r_ref, b_ref, o_ref):
        for hp in range(HP):
            qs = slice(hp * G, (hp + 1) * G)
            o = _swa_block(q_ref[qs], kp_ref[hp], kc_ref[hp], vp_ref[hp], vc_ref[hp], sk_ref[hp], cq_ref[...], sq_ref[...],
                           cp_ref[...], sp_ref[...], r_ref[...], b_ref[0], G)
            o_ref[qs] = o.astype(o_ref.dtype)

    return pl.pallas_call(
        body, name=name, grid=(Hk // HP, S // SEQ_BLOCK), in_specs=_swa_specs(G, HP),
        out_specs=pl.BlockSpec((HP * G, SEQ_BLOCK, Dh), lambda h, n: (h, n, 0)),
        out_shape=jax.ShapeDtypeStruct((Hq, S, Dh), BF16), compiler_params=_cp(("parallel", "parallel")),
    )(qh, kh, kh, vh, vh, sink_col, C, Sn, C, Sn, R, bias)


def swa_bwd(qh, kh, vh, sink_col, C, Sn, R, bias, doh, name):
    Hq, S, Dh = qh.shape
    Hk = kh.shape[0]
    G = Hq // Hk
    B = SEQ_BLOCK
    HP = _tile(Hk, SWA_HEADS_PER_STEP, 1)

    def body(q_ref, kp_ref, kc_ref, vp_ref, vc_ref, sk_ref, cq_ref, sq_ref, cp_ref, sp_ref, r_ref, b_ref, do_ref,
             dq_ref, dkp_ref, dkc_ref, dvp_ref, dvc_ref, dsk_ref):
        @pl.when(pl.program_id(1) == 0)
        def _():
            dsk_ref[...] = jnp.zeros_like(dsk_ref)

        fn = lambda q4, kp, kc, vp, vc, sk: _swa_block(q4, kp, kc, vp, vc, sk, cq_ref[...], sq_ref[...], cp_ref[...], sp_ref[...],
                                                      r_ref[...], b_ref[0], G)
        for hp in range(HP):
            qs = slice(hp * G, (hp + 1) * G)
            _, vjp = jax.vjp(fn, q_ref[qs], kp_ref[hp], kc_ref[hp], vp_ref[hp], vc_ref[hp], sk_ref[hp])
            dq, dkp, dkc, dvp, dvc, dsk = vjp(do_ref[qs].astype(F32))
            dq_ref[qs] = dq
            dkp_ref[hp] = dkp
            dkc_ref[hp] = dkc
            dvp_ref[hp] = dvp
            dvc_ref[hp] = dvc
            for g in range(G):
                part = jnp.sum(dsk[g * B:(g + 1) * B], axis=0, keepdims=True)
                dsk_ref[hp, g:g + 1, :] += jnp.broadcast_to(part, (1, LANES))

    qspec = pl.BlockSpec((HP * G, B, Dh), lambda h, n: (h, n, 0))
    kspec = pl.BlockSpec((HP, B, Dh), lambda h, n: (h, n, 0))
    kshape = jax.ShapeDtypeStruct((Hk, S, Dh), F32)
    return pl.pallas_call(
        body, name=name, grid=(Hk // HP, S // B), in_specs=_swa_specs(G, HP) + [qspec],
        out_specs=[qspec, kspec, kspec, kspec, kspec, pl.BlockSpec((HP, G, LANES), lambda h, n: (h, 0, 0))],
        out_shape=[jax.ShapeDtypeStruct((Hq, S, Dh), F32), kshape, kshape, kshape, kshape, jax.ShapeDtypeStruct((Hk, G, LANES), F32)],
        compiler_params=_cp(("parallel", "arbitrary")),
    )(qh, kh, kh, vh, vh, sink_col, C, Sn, C, Sn, R, bias, doh)


def shift_add(cur, prev, name):
    Hk, S, Dh = cur.shape
    B = SEQ_BLOCK

    def body(c_ref, p_ref, o_ref):
        o_ref[0, :S - B] = c_ref[0, :S - B] + p_ref[0, B:]
        o_ref[0, S - B:] = c_ref[0, S - B:]

    spec = pl.BlockSpec((1, S, Dh), lambda h: (h, 0, 0))
    return pl.pallas_call(
        body, name=name, grid=(Hk,), in_specs=[spec, spec], out_specs=spec, out_shape=jax.ShapeDtypeStruct(cur.shape, F32),
        compiler_params=_cp(("parallel",)),
    )(cur, prev)


def loss_head(y, target, name):
    S, D = y.shape
    tr = _tile(S, ROW_TILE, 16)

    def body(y_ref, t_ref, acc_ref, dy_ref):
        err = y_ref[...] - t_ref[...]
        dy_ref[...] = err * (1.0 / D)

        @pl.when(pl.program_id(0) == 0)
        def _():
            acc_ref[...] = jnp.zeros_like(acc_ref)

        acc_ref[...] += jnp.broadcast_to(jnp.sum(err * err).reshape(1, 1), (1, LANES))

    return pl.pallas_call(
        body, name=name, grid=(S // tr,), in_specs=[_row_spec(tr, D)] * 2,
        out_specs=[pl.BlockSpec((1, LANES), lambda i: (0, 0)), _row_spec(tr, D)],
        out_shape=[jax.ShapeDtypeStruct((1, LANES), F32), jax.ShapeDtypeStruct((S, D), F32)], compiler_params=_cp(("arbitrary",)),
    )(y, target)


def _adam_update(w, g, m, v):
    m = ADAM_B1 * m + (1.0 - ADAM_B1) * g
    v = ADAM_B2 * v + (1.0 - ADAM_B2) * jnp.square(g)
    m_hat = m / (1.0 - ADAM_B1 ** ADAM_STEP)
    v_hat = v / (1.0 - ADAM_B2 ** ADAM_STEP)
    delta = -ADAM_LR * (m_hat / (jnp.sqrt(v_hat) + ADAM_EPS) + ADAM_WD * w)
    return delta, m, v


def adamw(w, m, v, gparts, name, gstack=0, emit_g=True):
    R, C = w.shape
    tr = _tile(R, max(8, (128 * 1024) // C), 8)
    spec = pl.BlockSpec((tr, C), lambda i: (i, 0))
    nplain = len(gparts) - (1 if gstack else 0)
    nout = 4 if emit_g else 3

    def body(w_ref, m_ref, v_ref, *rest):
        g_refs, outs = rest[:len(gparts)], rest[len(gparts):]
        g = None
        for r in g_refs[:nplain]:
            g = r[...].astype(F32) if g is None else g + r[...].astype(F32)
        if gstack:
            for t in range(gstack):
                part = g_refs[-1][t].astype(F32)
                g = part if g is None else g + part
        res = _adam_update(w_ref[...], g, m_ref[...], v_ref[...])
        for o_ref, val in zip(outs, ((g,) if emit_g else ()) + res):
            o_ref[...] = val

    gspecs = [spec] * nplain + ([pl.BlockSpec((gstack, tr, C), lambda i: (0, i, 0))] if gstack else [])
    out = jax.ShapeDtypeStruct((R, C), F32)
    return pl.pallas_call(
        body, name=name, grid=(R // tr,), in_specs=[spec] * 3 + gspecs, out_specs=[spec] * nout, out_shape=[out] * nout,
        compiler_params=_cp(("parallel",)),
    )(w, m, v, *gparts)


def ada_fwd(c_all, ada_w, ada_b, name):
    L, D, N = ada_w.shape
    Bp = c_all.shape[0]
    tn = _tile(N, 512)

    def body(c_ref, w_ref, b_ref, o_ref):
        ca = jax.nn.silu(c_ref[...]).astype(BF16)
        o_ref[0] = jnp.dot(ca, w_ref[0].astype(BF16), preferred_element_type=F32) + b_ref[0]

    return pl.pallas_call(
        body, name=name, grid=(L, N // tn),
        in_specs=[pl.BlockSpec((Bp, D), lambda l, j: (0, 0)), pl.BlockSpec((1, D, tn), lambda l, j: (l, 0, j)),
                  pl.BlockSpec((1, 1, tn), lambda l, j: (l, 0, j))],
        out_specs=pl.BlockSpec((1, Bp, tn), lambda l, j: (l, 0, j)), out_shape=jax.ShapeDtypeStruct((L, Bp, N), F32),
        compiler_params=_cp(("parallel", "parallel")),
    )(c_all, ada_w, ada_b)


def ada_wgrad(c_all, dmod, name):
    L, Bp, N = dmod.shape
    D = c_all.shape[1]
    tn = _tile(N, 512)

    def body(c_ref, d_ref, o_ref):
        ca = jax.nn.silu(c_ref[...]).astype(BF16)
        o_ref[0] = lax.dot_general(ca, d_ref[0].astype(BF16), _DIMS["tn"], preferred_element_type=F32)

    return pl.pallas_call(
        body, name=name, grid=(L, N // tn),
        in_specs=[pl.BlockSpec((Bp, D), lambda l, j: (0, 0)), pl.BlockSpec((1, Bp, tn), lambda l, j: (l, 0, j))],
        out_specs=pl.BlockSpec((1, D, tn), lambda l, j: (l, 0, j)), out_shape=jax.ShapeDtypeStruct((L, D, N), F32),
        compiler_params=_cp(("parallel", "parallel")),
    )(c_all, dmod)


N_DEV = 8
N_CHIP = 4
ANY = pl.BlockSpec(memory_space=pl.ANY)


def _place():
    return lax.axis_index("x"), lax.axis_index("y"), lax.axis_index("c")


def _other_chips(x, y):
    chips = [(1 - x, y), (x, 1 - y), (1 - x, 1 - y)]
    return chips, [2 * cx + cy for cx, cy in chips]


def _rcopy(src, dst, ssem, rsem, to):
    return pltpu.make_async_remote_copy(src_ref=src, dst_ref=dst, send_sem=ssem, recv_sem=rsem, device_id=to, device_id_type=MESH)


def ag_small(xs, name):
    R, Wd = xs.shape

    def body(x_ref, out_ref, send_sems, recv_sems, local_sem):
        x, y, c = _place()
        me, sibling = (x, y, c), (x, y, 1 - c)
        chips, _ = _other_chips(x, y)

        def slot(px, py, pc):
            return out_ref.at[4 * px + 2 * py + pc]

        def copy(k, block, to, src=None):
            return _rcopy(slot(*block) if src is None else src, slot(*block), send_sems.at[k], recv_sems.at[k], to)

        mine = pltpu.make_async_copy(x_ref, slot(*me), local_sem)
        mine.start()
        first = [copy(0, me, sibling, src=x_ref)]
        first += [copy(1 + j, me, (*chip, c), src=x_ref) for j, chip in enumerate(chips)]
        for cp in first:
            cp.start()
        passed = [copy(4 + j, (*chip, c), sibling) for j, chip in enumerate(chips)]
        for j, chip in enumerate(chips):
            copy(1 + j, (*chip, c), me).wait_recv()
            passed[j].start()
        copy(0, sibling, me).wait_recv()
        for j, chip in enumerate(chips):
            copy(4 + j, (*chip, 1 - c), me).wait_recv()
        for cp in first + passed:
            cp.wait_send()
        mine.wait()

    vm = pl.BlockSpec(memory_space=pltpu.VMEM)
    return pl.pallas_call(
        body, name=name, out_shape=jax.ShapeDtypeStruct((N_DEV, R, Wd), xs.dtype), in_specs=[vm], out_specs=vm,
        scratch_shapes=[pltpu.SemaphoreType.DMA((7,)), pltpu.SemaphoreType.DMA((7,)), pltpu.SemaphoreType.DMA],
        compiler_params=_cp(),
    )(xs)


def _half_of_shard(by_cols, A, B, h):
    return (h * (A // 2), A // 2, 0, B) if by_cols else (0, A, h * (B // 2), B // 2)


def _shard_in_full(by_cols, A, B, q):
    return (0, q * B) if by_cols else (q * A, 0)


def _window(ref, r0, nr, c0, nc):
    return ref.at[:, pl.ds(r0, nr), pl.ds(c0, nc)]


def ag_weights(shards, by_cols, name):
    n = len(shards)
    geo, full = _ag_shapes(shards, by_cols)

    def body(*refs):
        start, finish = _ag_steps(geo, refs[:n], refs[n:2 * n], refs[2 * n:])
        start()
        finish()

    return pl.pallas_call(
        body, name=name, out_shape=full, in_specs=[ANY] * n, out_specs=[ANY] * n, scratch_shapes=_ag_semaphores(n),
        compiler_params=_cp(),
    )(*shards)


def _ag_shapes(shards, by_cols):
    geo = [(bc,) + s.shape[1:] for bc, s in zip(by_cols, shards)]
    full = [jax.ShapeDtypeStruct((s.shape[0], A, N_CHIP * B) if bc else (s.shape[0], N_CHIP * A, B), s.dtype)
            for (bc, A, B), s in zip(geo, shards)]
    return geo, full


def _ag_semaphores(n):
    return [pltpu.SemaphoreType.DMA((n, 3)) for _ in range(4)]


def _ag_steps(geo, x_refs, o_refs, sems):
    s_ici, r_ici, s_d2d, r_d2d = sems
    pairs = [(t, j) for t in range(len(geo)) for j in range(3)]

    def copies():
        x, y, c = _place()
        q = 2 * x + y
        chips, qs = _other_chips(x, y)

        def landing(t, chip_q, half):
            r0, nr, c0, nc = _half_of_shard(*geo[t], half)
            ro, co = _shard_in_full(*geo[t], chip_q)
            return _window(o_refs[t], ro + r0, nr, co + c0, nc)

        def ici(t, j, landing_q):
            src = _window(x_refs[t], *_half_of_shard(*geo[t], c))
            return _rcopy(src, landing(t, landing_q, c), s_ici.at[t, j], r_ici.at[t, j], (*chips[j], c))

        def handoff(t, j, half):
            blk = landing(t, qs[j], half)
            return _rcopy(blk, blk, s_d2d.at[t, j], r_d2d.at[t, j], (x, y, 1 - c))

        return c, q, qs, ici, handoff

    def start():
        c, q, qs, ici, handoff = copies()
        for t, j in pairs:
            ici(t, j, q).start()

    def finish():
        c, q, qs, ici, handoff = copies()
        for t, j in pairs:
            ici(t, j, qs[j]).wait_recv()
            handoff(t, j, c).start()
        for t, j in pairs:
            handoff(t, j, 1 - c).wait_recv()
        for t, j in pairs:
            ici(t, j, q).wait_send()
            handoff(t, j, c).wait_send()

    return start, finish


def _half_of_full(by_cols, A, B, h):
    return (h * (A // 2), A // 2, 0, N_CHIP * B) if by_cols else (0, N_CHIP * A, h * (B // 2), B // 2)


def _half_shape(by_cols, L, A, B):
    return (L, A // 2, N_CHIP * B) if by_cols else (L, N_CHIP * A, B // 2)


def _piece_shape(by_cols, L, A, B):
    return (L, A // 2, B) if by_cols else (L, A, B // 2)


def sibling_fold(gs, geo, name):
    sc = fold_sidecar(gs, geo)
    n = len(gs)

    def body(*refs):
        start, finish = sc.steps(refs[:n], refs[n:2 * n], refs[2 * n:])
        start()
        finish()

    return pl.pallas_call(
        body, name=name, out_shape=sc.out_shape, in_specs=[ANY] * n, out_specs=[ANY] * n, scratch_shapes=sc.semaphores,
        compiler_params=_cp(),
    )(*gs)


def fold_sidecar(gs, geo):
    n = len(gs)

    def steps(x_refs, o_refs, sems):
        ssem, rsem = sems

        def copies():
            x, y, c = _place()
            return [_rcopy(_window(x_refs[t], *_half_of_full(*geo[t], 1 - c)), o_refs[t], ssem.at[t], rsem.at[t], (x, y, 1 - c))
                    for t in range(n)]

        def start():
            for cp in copies():
                cp.start()

        def finish():
            for cp in copies():
                cp.wait()

        return start, finish

    dma = pltpu.SemaphoreType.DMA
    out = [jax.ShapeDtypeStruct(_half_shape(bc, g.shape[0], A, B), g.dtype) for (bc, A, B), g in zip(geo, gs)]
    return SideCar(list(gs), out, [dma((n,)), dma((n,))], steps)


def chip_exchange(rs, geo, name):
    sc = exchange_sidecar(rs, geo)
    n = len(rs)

    def body(*refs):
        start, finish = sc.steps(refs[:n], refs[n:2 * n], refs[2 * n:])
        start()
        finish()

    return pl.pallas_call(
        body, name=name, out_shape=sc.out_shape, in_specs=[ANY] * n, out_specs=[ANY] * n, scratch_shapes=sc.semaphores,
        compiler_params=_cp(),
    )(*rs)


def exchange_sidecar(rs, geo):
    n = len(rs)

    def steps(x_refs, o_refs, sems):
        ssem, rsem = sems

        def copies():
            x, y, c = _place()
            chips, qs = _other_chips(x, y)

            def part(t, chip_q):
                bc, A, B = geo[t]
                return _window(x_refs[t], 0, A // 2, chip_q * B, B) if bc else _window(x_refs[t], chip_q * A, A, 0, B // 2)

            return [_rcopy(part(t, qs[j]), o_refs[t].at[j], ssem.at[t, j], rsem.at[t, j], (*chips[j], c))
                    for t in range(n) for j in range(3)]

        def start():
            for cp in copies():
                cp.start()

        def finish():
            for cp in copies():
                cp.wait()

        return start, finish

    dma = pltpu.SemaphoreType.DMA
    out = [jax.ShapeDtypeStruct((3,) + _piece_shape(bc, r.shape[0], A, B), r.dtype) for (bc, A, B), r in zip(geo, rs)]
    return SideCar(list(rs), out, [dma((n, 3)), dma((n, 3))], steps)


def ag_sidecar(shards, by_cols):
    geo, full = _ag_shapes(shards, by_cols)
    return SideCar(list(shards), full, _ag_semaphores(len(shards)), lambda ins, outs, sems: _ag_steps(geo, ins, outs, sems))


def sibling_share(fs, geo, name):
    n = len(fs)

    def body(*refs):
        x_refs, o_refs, (ssem, rsem) = refs[:n], refs[n:2 * n], refs[2 * n:]
        x, y, c = _place()
        for t in range(n):
            mine = _window(o_refs[t], *_half_of_shard(*geo[t], c))
            _rcopy(mine, mine, ssem.at[t], rsem.at[t], (x, y, 1 - c)).start()
        for t in range(n):
            mine = _window(o_refs[t], *_half_of_shard(*geo[t], c))
            theirs = _window(o_refs[t], *_half_of_shard(*geo[t], 1 - c))
            _rcopy(mine, theirs, ssem.at[t], rsem.at[t], (x, y, 1 - c)).wait_recv()
            _rcopy(mine, mine, ssem.at[t], rsem.at[t], (x, y, 1 - c)).wait_send()
        del x_refs

    dma = pltpu.SemaphoreType.DMA
    return pl.pallas_call(
        body, name=name, out_shape=[jax.ShapeDtypeStruct(f.shape, f.dtype) for f in fs], in_specs=[ANY] * n, out_specs=[ANY] * n,
        input_output_aliases={t: t for t in range(n)}, scratch_shapes=[dma((n,)), dma((n,))], compiler_params=_cp(),
    )(*fs)


SUM_ROWS = 256


def fold_sum(g, recv, by_cols, A, B, qc_idx, name):
    L = g.shape[0]
    _, hr, hc = _half_shape(by_cols, L, A, B)
    tr, tc = _tile(A // 2 if by_cols else A, SUM_ROWS, 16), (B if by_cols else B // 2)
    ro, co = ((A // 2) // tr, 0) if by_cols else (0, 1)

    def body(qc_ref, g_ref, r_ref, o_ref):
        del qc_ref
        o_ref[...] = (g_ref[...].astype(F32) + r_ref[...].astype(F32)).astype(o_ref.dtype)

    spec = pl.BlockSpec((1, tr, tc), lambda l, i, j, qc: (l, i, j))
    grid_spec = pltpu.PrefetchScalarGridSpec(
        num_scalar_prefetch=1, grid=(L, hr // tr, hc // tc),
        in_specs=[pl.BlockSpec((1, tr, tc), lambda l, i, j, qc: (l, i + qc[1] * ro, j + qc[1] * co)), spec], out_specs=spec,
    )
    return pl.pallas_call(
        body, name=name, grid_spec=grid_spec, out_shape=jax.ShapeDtypeStruct((L, hr, hc), BF16),
        compiler_params=_cp(("parallel", "parallel", "parallel")),
    )(qc_idx, g, recv)


def chip_sum(r, ex, by_cols, A, B, qc_idx, name):
    L = r.shape[0]
    _, wr, wc = _piece_shape(by_cols, L, A, B)
    tr = _tile(wr, SUM_ROWS, 16)
    r_ro, r_co = (0, 1) if by_cols else (A // tr, 0)
    o_ro, o_co = ((A // 2) // tr, 0) if by_cols else (0, 1)

    def body(qc_ref, r_ref, e_ref, o_ref):
        del qc_ref
        o_ref[0] = ((r_ref[0].astype(F32) + e_ref[0, 0].astype(F32)) + e_ref[1, 0].astype(F32)) + e_ref[2, 0].astype(F32)

    grid_spec = pltpu.PrefetchScalarGridSpec(
        num_scalar_prefetch=1, grid=(L, wr // tr),
        in_specs=[pl.BlockSpec((1, tr, wc), lambda l, i, qc: (l, i + qc[0] * r_ro, qc[0] * r_co)),
                  pl.BlockSpec((3, 1, tr, wc), lambda l, i, qc: (0, l, i, 0))],
        out_specs=pl.BlockSpec((1, tr, wc), lambda l, i, qc: (l, i + qc[1] * o_ro, qc[1] * o_co)),
    )
    return pl.pallas_call(
        body, name=name, grid_spec=grid_spec, out_shape=jax.ShapeDtypeStruct((L, A, B), F32),
        compiler_params=_cp(("parallel", "parallel")),
    )(qc_idx, r, ex)


BIG = ("ffn_w_gu", "ffn_w_down", "fox_w_in", "fox_w_out", "sgu_w_in", "sgu_w_out", "swa_w_in", "swa_w_out")
COLUMN_SHARDED = ("ffn_w_gu", "fox_w_in", "sgu_w_in", "swa_w_in")
SMALL = ("ada_b", "mix_pre_g", "mix_post_g", "ffn_pre_g", "ffn_post_g", "fox_b_f", "sgu_ln_g", "sgu_ln_b", "sgu_w_s", "sgu_b_s",
         "swa_sinks")
WEIGHTS = ("ada_w", "ada_b", "mix_pre_g", "mix_post_g", "ffn_pre_g", "ffn_post_g", "ffn_w_gu", "ffn_w_down", "fox_w_in", "fox_b_f",
           "fox_w_out", "sgu_w_in", "sgu_ln_g", "sgu_ln_b", "sgu_w_s", "sgu_b_s", "sgu_w_out", "swa_w_in", "swa_sinks", "swa_w_out")
INPUTS = ("x", "c", "positions") + WEIGHTS + ("loss_target",) + tuple("m_" + n for n in WEIGHTS) + tuple("v_" + n for n in WEIGHTS)


def _lane_pad(n):
    return (-n) % LANES


def _pad_shard_columns(t, B):
    if _lane_pad(B) == 0:
        return t
    L, A, _ = t.shape
    return jnp.pad(t.reshape(L, A, N_CHIP, B), ((0, 0), (0, 0), (0, 0), (0, _lane_pad(B)))).reshape(L, A, -1)


def _unpad_shard_columns(t, B):
    if _lane_pad(B) == 0:
        return t
    L, A, _ = t.shape
    return t.reshape(L, A, N_CHIP, B + _lane_pad(B))[..., :B].reshape(L, A, N_CHIP * B)


def place_shard(full, shard, by_cols, qc_idx, name):
    L, A, B = shard.shape
    tr = _tile(A, SUM_ROWS, 16)
    ro, co = (0, 1) if by_cols else (A // tr, 0)

    def body(qc_ref, s_ref, f_ref, o_ref):
        del qc_ref, f_ref
        o_ref[...] = s_ref[...]

    grid_spec = pltpu.PrefetchScalarGridSpec(
        num_scalar_prefetch=1, grid=(L, A // tr), in_specs=[pl.BlockSpec((1, tr, B), lambda l, i, qc: (l, i, 0)), ANY],
        out_specs=pl.BlockSpec((1, tr, B), lambda l, i, qc: (l, i + qc[0] * ro, qc[0] * co)),
    )
    return pl.pallas_call(
        body, name=name, grid_spec=grid_spec, out_shape=jax.ShapeDtypeStruct(full.shape, full.dtype),
        input_output_aliases={2: 0}, compiler_params=_cp(("parallel", "parallel")),
    )(qc_idx, shard, full)


def _pad_rows(flat1d):
    n = flat1d.shape[0]
    pad = (-n) % (8 * LANES)
    return jnp.pad(flat1d, (0, pad)).reshape(-1, LANES)


def _pack_small(parts):
    return jnp.concatenate([_pad_rows(parts[n].astype(F32).reshape(-1)) for n in SMALL], axis=0)


def _unpack_small(packed, shapes):
    out, off = {}, 0
    for n in SMALL:
        size = int(np.prod(shapes[n]))
        rows = (size + 8 * LANES - 1) // (8 * LANES) * 8
        out[n] = packed[off:off + rows].reshape(-1)[:size].reshape(shapes[n])
        off += rows
    return out


def _fox_fwd(h, w_in, b_f, w_out, tag, ride_qkv=None, ride_attn=None):
    S, D = h.shape
    H = b_f.shape[0]
    qkv = mm(h, w_in, "nn", BF16, name=tag + "_qkv", b_cols=(0, 3 * D), sidecar=ride_qkv)
    qkv, rode_qkv = qkv if ride_qkv is not None else (qkv, ())
    fgp = mm(h, w_in, "nn", F32, name=tag + "_fg", b_cols=(3 * D, LANES))
    fgT = fgp[:, :H].T
    cum = fox_gate_fwd(fgT, b_f.reshape(H, 1), tag + "_gate")
    cum_col, cum_row = cum.reshape(H, S, 1), cum.reshape(H, 1, S)
    o, lse, rode_attn = fox_attn_fwd(qkv, cum_col, cum_row, H, tag + "_attn", ride_attn)
    y = mm(o, w_out, "nn", F32, name=tag + "_out")
    return y, (qkv, fgT, cum_col, cum_row, o, lse), rode_qkv, rode_attn


def _fox_bwd(dy, h, w_in, b_f, w_out, ctx, tag, ride_dq=None, ride_dkv=None):
    qkv, fgT, cum_col, cum_row, o, lse = ctx
    S, D = h.shape
    H = b_f.shape[0]
    do = mm(dy, w_out, "nt", BF16, name=tag + "_do")
    dw_out = mm(o, dy, "tn", BF16, name=tag + "_dwout")
    dq, delta, rode_dq = fox_attn_bwd_dq(qkv, do, o, lse, cum_col, cum_row, H, tag + "_dq", ride_dq)
    dk, dv, dcum, rode_dkv = fox_attn_bwd_dkv(qkv, do, lse.reshape(H, 1, S), delta.reshape(H, 1, S), cum_col, cum_row, H,
                                              tag + "_dkv", ride_dkv)
    dfgT, db = fox_gate_bwd(dcum.reshape(H, S), fgT, b_f.reshape(H, 1), tag + "_dgate")
    dfgp = jnp.pad(dfgT.T, ((0, 0), (0, LANES - H))).astype(BF16)
    dh = mm(dq, w_in, "nt", F32, name=tag + "_dhq", b_cols=(0, D))
    dh = mm(dk, w_in, "nt", F32, add=dh, name=tag + "_dhk", b_cols=(D, D))
    dh = mm(dv, w_in, "nt", F32, add=dh, name=tag + "_dhv", b_cols=(2 * D, D))
    dh = mm(dfgp, w_in, "nt", F32, add=dh, name=tag + "_dhf", b_cols=(3 * D, LANES))
    dw_in = jnp.concatenate(
        [mm(h, dq, "tn", BF16, name=tag + "_dwq"), mm(h, dk, "tn", BF16, name=tag + "_dwk"), mm(h, dv, "tn", BF16, name=tag + "_dwv"),
         mm(h, dfgp, "tn", BF16, name=tag + "_dwf")[:, :H]], axis=1)
    return dh, dw_in, dw_out, db[:, 0], rode_dq, rode_dkv


def _sgu_consts(G, W):
    return jnp.asarray(np.repeat(np.eye(G, dtype=np.float32), W // G, axis=1))


def _sgu_fwd(h, w_in, ln_g, ln_b, w_s, b_s, w_out, tag):
    G, W = w_s.shape[0], ln_g.shape[0]
    zp = mm(h, w_in, "nn", F32, name=tag + "_in")
    args = (zp, ln_g.reshape(1, W), ln_b.reshape(1, W), w_s, b_s.T, _sgu_consts(G, W))
    gated = sgu_fwd(*args, tag + "_core")
    y = mm(gated, w_out, "nn", F32, name=tag + "_out")
    return y, (args, gated)


def _sgu_bwd(dy, h, w_in, w_out, ctx, tag):
    args, gated = ctx
    dgated = mm(dy, w_out, "nt", BF16, name=tag + "_dgated")
    dw_out = mm(gated, dy, "tn", BF16, name=tag + "_dwout")
    dzp, dlg, dlb, dws, dbsT = sgu_bwd(*args, dgated, tag + "_dcore")
    dh = mm(dzp, w_in, "nt", F32, name=tag + "_dh")
    dw_in = mm(h, dzp, "tn", BF16, name=tag + "_dwin")
    return dh, dw_in, dw_out, dlg[0], dlb[0], dws, dbsT.T


def _rope_tables(positions):
    inv = ROPE_THETA ** (-jnp.arange(0, ROPE_DIM, 2, dtype=F32) / ROPE_DIM)
    ang = positions.astype(F32)[:, None] * inv
    S = positions.shape[0]
    rest = SWA_HEAD_DIM - ROPE_DIM
    C = jnp.concatenate([jnp.cos(ang), jnp.cos(ang), jnp.ones((S, rest), F32)], axis=1)
    Sn = jnp.concatenate([jnp.sin(ang), jnp.sin(ang), jnp.zeros((S, rest), F32)], axis=1)
    return C, Sn


def _heads(t, n):
    return t.reshape(t.shape[0], n, SWA_HEAD_DIM).transpose(1, 0, 2)


def _unheads(t):
    return t.transpose(1, 0, 2).reshape(t.shape[1], -1)


def _swa_fwd(h, w_in, sinks, w_out, tables, tag):
    Hq = sinks.shape[0]
    Hk = (w_in[0].shape[-1] // SWA_HEAD_DIM - Hq) // 2
    G = Hq // Hk
    proj = mm(h, w_in, "nn", F32, name=tag + "_in")
    qh = _heads(proj[:, :Hq * SWA_HEAD_DIM], Hq)
    kh = _heads(proj[:, Hq * SWA_HEAD_DIM:(Hq + Hk) * SWA_HEAD_DIM], Hk)
    vh = _heads(proj[:, (Hq + Hk) * SWA_HEAD_DIM:], Hk)
    sink_col = jnp.repeat(sinks.reshape(Hk, G), SEQ_BLOCK, axis=1).reshape(Hk, G * SEQ_BLOCK, 1)
    args = (qh, kh, vh, sink_col, tables[0], tables[1], jnp.asarray(_rope_matrix()), jnp.asarray(_swa_mask_bias(G)))
    o = _unheads(swa_fwd(*args, tag + "_core"))
    y = mm(o, w_out, "nn", F32, name=tag + "_out")
    return y, (args, o)


def _swa_bwd(dy, h, w_in, w_out, ctx, tag):
    args, o = ctx
    Hq = args[0].shape[0]
    do = mm(dy, w_out, "nt", BF16, name=tag + "_do")
    dw_out = mm(o, dy, "tn", BF16, name=tag + "_dwout")
    dqh, dkp, dkc, dvp, dvc, dsk = swa_bwd(*args, _heads(do, Hq), tag + "_dcore")
    dk = shift_add(dkc, dkp, tag + "_dk")
    dv = shift_add(dvc, dvp, tag + "_dv")
    dproj = jnp.concatenate([_unheads(dqh), _unheads(dk), _unheads(dv)], axis=1).astype(BF16)
    dh = mm(dproj, w_in, "nt", F32, name=tag + "_dh")
    dw_in = mm(h, dproj, "tn", BF16, name=tag + "_dwin")
    return dh, dw_in, dw_out, dsk[:, :, 0].reshape(Hq)


def kernel(x, c, positions, ada_w, ada_b, mix_pre_g, mix_post_g, ffn_pre_g, ffn_post_g, ffn_w_gu, ffn_w_down, fox_w_in, fox_b_f, fox_w_out, sgu_w_in, sgu_ln_g, sgu_ln_b, sgu_w_s, sgu_b_s, sgu_w_out, swa_w_in, swa_sinks, swa_w_out, loss_target, m_ada_w, m_ada_b, m_mix_pre_g, m_mix_post_g, m_ffn_pre_g, m_ffn_post_g, m_ffn_w_gu, m_ffn_w_down, m_fox_w_in, m_fox_b_f, m_fox_w_out, m_sgu_w_in, m_sgu_ln_g, m_sgu_ln_b, m_sgu_w_s, m_sgu_b_s, m_sgu_w_out, m_swa_w_in, m_swa_sinks, m_swa_w_out, v_ada_w, v_ada_b, v_mix_pre_g, v_mix_post_g, v_ffn_pre_g, v_ffn_post_g, v_ffn_w_gu, v_ffn_w_down, v_fox_w_in, v_fox_b_f, v_fox_w_out, v_sgu_w_in, v_sgu_ln_g, v_sgu_ln_b, v_sgu_w_s, v_sgu_b_s, v_sgu_w_out, v_swa_w_in, v_swa_sinks, v_swa_w_out):
    P = dict(zip(INPUTS, (x, c, positions, ada_w, ada_b, mix_pre_g, mix_post_g, ffn_pre_g, ffn_post_g, ffn_w_gu, ffn_w_down, fox_w_in, fox_b_f, fox_w_out, sgu_w_in, sgu_ln_g, sgu_ln_b, sgu_w_s, sgu_b_s, sgu_w_out, swa_w_in, swa_sinks, swa_w_out, loss_target, m_ada_w, m_ada_b, m_mix_pre_g, m_mix_post_g, m_ffn_pre_g, m_ffn_post_g, m_ffn_w_gu, m_ffn_w_down, m_fox_w_in, m_fox_b_f, m_fox_w_out, m_sgu_w_in, m_sgu_ln_g, m_sgu_ln_b, m_sgu_w_s, m_sgu_b_s, m_sgu_w_out, m_swa_w_in, m_swa_sinks, m_swa_w_out, v_ada_w, v_ada_b, v_mix_pre_g, v_mix_post_g, v_ffn_pre_g, v_ffn_post_g, v_ffn_w_gu, v_ffn_w_down, v_fox_w_in, v_fox_b_f, v_fox_w_out, v_sgu_w_in, v_sgu_ln_g, v_sgu_ln_b, v_sgu_w_s, v_sgu_b_s, v_sgu_w_out, v_swa_w_in, v_swa_sinks, v_swa_w_out)))
    xs, target, pos = x[0], loss_target[0], positions[0]
    S, D = xs.shape
    L = ada_w.shape[0]
    n_mix = 3
    F = ffn_w_down.shape[1] * N_CHIP
    xi, yi, ci = _place()
    q_me = 2 * xi + yi
    dev = 4 * xi + 2 * yi + ci

    qc = jnp.stack([q_me, ci]).astype(jnp.int32)
    by_cols = {n: n in COLUMN_SHARDED for n in BIG}
    geo = {n: (by_cols[n], P[n].shape[1], P[n].shape[2] + (_lane_pad(P[n].shape[2]) if by_cols[n] else 0)) for n in BIG}
    groups = {
        "fox0": {n: (0, 1) for n in BIG if n.startswith("fox_")},
        "ffn0": {n: (0, 1) for n in BIG if n.startswith("ffn_")},
        "rest": {n: (1 if n.startswith(("fox_", "ffn_")) else 0, P[n].shape[0]) for n in BIG},
    }
    groups["rest"] = {n: r for n, r in groups["rest"].items() if r[1] > r[0]}

    def shard_of(n, lo, hi):
        s = P[n][lo:hi].astype(BF16)
        return jnp.pad(s, ((0, 0), (0, 0), (0, _lane_pad(s.shape[2])))) if by_cols[n] else s

    Wt = {n: [] for n in BIG}

    def finish_gather(g, fulls):
        for n, s, f in zip(groups[g], shards[g], fulls):
            f = place_shard(f, s, by_cols[n], qc, f"place_{g}_{n}")
            f = _unpad_shard_columns(f, P[n].shape[2]) if by_cols[n] else f
            f = jnp.pad(f, ((0, 0), (0, 0), (0, 3 * D + LANES - f.shape[2]))) if n == "fox_w_in" else f
            Wt[n] += [(f, l) for l in range(f.shape[0])]

    shards = {g: [shard_of(n, *r) for n, r in groups[g].items()] for g in groups}
    group_cols = {g: [by_cols[n] for n in groups[g]] for g in groups}
    finish_gather("fox0", ag_weights(shards["fox0"], group_cols["fox0"], "ag_weights_fox0"))

    c_all = ag_small(c.reshape(D // LANES, LANES), "ag_c").reshape(N_DEV, D)
    c_all = jnp.pad(c_all, ((0, 16 - N_DEV), (0, 0)))
    Nm = ada_w.shape[2]
    ada_b_mine = lax.dynamic_slice_in_dim(ada_b, q_me * Nm, Nm, axis=1).reshape(L, 1, Nm)
    modp = ada_fwd(c_all, ada_w, ada_b_mine, "ada_fwd")[:, :N_DEV]
    mod_all = ag_small(modp.reshape(-1, LANES), "ag_mod").reshape(N_DEV, L, N_DEV, Nm)
    mod_mine = lax.dynamic_index_in_dim(mod_all[0::2], dev, axis=2, keepdims=False)
    mods = mod_mine.transpose(1, 0, 2).reshape(L, 6, 1, D)

    tables = _rope_tables(pos)

    saved = []
    xc = xs
    for i in range(L):
        kind, j = i % n_mix, i // n_mix
        sh_m, sc_m, g_m, sh_f, sc_f, g_f = (mods[i, t] for t in range(6))
        t = f"l{i}"
        h1 = pre_fwd(xc, mix_pre_g[i:i + 1], sh_m, sc_m, t + "_pre_m")
        if kind == 0:
            rides = [ag_sidecar(shards[g], group_cols[g]) if i == 0 else None for g in ("ffn0", "rest")]
            y1, ctx, ffn0_fulls, rest_fulls = _fox_fwd(h1, Wt["fox_w_in"][j], fox_b_f[j], Wt["fox_w_out"][j], t + "_fox", *rides)
            if i == 0:
                finish_gather("ffn0", ffn0_fulls)
                finish_gather("rest", rest_fulls)
        elif kind == 1:
            y1, ctx = _sgu_fwd(h1, Wt["sgu_w_in"][j], sgu_ln_g[j], sgu_ln_b[j], sgu_w_s[j], sgu_b_s[j], Wt["sgu_w_out"][j], t + "_sgu")
        else:
            y1, ctx = _swa_fwd(h1, Wt["swa_w_in"][j], swa_sinks[j], Wt["swa_w_out"][j], tables, t + "_swa")
        xm = post_fwd(xc, y1, mix_post_g[i:i + 1], g_m, t + "_post_m")
        h2 = pre_fwd(xm, ffn_pre_g[i:i + 1], sh_f, sc_f, t + "_pre_f")
        gu = mm(h2, Wt["ffn_w_gu"][i], "nn", BF16, name=t + "_ffn_gu")
        a = act_fwd(gu, t + "_act")
        y2 = mm(a, Wt["ffn_w_down"][i], "nn", F32, name=t + "_ffn_down")
        xn = post_fwd(xm, y2, ffn_post_g[i:i + 1], g_f, t + "_post_f")
        saved.append((xc, h1, y1, ctx, xm, h2, gu, a, y2))
        xc = xn

    sq, dx = loss_head(xc, target, "loss_head")
    loss = lax.psum(sq[0, 0] * (0.5 / D), ("x", "y", "c"))

    big_g = {n: [None] * P[n].shape[0] for n in BIG}
    small_g = {n: [None] * P[n].shape[0] for n in SMALL}
    group_geo = {g: [geo[n] for n in groups[g]] for g in groups}
    group_grads, from_sibling, chip_part, from_chips = {}, {}, {}, {}

    def grads_of(g):
        if g not in group_grads:
            gs = [jnp.stack(big_g[n][lo:hi]) for n, (lo, hi) in groups[g].items()]
            group_grads[g] = [_pad_shard_columns(t, P[n].shape[2]) if by_cols[n] else t for n, t in zip(groups[g], gs)]
        return group_grads[g]

    def fold(g):
        if g not in from_sibling:
            from_sibling[g] = sibling_fold(grads_of(g), group_geo[g], "rs_fold_" + g)
        chip_part[g] = [fold_sum(t, r, *m, qc, f"rs_fold_sum_{g}_{n}")
                        for n, t, r, m in zip(groups[g], grads_of(g), from_sibling[g], group_geo[g])]
        return exchange_sidecar(chip_part[g], group_geo[g])

    for i in reversed(range(L)):
        kind, j = i % n_mix, i // n_mix
        sh_m, sc_m, g_m, sh_f, sc_f, g_f = (mods[i, t] for t in range(6))
        xc, h1, y1, ctx, xm, h2, gu, a, y2 = saved[i]
        t = f"l{i}"
        dy2, dgpost_f, dgate_f = post_bwd(y2, ffn_post_g[i:i + 1], g_f, dx, t + "_dpost_f")
        if i == 0:
            da, from_sibling["rest"] = mm(dy2, Wt["ffn_w_down"][i], "nt", F32, name=t + "_da",
                                          sidecar=fold_sidecar(grads_of("rest"), group_geo["rest"]))
        else:
            da = mm(dy2, Wt["ffn_w_down"][i], "nt", F32, name=t + "_da")
        big_g["ffn_w_down"][i] = mm(a, dy2, "tn", BF16, name=t + "_dwdown")
        dgu = act_bwd(gu, da, t + "_dact")
        dh2 = mm(dgu, Wt["ffn_w_gu"][i], "nt", F32, name=t + "_dh2")
        big_g["ffn_w_gu"][i] = mm(h2, dgu, "tn", BF16, name=t + "_dwgu")
        dxm, dgpre_f, dsh_f, dsc_f = pre_bwd(xm, ffn_pre_g[i:i + 1], sh_f, sc_f, dh2, dx, t + "_dpre_f")
        dy1, dgpost_m, dgate_m = post_bwd(y1, mix_post_g[i:i + 1], g_m, dxm, t + "_dpost_m")
        if kind == 0:
            rides = [fold(g) if i == 0 else None for g in ("ffn0", "rest")]
            dh1, dw_in, dw_out, db, from_ffn0, from_rest = _fox_bwd(dy1, h1, Wt["fox_w_in"][j], fox_b_f[j], Wt["fox_w_out"][j], ctx,
                                                                    t + "_fox", *rides)
            big_g["fox_w_in"][j], big_g["fox_w_out"][j], small_g["fox_b_f"][j] = dw_in, dw_out, db
            if i == 0:
                from_chips["ffn0"], from_chips["rest"] = from_ffn0, from_rest
        elif kind == 1:
            dh1, dw_in, dw_out, dlg, dlb, dws, dbs = _sgu_bwd(dy1, h1, Wt["sgu_w_in"][j], Wt["sgu_w_out"][j], ctx, t + "_sgu")
            big_g["sgu_w_in"][j], big_g["sgu_w_out"][j] = dw_in, dw_out
            small_g["sgu_ln_g"][j], small_g["sgu_ln_b"][j], small_g["sgu_w_s"][j], small_g["sgu_b_s"][j] = dlg, dlb, dws, dbs
        else:
            dh1, dw_in, dw_out, dsk = _swa_bwd(dy1, h1, Wt["swa_w_in"][j], Wt["swa_w_out"][j], ctx, t + "_swa")
            big_g["swa_w_in"][j], big_g["swa_w_out"][j], small_g["swa_sinks"][j] = dw_in, dw_out, dsk
        dx, dgpre_m, dsh_m, dsc_m = pre_bwd(xc, mix_pre_g[i:i + 1], sh_m, sc_m, dh1, dxm, t + "_dpre_m")
        small_g["ada_b"][i] = jnp.concatenate([dsh_m, dsc_m, dgate_m, dsh_f, dsc_f, dgate_f], axis=1)[0]
        small_g["mix_pre_g"][i], small_g["mix_post_g"][i] = dgpre_m[0], dgpost_m[0]
        small_g["ffn_pre_g"][i], small_g["ffn_post_g"][i] = dgpre_f[0], dgpost_f[0]
    grad_x = dx[None]

    shapes = {n: P[n].shape for n in SMALL}
    small_parts = ag_small(_pack_small({n: jnp.stack(small_g[n]) for n in SMALL}), "ag_small_grads")
    sg, sd, sm, sv = adamw(_pack_small({n: P[n] for n in SMALL}), _pack_small({n: P["m_" + n] for n in SMALL}),
                           _pack_small({n: P["v_" + n] for n in SMALL}), [small_parts], "adamw_small", gstack=N_DEV)
    out_g, out_d, out_m, out_v = (_unpack_small(t, shapes) for t in (sg, sd, sm, sv))

    dmod_all = small_parts[:, :L * 6 * D // LANES].reshape(N_DEV, L, 6 * D)
    dmod_mine = lax.dynamic_slice_in_dim(dmod_all, q_me * Nm, Nm, axis=2).transpose(1, 0, 2)
    dmod_mine = jnp.pad(dmod_mine, ((0, 0), (0, 16 - N_DEV), (0, 0)))
    g_ada = ada_wgrad(c_all, dmod_mine, "ada_wgrad")
    r2 = lambda t: t.reshape(-1, t.shape[-1])
    res = adamw(r2(ada_w), r2(m_ada_w), r2(v_ada_w), [r2(g_ada)], "adamw_ada_w", emit_g=False)
    out_g["ada_w"] = g_ada
    out_d["ada_w"], out_m["ada_w"], out_v["ada_w"] = (t.reshape(ada_w.shape) for t in res)

    fold("fox0")
    from_chips["fox0"] = chip_exchange(chip_part["fox0"], group_geo["fox0"], "rs_exchange_fox0")
    pieces = [(g, n, m, r, e) for g in groups for n, m, r, e in zip(groups[g], group_geo[g], chip_part[g], from_chips[g])]
    mine = [chip_sum(r, e, *m, qc, f"rs_chip_sum_{g}_{n}") for g, n, m, r, e in pieces]
    shared = sibling_share(mine, [m for _, _, m, _, _ in pieces], "rs_share")
    by_name = {n: [] for n in BIG}
    for (g, n, *_), t in zip(pieces, shared):
        by_name[n].append(t)
    for n in BIG:
        gsh = by_name[n][0] if len(by_name[n]) == 1 else jnp.concatenate(by_name[n], axis=0)
        gsh = gsh[:, :, :P[n].shape[2]]
        res = adamw(r2(P[n]), r2(P["m_" + n]), r2(P["v_" + n]), [r2(gsh)], "adamw_" + n, emit_g=False)
        out_g[n] = gsh
        out_d[n], out_m[n], out_v[n] = (t.reshape(P[n].shape) for t in res)

    return (loss, grad_x, *[out_g[n] for n in WEIGHTS], *[out_d[n] for n in WEIGHTS], *[out_m[n] for n in WEIGHTS],
            *[out_v[n] for n in WEIGHTS])
```
